```python
import math
import jax, jax.numpy as jnp
from jax import lax
import numpy as np

D_MODEL = 1024
BATCH = 8
SEQ = 2048
DEPTH = 2
DEC_BATCH = 32
DEC_SEQ = 8
PAST_LEN = 8192
PAGE_SIZE = 128

N_EVEN = (DEPTH + 1) // 2
N_ODD = DEPTH // 2
CONV_CH = D_MODEL // 2
CONV_WIDTH = 31
MOBA_HEADS = 8
MOBA_HEAD_DIM = 64
MOBA_WIDTH = MOBA_HEADS * MOBA_HEAD_DIM
MOBA_BLOCK = 256
MOBA_TOPK = 3
MOBA_Q_CHUNK = 32
NUM_BUCKETS = 32
MAX_DISTANCE = 128
RET_HEADS = 4
RET_DK = D_MODEL // RET_HEADS
RET_DV = 2 * D_MODEL // RET_HEADS
RET_CHUNK = 128
MEM_TOKENS = 256
MEM_HEADS = 4
MEM_HEAD_DIM = D_MODEL // MEM_HEADS
MOE_GROUPS = 4
MOE_EXPERTS_PER_GROUP = 8
MOE_EXPERTS = MOE_GROUPS * MOE_EXPERTS_PER_GROUP
MOE_TOPK = 2
MOE_D_FF = 512
MOE_TOKEN_BLOCK = 1024
DEEPNORM_ALPHA = (2 * DEPTH) ** 0.25
DEEPNORM_BETA = (8 * DEPTH) ** -0.25
LN_EPS = 1e-5

kernel_name = "hybrid_conv_moba_retnet_hmoe_step"

F32 = jnp.float32


def layer_norm(x, g, b):
    xf = x.astype(F32)
    mu = jnp.mean(xf, -1, keepdims=True)
    var = jnp.mean(jnp.square(xf - mu), -1, keepdims=True)
    return ((xf - mu) * lax.rsqrt(var + LN_EPS) * g + b).astype(x.dtype)


def deepnorm(x, h, g, b):
    return layer_norm(DEEPNORM_ALPHA * x + h.astype(x.dtype), g, b)


def rel_bucket(n):
    n = jnp.maximum(n, 0)
    max_exact = NUM_BUCKETS // 2
    large = max_exact + (jnp.log(jnp.maximum(n, 1).astype(F32) / max_exact)
                         / math.log(MAX_DISTANCE / max_exact) * (NUM_BUCKETS - max_exact)).astype(jnp.int32)
    return jnp.where(n < max_exact, n, jnp.minimum(large, NUM_BUCKETS - 1))


def moba_attention(q, k, v, q_pos, rel_bias):
    B, Tq, H, hd = q.shape
    L = k.shape[1]
    nb = -(-L // MOBA_BLOCK)
    pad = nb * MOBA_BLOCK - L
    kb = jnp.pad(k, ((0, 0), (0, pad), (0, 0), (0, 0))).reshape(B, nb, MOBA_BLOCK, H, hd)
    vb = jnp.pad(v, ((0, 0), (0, pad), (0, 0), (0, 0))).reshape(B, nb, MOBA_BLOCK, H, hd)
    kmean = jnp.mean(kb, axis=2, dtype=F32)
    q_blk = q_pos // MOBA_BLOCK
    gate = jnp.einsum('bthd,bnhd->bhtn', q.astype(F32), kmean)
    fully_past = jnp.arange(nb)[None, :] < q_blk[:, None]
    gate = jnp.where(fully_past, gate, -jnp.inf)
    n_sel = min(MOBA_TOPK, nb)
    _, sel = lax.top_k(gate, n_sel)
    own = jnp.broadcast_to(q_blk[None, None, :, None], (B, H, Tq, 1)).astype(sel.dtype)
    blocks = jnp.concatenate([sel, own], -1)
    valid = jnp.concatenate([jnp.arange(n_sel)[None, :] < q_blk[:, None],
                             jnp.ones((Tq, 1), bool)], -1)
    S = n_sel + 1
    C = math.gcd(Tq, MOBA_Q_CHUNK)
    nC = Tq // C
    bi = jnp.arange(B)[:, None, None, None]
    hi = jnp.arange(H)[None, :, None, None]
    offs = jnp.arange(MOBA_BLOCK)
    bias_hb = rel_bias.T
    scale = hd ** -0.5

    def attend(args):
        qc, pc, bc, vc = args
        kg = kb[bi, bc, :, hi]
        vg = vb[bi, bc, :, hi]
        logits = jnp.einsum('bchd,bhcskd->bhcsk', qc, kg, preferred_element_type=F32) * scale
        rel = pc[None, None, :, None, None] - (bc[..., None] * MOBA_BLOCK + offs)
        logits = logits + bias_hb[hi[..., None], rel_bucket(rel)]
        mask = vc[None, None, :, :, None] & (rel >= 0)
        logits = jnp.where(mask, logits, -jnp.inf)
        p = jax.nn.softmax(logits, axis=(-2, -1))
        out = jnp.einsum('bhcsk,bhcskd->bchd', p.astype(vg.dtype), vg, preferred_element_type=F32)
        return out.astype(q.dtype)

    qs = q.reshape(B, nC, C, H, hd).transpose(1, 0, 2, 3, 4)
    ps = q_pos.reshape(nC, C)
    bs = blocks.reshape(B, H, nC, C, S).transpose(2, 0, 1, 3, 4)
    vs = valid.reshape(nC, C, S)
    out = lax.map(attend, (qs, ps, bs, vs))
    return out.transpose(1, 0, 2, 3, 4).reshape(B, Tq, H, hd)


def even_mixer(x, pos, k_past, v_past, conv_hist, rel_bias, w_in, conv_w, conv_b, ln_g, ln_b, w_out):
    B, T, _ = x.shape
    a, gt, q, k, v = jnp.split(x @ w_in, [CONV_CH, 2 * CONV_CH, 2 * CONV_CH + MOBA_WIDTH,
                                          2 * CONV_CH + 2 * MOBA_WIDTH], axis=-1)
    u = a * jax.nn.sigmoid(gt)
    u_hist = jnp.concatenate([conv_hist.astype(u.dtype), u], 1)
    c = lax.conv_general_dilated(u_hist, conv_w[:, None, :].astype(u.dtype), (1,), 'VALID',
                                 dimension_numbers=('NWC', 'WIO', 'NWC'),
                                 feature_group_count=CONV_CH) + conv_b
    c = jax.nn.silu(layer_norm(c, ln_g, ln_b))
    q = q.reshape(B, T, MOBA_HEADS, MOBA_HEAD_DIM)
    k = k.reshape(B, T, MOBA_HEADS, MOBA_HEAD_DIM)
    v = v.reshape(B, T, MOBA_HEADS, MOBA_HEAD_DIM)
    k_all = jnp.concatenate([k_past.astype(k.dtype), k], 1)
    v_all = jnp.concatenate([v_past.astype(v.dtype), v], 1)
    attn = moba_attention(q, k_all, v_all, pos, rel_bias).reshape(B, T, MOBA_WIDTH)
    out = jnp.concatenate([c.astype(x.dtype), attn], -1) @ w_out
    return out, k, v, u_hist[:, -(CONV_WIDTH - 1):]


def rotary(x, pos):
    half = x.shape[-1] // 2
    inv = 10000.0 ** (-jnp.linspace(0.0, 1.0, half, dtype=F32))
    ang = pos.astype(F32)[:, None] * inv[None, :]
    cos = jnp.cos(ang)[None, :, None, :]
    sin = jnp.sin(ang)[None, :, None, :]
    x1 = x[..., :half].astype(F32)
    x2 = x[..., half:].astype(F32)
    return jnp.concatenate([x1 * cos - x2 * sin, x2 * cos + x1 * sin], -1)


def retention(q, k, v, s0):
    B, T, H, _ = q.shape
    dv = v.shape[-1]
    C = math.gcd(T, RET_CHUNK)
    nC = T // C
    lg = jnp.log1p(-jnp.exp2(-5.0 - jnp.arange(H, dtype=F32)))
    idx = jnp.arange(C, dtype=F32)
    diff = idx[:, None] - idx[None, :]
    dmask = jnp.where(diff >= 0, jnp.exp(lg[:, None, None] * jnp.maximum(diff, 0.0)), 0.0)
    xi = jnp.exp(lg[None, :] * (idx[:, None] + 1.0))
    zeta = jnp.exp(lg[None, :] * (C - 1.0 - idx[:, None]))
    g_c = jnp.exp(lg * C)

    def to_chunks(a):
        return a.astype(F32).reshape(B, nC, C, H, a.shape[-1]).transpose(1, 0, 2, 3, 4)

    def step(s, inp):
        qc, kc, vc = inp
        att = jnp.einsum('bihd,bjhd->bhij', qc, kc) * dmask
        o = (jnp.einsum('bhij,bjhe->bihe', att, vc)
             + jnp.einsum('bihd,bhde->bihe', qc, s) * xi[None, :, :, None])
        s = s * g_c[None, :, None, None] + jnp.einsum('bjhd,bjhe->bhde', kc * zeta[None, :, :, None], vc)
        return s, o

    s, o = lax.scan(step, s0.astype(F32), (to_chunks(q), to_chunks(k), to_chunks(v)))
    return o.transpose(1, 0, 2, 3, 4).reshape(B, T, H, dv), s


def odd_mixer(x, pos, s0, w_in, gn_g, gn_b, w_out):
    B, T, _ = x.shape
    qk = RET_HEADS * RET_DK
    vw = RET_HEADS * RET_DV
    q, k, v, g = jnp.split(x @ w_in, [qk, 2 * qk, 2 * qk + vw], axis=-1)
    q = rotary(q.reshape(B, T, RET_HEADS, RET_DK), pos)
    k = rotary(k.reshape(B, T, RET_HEADS, RET_DK), pos) * RET_DK ** -0.5
    o, s = retention(q, k, v.reshape(B, T, RET_HEADS, RET_DV), s0)
    mu = jnp.mean(o, -1, keepdims=True)
    var = jnp.mean(jnp.square(o - mu), -1, keepdims=True)
    o = ((o - mu) * lax.rsqrt(var + LN_EPS)).reshape(B, T, vw) * gn_g + gn_b
    y = (jax.nn.silu(g.astype(F32)) * o).astype(x.dtype) @ w_out
    return y, s


def mem_kv(mem, wk, wv):
    B, M, _ = mem.shape
    return ((mem @ wk).reshape(B, M, MEM_HEADS, MEM_HEAD_DIM),
            (mem @ wv).reshape(B, M, MEM_HEADS, MEM_HEAD_DIM))


def mem_attend(x, mk, mv, wq, wo):
    B, T, D = x.shape
    q = (x @ wq).reshape(B, T, MEM_HEADS, MEM_HEAD_DIM)
    logits = jnp.einsum('bthd,bmhd->bhtm', q, mk.astype(q.dtype), preferred_element_type=F32) * MEM_HEAD_DIM ** -0.5
    p = jax.nn.softmax(logits, -1)
    o = jnp.einsum('bhtm,bmhd->bthd', p.astype(x.dtype), mv.astype(x.dtype)).reshape(B, T, D)
    return o @ wo


def hier_moe(x, w_group, b_group, w_router, b_router, w1, w3, w2):
    B, T, D = x.shape
    N = B * T
    xf = x.reshape(N, D)
    g_prob = jax.nn.softmax((xf @ w_group).astype(F32) + b_group, -1)
    gp, gi = lax.top_k(g_prob, 1)
    e_logits = ((xf @ w_router).astype(F32) + b_router).reshape(N, MOE_GROUPS, MOE_EXPERTS_PER_GROUP)
    e_in = jnp.take_along_axis(e_logits, gi[:, :, None], axis=1)[:, 0]
    ep, ei = lax.top_k(jax.nn.softmax(e_in, -1), MOE_TOPK)
    ep = ep / jnp.sum(ep, -1, keepdims=True)
    wts = gp * ep
    eidx = gi * MOE_EXPERTS_PER_GROUP + ei
    combine = jnp.sum(jax.nn.one_hot(eidx, MOE_EXPERTS, dtype=F32) * wts[..., None], axis=1)
    blk = min(MOE_TOKEN_BLOCK, N)
    nblk = -(-N // blk)
    pad = nblk * blk - N
    xb = jnp.pad(xf, ((0, pad), (0, 0))).reshape(nblk, blk, D)
    cb = jnp.pad(combine, ((0, pad), (0, 0))).reshape(nblk, blk, MOE_EXPERTS)

    def expert_block(args):
        xc, cc = args
        h = jax.nn.silu(jnp.einsum('nd,edf->nef', xc, w1)) * jnp.einsum('nd,edf->nef', xc, w3)
        return jnp.einsum('nef,efd->nd', h * cc[:, :, None].astype(h.dtype), w2)

    y = lax.map(expert_block, (xb, cb)).reshape(nblk * blk, D)[:N]
    return y.reshape(B, T, D)


def setup_inputs(seed: int = 0) -> dict:
    key = jax.random.key(seed)
    ks = iter(jax.random.split(key, 40))

    def nrm(shape, scale=1.0):
        return jax.random.normal(next(ks), shape, F32) * scale

    def w(shape, fan_in, gain=1.0):
        return nrm(shape, gain * fan_in ** -0.5)

    n_pages = PAST_LEN // PAGE_SIZE
    n_used = DEC_BATCH * n_pages
    n_pool = n_used + n_used // 4
    d_in_even = 2 * CONV_CH + 3 * MOBA_WIDTH
    d_in_odd = 2 * RET_HEADS * RET_DK + 2 * RET_HEADS * RET_DV
    return {
        "x_prompt": nrm((BATCH, SEQ, D_MODEL)),
        "x_sample": nrm((DEC_BATCH, DEC_SEQ, D_MODEL)),
        "cache_moba_k": nrm((n_pool, PAGE_SIZE, N_EVEN, MOBA_HEADS, MOBA_HEAD_DIM)),
        "cache_moba_v": nrm((n_pool, PAGE_SIZE, N_EVEN, MOBA_HEADS, MOBA_HEAD_DIM)),
        "state_conv": nrm((N_EVEN, DEC_BATCH, CONV_WIDTH - 1, CONV_CH), 0.5),
        "state_ret": nrm((N_ODD, DEC_BATCH, RET_HEADS, RET_DK, RET_DV), 0.3),
        "cache_mem_k": nrm((DEPTH, DEC_BATCH, MEM_TOKENS, MEM_HEADS, MEM_HEAD_DIM)),
        "cache_mem_v": nrm((DEPTH, DEC_BATCH, MEM_TOKENS, MEM_HEADS, MEM_HEAD_DIM)),
        "page_table": jax.random.permutation(next(ks), n_pool)[:n_used].reshape(DEC_BATCH, n_pages).astype(jnp.int32),
        "mem_prompt": nrm((BATCH, MEM_TOKENS, D_MODEL)),
        "rel_bias": nrm((NUM_BUCKETS, MOBA_HEADS), 0.5),
        "ev_w_in": w((N_EVEN, D_MODEL, d_in_even), D_MODEL),
        "ev_conv_w": w((N_EVEN, CONV_WIDTH, CONV_CH), CONV_WIDTH),
        "ev_conv_b": nrm((N_EVEN, CONV_CH), 0.02),
        "ev_conv_ln_g": 1.0 + nrm((N_EVEN, CONV_CH), 0.02),
        "ev_conv_ln_b": nrm((N_EVEN, CONV_CH), 0.02),
        "ev_w_out": w((N_EVEN, CONV_CH + MOBA_WIDTH, D_MODEL), CONV_CH + MOBA_WIDTH, DEEPNORM_BETA),
        "od_w_in": w((N_ODD, D_MODEL, d_in_odd), D_MODEL),
        "od_gn_g": 1.0 + nrm((N_ODD, RET_HEADS * RET_DV), 0.02),
        "od_gn_b": nrm((N_ODD, RET_HEADS * RET_DV), 0.02),
        "od_w_out": w((N_ODD, RET_HEADS * RET_DV, D_MODEL), RET_HEADS * RET_DV, DEEPNORM_BETA),
        "mem_wq": w((DEPTH, D_MODEL, D_MODEL), D_MODEL),
        "mem_wk": w((DEPTH, D_MODEL, D_MODEL), D_MODEL),
        "mem_wv": w((DEPTH, D_MODEL, D_MODEL), D_MODEL),
        "mem_wo": w((DEPTH, D_MODEL, D_MODEL), D_MODEL, DEEPNORM_BETA),
        "ln_g": 1.0 + nrm((DEPTH, 3, D_MODEL), 0.02),
        "ln_b": nrm((DEPTH, 3, D_MODEL), 0.02),
        "moe_w_group": w((DEPTH, D_MODEL, MOE_GROUPS), D_MODEL),
        "moe_b_group": nrm((DEPTH, MOE_GROUPS), 0.01),
        "moe_w_router": w((DEPTH, D_MODEL, MOE_EXPERTS), D_MODEL),
        "moe_b_router": nrm((DEPTH, MOE_EXPERTS), 0.01),
        "moe_w1": w((DEPTH, MOE_EXPERTS, D_MODEL, MOE_D_FF), D_MODEL),
        "moe_w3": w((DEPTH, MOE_EXPERTS, D_MODEL, MOE_D_FF), D_MODEL),
        "moe_w2": w((DEPTH, MOE_EXPERTS, MOE_D_FF, D_MODEL), MOE_D_FF, DEEPNORM_BETA),
    }


def reference(x_prompt, x_sample, cache_moba_k, cache_moba_v, state_conv, state_ret, cache_mem_k, cache_mem_v,
              page_table, mem_prompt, rel_bias, ev_w_in, ev_conv_w, ev_conv_b, ev_conv_ln_g, ev_conv_ln_b,
              ev_w_out, od_w_in, od_gn_g, od_gn_b, od_w_out, mem_wq, mem_wk, mem_wv, mem_wo, ln_g, ln_b,
              moe_w_group, moe_b_group, moe_w_router, moe_b_router, moe_w1, moe_w3, moe_w2):
    bp, tp, _ = x_prompt.shape
    bs, ts, _ = x_sample.shape
    past_len = page_table.shape[1] * cache_moba_k.shape[1]
    pos_p = jnp.arange(tp, dtype=jnp.int32)
    pos_s = past_len + jnp.arange(ts, dtype=jnp.int32)
    xp, xs = x_prompt, x_sample
    kp_l, vp_l, ks_l, vs_l, cp_l, cs_l, sp_l, ss_l, mk_l, mv_l = ([] for _ in range(10))
    for layer in range(DEPTH):
        if layer % 2 == 0:
            e = layer // 2
            ev = (rel_bias, ev_w_in[e], ev_conv_w[e], ev_conv_b[e], ev_conv_ln_g[e], ev_conv_ln_b[e], ev_w_out[e])
            empty = jnp.zeros((bp, 0, MOBA_HEADS, MOBA_HEAD_DIM), xp.dtype)
            hp, kpn, vpn, cpn = even_mixer(xp, pos_p, empty, empty,
                                           jnp.zeros((bp, CONV_WIDTH - 1, CONV_CH), xp.dtype), *ev)
            k_past = cache_moba_k[page_table, :, e].reshape(bs, past_len, MOBA_HEADS, MOBA_HEAD_DIM)
            v_past = cache_moba_v[page_table, :, e].reshape(bs, past_len, MOBA_HEADS, MOBA_HEAD_DIM)
            hs, ksn, vsn, csn = even_mixer(xs, pos_s, k_past, v_past, state_conv[e], *ev)
            kp_l.append(kpn); vp_l.append(vpn); ks_l.append(ksn); vs_l.append(vsn)
            cp_l.append(cpn); cs_l.append(csn)
        else:
            o = layer // 2
            od = (od_w_in[o], od_gn_g[o], od_gn_b[o], od_w_out[o])
            hp, spn = odd_mixer(xp, pos_p, jnp.zeros((bp, RET_HEADS, RET_DK, RET_DV), F32), *od)
            hs, ssn = odd_mixer(xs, pos_s, state_ret[o], *od)
            sp_l.append(spn); ss_l.append(ssn)
        xp = deepnorm(xp, hp, ln_g[layer, 0], ln_b[layer, 0])
        xs = deepnorm(xs, hs, ln_g[layer, 0], ln_b[layer, 0])
        mkp, mvp = mem_kv(mem_prompt, mem_wk[layer], mem_wv[layer])
        mk_l.append(mkp); mv_l.append(mvp)
        xp = deepnorm(xp, mem_attend(xp, mkp, mvp, mem_wq[layer], mem_wo[layer]), ln_g[layer, 1], ln_b[layer, 1])
        xs = deepnorm(xs, mem_attend(xs, cache_mem_k[layer], cache_mem_v[layer], mem_wq[layer], mem_wo[layer]),
                      ln_g[layer, 1], ln_b[layer, 1])
        moe = (moe_w_group[layer], moe_b_group[layer], moe_w_router[layer], moe_b_router[layer],
               moe_w1[layer], moe_w3[layer], moe_w2[layer])
        xp = deepnorm(xp, hier_moe(xp, *moe), ln_g[layer, 2], ln_b[layer, 2])
        xs = deepnorm(xs, hier_moe(xs, *moe), ln_g[layer, 2], ln_b[layer, 2])
    moba_k_prompt = jnp.stack(kp_l, axis=2)
    moba_v_prompt = jnp.stack(vp_l, axis=2)
    moba_k_sample = jnp.stack(ks_l, axis=2)
    moba_v_sample = jnp.stack(vs_l, axis=2)
    conv_state_prompt = jnp.stack(cp_l, axis=0)
    conv_state_sample = jnp.stack(cs_l, axis=0)
    ret_state_prompt = jnp.stack(sp_l, axis=0)
    ret_state_sample = jnp.stack(ss_l, axis=0)
    mem_k_prompt = jnp.stack(mk_l, axis=0)
    mem_v_prompt = jnp.stack(mv_l, axis=0)
    return (xp, xs, moba_k_prompt, moba_v_prompt, moba_k_sample, moba_v_sample,
            conv_state_prompt, conv_state_sample, ret_state_prompt, ret_state_sample,
            mem_k_prompt, mem_v_prompt)
```

```python
import functools
import math

import numpy as np
import jax
import jax.numpy as jnp
from jax import lax
from jax.experimental import pallas as pl
from jax.experimental.pallas import tpu as pltpu

F32 = jnp.float32
BF16 = jnp.bfloat16
I32 = jnp.int32

D_MODEL = 1024
DEPTH = 2
CONV_CH = 512
CONV_WIDTH = 31
MOBA_HEADS = 8
MOBA_HEAD_DIM = 64
MOBA_WIDTH = 512
MOBA_BLOCK = 256
MOBA_TOPK = 3
NUM_BUCKETS = 32
MAX_DISTANCE = 128
RET_HEADS = 4
RET_DK = 256
RET_DV = 512
RET_CHUNK = 128
MEM_HEADS = 4
MEM_HEAD_DIM = 256
MOE_GROUPS = 4
MOE_EPG = 8
MOE_EXPERTS = 32
MOE_TOPK = 2
MOE_D_FF = 512
DEEPNORM_ALPHA = (2 * DEPTH) ** 0.25
LN_EPS = 1e-5

LANES = 128
VMEM_LIMIT = 56 * 1024 * 1024
MOE_TILE = 256
NEG_INF = float("-inf")


def _cparams(sem):
    return pltpu.CompilerParams(dimension_semantics=sem, vmem_limit_bytes=VMEM_LIMIT)


def _layer_norm(y, g, b):
    mu = jnp.mean(y, axis=-1, keepdims=True)
    yc = y - mu
    var = jnp.mean(yc * yc, axis=-1, keepdims=True)
    return yc * lax.rsqrt(var + LN_EPS) * g + b


def _silu(x):
    return x * (1.0 / (1.0 + jnp.exp(-x)))


def _bdot(a, b):
    return jnp.dot(a.astype(BF16), b.astype(BF16), preferred_element_type=F32)


def _bdot_nt(a, b):
    return lax.dot_general(a.astype(BF16), b.astype(BF16), (((1,), (1,)), ((), ())),
                           preferred_element_type=F32)


def _bdot_tn(a, b):
    return lax.dot_general(a.astype(BF16), b.astype(BF16), (((0,), (0,)), ((), ())),
                           preferred_element_type=F32)


def _split3(x):
    hi = x.astype(BF16)
    lo = (x - hi.astype(F32)).astype(BF16)
    return hi, lo


def _dot_hi(a, b, dims=(((1,), (0,)), ((), ()))):
    ah, al = _split3(a)
    bh, bl = _split3(b)
    dg = functools.partial(lax.dot_general, dimension_numbers=dims, preferred_element_type=F32)
    return dg(ah, bh) + (dg(al, bh) + dg(ah, bl))


def _mm_kernel(x_ref, w_ref, o_ref):
    o_ref[...] = _bdot(x_ref[...], w_ref[...]).astype(o_ref.dtype)


def matmul(x, w, *, tm, tn, out_dtype=F32):
    m, k = x.shape
    n = w.shape[1]
    assert m % tm == 0 and n % tn == 0
    return pl.pallas_call(
        _mm_kernel,
        out_shape=jax.ShapeDtypeStruct((m, n), out_dtype),
        grid=(m // tm, n // tn),
        in_specs=[pl.BlockSpec((tm, k), lambda i, j: (i, 0)),
                  pl.BlockSpec((k, tn), lambda i, j: (0, j))],
        out_specs=pl.BlockSpec((tm, tn), lambda i, j: (i, j)),
        compiler_params=_cparams(("parallel", "parallel")),
        name="matmul",
    )(x, w)


def _mm_dn_kernel(*refs, n_lhs):
    lhs = refs[:n_lhs]
    ws = refs[n_lhs:2 * n_lhs]
    res_ref, g_ref, b_ref, o_ref = refs[2 * n_lhs:]
    acc = _bdot(lhs[0][...], ws[0][...])
    for a, w in zip(lhs[1:], ws[1:]):
        acc = acc + _bdot(a[...], w[...])
    y = DEEPNORM_ALPHA * res_ref[...] + acc
    o_ref[...] = _layer_norm(y, g_ref[...], b_ref[...])


def matmul_deepnorm(lhs_list, w_list, res, g, b, *, tm):
    m, d = res.shape
    n_lhs = len(lhs_list)
    in_specs = ([pl.BlockSpec((tm, a.shape[1]), lambda i: (i, 0)) for a in lhs_list]
                + [pl.BlockSpec(w.shape, lambda i: (0, 0)) for w in w_list]
                + [pl.BlockSpec((tm, d), lambda i: (i, 0)),
                   pl.BlockSpec((1, d), lambda i: (0, 0)),
                   pl.BlockSpec((1, d), lambda i: (0, 0))])
    return pl.pallas_call(
        functools.partial(_mm_dn_kernel, n_lhs=n_lhs),
        out_shape=jax.ShapeDtypeStruct((m, d), F32),
        grid=(m // tm,),
        in_specs=in_specs,
        out_specs=pl.BlockSpec((tm, d), lambda i: (i, 0)),
        compiler_params=_cparams(("parallel",)),
        name="matmul_deepnorm",
    )(*lhs_list, *w_list, res, g.reshape(1, d), b.reshape(1, d))


CONV_PAD = 32


def _conv_kernel(p_ref, hist_ref, w_ref, cb_ref, g_ref, b_ref, c_ref, st_ref, u_scr, *, nb, t_len, rc):
    hw = CONV_WIDTH - 1
    w = w_ref[...]
    for bb in range(nb):
        rows = pl.ds(bb * t_len, t_len)
        a = p_ref[rows, 0:CONV_CH]
        gt = p_ref[rows, CONV_CH:2 * CONV_CH]
        u = a * (1.0 / (1.0 + jnp.exp(-gt)))
        u_scr[0:CONV_PAD, :] = jnp.concatenate(
            [jnp.zeros((CONV_PAD - hw, CONV_CH), F32), hist_ref[bb]], axis=0)
        u_scr[CONV_PAD:CONV_PAD + t_len, :] = u
        st_ref[bb] = u_scr[CONV_PAD + t_len - hw:CONV_PAD + t_len, :]

        def chunk(ci, carry):
            r0 = pl.multiple_of(ci * rc, rc)
            blk = u_scr[pl.ds(r0, rc + CONV_PAD), :]
            acc = jnp.broadcast_to(cb_ref[...], (rc, CONV_CH))
            for j in range(CONV_WIDTH):
                off = j + CONV_PAD - hw
                acc = acc + w[j:j + 1, :] * blk[off:off + rc, :]
            y = _silu(_layer_norm(acc, g_ref[...], b_ref[...]))
            c_ref[pl.ds(bb * t_len + r0, rc), :] = y.astype(c_ref.dtype)
            return carry

        n_chunks = t_len // rc
        if n_chunks == 1:
            chunk(0, 0)
        else:
            lax.fori_loop(0, n_chunks, chunk, 0)


def conv_module(p, hist, conv_w, conv_b, ln_g, ln_b, *, row0, n_batch, t_len, nb, out_dtype):
    rc = min(64, t_len)
    blk_rows = nb * t_len
    assert row0 % blk_rows == 0 and n_batch % nb == 0
    base = row0 // blk_rows
    hw = CONV_WIDTH - 1
    c, st = pl.pallas_call(
        functools.partial(_conv_kernel, nb=nb, t_len=t_len, rc=rc),
        out_shape=(jax.ShapeDtypeStruct((n_batch * t_len, CONV_CH), out_dtype),
                   jax.ShapeDtypeStruct((n_batch, hw, CONV_CH), F32)),
        grid=(n_batch // nb,),
        in_specs=[pl.BlockSpec((blk_rows, 2 * CONV_CH), lambda i: (base + i, 0)),
                  pl.BlockSpec((nb, hw, CONV_CH), lambda i: (i, 0, 0)),
                  pl.BlockSpec((CONV_WIDTH, CONV_CH), lambda i: (0, 0)),
                  pl.BlockSpec((1, CONV_CH), lambda i: (0, 0)),
                  pl.BlockSpec((1, CONV_CH), lambda i: (0, 0)),
                  pl.BlockSpec((1, CONV_CH), lambda i: (0, 0))],
        out_specs=(pl.BlockSpec((blk_rows, CONV_CH), lambda i: (i, 0)),
                   pl.BlockSpec((nb, hw, CONV_CH), lambda i: (i, 0, 0))),
        scratch_shapes=[pltpu.VMEM((t_len + CONV_PAD, CONV_CH), F32)],
        compiler_params=_cparams(("parallel",)),
        name="conv_module",
    )(p, hist, conv_w, conv_b.reshape(1, -1), ln_g.reshape(1, -1), ln_b.reshape(1, -1))
    return c, st


def _moba_prompt_kernel(q_ref, k_ref, v_ref, d0_ref, d1_ref, far_ref, o_ref, kmean_scr, *, n_blk):
    i = pl.program_id(1)
    blk = MOBA_BLOCK
    hd = MOBA_HEAD_DIM
    scale = hd ** -0.5

    @pl.when(i == 0)
    def _():
        for n in range(n_blk):
            kmean_scr[n:n + 1, :] = jnp.sum(k_ref[n * blk:(n + 1) * blk, :], axis=0, keepdims=True) * (1.0 / blk)

    q = q_ref[...]
    lane = lax.broadcasted_iota(I32, (blk, LANES), 1)
    col8 = lax.broadcasted_iota(I32, (blk, n_blk), 1)
    row_t = lax.broadcasted_iota(I32, (blk, blk), 0)
    col_s = lax.broadcasted_iota(I32, (blk, blk), 1)
    causal = row_t >= col_s
    outs = []
    for pair in range(MOBA_HEADS // 2):
        qp = q[:, pair * LANES:(pair + 1) * LANES]
        pair_out = []
        for sub in range(2):
            h = 2 * pair + sub
            in_head = (lane >= sub * hd) & (lane < (sub + 1) * hd)
            qm = jnp.where(in_head, qp, 0.0)
            km = kmean_scr[:, pair * LANES:(pair + 1) * LANES]
            gate = _dot_hi(qm, km, (((1,), (1,)), ((), ())))
            past = col8 < i
            rank = jnp.zeros((blk, n_blk), I32)
            for m in range(n_blk):
                gm = gate[:, m:m + 1]
                beats = (gm > gate) | ((gm == gate) & (m < col8))
                rank = rank + jnp.where(beats & (m < i), 1, 0)
            sel = jnp.where(past & (rank < MOBA_TOPK), 1.0, 0.0)

            qb = qm.astype(BF16)
            kp = k_ref[pl.ds(pl.multiple_of(i * blk, blk), blk), pair * LANES:(pair + 1) * LANES]
            vp = v_ref[pl.ds(pl.multiple_of(i * blk, blk), blk), pair * LANES:(pair + 1) * LANES]
            s = _bdot_nt(qb, kp) * scale + d0_ref[h]
            s = jnp.where(causal, s, NEG_INF)
            m_run = jnp.max(s, axis=1, keepdims=True)
            p = jnp.exp(s - m_run)
            l_run = jnp.sum(p, axis=1, keepdims=True)
            acc = _bdot(p, vp)

            def body(n, carry):
                m_run, l_run, acc = carry
                r0 = pl.multiple_of(n * blk, blk)
                kp = k_ref[pl.ds(r0, blk), pair * LANES:(pair + 1) * LANES]
                vp = v_ref[pl.ds(r0, blk), pair * LANES:(pair + 1) * LANES]
                bias = jnp.where(n == i - 1, d1_ref[h], far_ref[h:h + 1, 0:1])
                s = _bdot_nt(qb, kp) * scale + bias
                seln = jnp.sum(jnp.where(col8 == n, sel, 0.0), axis=1, keepdims=True)
                s = jnp.where(seln > 0.5, s, NEG_INF)
                m_new = jnp.maximum(m_run, jnp.max(s, axis=1, keepdims=True))
                alpha = jnp.exp(m_run - m_new)
                p = jnp.exp(s - m_new)
                l_new = alpha * l_run + jnp.sum(p, axis=1, keepdims=True)
                acc = alpha * acc + _bdot(p, vp)
                return m_new, l_new, acc

            m_run, l_run, acc = lax.fori_loop(0, i, body, (m_run, l_run, acc))
            pair_out.append(acc / l_run)
        lane_lo = lane < hd
        outs.append(jnp.where(lane_lo, pair_out[0], pair_out[1]))
    o_ref[...] = jnp.concatenate(outs, axis=1).astype(o_ref.dtype)


def moba_prompt(p, d0, d1, far, *, n_batch, t_len, out_dtype):
    n_blk = t_len // MOBA_BLOCK
    return pl.pallas_call(
        functools.partial(_moba_prompt_kernel, n_blk=n_blk),
        out_shape=jax.ShapeDtypeStruct((n_batch * t_len, MOBA_WIDTH), out_dtype),
        grid=(n_batch, n_blk),
        in_specs=[pl.BlockSpec((MOBA_BLOCK, MOBA_WIDTH), lambda b, i: (b * n_blk + i, 2)),
                  pl.BlockSpec((t_len, MOBA_WIDTH), lambda b, i: (b, 3)),
                  pl.BlockSpec((t_len, MOBA_WIDTH), lambda b, i: (b, 4)),
                  pl.BlockSpec(d0.shape, lambda b, i: (0, 0, 0)),
                  pl.BlockSpec(d1.shape, lambda b, i: (0, 0, 0)),
                  pl.BlockSpec(far.shape, lambda b, i: (0, 0))],
        out_specs=pl.BlockSpec((MOBA_BLOCK, MOBA_WIDTH), lambda b, i: (b * n_blk + i, 0)),
        scratch_shapes=[pltpu.VMEM((n_blk, MOBA_WIDTH), F32)],
        compiler_params=_cparams(("parallel", "arbitrary")),
        name="moba_prompt",
    )(p, p, p, d0, d1, far)


def _page_stats(s, valid, v2):
    s = jnp.where(valid, s, NEG_INF)
    m = jnp.max(s, axis=0, keepdims=True)
    p = jnp.exp(s - m)
    l = jnp.sum(p, axis=0, keepdims=True)
    o_t = _bdot_tn(v2, p)
    return m, l, o_t


def _moba_sample_kernel(pt_ref, k0_ref, k1_ref, v0_ref, v1_ref, qt_ref, q2_ref, kn_ref, vn_ref,
                        blast_ref, bown_ref, far_ref, o_ref,
                        m_scr, l_scr, o_scr, km_scr, *, n_blk, page, t_len):
    n = pl.program_id(1)
    hd = MOBA_HEAD_DIM
    nh = MOBA_HEADS
    ncol = nh * t_len
    scale = hd ** -0.5
    qt = qt_ref[0]
    rows = page * nh

    def head_match(nrows):
        r = lax.broadcasted_iota(I32, (nrows, ncol), 0)
        c = lax.broadcasted_iota(I32, (nrows, ncol), 1)
        return (r % nh) == (c // t_len), r, c

    match, _, _ = head_match(rows)
    is_last = n == n_blk - 1
    ksum = jnp.zeros((nh, hd), F32)
    for half, (k_ref, v_ref) in enumerate(((k0_ref, v0_ref), (k1_ref, v1_ref))):
        k3 = k_ref[...]
        ksum = ksum + jnp.sum(k3, axis=0)
        k2 = k3.reshape(rows, hd)
        v2 = v_ref[...].reshape(rows, hd)
        bias = jnp.where(is_last, blast_ref[half], far_ref[...])
        s = _bdot(k2, qt) * scale + bias
        m_h, l_h, o_h = _page_stats(s, match, v2)
        m_scr[half, pl.ds(n, 1), :] = m_h
        l_scr[half, pl.ds(n, 1), :] = l_h
        o_scr[half, n] = o_h
    km_scr[n] = ksum * (1.0 / (2 * page))

    @pl.when(is_last)
    def _():
        km2 = km_scr[...].reshape(n_blk * nh, hd)
        g_all = _dot_hi(km2, q2_ref[0])
        gmatch, _, _ = head_match(n_blk * nh)
        gate = jnp.sum(jnp.where(gmatch, g_all, 0.0).reshape(n_blk, nh, ncol), axis=1)
        rown = lax.broadcasted_iota(I32, (n_blk, ncol), 0)
        rank = jnp.zeros((n_blk, ncol), I32)
        for mm in range(n_blk):
            gm = gate[mm:mm + 1, :]
            beats = (gm > gate) | ((gm == gate) & (mm < rown))
            rank = rank + jnp.where(beats, 1, 0)
        sel = rank < MOBA_TOPK

        own_match, r, c = head_match(t_len * nh)
        own_valid = own_match & ((c % t_len) >= (r // nh))
        s_own = _bdot(kn_ref[0], qt) * scale + bown_ref[...]
        m_o, l_o, o_o = _page_stats(s_own, own_valid, vn_ref[0])

        m0 = jnp.where(sel, m_scr[0], NEG_INF)
        m1 = jnp.where(sel, m_scr[1], NEG_INF)
        m_fin = jnp.maximum(jnp.max(jnp.maximum(m0, m1), axis=0, keepdims=True), m_o)
        w0 = jnp.where(sel, jnp.exp(m0 - m_fin), 0.0)
        w1 = jnp.where(sel, jnp.exp(m1 - m_fin), 0.0)
        w_o = jnp.exp(m_o - m_fin)
        l_fin = jnp.sum(w0 * l_scr[0] + w1 * l_scr[1], axis=0, keepdims=True) + w_o * l_o
        acc = o_o * w_o
        for nn in range(n_blk):
            acc = acc + w0[nn:nn + 1, :] * o_scr[0, nn] + w1[nn:nn + 1, :] * o_scr[1, nn]
        o_ref[0] = acc / l_fin


def moba_sample(page_table, cache_k, cache_v, qt, q2, kn2, vn2, blast, bown, far_row, *, layer_e, t_len):
    n_b, n_pages = page_table.shape
    page = cache_k.shape[1]
    ppb = MOBA_BLOCK // page
    assert ppb == 2
    n_blk = n_pages // ppb
    nh, hd = MOBA_HEADS, MOBA_HEAD_DIM
    ncol = nh * t_len

    def pspec(half):
        return pl.BlockSpec((None, page, None, nh, hd),
                            lambda b, n, pt: (pt[b, n * ppb + half], 0, layer_e, 0, 0))

    grid_spec = pltpu.PrefetchScalarGridSpec(
        num_scalar_prefetch=1,
        grid=(n_b, n_blk),
        in_specs=[pspec(0), pspec(1), pspec(0), pspec(1),
                  pl.BlockSpec((1, hd, ncol), lambda b, n, pt: (b, 0, 0)),
                  pl.BlockSpec((1, hd, ncol), lambda b, n, pt: (b, 0, 0)),
                  pl.BlockSpec((1, t_len * nh, hd), lambda b, n, pt: (b, 0, 0)),
                  pl.BlockSpec((1, t_len * nh, hd), lambda b, n, pt: (b, 0, 0)),
                  pl.BlockSpec(blast.shape, lambda b, n, pt: (0, 0, 0)),
                  pl.BlockSpec(bown.shape, lambda b, n, pt: (0, 0)),
                  pl.BlockSpec(far_row.shape, lambda b, n, pt: (0, 0))],
        out_specs=pl.BlockSpec((1, hd, ncol), lambda b, n, pt: (b, 0, 0)),
        scratch_shapes=[pltpu.VMEM((2, n_blk, ncol), F32), pltpu.VMEM((2, n_blk, ncol), F32),
                        pltpu.VMEM((2, n_blk, hd, ncol), F32), pltpu.VMEM((n_blk, nh, hd), F32)],
    )
    return pl.pallas_call(
        functools.partial(_moba_sample_kernel, n_blk=n_blk, page=page, t_len=t_len),
        out_shape=jax.ShapeDtypeStruct((n_b, hd, ncol), F32),
        grid_spec=grid_spec,
        compiler_params=_cparams(("parallel", "arbitrary")),
        name="moba_sample",
    )(page_table, cache_k, cache_k, cache_v, cache_v, qt, q2, kn2, vn2, blast, bown, far_row)


def _retention_kernel(q_ref, k_ref, v_ref, g_ref, cos_ref, sin_ref, dmask_ref, xi_ref, zeta_ref, gc_ref,
                      gng_ref, gnb_ref, *rest, nb, chunk, has_s0):
    if has_s0:
        s0_ref, o_ref, s_out_ref, s_scr = rest
    else:
        o_ref, s_out_ref, s_scr = rest
    c = pl.program_id(1)
    dk, dv = RET_DK, RET_DV
    half = dk // 2

    @pl.when(c == 0)
    def _():
        if has_s0:
            s_scr[...] = s0_ref[...]
        else:
            s_scr[...] = jnp.zeros(s_scr.shape, F32)

    cos = cos_ref[...]
    sin = sin_ref[...]

    def rot(x):
        x1 = x[:, :half]
        x2 = x[:, half:]
        return jnp.concatenate([x1 * cos - x2 * sin, x2 * cos + x1 * sin], axis=1)

    out_rows = []
    for bb in range(nb):
        rows = pl.ds(bb * chunk, chunk)
        out_heads = []
        for h in range(RET_HEADS):
            qh = rot(q_ref[rows, h * dk:(h + 1) * dk])
            kh = rot(k_ref[rows, h * dk:(h + 1) * dk]) * (dk ** -0.5)
            vh = v_ref[rows, h * dv:(h + 1) * dv]
            s = s_scr[bb, h]
            att = _bdot_nt(qh, kh) * dmask_ref[h]
            o = _bdot(att, vh) + _bdot(qh, s) * xi_ref[h]
            s_scr[bb, h] = s * gc_ref[h, 0:1, 0:1] + _bdot_tn(kh * zeta_ref[h], vh)
            mu = jnp.mean(o, axis=-1, keepdims=True)
            oc = o - mu
            var = jnp.mean(oc * oc, axis=-1, keepdims=True)
            on = oc * lax.rsqrt(var + LN_EPS) * gng_ref[:, h * dv:(h + 1) * dv] + gnb_ref[:, h * dv:(h + 1) * dv]
            gate = _silu(g_ref[rows, h * dv:(h + 1) * dv])
            out_heads.append(gate * on)
        out_rows.append(jnp.concatenate(out_heads, axis=1))
    o_ref[...] = jnp.concatenate(out_rows, axis=0).astype(o_ref.dtype)

    @pl.when(c == pl.num_programs(1) - 1)
    def _():
        s_out_ref[...] = s_scr[...]


def retention(p, cos, sin, consts, gn_g, gn_b, s0, *, row0, n_batch, t_len, nb, out_dtype):
    chunk = math.gcd(t_len, RET_CHUNK)
    n_chunk = t_len // chunk
    dmask, xi, zeta, gc = consts
    blk_rows = nb * chunk
    if nb > 1:
        assert n_chunk == 1
    assert row0 % blk_rows == 0
    base = row0 // blk_rows
    qk_w = RET_HEADS * RET_DK
    v_w = RET_HEADS * RET_DV
    has_s0 = s0 is not None

    def rmap(col):
        return lambda b, c: (base + b * n_chunk + c, col)

    in_specs = [pl.BlockSpec((blk_rows, qk_w), rmap(0)),
                pl.BlockSpec((blk_rows, qk_w), rmap(1)),
                pl.BlockSpec((blk_rows, v_w), rmap(1)),
                pl.BlockSpec((blk_rows, v_w), rmap(2)),
                pl.BlockSpec((chunk, RET_DK // 2), lambda b, c: (c, 0)),
                pl.BlockSpec((chunk, RET_DK // 2), lambda b, c: (c, 0)),
                pl.BlockSpec(dmask.shape, lambda b, c: (0, 0, 0)),
                pl.BlockSpec(xi.shape, lambda b, c: (0, 0, 0)),
                pl.BlockSpec(zeta.shape, lambda b, c: (0, 0, 0)),
                pl.BlockSpec(gc.shape, lambda b, c: (0, 0, 0)),
                pl.BlockSpec((1, v_w), lambda b, c: (0, 0)),
                pl.BlockSpec((1, v_w), lambda b, c: (0, 0))]
    args = [p, p, p, p, cos, sin, dmask, xi, zeta, gc, gn_g.reshape(1, -1), gn_b.reshape(1, -1)]
    s_spec = pl.BlockSpec((nb, RET_HEADS, RET_DK, RET_DV), lambda b, c: (b, 0, 0, 0))
    if has_s0:
        in_specs.append(s_spec)
        args.append(s0)
    o, s_out = pl.pallas_call(
        functools.partial(_retention_kernel, nb=nb, chunk=chunk, has_s0=has_s0),
        out_shape=(jax.ShapeDtypeStruct((n_batch * t_len, v_w), out_dtype),
                   jax.ShapeDtypeStruct((n_batch, RET_HEADS, RET_DK, RET_DV), F32)),
        grid=(n_batch // nb, n_chunk),
        in_specs=in_specs,
        out_specs=(pl.BlockSpec((blk_rows, v_w), lambda b, c: (b * n_chunk + c, 0)), s_spec),
        scratch_shapes=[pltpu.VMEM((nb, RET_HEADS, RET_DK, RET_DV), F32)],
        compiler_params=_cparams(("parallel", "arbitrary")),
        name="retention",
    )(*args)
    return o, s_out


def _retention_consts(chunk):
    h = jnp.arange(RET_HEADS, dtype=F32)
    lg = jnp.log1p(-jnp.exp2(-5.0 - h))
    idx = jnp.arange(chunk, dtype=F32)
    diff = idx[:, None] - idx[None, :]
    dmask = jnp.where(diff >= 0, jnp.exp(lg[:, None, None] * jnp.maximum(diff, 0.0)), 0.0)
    xi = jnp.exp(lg[:, None] * (idx[None, :] + 1.0))
    zeta = jnp.exp(lg[:, None] * (chunk - 1.0 - idx[None, :]))
    g_c = jnp.exp(lg * chunk)
    return (dmask,
            jnp.broadcast_to(xi[:, :, None], (RET_HEADS, chunk, RET_DV)),
            jnp.broadcast_to(zeta[:, :, None], (RET_HEADS, chunk, RET_DK)),
            jnp.broadcast_to(g_c[:, None, None], (RET_HEADS, 8, LANES)))


def _rotary_tables(pos):
    half = RET_DK // 2
    inv = 10000.0 ** (-jnp.linspace(0.0, 1.0, half, dtype=F32))
    ang = pos.astype(F32)[:, None] * inv[None, :]
    return jnp.cos(ang), jnp.sin(ang)


def _mem_attn_kernel(q_ref, mk_ref, mv_ref, o_ref, *, nb, tq):
    hd = MEM_HEAD_DIM
    scale = hd ** -0.5
    qf = q_ref[...].astype(F32)
    out_rows = []
    for bb in range(nb):
        out_heads = []
        for h in range(MEM_HEADS):
            cols = slice(h * hd, (h + 1) * hd)
            q = qf[bb * tq:(bb + 1) * tq, cols]
            s = _bdot_nt(q, mk_ref[bb, :, cols]) * scale
            m = jnp.max(s, axis=-1, keepdims=True)
            p = jnp.exp(s - m)
            p = p / jnp.sum(p, axis=-1, keepdims=True)
            out_heads.append(_bdot(p, mv_ref[bb, :, cols]))
        out_rows.append(jnp.concatenate(out_heads, axis=1))
    o_ref[...] = jnp.concatenate(out_rows, axis=0).astype(o_ref.dtype)


def mem_attention(q, mk, mv, *, row0, n_batch, t_len, nb, tq, out_dtype):
    n_t = t_len // tq
    blk_rows = nb * tq
    if nb > 1:
        assert n_t == 1
    assert row0 % blk_rows == 0
    base = row0 // blk_rows
    m_tok = mk.shape[1]
    return pl.pallas_call(
        functools.partial(_mem_attn_kernel, nb=nb, tq=tq),
        out_shape=jax.ShapeDtypeStruct((n_batch * t_len, D_MODEL), out_dtype),
        grid=(n_batch // nb, n_t),
        in_specs=[pl.BlockSpec((blk_rows, D_MODEL), lambda b, t: (base + b * n_t + t, 0)),
                  pl.BlockSpec((nb, m_tok, D_MODEL), lambda b, t: (b, 0, 0)),
                  pl.BlockSpec((nb, m_tok, D_MODEL), lambda b, t: (b, 0, 0))],
        out_specs=pl.BlockSpec((blk_rows, D_MODEL), lambda b, t: (b * n_t + t, 0)),
        compiler_params=_cparams(("parallel", "arbitrary")),
        name="mem_attention",
    )(q, mk, mv)


def _router_kernel(x_ref, w_ref, b_ref, wt_ref, ei_ref):
    x = x_ref[...]
    logits = _dot_hi(x, w_ref[...]) + b_ref[...]
    tm = x.shape[0]
    lane = lax.broadcasted_iota(I32, (tm, LANES), 1)
    big = jnp.int32(LANES)

    def masked_softmax_top(mask):
        lm = jnp.where(mask, logits, NEG_INF)
        mx = jnp.max(lm, axis=-1, keepdims=True)
        e = jnp.exp(lm - mx)
        prob = e / jnp.sum(e, axis=-1, keepdims=True)
        return prob

    g_prob = masked_softmax_top(lane < MOE_GROUPS)
    gp = jnp.max(g_prob, axis=-1, keepdims=True)
    gi = jnp.min(jnp.where(g_prob == gp, lane, big), axis=-1, keepdims=True)
    lo = MOE_GROUPS + gi * MOE_EPG
    in_group = (lane >= lo) & (lane < lo + MOE_EPG)
    e_prob = masked_softmax_top(in_group)
    e_prob = jnp.where(in_group, e_prob, -1.0)
    p1 = jnp.max(e_prob, axis=-1, keepdims=True)
    i1 = jnp.min(jnp.where(e_prob == p1, lane, big), axis=-1, keepdims=True)
    rest = jnp.where(lane == i1, -1.0, e_prob)
    p2 = jnp.max(rest, axis=-1, keepdims=True)
    i2 = jnp.min(jnp.where(rest == p2, lane, big), axis=-1, keepdims=True)
    tot = p1 + p2
    w1 = gp * (p1 / tot)
    w2 = gp * (p2 / tot)
    wt_ref[...] = jnp.where(lane == 0, w1, jnp.where(lane == 1, w2, 0.0))
    ei_ref[...] = jnp.where(lane == 0, i1 - MOE_GROUPS, jnp.where(lane == 1, i2 - MOE_GROUPS, 0))


def moe_router(x, w_all, b_all, *, tm):
    m, d = x.shape
    return pl.pallas_call(
        _router_kernel,
        out_shape=(jax.ShapeDtypeStruct((m, LANES), F32), jax.ShapeDtypeStruct((m, LANES), I32)),
        grid=(m // tm,),
        in_specs=[pl.BlockSpec((tm, d), lambda i: (i, 0)),
                  pl.BlockSpec((d, LANES), lambda i: (0, 0)),
                  pl.BlockSpec((1, LANES), lambda i: (0, 0))],
        out_specs=(pl.BlockSpec((tm, LANES), lambda i: (i, 0)), pl.BlockSpec((tm, LANES), lambda i: (i, 0))),
        compiler_params=_cparams(("parallel",)),
        name="moe_router",
    )(x, w_all, b_all)


def _row_copy(src_hbm, row, dst, r, sem):
    return pltpu.make_async_copy(src_hbm.at[pl.ds(row, 1), :], dst.at[pl.ds(r, 1), :], sem)


def _issue_rows(idx_ref, src_hbm, dst, sem, n_rows):
    def body(r, carry):
        _row_copy(src_hbm, idx_ref[r], dst, r, sem).start()
        return carry
    lax.fori_loop(0, n_rows, body, 0, unroll=8)


def _wait_rows(src_hbm, dst, sem, n_rows):
    pltpu.make_async_copy(src_hbm.at[pl.ds(0, n_rows), :], dst, sem).wait()


def _moe_ffn_kernel(te_ref, nt_ref, src_cur, src_nxt, x_hbm, w1_ref, w3_ref, w2_ref, y_ref, xbuf, sem):
    j = pl.program_id(0)
    nt = nt_ref[0]
    slot = j % 2

    @pl.when(j == 0)
    def _():
        _issue_rows(src_cur, x_hbm, xbuf.at[0], sem.at[0], MOE_TILE)

    @pl.when(j + 1 < nt)
    def _():
        _issue_rows(src_nxt, x_hbm, xbuf.at[1 - slot], sem.at[1 - slot], MOE_TILE)

    @pl.when(j < nt)
    def _():
        _wait_rows(x_hbm, xbuf.at[slot], sem.at[slot], MOE_TILE)
        x = xbuf[slot].astype(BF16)
        h = _silu(_bdot(x, w1_ref[0])) * _bdot(x, w3_ref[0])
        y_ref[...] = _bdot(h, w2_ref[0])

    @pl.when(j >= jnp.maximum(nt, 1))
    def _():
        y_ref[...] = jnp.zeros(y_ref.shape, F32)


def moe_ffn(x, tile_expert, n_tiles, src, w1, w3, w2):
    n_t = src.shape[0]
    d = x.shape[1]
    dff = w1.shape[2]
    src3 = src.reshape(n_t, 1, MOE_TILE)
    grid_spec = pltpu.PrefetchScalarGridSpec(
        num_scalar_prefetch=2,
        grid=(n_t,),
        in_specs=[pl.BlockSpec((None, None, MOE_TILE), lambda j, te, nt: (j, 0, 0), memory_space=pltpu.SMEM),
                  pl.BlockSpec((None, None, MOE_TILE), lambda j, te, nt: (jnp.minimum(j + 1, n_t - 1), 0, 0),
                               memory_space=pltpu.SMEM),
                  pl.BlockSpec(memory_space=pl.ANY),
                  pl.BlockSpec((1, d, dff), lambda j, te, nt: (te[j], 0, 0)),
                  pl.BlockSpec((1, d, dff), lambda j, te, nt: (te[j], 0, 0)),
                  pl.BlockSpec((1, dff, d), lambda j, te, nt: (te[j], 0, 0))],
        out_specs=pl.BlockSpec((MOE_TILE, d), lambda j, te, nt: (j, 0)),
        scratch_shapes=[pltpu.VMEM((2, MOE_TILE, d), F32), pltpu.SemaphoreType.DMA((2,))],
    )
    return pl.pallas_call(
        _moe_ffn_kernel,
        out_shape=jax.ShapeDtypeStruct((n_t * MOE_TILE, d), F32),
        grid_spec=grid_spec,
        compiler_params=_cparams(("arbitrary",)),
        name="moe_ffn",
    )(tile_expert, n_tiles, src3, src3, x, w1, w3, w2)


def _moe_combine_kernel(d_cur, d_nxt, y_hbm, x_ref, wt_ref, g_ref, b_ref, o_ref, ybuf, sem, *, tm):
    i = pl.program_id(0)
    n = pl.num_programs(0)
    slot = i % 2

    @pl.when(i == 0)
    def _():
        _issue_rows(d_cur, y_hbm, ybuf.at[0], sem.at[0], 2 * tm)

    @pl.when(i + 1 < n)
    def _():
        _issue_rows(d_nxt, y_hbm, ybuf.at[1 - slot], sem.at[1 - slot], 2 * tm)

    _wait_rows(y_hbm, ybuf.at[slot], sem.at[slot], 2 * tm)
    wt = wt_ref[...]
    moe = wt[:, 0:1] * ybuf[slot, 0:tm, :] + wt[:, 1:2] * ybuf[slot, tm:2 * tm, :]
    y = DEEPNORM_ALPHA * x_ref[...] + moe
    o_ref[...] = _layer_norm(y, g_ref[...], b_ref[...])


def moe_combine_deepnorm(y_sorted, dest, x, wts, g, b, *, tm):
    m, d = x.shape
    n_t = m // tm
    dest3 = dest.reshape(n_t, 1, 2 * tm)
    return pl.pallas_call(
        functools.partial(_moe_combine_kernel, tm=tm),
        out_shape=jax.ShapeDtypeStruct((m, d), F32),
        grid=(n_t,),
        in_specs=[pl.BlockSpec((None, None, 2 * tm), lambda i: (i, 0, 0), memory_space=pltpu.SMEM),
                  pl.BlockSpec((None, None, 2 * tm), lambda i: (jnp.minimum(i + 1, n_t - 1), 0, 0),
                               memory_space=pltpu.SMEM),
                  pl.BlockSpec(memory_space=pl.ANY),
                  pl.BlockSpec((tm, d), lambda i: (i, 0)),
                  pl.BlockSpec((tm, LANES), lambda i: (i, 0)),
                  pl.BlockSpec((1, d), lambda i: (0, 0)),
                  pl.BlockSpec((1, d), lambda i: (0, 0))],
        out_specs=pl.BlockSpec((tm, d), lambda i: (i, 0)),
        scratch_shapes=[pltpu.VMEM((2, 2 * tm, d), F32), pltpu.SemaphoreType.DMA((2,))],
        compiler_params=_cparams(("arbitrary",)),
        name="moe_combine_deepnorm",
    )(dest3, dest3, y_sorted, x, wts, g.reshape(1, d), b.reshape(1, d))


def _moe_plan(eidx, n_tok, tm):
    n_pair = n_tok * MOE_TOPK
    n_tiles_max = n_pair // MOE_TILE + MOE_EXPERTS
    eflat = eidx.reshape(-1)
    order = jnp.argsort(eflat, stable=True).astype(I32)
    e_sorted = eflat[order]
    counts = jnp.zeros((MOE_EXPERTS,), I32).at[eflat].add(1)
    tiles_per = (counts + MOE_TILE - 1) // MOE_TILE
    tile_end = jnp.cumsum(tiles_per)
    n_tiles = tile_end[-1]
    pad_off = (tile_end - tiles_per) * MOE_TILE
    grp_start = jnp.cumsum(counts) - counts
    slot_sorted = pad_off[e_sorted] + (jnp.arange(n_pair, dtype=I32) - grp_start[e_sorted])
    src = jnp.zeros((n_tiles_max * MOE_TILE,), I32).at[slot_sorted].set(order // MOE_TOPK)
    dest = jnp.zeros((n_pair,), I32).at[order].set(slot_sorted)
    tile_expert = jnp.minimum(
        jnp.searchsorted(tile_end, jnp.arange(n_tiles_max, dtype=I32), side="right"), MOE_EXPERTS - 1).astype(I32)
    dest = dest.reshape(n_tok // tm, tm, MOE_TOPK).transpose(0, 2, 1).reshape(n_tok // tm, MOE_TOPK * tm)
    return tile_expert, n_tiles.reshape(1).astype(I32), src.reshape(n_tiles_max, MOE_TILE), dest


def hier_moe_deepnorm(x, w_all, b_all, w1, w3, w2, g, b):
    n_tok = x.shape[0]
    wts, eidx = moe_router(x, w_all, b_all, tm=640)
    tile_expert, n_tiles, src, dest = _moe_plan(eidx[:, :MOE_TOPK], n_tok, MOE_TILE)
    y_sorted = moe_ffn(x, tile_expert, n_tiles, src, w1, w3, w2)
    return moe_combine_deepnorm(y_sorted, dest, x, wts, g, b, tm=MOE_TILE)


def _rel_bucket(n):
    n = jnp.maximum(n, 0)
    max_exact = NUM_BUCKETS // 2
    large = max_exact + (jnp.log(jnp.maximum(n, 1).astype(F32) / max_exact)
                         / math.log(MAX_DISTANCE / max_exact) * (NUM_BUCKETS - max_exact)).astype(I32)
    return jnp.where(n < max_exact, n, jnp.minimum(large, NUM_BUCKETS - 1))


def _bias_by_distance(rel_bias, n_dist):
    return rel_bias.T[:, _rel_bucket(jnp.arange(n_dist, dtype=I32))]


def kernel(x_prompt, x_sample, cache_moba_k, cache_moba_v, state_conv, state_ret, cache_mem_k, cache_mem_v,
           page_table, mem_prompt, rel_bias, ev_w_in, ev_conv_w, ev_conv_b, ev_conv_ln_g, ev_conv_ln_b,
           ev_w_out, od_w_in, od_gn_g, od_gn_b, od_w_out, mem_wq, mem_wk, mem_wv, mem_wo, ln_g, ln_b,
           moe_w_group, moe_b_group, moe_w_router, moe_b_router, moe_w1, moe_w3, moe_w2):
    bp, tp, d = x_prompt.shape
    bs, ts, _ = x_sample.shape
    n_p = bp * tp
    n_s = bs * ts
    n_all = n_p + n_s
    page = cache_moba_k.shape[1]
    past_len = page_table.shape[1] * page
    nh, hd = MOBA_HEADS, MOBA_HEAD_DIM
    assert tp % MOBA_BLOCK == 0 and past_len % MOBA_BLOCK == 0 and ts <= MOBA_BLOCK
    assert MOBA_BLOCK >= MAX_DISTANCE
    tm = 640
    assert n_all % tm == 0

    x = jnp.concatenate([x_prompt.reshape(n_p, d), x_sample.reshape(n_s, d)], axis=0)

    bias_d = _bias_by_distance(rel_bias, 2 * MOBA_BLOCK + ts)
    far = jnp.broadcast_to(bias_d[:, 2 * MOBA_BLOCK - 1:2 * MOBA_BLOCK], (nh, LANES))
    tt = np.arange(MOBA_BLOCK)
    dist0 = np.maximum(tt[:, None] - tt[None, :], 0)
    d0 = bias_d[:, dist0]
    d1 = bias_d[:, dist0 * 0 + (MOBA_BLOCK + tt[:, None] - tt[None, :])]
    r = np.arange(MOBA_BLOCK * nh)
    c = np.arange(nh * ts)
    key_s, row_h = r // nh, r % nh
    col_t = c % ts
    dist_last = (MOBA_BLOCK + col_t[None, :] - key_s[:, None])
    blast = bias_d[row_h[:, None], dist_last].reshape(2, page * nh, nh * ts)
    ro = np.arange(ts * nh)
    dist_own = np.maximum(col_t[None, :] - (ro // nh)[:, None], 0)
    bown = bias_d[(ro % nh)[:, None], dist_own]
    far_row = jnp.repeat(bias_d[:, 2 * MOBA_BLOCK - 1], ts)[None, :]

    pos_p = jnp.arange(tp, dtype=I32)
    pos_s = past_len + jnp.arange(ts, dtype=I32)
    cos_p, sin_p = _rotary_tables(pos_p)
    cos_s, sin_s = _rotary_tables(pos_s)
    ret_c_p = _retention_consts(math.gcd(tp, RET_CHUNK))
    ret_c_s = _retention_consts(math.gcd(ts, RET_CHUNK))

    outs = {}
    for layer in range(DEPTH):
        if layer % 2 == 0:
            e = layer // 2
            proj = matmul(x, ev_w_in[e].astype(BF16), tm=tm, tn=ev_w_in.shape[2])
            k_new = proj[:, 2 * CONV_CH + MOBA_WIDTH:2 * CONV_CH + 2 * MOBA_WIDTH]
            v_new = proj[:, 2 * CONV_CH + 2 * MOBA_WIDTH:]
            q_s = proj[n_p:, 2 * CONV_CH:2 * CONV_CH + MOBA_WIDTH]
            conv_args = (ev_conv_w[e], ev_conv_b[e], ev_conv_ln_g[e], ev_conv_ln_b[e])
            c_p, cst_p = conv_module(proj, jnp.zeros((bp, CONV_WIDTH - 1, CONV_CH), F32), *conv_args,
                                     row0=0, n_batch=bp, t_len=tp, nb=1, out_dtype=BF16)
            c_s, cst_s = conv_module(proj, state_conv[e], *conv_args,
                                     row0=n_p, n_batch=bs, t_len=ts, nb=bs, out_dtype=F32)
            a_p = moba_prompt(proj, d0, d1, far, n_batch=bp, t_len=tp, out_dtype=BF16)
            q4 = q_s.reshape(bs, ts, nh, hd)
            qt = q4.transpose(0, 3, 2, 1).reshape(bs, hd, nh * ts)
            kn2 = k_new[n_p:].reshape(bs, ts * nh, hd)
            vn2 = v_new[n_p:].reshape(bs, ts * nh, hd)
            a_s = moba_sample(page_table, cache_moba_k, cache_moba_v, qt.astype(BF16), qt, kn2, vn2,
                              blast, bown, far_row, layer_e=e, t_len=ts)
            a_s = a_s.reshape(bs, hd, nh, ts).transpose(0, 3, 2, 1).reshape(n_s, MOBA_WIDTH).astype(BF16)
            c_all = jnp.concatenate([c_p, c_s.astype(BF16)], axis=0)
            a_all = jnp.concatenate([a_p, a_s], axis=0)
            w_out = ev_w_out[e].astype(BF16)
            x = matmul_deepnorm([c_all, a_all], [w_out[:CONV_CH], w_out[CONV_CH:]], x,
                                ln_g[layer, 0], ln_b[layer, 0], tm=tm)
            outs.setdefault("kp", []).append(k_new[:n_p].reshape(bp, tp, nh, hd))
            outs.setdefault("vp", []).append(v_new[:n_p].reshape(bp, tp, nh, hd))
            outs.setdefault("ks", []).append(k_new[n_p:].reshape(bs, ts, nh, hd))
            outs.setdefault("vs", []).append(v_new[n_p:].reshape(bs, ts, nh, hd))
            outs.setdefault("cp", []).append(cst_p)
            outs.setdefault("cs", []).append(cst_s)
        else:
            o = layer // 2
            proj = matmul(x, od_w_in[o].astype(BF16), tm=tm, tn=2048)
            r_p, st_p = retention(proj, cos_p, sin_p, ret_c_p, od_gn_g[o], od_gn_b[o], None,
                                  row0=0, n_batch=bp, t_len=tp, nb=1, out_dtype=BF16)
            r_s, st_s = retention(proj, cos_s, sin_s, ret_c_s, od_gn_g[o], od_gn_b[o], state_ret[o],
                                  row0=n_p, n_batch=bs, t_len=ts, nb=2, out_dtype=BF16)
            r_all = jnp.concatenate([r_p, r_s], axis=0)
            x = matmul_deepnorm([r_all], [od_w_out[o].astype(BF16)], x, ln_g[layer, 0], ln_b[layer, 0], tm=tm)
            outs.setdefault("sp", []).append(st_p)
            outs.setdefault("ss", []).append(st_s)

        m_tok = mem_prompt.shape[1]
        mem2 = mem_prompt.reshape(bp * m_tok, d)
        mk_p = matmul(mem2, mem_wk[layer].astype(BF16), tm=512, tn=d)
        mv_p = matmul(mem2, mem_wv[layer].astype(BF16), tm=512, tn=d)
        outs.setdefault("mk", []).append(mk_p.reshape(bp, m_tok, MEM_HEADS, MEM_HEAD_DIM))
        outs.setdefault("mv", []).append(mv_p.reshape(bp, m_tok, MEM_HEADS, MEM_HEAD_DIM))
        q = matmul(x, mem_wq[layer].astype(BF16), tm=tm, tn=d, out_dtype=BF16)
        o_p = mem_attention(q, mk_p.reshape(bp, m_tok, d), mv_p.reshape(bp, m_tok, d),
                            row0=0, n_batch=bp, t_len=tp, nb=1, tq=512, out_dtype=BF16)
        o_s = mem_attention(q, cache_mem_k[layer].reshape(bs, m_tok, d), cache_mem_v[layer].reshape(bs, m_tok, d),
                            row0=n_p, n_batch=bs, t_len=ts, nb=2, tq=ts, out_dtype=BF16)
        o_all = jnp.concatenate([o_p, o_s], axis=0)
        x = matmul_deepnorm([o_all], [mem_wo[layer].astype(BF16)], x, ln_g[layer, 1], ln_b[layer, 1], tm=tm)

        w_all = jnp.zeros((d, LANES), F32)
        w_all = w_all.at[:, :MOE_GROUPS].set(moe_w_group[layer])
        w_all = w_all.at[:, MOE_GROUPS:MOE_GROUPS + MOE_EXPERTS].set(moe_w_router[layer])
        b_all = jnp.zeros((1, LANES), F32)
        b_all = b_all.at[0, :MOE_GROUPS].set(moe_b_group[layer])
        b_all = b_all.at[0, MOE_GROUPS:MOE_GROUPS + MOE_EXPERTS].set(moe_b_router[layer])
        x = hier_moe_deepnorm(x, w_all, b_all, moe_w1[layer], moe_w3[layer], moe_w2[layer],
                              ln_g[layer, 2], ln_b[layer, 2])

    y_prompt = x[:n_p].reshape(bp, tp, d)
    y_sample = x[n_p:].reshape(bs, ts, d)
    return (y_prompt, y_sample,
            jnp.stack(outs["kp"], axis=2), jnp.stack(outs["vp"], axis=2),
            jnp.stack(outs["ks"], axis=2), jnp.stack(outs["vs"], axis=2),
            jnp.stack(outs["cp"], axis=0), jnp.stack(outs["cs"], axis=0),
            jnp.stack(outs["sp"], axis=0), jnp.stack(outs["ss"], axis=0),
            jnp.stack(outs["mk"], axis=0), jnp.stack(outs["mv"], axis=0))
```

```python
import functools
import math

import numpy as np
import jax
import jax.numpy as jnp
from jax import lax
from jax.experimental import pallas as pl
from jax.experimental.pallas import tpu as pltpu

F32 = jnp.float32
BF16 = jnp.bfloat16
I32 = jnp.int32

D_MODEL = 1024
DEPTH = 2
CONV_CH = 512
CONV_WIDTH = 31
MOBA_HEADS = 8
MOBA_HEAD_DIM = 64
MOBA_WIDTH = 512
MOBA_BLOCK = 256
MOBA_TOPK = 3
NUM_BUCKETS = 32
MAX_DISTANCE = 128
RET_HEADS = 4
RET_DK = 256
RET_DV = 512
RET_CHUNK = 128
MEM_HEADS = 4
MEM_HEAD_DIM = 256
MOE_GROUPS = 4
MOE_EPG = 8
MOE_EXPERTS = 32
MOE_TOPK = 2
MOE_D_FF = 512
DEEPNORM_ALPHA = (2 * DEPTH) ** 0.25
LN_EPS = 1e-5

LANES = 128
VMEM_LIMIT = 56 * 1024 * 1024
MOE_TILE = 256
NEG_INF = float("-inf")


def _cparams(sem):
    return pltpu.CompilerParams(dimension_semantics=sem, vmem_limit_bytes=VMEM_LIMIT)


def _layer_norm(y, g, b):
    mu = jnp.mean(y, axis=-1, keepdims=True)
    yc = y - mu
    var = jnp.mean(yc * yc, axis=-1, keepdims=True)
    return yc * lax.rsqrt(var + LN_EPS) * g + b


def _silu(x):
    return x * (1.0 / (1.0 + jnp.exp(-x)))


def _bdot(a, b):
    return jnp.dot(a.astype(BF16), b.astype(BF16), preferred_element_type=F32)


def _bdot_nt(a, b):
    return lax.dot_general(a.astype(BF16), b.astype(BF16), (((1,), (1,)), ((), ())),
                           preferred_element_type=F32)


def _bdot_tn(a, b):
    return lax.dot_general(a.astype(BF16), b.astype(BF16), (((0,), (0,)), ((), ())),
                           preferred_element_type=F32)


def _split3(x):
    hi = x.astype(BF16)
    lo = (x - hi.astype(F32)).astype(BF16)
    return hi, lo


def _dot_hi(a, b, dims=(((1,), (0,)), ((), ()))):
    ah, al = _split3(a)
    bh, bl = _split3(b)
    dg = functools.partial(lax.dot_general, dimension_numbers=dims, preferred_element_type=F32)
    return dg(ah, bh) + (dg(al, bh) + dg(ah, bl))


ROW_TILE = D_MODEL // LANES


def _load_token_tiles(ref, start, n_tok):
    return jnp.concatenate([ref[pl.ds(start + c, n_tok, stride=ROW_TILE), :] for c in range(ROW_TILE)], axis=1)


def _store_token_tiles(ref, val):
    n_tok = val.shape[0]
    for c in range(ROW_TILE):
        ref[pl.ds(c, n_tok, stride=ROW_TILE), :] = val[:, c * LANES:(c + 1) * LANES]


def _tile_copy(src_hbm, src_tok, dst, dst_tok, sem):
    return pltpu.make_async_copy(src_hbm.at[pl.ds(src_tok * ROW_TILE, ROW_TILE), :],
                                 dst.at[pl.ds(dst_tok * ROW_TILE, ROW_TILE), :], sem)


def _wait_tiles(src_hbm, dst, dst_tok, sem, n_tok):
    pltpu.make_async_copy(src_hbm.at[pl.ds(0, n_tok * ROW_TILE), :],
                          dst.at[pl.ds(dst_tok * ROW_TILE, n_tok * ROW_TILE), :], sem).wait()


def _mm_kernel(x_ref, w_ref, o_ref):
    o_ref[...] = _bdot(x_ref[...], w_ref[...]).astype(o_ref.dtype)


def matmul(x, w, *, tm, tn, out_dtype=F32):
    m, k = x.shape
    n = w.shape[1]
    assert m % tm == 0 and n % tn == 0
    return pl.pallas_call(
        _mm_kernel,
        out_shape=jax.ShapeDtypeStruct((m, n), out_dtype),
        grid=(m // tm, n // tn),
        in_specs=[pl.BlockSpec((tm, k), lambda i, j: (i, 0)),
                  pl.BlockSpec((k, tn), lambda i, j: (0, j))],
        out_specs=pl.BlockSpec((tm, tn), lambda i, j: (i, j)),
        compiler_params=_cparams(("parallel", "parallel")),
        name="matmul",
    )(x, w)


def _mm_dn_kernel(*refs, n_lhs, token_tiles):
    lhs = refs[:n_lhs]
    ws = refs[n_lhs:2 * n_lhs]
    res_ref, g_ref, b_ref, o_ref = refs[2 * n_lhs:2 * n_lhs + 4]
    acc = _bdot(lhs[0][...], ws[0][...])
    for a, w in zip(lhs[1:], ws[1:]):
        acc = acc + _bdot(a[...], w[...])
    y = DEEPNORM_ALPHA * res_ref[...] + acc
    out = _layer_norm(y, g_ref[...], b_ref[...])
    o_ref[...] = out
    if token_tiles:
        _store_token_tiles(refs[2 * n_lhs + 4], out)


def matmul_deepnorm(lhs_list, w_list, res, g, b, *, tm, token_tiles=False):
    m, d = res.shape
    n_lhs = len(lhs_list)
    in_specs = ([pl.BlockSpec((tm, a.shape[1]), lambda i: (i, 0)) for a in lhs_list]
                + [pl.BlockSpec(w.shape, lambda i: (0, 0)) for w in w_list]
                + [pl.BlockSpec((tm, d), lambda i: (i, 0)),
                   pl.BlockSpec((1, d), lambda i: (0, 0)),
                   pl.BlockSpec((1, d), lambda i: (0, 0))])
    out_shape = [jax.ShapeDtypeStruct((m, d), F32)]
    out_specs = [pl.BlockSpec((tm, d), lambda i: (i, 0))]
    if token_tiles:
        out_shape.append(jax.ShapeDtypeStruct((m * ROW_TILE, LANES), F32))
        out_specs.append(pl.BlockSpec((tm * ROW_TILE, LANES), lambda i: (i, 0)))
    res_out = pl.pallas_call(
        functools.partial(_mm_dn_kernel, n_lhs=n_lhs, token_tiles=token_tiles),
        out_shape=tuple(out_shape),
        grid=(m // tm,),
        in_specs=in_specs,
        out_specs=tuple(out_specs),
        compiler_params=_cparams(("parallel",)),
        name="matmul_deepnorm",
    )(*lhs_list, *w_list, res, g.reshape(1, d), b.reshape(1, d))
    return res_out if token_tiles else res_out[0]


CONV_PAD = 32


def _conv_kernel(p_ref, hist_ref, w_ref, cb_ref, g_ref, b_ref, c_ref, st_ref, u_scr, *, nb, t_len, rc):
    hw = CONV_WIDTH - 1
    w = w_ref[...]
    for bb in range(nb):
        rows = pl.ds(bb * t_len, t_len)
        a = p_ref[rows, 0:CONV_CH]
        gt = p_ref[rows, CONV_CH:2 * CONV_CH]
        u = a * (1.0 / (1.0 + jnp.exp(-gt)))
        u_scr[0:CONV_PAD, :] = jnp.concatenate(
            [jnp.zeros((CONV_PAD - hw, CONV_CH), F32), hist_ref[bb]], axis=0)
        u_scr[CONV_PAD:CONV_PAD + t_len, :] = u
        st_ref[bb] = u_scr[CONV_PAD + t_len - hw:CONV_PAD + t_len, :]

        def chunk(ci, carry):
            r0 = pl.multiple_of(ci * rc, rc)
            blk = u_scr[pl.ds(r0, rc + CONV_PAD), :]
            acc = jnp.broadcast_to(cb_ref[...], (rc, CONV_CH))
            for j in range(CONV_WIDTH):
                off = j + CONV_PAD - hw
                acc = acc + w[j:j + 1, :] * blk[off:off + rc, :]
            y = _silu(_layer_norm(acc, g_ref[...], b_ref[...]))
            c_ref[pl.ds(bb * t_len + r0, rc), :] = y.astype(c_ref.dtype)
            return carry

        n_chunks = t_len // rc
        if n_chunks == 1:
            chunk(0, 0)
        else:
            lax.fori_loop(0, n_chunks, chunk, 0)


def conv_module(p, hist, conv_w, conv_b, ln_g, ln_b, *, row0, n_batch, t_len, nb, out_dtype):
    rc = min(64, t_len)
    blk_rows = nb * t_len
    assert row0 % blk_rows == 0 and n_batch % nb == 0
    base = row0 // blk_rows
    hw = CONV_WIDTH - 1
    c, st = pl.pallas_call(
        functools.partial(_conv_kernel, nb=nb, t_len=t_len, rc=rc),
        out_shape=(jax.ShapeDtypeStruct((n_batch * t_len, CONV_CH), out_dtype),
                   jax.ShapeDtypeStruct((n_batch, hw, CONV_CH), F32)),
        grid=(n_batch // nb,),
        in_specs=[pl.BlockSpec((blk_rows, 2 * CONV_CH), lambda i: (base + i, 0)),
                  pl.BlockSpec((nb, hw, CONV_CH), lambda i: (i, 0, 0)),
                  pl.BlockSpec((CONV_WIDTH, CONV_CH), lambda i: (0, 0)),
                  pl.BlockSpec((1, CONV_CH), lambda i: (0, 0)),
                  pl.BlockSpec((1, CONV_CH), lambda i: (0, 0)),
                  pl.BlockSpec((1, CONV_CH), lambda i: (0, 0))],
        out_specs=(pl.BlockSpec((blk_rows, CONV_CH), lambda i: (i, 0)),
                   pl.BlockSpec((nb, hw, CONV_CH), lambda i: (i, 0, 0))),
        scratch_shapes=[pltpu.VMEM((t_len + CONV_PAD, CONV_CH), F32)],
        compiler_params=_cparams(("parallel",)),
        name="conv_module",
    )(p, hist, conv_w, conv_b.reshape(1, -1), ln_g.reshape(1, -1), ln_b.reshape(1, -1))
    return c, st


def _bucket_thresholds():
    max_exact = NUM_BUCKETS // 2
    d = np.arange(0, MAX_DISTANCE + 1)
    val = (np.log(np.maximum(d, 1).astype(np.float32) / np.float32(max_exact))
           / np.float32(math.log(MAX_DISTANCE / max_exact)) * np.float32(NUM_BUCKETS - max_exact))
    inner = (d > max_exact) & (d < MAX_DISTANCE)
    assert np.all(np.abs(val[inner] - np.round(val[inner])) > 1e-3)
    bucket = np.where(d < max_exact, d, np.minimum(max_exact + val.astype(np.int32), NUM_BUCKETS - 1))
    return [int(np.argmax(bucket >= k)) for k in range(1, NUM_BUCKETS)]


def _bias_chain(dist, tab_at, thr):
    b = jnp.where(dist >= thr[0], tab_at(1), tab_at(0))
    for k in range(2, NUM_BUCKETS):
        b = jnp.where(dist >= thr[k - 1], tab_at(k), b)
    return b


def _moba_prompt_kernel(tab_ref, q_ref, k_ref, v_ref, o_ref,
                        kmean_scr, kbf_scr, vt_scr, d0_scr, d1_scr, sel_scr, *, n_blk, thr):
    b = pl.program_id(0)
    i = pl.program_id(1)
    blk = MOBA_BLOCK
    hd = MOBA_HEAD_DIM
    scale = hd ** -0.5
    nt_dims = (((1,), (1,)), ((), ()))
    key = lax.broadcasted_iota(I32, (blk, blk), 0)
    qry = lax.broadcasted_iota(I32, (blk, blk), 1)

    @pl.when((b == 0) & (i == 0))
    def _():
        for h in range(MOBA_HEADS):
            tab_at = functools.partial(lambda k, hh: tab_ref[hh, k], hh=h)
            d0_scr[h] = _bias_chain(qry - key, tab_at, thr)
            d1_scr[h] = _bias_chain(blk + qry - key, tab_at, thr)

    @pl.when(i == 0)
    def _():
        for n in range(n_blk):
            kb = k_ref[n * blk:(n + 1) * blk, :]
            kmean_scr[n:n + 1, :] = jnp.sum(kb, axis=0, keepdims=True) * (1.0 / blk)
            kbf_scr[n * blk:(n + 1) * blk, :] = kb.astype(BF16)
            for pair in range(MOBA_HEADS // 2):
                vt_scr[n, pair * LANES:(pair + 1) * LANES, :] = (
                    v_ref[n * blk:(n + 1) * blk, pair * LANES:(pair + 1) * LANES].T.astype(BF16))

    q = q_ref[...]
    lane = lax.broadcasted_iota(I32, (blk, LANES), 1)
    rown = lax.broadcasted_iota(I32, (n_blk, blk), 0)
    rowd = lax.broadcasted_iota(I32, (LANES, blk), 0)
    r0 = pl.multiple_of(i * blk, blk)
    outs = []
    for pair in range(MOBA_HEADS // 2):
        cols = slice(pair * LANES, (pair + 1) * LANES)
        qp = q[:, cols]
        pair_out = []
        for sub in range(2):
            h = 2 * pair + sub
            in_head = (lane >= sub * hd) & (lane < (sub + 1) * hd)
            qm = jnp.where(in_head, qp, 0.0)
            gate = _dot_hi(kmean_scr[:, cols], qm, nt_dims)
            rank = jnp.zeros((n_blk, blk), F32)
            for m in range(n_blk):
                gm = gate[m:m + 1, :]
                tie = jnp.where(rown > m, 1.0, 0.0)
                cnt = jnp.where(gm > gate, 1.0, jnp.where(gm == gate, tie, 0.0))
                rank = rank + jnp.where(m < i, cnt, 0.0)
            sel_scr[...] = jnp.where((rown < i) & (rank < MOBA_TOPK), 1.0, 0.0)

            qb = qm.astype(BF16)
            s = lax.dot_general(kbf_scr[pl.ds(r0, blk), cols], qb, nt_dims,
                                preferred_element_type=F32) * scale + d0_scr[h]
            s = jnp.where(key <= qry, s, NEG_INF)
            m_run = jnp.max(s, axis=0, keepdims=True)
            p = jnp.exp(s - m_run)
            l_run = jnp.sum(p, axis=0, keepdims=True)
            acc = jnp.dot(vt_scr[i, cols, :], p.astype(BF16), preferred_element_type=F32)
            far = tab_ref[h, NUM_BUCKETS - 1]

            def body(n, carry, h=h, cols=cols, qb=qb, far=far):
                m_run, l_run, acc = carry
                rr = pl.multiple_of(n * blk, blk)
                bias = jnp.where(n == i - 1, d1_scr[h], far)
                s = lax.dot_general(kbf_scr[pl.ds(rr, blk), cols], qb, nt_dims,
                                    preferred_element_type=F32) * scale + bias
                s = jnp.where(sel_scr[pl.ds(n, 1), :] > 0.5, s, NEG_INF)
                m_new = jnp.maximum(m_run, jnp.max(s, axis=0, keepdims=True))
                alpha = jnp.exp(m_run - m_new)
                p = jnp.exp(s - m_new)
                l_new = alpha * l_run + jnp.sum(p, axis=0, keepdims=True)
                acc = alpha * acc + jnp.dot(vt_scr[n, cols, :], p.astype(BF16), preferred_element_type=F32)
                return m_new, l_new, acc

            m_run, l_run, acc = lax.fori_loop(0, i, body, (m_run, l_run, acc))
            pair_out.append(acc / l_run)
        both = jnp.where(rowd < hd, pair_out[0], pair_out[1])
        outs.append(both.T)
    o_ref[...] = jnp.concatenate(outs, axis=1).astype(o_ref.dtype)


def moba_prompt(p, tab, *, n_batch, t_len, out_dtype):
    n_blk = t_len // MOBA_BLOCK
    return pl.pallas_call(
        functools.partial(_moba_prompt_kernel, n_blk=n_blk, thr=_bucket_thresholds()),
        out_shape=jax.ShapeDtypeStruct((n_batch * t_len, MOBA_WIDTH), out_dtype),
        grid=(n_batch, n_blk),
        in_specs=[pl.BlockSpec(memory_space=pltpu.SMEM),
                  pl.BlockSpec((MOBA_BLOCK, MOBA_WIDTH), lambda b, i: (b * n_blk + i, 2)),
                  pl.BlockSpec((t_len, MOBA_WIDTH), lambda b, i: (b, 3)),
                  pl.BlockSpec((t_len, MOBA_WIDTH), lambda b, i: (b, 4))],
        out_specs=pl.BlockSpec((MOBA_BLOCK, MOBA_WIDTH), lambda b, i: (b * n_blk + i, 0)),
        scratch_shapes=[pltpu.VMEM((n_blk, MOBA_WIDTH), F32),
                        pltpu.VMEM((t_len, MOBA_WIDTH), BF16),
                        pltpu.VMEM((n_blk, MOBA_WIDTH, MOBA_BLOCK), BF16),
                        pltpu.VMEM((MOBA_HEADS, MOBA_BLOCK, MOBA_BLOCK), F32),
                        pltpu.VMEM((MOBA_HEADS, MOBA_BLOCK, MOBA_BLOCK), F32),
                        pltpu.VMEM((n_blk, MOBA_BLOCK), F32)],
        compiler_params=_cparams(("arbitrary", "arbitrary")),
        name="moba_prompt",
    )(tab, p, p, p)


def _moba_sample_kernel(pt_ref, k0_ref, k1_ref, v0_ref, v1_ref, q_ref, kn_ref, vn_ref, tab_ref, o_ref,
                        g_scr, m_scr, l_scr, o_scr, bl_scr, *, n_blk, page, t_len, thr):
    n = pl.program_id(1)
    nh, hd = MOBA_HEADS, MOBA_HEAD_DIM
    nrow = nh * t_len
    scale = hd ** -0.5
    nt_dims = (((1,), (1,)), ((), ()))
    is_last = n == n_blk - 1
    lane = lax.broadcasted_iota(I32, (nrow, LANES), 1)
    row = lax.broadcasted_iota(I32, (nrow, LANES), 0)
    tab = tab_ref[...]

    def tab_at(k):
        return tab[:, k:k + 1]

    @pl.when(n == 0)
    def _():
        g_scr[...] = jnp.zeros(g_scr.shape, F32)
        m_scr[...] = jnp.zeros(m_scr.shape, F32)
        l_scr[...] = jnp.zeros(l_scr.shape, F32)
        for half in range(2):
            bl_scr[half] = _bias_chain(MOBA_BLOCK + row % t_len - (half * page + lane), tab_at, thr)

    q = q_ref[0]
    qb = [q[h].astype(BF16) for h in range(nh)]
    far = tab_at(NUM_BUCKETS - 1)
    at_n = lane == n
    gsum = jnp.zeros((nrow, 1), F32)
    for half, (k_ref, v_ref) in enumerate(((k0_ref, v0_ref), (k1_ref, v1_ref))):
        kt = k_ref[...].astype(BF16)
        vt = v_ref[...].astype(BF16)
        s_raw = jnp.concatenate([jnp.dot(qb[h], kt[h], preferred_element_type=F32) for h in range(nh)], axis=0)
        gsum = gsum + jnp.sum(s_raw, axis=1, keepdims=True)
        s = s_raw * scale + jnp.where(is_last, bl_scr[half], far)
        m = jnp.max(s, axis=1, keepdims=True)
        p = jnp.exp(s - m)
        l = jnp.sum(p, axis=1, keepdims=True)
        o = jnp.concatenate(
            [lax.dot_general(p[h * t_len:(h + 1) * t_len].astype(BF16), vt[h], nt_dims, preferred_element_type=F32)
             for h in range(nh)], axis=0)
        m_scr[half] = jnp.where(at_n, m, m_scr[half])
        l_scr[half] = jnp.where(at_n, l, l_scr[half])
        o_scr[half, n] = o
    g_scr[...] = jnp.where(at_n, gsum * (1.0 / MOBA_BLOCK), g_scr[...])

    @pl.when(is_last)
    def _():
        gate = g_scr[...]
        rank = jnp.zeros((nrow, LANES), F32)
        for mm in range(n_blk):
            gm = gate[:, mm:mm + 1]
            tie = jnp.where(lane > mm, 1.0, 0.0)
            rank = rank + jnp.where(gm > gate, 1.0, jnp.where(gm == gate, tie, 0.0))
        sel = (rank < MOBA_TOPK) & (lane < n_blk)

        kn = kn_ref[0]
        vn = vn_ref[0]
        tq = lax.broadcasted_iota(I32, (nrow, t_len), 0) % t_len
        sk = lax.broadcasted_iota(I32, (nrow, t_len), 1)
        s_own = jnp.concatenate(
            [lax.dot_general(qb[h], kn[h].astype(BF16), nt_dims, preferred_element_type=F32) for h in range(nh)],
            axis=0) * scale + _bias_chain(tq - sk, tab_at, thr)
        s_own = jnp.where(sk <= tq, s_own, NEG_INF)
        m_o = jnp.max(s_own, axis=1, keepdims=True)
        p_o = jnp.exp(s_own - m_o)
        l_o = jnp.sum(p_o, axis=1, keepdims=True)
        o_o = jnp.concatenate(
            [jnp.dot(p_o[h * t_len:(h + 1) * t_len].astype(BF16), vn[h].astype(BF16), preferred_element_type=F32)
             for h in range(nh)], axis=0)

        m0 = jnp.where(sel, m_scr[0], NEG_INF)
        m1 = jnp.where(sel, m_scr[1], NEG_INF)
        m_fin = jnp.maximum(jnp.max(jnp.maximum(m0, m1), axis=1, keepdims=True), m_o)
        w0 = jnp.where(sel, jnp.exp(m0 - m_fin), 0.0)
        w1 = jnp.where(sel, jnp.exp(m1 - m_fin), 0.0)
        w_o = jnp.exp(m_o - m_fin)
        l_fin = jnp.sum(w0 * l_scr[0] + w1 * l_scr[1], axis=1, keepdims=True) + w_o * l_o
        acc = w_o * o_o
        for nn in range(n_blk):
            acc = acc + w0[:, nn:nn + 1] * o_scr[0, nn] + w1[:, nn:nn + 1] * o_scr[1, nn]
        o_ref[0] = acc / l_fin


def moba_sample(page_table, cache_kt, cache_vt, q, kn, vn, tab, *, layer_e, t_len):
    n_b, n_pages = page_table.shape
    page = cache_kt.shape[-1]
    ppb = MOBA_BLOCK // page
    assert ppb == 2 and page == LANES and n_pages % ppb == 0
    n_blk = n_pages // ppb
    assert n_blk <= LANES
    nh, hd = MOBA_HEADS, MOBA_HEAD_DIM
    nrow = nh * t_len

    def pspec(half):
        return pl.BlockSpec((None, None, nh, hd, page),
                            lambda b, n, pt: (pt[b, n * ppb + half], layer_e, 0, 0, 0))

    def bspec():
        return pl.BlockSpec((1, nh, t_len, hd), lambda b, n, pt: (b, 0, 0, 0))

    grid_spec = pltpu.PrefetchScalarGridSpec(
        num_scalar_prefetch=1,
        grid=(n_b, n_blk),
        in_specs=[pspec(0), pspec(1), pspec(0), pspec(1), bspec(), bspec(), bspec(),
                  pl.BlockSpec(tab.shape, lambda b, n, pt: (0, 0))],
        out_specs=pl.BlockSpec((1, nrow, hd), lambda b, n, pt: (b, 0, 0)),
        scratch_shapes=[pltpu.VMEM((nrow, LANES), F32), pltpu.VMEM((2, nrow, LANES), F32),
                        pltpu.VMEM((2, nrow, LANES), F32), pltpu.VMEM((2, n_blk, nrow, hd), F32),
                        pltpu.VMEM((2, nrow, LANES), F32)],
    )
    return pl.pallas_call(
        functools.partial(_moba_sample_kernel, n_blk=n_blk, page=page, t_len=t_len, thr=_bucket_thresholds()),
        out_shape=jax.ShapeDtypeStruct((n_b, nrow, hd), F32),
        grid_spec=grid_spec,
        compiler_params=_cparams(("parallel", "arbitrary")),
        name="moba_sample",
    )(page_table, cache_kt, cache_kt, cache_vt, cache_vt, q, kn, vn, tab)


def _retention_kernel(q_ref, k_ref, v_ref, g_ref, cos_ref, sin_ref, dmask_ref, xi_ref, zeta_ref, gc_ref,
                      gng_ref, gnb_ref, *rest, nb, chunk, has_s0):
    if has_s0:
        s0_ref, o_ref, s_out_ref, s_scr = rest
    else:
        o_ref, s_out_ref, s_scr = rest
    c = pl.program_id(1)
    dk, dv = RET_DK, RET_DV
    half = dk // 2

    @pl.when(c == 0)
    def _():
        if has_s0:
            s_scr[...] = s0_ref[...]
        else:
            s_scr[...] = jnp.zeros(s_scr.shape, F32)

    cos = cos_ref[...]
    sin = sin_ref[...]

    def rot(x):
        x1 = x[:, :half]
        x2 = x[:, half:]
        return jnp.concatenate([x1 * cos - x2 * sin, x2 * cos + x1 * sin], axis=1)

    out_rows = []
    for bb in range(nb):
        rows = pl.ds(bb * chunk, chunk)
        out_heads = []
        for h in range(RET_HEADS):
            qh = rot(q_ref[rows, h * dk:(h + 1) * dk])
            kh = rot(k_ref[rows, h * dk:(h + 1) * dk]) * (dk ** -0.5)
            vh = v_ref[rows, h * dv:(h + 1) * dv]
            s = s_scr[bb, h]
            att = _bdot_nt(qh, kh) * dmask_ref[h]
            o = _bdot(att, vh) + _bdot(qh, s) * xi_ref[h]
            s_scr[bb, h] = s * gc_ref[h, 0:1, 0:1] + _bdot_tn(kh * zeta_ref[h], vh)
            mu = jnp.mean(o, axis=-1, keepdims=True)
            oc = o - mu
            var = jnp.mean(oc * oc, axis=-1, keepdims=True)
            on = oc * lax.rsqrt(var + LN_EPS) * gng_ref[:, h * dv:(h + 1) * dv] + gnb_ref[:, h * dv:(h + 1) * dv]
            gate = _silu(g_ref[rows, h * dv:(h + 1) * dv])
            out_heads.append(gate * on)
        out_rows.append(jnp.concatenate(out_heads, axis=1))
    o_ref[...] = jnp.concatenate(out_rows, axis=0).astype(o_ref.dtype)

    @pl.when(c == pl.num_programs(1) - 1)
    def _():
        s_out_ref[...] = s_scr[...]


def retention(p, cos, sin, consts, gn_g, gn_b, s0, *, row0, n_batch, t_len, nb, out_dtype):
    chunk = math.gcd(t_len, RET_CHUNK)
    n_chunk = t_len // chunk
    dmask, xi, zeta, gc = consts
    blk_rows = nb * chunk
    if nb > 1:
        assert n_chunk == 1
    assert row0 % blk_rows == 0
    base = row0 // blk_rows
    qk_w = RET_HEADS * RET_DK
    v_w = RET_HEADS * RET_DV
    has_s0 = s0 is not None

    def rmap(col):
        return lambda b, c: (base + b * n_chunk + c, col)

    in_specs = [pl.BlockSpec((blk_rows, qk_w), rmap(0)),
                pl.BlockSpec((blk_rows, qk_w), rmap(1)),
                pl.BlockSpec((blk_rows, v_w), rmap(1)),
                pl.BlockSpec((blk_rows, v_w), rmap(2)),
                pl.BlockSpec((chunk, RET_DK // 2), lambda b, c: (c, 0)),
                pl.BlockSpec((chunk, RET_DK // 2), lambda b, c: (c, 0)),
                pl.BlockSpec(dmask.shape, lambda b, c: (0, 0, 0)),
                pl.BlockSpec(xi.shape, lambda b, c: (0, 0, 0)),
                pl.BlockSpec(zeta.shape, lambda b, c: (0, 0, 0)),
                pl.BlockSpec(gc.shape, lambda b, c: (0, 0, 0)),
                pl.BlockSpec((1, v_w), lambda b, c: (0, 0)),
                pl.BlockSpec((1, v_w), lambda b, c: (0, 0))]
    args = [p, p, p, p, cos, sin, dmask, xi, zeta, gc, gn_g.reshape(1, -1), gn_b.reshape(1, -1)]
    s_spec = pl.BlockSpec((nb, RET_HEADS, RET_DK, RET_DV), lambda b, c: (b, 0, 0, 0))
    if has_s0:
        in_specs.append(s_spec)
        args.append(s0)
    o, s_out = pl.pallas_call(
        functools.partial(_retention_kernel, nb=nb, chunk=chunk, has_s0=has_s0),
        out_shape=(jax.ShapeDtypeStruct((n_batch * t_len, v_w), out_dtype),
                   jax.ShapeDtypeStruct((n_batch, RET_HEADS, RET_DK, RET_DV), F32)),
        grid=(n_batch // nb, n_chunk),
        in_specs=in_specs,
        out_specs=(pl.BlockSpec((blk_rows, v_w), lambda b, c: (b * n_chunk + c, 0)), s_spec),
        scratch_shapes=[pltpu.VMEM((nb, RET_HEADS, RET_DK, RET_DV), F32)],
        compiler_params=_cparams(("parallel", "arbitrary")),
        name="retention",
    )(*args)
    return o, s_out


def _retention_consts(chunk):
    h = jnp.arange(RET_HEADS, dtype=F32)
    lg = jnp.log1p(-jnp.exp2(-5.0 - h))
    idx = jnp.arange(chunk, dtype=F32)
    diff = idx[:, None] - idx[None, :]
    dmask = jnp.where(diff >= 0, jnp.exp(lg[:, None, None] * jnp.maximum(diff, 0.0)), 0.0)
    xi = jnp.exp(lg[:, None] * (idx[None, :] + 1.0))
    zeta = jnp.exp(lg[:, None] * (chunk - 1.0 - idx[None, :]))
    g_c = jnp.exp(lg * chunk)
    return (dmask,
            jnp.broadcast_to(xi[:, :, None], (RET_HEADS, chunk, RET_DV)),
            jnp.broadcast_to(zeta[:, :, None], (RET_HEADS, chunk, RET_DK)),
            jnp.broadcast_to(g_c[:, None, None], (RET_HEADS, 8, LANES)))


def _rotary_tables(pos):
    half = RET_DK // 2
    inv = 10000.0 ** (-jnp.linspace(0.0, 1.0, half, dtype=F32))
    ang = pos.astype(F32)[:, None] * inv[None, :]
    return jnp.cos(ang), jnp.sin(ang)


def _mem_attn_kernel(q_ref, mk_ref, mv_ref, o_ref, *, nb, tq):
    hd = MEM_HEAD_DIM
    scale = hd ** -0.5
    qf = q_ref[...].astype(F32)
    out_rows = []
    for bb in range(nb):
        out_heads = []
        for h in range(MEM_HEADS):
            cols = slice(h * hd, (h + 1) * hd)
            q = qf[bb * tq:(bb + 1) * tq, cols]
            s = _bdot_nt(q, mk_ref[bb, :, cols]) * scale
            m = jnp.max(s, axis=-1, keepdims=True)
            p = jnp.exp(s - m)
            p = p / jnp.sum(p, axis=-1, keepdims=True)
            out_heads.append(_bdot(p, mv_ref[bb, :, cols]))
        out_rows.append(jnp.concatenate(out_heads, axis=1))
    o_ref[...] = jnp.concatenate(out_rows, axis=0).astype(o_ref.dtype)


def mem_attention(q, mk, mv, *, row0, n_batch, t_len, nb, tq, out_dtype):
    n_t = t_len // tq
    blk_rows = nb * tq
    if nb > 1:
        assert n_t == 1
    assert row0 % blk_rows == 0
    base = row0 // blk_rows
    m_tok = mk.shape[1]
    return pl.pallas_call(
        functools.partial(_mem_attn_kernel, nb=nb, tq=tq),
        out_shape=jax.ShapeDtypeStruct((n_batch * t_len, D_MODEL), out_dtype),
        grid=(n_batch // nb, n_t),
        in_specs=[pl.BlockSpec((blk_rows, D_MODEL), lambda b, t: (base + b * n_t + t, 0)),
                  pl.BlockSpec((nb, m_tok, D_MODEL), lambda b, t: (b, 0, 0)),
                  pl.BlockSpec((nb, m_tok, D_MODEL), lambda b, t: (b, 0, 0))],
        out_specs=pl.BlockSpec((blk_rows, D_MODEL), lambda b, t: (b * n_t + t, 0)),
        compiler_params=_cparams(("parallel", "arbitrary")),
        name="mem_attention",
    )(q, mk, mv)


def _router_kernel(x_ref, w_ref, b_ref, tri_ref, wt_ref, ei_ref, cnt_ref, base_scr):
    @pl.when(pl.program_id(0) == 0)
    def _():
        base_scr[...] = jnp.zeros(base_scr.shape, F32)

    x = x_ref[...]
    logits = _dot_hi(x, w_ref[...]) + b_ref[...]
    tm = x.shape[0]
    lane = lax.broadcasted_iota(I32, (tm, LANES), 1)
    big = jnp.int32(LANES)

    def masked_softmax(mask):
        lm = jnp.where(mask, logits, NEG_INF)
        mx = jnp.max(lm, axis=-1, keepdims=True)
        e = jnp.exp(lm - mx)
        return e / jnp.sum(e, axis=-1, keepdims=True)

    g_prob = masked_softmax(lane < MOE_GROUPS)
    gp = jnp.max(g_prob, axis=-1, keepdims=True)
    gi = jnp.min(jnp.where(g_prob == gp, lane, big), axis=-1, keepdims=True)
    lo = MOE_GROUPS + gi * MOE_EPG
    in_group = (lane >= lo) & (lane < lo + MOE_EPG)
    e_prob = jnp.where(in_group, masked_softmax(in_group), -1.0)
    p1 = jnp.max(e_prob, axis=-1, keepdims=True)
    i1 = jnp.min(jnp.where(e_prob == p1, lane, big), axis=-1, keepdims=True)
    rest = jnp.where(lane == i1, -1.0, e_prob)
    p2 = jnp.max(rest, axis=-1, keepdims=True)
    i2 = jnp.min(jnp.where(rest == p2, lane, big), axis=-1, keepdims=True)
    tot = p1 + p2
    w1 = gp * (p1 / tot)
    w2 = gp * (p2 / tot)
    e1 = i1 - MOE_GROUPS
    e2 = i2 - MOE_GROUPS
    oh1 = jnp.where(lane == e1, 1.0, 0.0)
    oh2 = jnp.where(lane == e2, 1.0, 0.0)
    both = oh1 + oh2
    base = base_scr[0:1, :]
    before = jnp.dot(tri_ref[...], both.astype(BF16), preferred_element_type=F32) + base
    r1 = jnp.sum(oh1 * before, axis=-1, keepdims=True).astype(I32)
    r2 = jnp.sum(oh2 * before, axis=-1, keepdims=True).astype(I32)
    base_new = base + jnp.sum(both, axis=0, keepdims=True)
    base_scr[0:1, :] = base_new
    wt_ref[...] = jnp.where(lane == 0, w1, jnp.where(lane == 1, w2, 0.0))
    ei_ref[...] = jnp.where(lane == 0, e1, jnp.where(lane == 1, e2, jnp.where(lane == 2, r1,
                                                                              jnp.where(lane == 3, r2, 0))))
    cnt_ref[...] = jnp.broadcast_to(base_new, cnt_ref.shape)


def moe_router(x, w_all, b_all, *, tm):
    m, d = x.shape
    tri = jnp.asarray(np.tril(np.ones((tm, tm), np.float32), -1), BF16)
    return pl.pallas_call(
        _router_kernel,
        out_shape=(jax.ShapeDtypeStruct((m, LANES), F32), jax.ShapeDtypeStruct((m, LANES), I32),
                   jax.ShapeDtypeStruct((8, LANES), F32)),
        grid=(m // tm,),
        in_specs=[pl.BlockSpec((tm, d), lambda i: (i, 0)),
                  pl.BlockSpec((d, LANES), lambda i: (0, 0)),
                  pl.BlockSpec((1, LANES), lambda i: (0, 0)),
                  pl.BlockSpec((tm, tm), lambda i: (0, 0))],
        out_specs=(pl.BlockSpec((tm, LANES), lambda i: (i, 0)), pl.BlockSpec((tm, LANES), lambda i: (i, 0)),
                   pl.BlockSpec((8, LANES), lambda i: (0, 0))),
        scratch_shapes=[pltpu.VMEM((8, LANES), F32)],
        compiler_params=_cparams(("arbitrary",)),
        name="moe_router",
    )(x, w_all, b_all, tri)


def _moe_dispatch_kernel(dest_ref, x_hbm, xs_in, xs_out, sem, *, tm):
    del xs_in
    i = pl.program_id(0)
    n = pl.num_programs(0)
    slot = i % 2

    def body(r, carry):
        tok = i * tm + r
        _tile_copy(x_hbm, tok, xs_out, dest_ref[r], sem.at[slot]).start()
        _tile_copy(x_hbm, tok, xs_out, dest_ref[tm + r], sem.at[slot]).start()
        return carry

    lax.fori_loop(0, tm, body, 0, unroll=8)

    @pl.when(i > 0)
    def _():
        _wait_tiles(x_hbm, xs_out, 0, sem.at[1 - slot], 2 * tm)

    @pl.when(i == n - 1)
    def _():
        _wait_tiles(x_hbm, xs_out, 0, sem.at[slot], 2 * tm)


def moe_dispatch(x8, dest, n_slots, *, tm):
    n_t = dest.shape[0]
    xs0 = jnp.zeros((n_slots * ROW_TILE, LANES), F32)
    return pl.pallas_call(
        functools.partial(_moe_dispatch_kernel, tm=tm),
        out_shape=jax.ShapeDtypeStruct(xs0.shape, F32),
        grid=(n_t,),
        in_specs=[pl.BlockSpec((None, None, 2 * tm), lambda i: (i, 0, 0), memory_space=pltpu.SMEM),
                  pl.BlockSpec(memory_space=pl.ANY),
                  pl.BlockSpec(memory_space=pl.ANY)],
        out_specs=pl.BlockSpec(memory_space=pl.ANY),
        scratch_shapes=[pltpu.SemaphoreType.DMA((2,))],
        input_output_aliases={2: 0},
        compiler_params=_cparams(("arbitrary",)),
        name="moe_dispatch",
    )(dest.reshape(n_t, 1, 2 * tm), x8, xs0)


def _moe_ffn_kernel(te_ref, nt_ref, xs_ref, w1_ref, w3_ref, w2_ref, ys_ref):
    j = pl.program_id(0)

    @pl.when(j < nt_ref[0])
    def _():
        x = _load_token_tiles(xs_ref, 0, MOE_TILE).astype(BF16)
        h = _silu(_bdot(x, w1_ref[0])) * _bdot(x, w3_ref[0])
        _store_token_tiles(ys_ref, _bdot(h, w2_ref[0]))

    @pl.when(j >= nt_ref[0])
    def _():
        ys_ref[...] = jnp.zeros(ys_ref.shape, F32)


def moe_ffn(xs8, tile_expert, n_tiles, w1, w3, w2, *, layer):
    n_t = tile_expert.shape[0]
    d, dff = w1.shape[2], w1.shape[3]
    blk = MOE_TILE * ROW_TILE
    grid_spec = pltpu.PrefetchScalarGridSpec(
        num_scalar_prefetch=2,
        grid=(n_t,),
        in_specs=[pl.BlockSpec((blk, LANES), lambda j, te, nt: (jnp.minimum(j, nt[0] - 1), 0)),
                  pl.BlockSpec((None, 1, d, dff), lambda j, te, nt: (layer, te[j], 0, 0)),
                  pl.BlockSpec((None, 1, d, dff), lambda j, te, nt: (layer, te[j], 0, 0)),
                  pl.BlockSpec((None, 1, dff, d), lambda j, te, nt: (layer, te[j], 0, 0))],
        out_specs=pl.BlockSpec((blk, LANES), lambda j, te, nt: (j, 0)),
    )
    return pl.pallas_call(
        _moe_ffn_kernel,
        out_shape=jax.ShapeDtypeStruct(xs8.shape, F32),
        grid_spec=grid_spec,
        compiler_params=_cparams(("arbitrary",)),
        name="moe_ffn",
    )(tile_expert, n_tiles, xs8, w1, w3, w2)


def _moe_combine_kernel(d_cur, d_nxt, ys_hbm, x_ref, wt_ref, g_ref, b_ref, o_ref, ybuf, sem, *, tm):
    i = pl.program_id(0)
    n = pl.num_programs(0)
    slot = i % 2

    def issue(d_ref, s):
        def body(r, carry):
            _tile_copy(ys_hbm, d_ref[r], ybuf, s * 2 * tm + r, sem.at[s]).start()
            return carry
        lax.fori_loop(0, 2 * tm, body, 0, unroll=8)

    @pl.when(i == 0)
    def _():
        issue(d_cur, 0)

    @pl.when(i + 1 < n)
    def _():
        issue(d_nxt, 1 - slot)

    base_tok = slot * 2 * tm
    _wait_tiles(ys_hbm, ybuf, base_tok, sem.at[slot], 2 * tm)
    wt = wt_ref[...]
    y0 = _load_token_tiles(ybuf, base_tok * ROW_TILE, tm)
    y1 = _load_token_tiles(ybuf, (base_tok + tm) * ROW_TILE, tm)
    y = DEEPNORM_ALPHA * x_ref[...] + (wt[:, 0:1] * y0 + wt[:, 1:2] * y1)
    o_ref[...] = _layer_norm(y, g_ref[...], b_ref[...])


def moe_combine_deepnorm(ys8, dest, x, wts, g, b, *, tm):
    m, d = x.shape
    n_t = m // tm
    dest3 = dest.reshape(n_t, 1, 2 * tm)
    return pl.pallas_call(
        functools.partial(_moe_combine_kernel, tm=tm),
        out_shape=jax.ShapeDtypeStruct((m, d), F32),
        grid=(n_t,),
        in_specs=[pl.BlockSpec((None, None, 2 * tm), lambda i: (i, 0, 0), memory_space=pltpu.SMEM),
                  pl.BlockSpec((None, None, 2 * tm), lambda i: (jnp.minimum(i + 1, n_t - 1), 0, 0),
                               memory_space=pltpu.SMEM),
                  pl.BlockSpec(memory_space=pl.ANY),
                  pl.BlockSpec((tm, d), lambda i: (i, 0)),
                  pl.BlockSpec((tm, LANES), lambda i: (i, 0)),
                  pl.BlockSpec((1, d), lambda i: (0, 0)),
                  pl.BlockSpec((1, d), lambda i: (0, 0))],
        out_specs=pl.BlockSpec((tm, d), lambda i: (i, 0)),
        scratch_shapes=[pltpu.VMEM((2 * 2 * tm * ROW_TILE, LANES), F32), pltpu.SemaphoreType.DMA((2,))],
        compiler_params=_cparams(("arbitrary",)),
        name="moe_combine_deepnorm",
    )(dest3, dest3, ys8, x, wts, g.reshape(1, d), b.reshape(1, d))


def _moe_plan(eidx, rank, counts, n_tok, tm):
    n_tiles_max = n_tok * MOE_TOPK // MOE_TILE + MOE_EXPERTS
    tiles_per = (counts + MOE_TILE - 1) // MOE_TILE
    tile_end = jnp.cumsum(tiles_per)
    pad_off = (tile_end - tiles_per) * MOE_TILE
    onehot = eidx[:, :, None] == jnp.arange(MOE_EXPERTS, dtype=I32)[None, None, :]
    dest = jnp.sum(jnp.where(onehot, pad_off[None, None, :], 0), axis=-1) + rank
    tile_expert = jnp.minimum(
        jnp.sum((jnp.arange(n_tiles_max, dtype=I32)[:, None] >= tile_end[None, :]).astype(I32), axis=1),
        MOE_EXPERTS - 1)
    dest = dest.reshape(n_tok // tm, tm, MOE_TOPK).transpose(0, 2, 1).reshape(n_tok // tm, MOE_TOPK * tm)
    return tile_expert, tile_end[-1:].astype(I32), dest, n_tiles_max * MOE_TILE


def hier_moe_deepnorm(x, x8, w_all, b_all, w1, w3, w2, g, b, *, layer):
    n_tok = x.shape[0]
    wts, ei, cnt = moe_router(x, w_all, b_all, tm=640)
    counts = cnt[0, :MOE_EXPERTS].astype(I32)
    tile_expert, n_tiles, dest, n_slots = _moe_plan(ei[:, 0:2], ei[:, 2:4], counts, n_tok, MOE_TILE)
    xs8 = moe_dispatch(x8, dest, n_slots, tm=MOE_TILE)
    ys8 = moe_ffn(xs8, tile_expert, n_tiles, w1, w3, w2, layer=layer)
    return moe_combine_deepnorm(ys8, dest, x, wts, g, b, tm=MOE_TILE)


def kernel(x_prompt, x_sample, cache_moba_k, cache_moba_v, state_conv, state_ret, cache_mem_k, cache_mem_v,
           page_table, mem_prompt, rel_bias, ev_w_in, ev_conv_w, ev_conv_b, ev_conv_ln_g, ev_conv_ln_b,
           ev_w_out, od_w_in, od_gn_g, od_gn_b, od_w_out, mem_wq, mem_wk, mem_wv, mem_wo, ln_g, ln_b,
           moe_w_group, moe_b_group, moe_w_router, moe_b_router, moe_w1, moe_w3, moe_w2):
    bp, tp, d = x_prompt.shape
    bs, ts, _ = x_sample.shape
    n_p = bp * tp
    n_s = bs * ts
    n_all = n_p + n_s
    page = cache_moba_k.shape[1]
    past_len = page_table.shape[1] * page
    nh, hd = MOBA_HEADS, MOBA_HEAD_DIM
    assert tp % MOBA_BLOCK == 0 and past_len % MOBA_BLOCK == 0 and ts <= MOBA_BLOCK
    assert MOBA_BLOCK >= MAX_DISTANCE
    tm = 640
    assert n_all % tm == 0

    x = jnp.concatenate([x_prompt.reshape(n_p, d), x_sample.reshape(n_s, d)], axis=0)

    tab_heads = rel_bias.T
    tab_rows = jnp.pad(jnp.repeat(tab_heads, ts, axis=0), ((0, 0), (0, LANES - NUM_BUCKETS)))
    cache_kt = cache_moba_k.transpose(0, 2, 3, 4, 1)
    cache_vt = cache_moba_v.transpose(0, 2, 3, 4, 1)

    pos_p = jnp.arange(tp, dtype=I32)
    pos_s = past_len + jnp.arange(ts, dtype=I32)
    cos_p, sin_p = _rotary_tables(pos_p)
    cos_s, sin_s = _rotary_tables(pos_s)
    ret_c_p = _retention_consts(math.gcd(tp, RET_CHUNK))
    ret_c_s = _retention_consts(math.gcd(ts, RET_CHUNK))

    outs = {}
    for layer in range(DEPTH):
        if layer % 2 == 0:
            e = layer // 2
            proj = matmul(x, ev_w_in[e].astype(BF16), tm=tm, tn=ev_w_in.shape[2])
            k_new = proj[:, 2 * CONV_CH + MOBA_WIDTH:2 * CONV_CH + 2 * MOBA_WIDTH]
            v_new = proj[:, 2 * CONV_CH + 2 * MOBA_WIDTH:]
            q_s = proj[n_p:, 2 * CONV_CH:2 * CONV_CH + MOBA_WIDTH]
            conv_args = (ev_conv_w[e], ev_conv_b[e], ev_conv_ln_g[e], ev_conv_ln_b[e])
            c_p, cst_p = conv_module(proj, jnp.zeros((bp, CONV_WIDTH - 1, CONV_CH), F32), *conv_args,
                                     row0=0, n_batch=bp, t_len=tp, nb=1, out_dtype=BF16)
            c_s, cst_s = conv_module(proj, state_conv[e], *conv_args,
                                     row0=n_p, n_batch=bs, t_len=ts, nb=bs, out_dtype=F32)
            a_p = moba_prompt(proj, tab_heads, n_batch=bp, t_len=tp, out_dtype=BF16)

            def by_head(a):
                return a.reshape(bs, ts, nh, hd).transpose(0, 2, 1, 3)

            a_s = moba_sample(page_table, cache_kt, cache_vt, by_head(q_s), by_head(k_new[n_p:]),
                              by_head(v_new[n_p:]), tab_rows, layer_e=e, t_len=ts)
            a_s = a_s.reshape(bs, nh, ts, hd).transpose(0, 2, 1, 3).reshape(n_s, MOBA_WIDTH).astype(BF16)
            c_all = jnp.concatenate([c_p, c_s.astype(BF16)], axis=0)
            a_all = jnp.concatenate([a_p, a_s], axis=0)
            w_out = ev_w_out[e].astype(BF16)
            x = matmul_deepnorm([c_all, a_all], [w_out[:CONV_CH], w_out[CONV_CH:]], x,
                                ln_g[layer, 0], ln_b[layer, 0], tm=tm)
            outs.setdefault("kp", []).append(k_new[:n_p].reshape(bp, tp, nh, hd))
            outs.setdefault("vp", []).append(v_new[:n_p].reshape(bp, tp, nh, hd))
            outs.setdefault("ks", []).append(k_new[n_p:].reshape(bs, ts, nh, hd))
            outs.setdefault("vs", []).append(v_new[n_p:].reshape(bs, ts, nh, hd))
            outs.setdefault("cp", []).append(cst_p)
            outs.setdefault("cs", []).append(cst_s)
        else:
            o = layer // 2
            proj = matmul(x, od_w_in[o].astype(BF16), tm=tm, tn=2048)
            r_p, st_p = retention(proj, cos_p, sin_p, ret_c_p, od_gn_g[o], od_gn_b[o], None,
                                  row0=0, n_batch=bp, t_len=tp, nb=1, out_dtype=BF16)
            r_s, st_s = retention(proj, cos_s, sin_s, ret_c_s, od_gn_g[o], od_gn_b[o], state_ret[o],
                                  row0=n_p, n_batch=bs, t_len=ts, nb=2, out_dtype=BF16)
            r_all = jnp.concatenate([r_p, r_s], axis=0)
            x = matmul_deepnorm([r_all], [od_w_out[o].astype(BF16)], x, ln_g[layer, 0], ln_b[layer, 0], tm=tm)
            outs.setdefault("sp", []).append(st_p)
            outs.setdefault("ss", []).append(st_s)

        m_tok = mem_prompt.shape[1]
        mem2 = mem_prompt.reshape(bp * m_tok, d)
        mk_p = matmul(mem2, mem_wk[layer].astype(BF16), tm=512, tn=d)
        mv_p = matmul(mem2, mem_wv[layer].astype(BF16), tm=512, tn=d)
        outs.setdefault("mk", []).append(mk_p.reshape(bp, m_tok, MEM_HEADS, MEM_HEAD_DIM))
        outs.setdefault("mv", []).append(mv_p.reshape(bp, m_tok, MEM_HEADS, MEM_HEAD_DIM))
        q = matmul(x, mem_wq[layer].astype(BF16), tm=tm, tn=d, out_dtype=BF16)
        o_p = mem_attention(q, mk_p.reshape(bp, m_tok, d), mv_p.reshape(bp, m_tok, d),
                            row0=0, n_batch=bp, t_len=tp, nb=1, tq=512, out_dtype=BF16)
        o_s = mem_attention(q, cache_mem_k[layer].reshape(bs, m_tok, d), cache_mem_v[layer].reshape(bs, m_tok, d),
                            row0=n_p, n_batch=bs, t_len=ts, nb=2, tq=ts, out_dtype=BF16)
        o_all = jnp.concatenate([o_p, o_s], axis=0)
        x, x8 = matmul_deepnorm([o_all], [mem_wo[layer].astype(BF16)], x, ln_g[layer, 1], ln_b[layer, 1], tm=tm,
                                token_tiles=True)

        w_all = jnp.zeros((d, LANES), F32)
        w_all = w_all.at[:, :MOE_GROUPS].set(moe_w_group[layer])
        w_all = w_all.at[:, MOE_GROUPS:MOE_GROUPS + MOE_EXPERTS].set(moe_w_router[layer])
        b_all = jnp.zeros((1, LANES), F32)
        b_all = b_all.at[0, :MOE_GROUPS].set(moe_b_group[layer])
        b_all = b_all.at[0, MOE_GROUPS:MOE_GROUPS + MOE_EXPERTS].set(moe_b_router[layer])
        x = hier_moe_deepnorm(x, x8, w_all, b_all, moe_w1, moe_w3, moe_w2, ln_g[layer, 2], ln_b[layer, 2],
                              layer=layer)

    y_prompt = x[:n_p].reshape(bp, tp, d)
    y_sample = x[n_p:].reshape(bs, ts, d)
    return (y_prompt, y_sample,
            jnp.stack(outs["kp"], axis=2), jnp.stack(outs["vp"], axis=2),
            jnp.stack(outs["ks"], axis=2), jnp.stack(outs["vs"], axis=2),
            jnp.stack(outs["cp"], axis=0), jnp.stack(outs["cs"], axis=0),
            jnp.stack(outs["sp"], axis=0), jnp.stack(outs["ss"], axis=0),
            jnp.stack(outs["mk"], axis=0), jnp.stack(outs["mv"], axis=0))
```

```python
import functools
import math

import numpy as np
import jax
import jax.numpy as jnp
from jax import lax
from jax.experimental import pallas as pl
from jax.experimental.pallas import tpu as pltpu

F32 = jnp.float32
BF16 = jnp.bfloat16
I32 = jnp.int32

D_MODEL = 1024
DEPTH = 2
CONV_CH = 512
CONV_WIDTH = 31
MOBA_HEADS = 8
MOBA_HEAD_DIM = 64
MOBA_WIDTH = 512
MOBA_BLOCK = 256
MOBA_TOPK = 3
NUM_BUCKETS = 32
MAX_DISTANCE = 128
RET_HEADS = 4
RET_DK = 256
RET_DV = 512
RET_CHUNK = 128
MEM_HEADS = 4
MEM_HEAD_DIM = 256
MOE_GROUPS = 4
MOE_EPG = 8
MOE_EXPERTS = 32
MOE_TOPK = 2
MOE_D_FF = 512
DEEPNORM_ALPHA = (2 * DEPTH) ** 0.25
LN_EPS = 1e-5

LANES = 128
VMEM_LIMIT = 56 * 1024 * 1024
MOE_TILE = 256
NEG_INF = float("-inf")


def _cparams(sem):
    return pltpu.CompilerParams(dimension_semantics=sem, vmem_limit_bytes=VMEM_LIMIT)


def _layer_norm(y, g, b):
    mu = jnp.mean(y, axis=-1, keepdims=True)
    yc = y - mu
    var = jnp.mean(yc * yc, axis=-1, keepdims=True)
    return yc * lax.rsqrt(var + LN_EPS) * g + b


def _silu(x):
    return x * (1.0 / (1.0 + jnp.exp(-x)))


def _bdot(a, b):
    return jnp.dot(a.astype(BF16), b.astype(BF16), preferred_element_type=F32)


def _bdot_nt(a, b):
    return lax.dot_general(a.astype(BF16), b.astype(BF16), (((1,), (1,)), ((), ())),
                           preferred_element_type=F32)


def _bdot_tn(a, b):
    return lax.dot_general(a.astype(BF16), b.astype(BF16), (((0,), (0,)), ((), ())),
                           preferred_element_type=F32)


def _split3(x):
    hi = x.astype(BF16)
    lo = (x - hi.astype(F32)).astype(BF16)
    return hi, lo


def _dot_hi(a, b, dims=(((1,), (0,)), ((), ()))):
    ah, al = _split3(a)
    bh, bl = _split3(b)
    dg = functools.partial(lax.dot_general, dimension_numbers=dims, preferred_element_type=F32)
    return dg(ah, bh) + (dg(al, bh) + dg(ah, bl))


ROW_TILE = D_MODEL // LANES


def _load_token_tiles(ref, start, n_tok):
    return jnp.concatenate([ref[pl.ds(start + c, n_tok, stride=ROW_TILE), :] for c in range(ROW_TILE)], axis=1)


def _store_token_tiles(ref, val):
    n_tok = val.shape[0]
    for c in range(ROW_TILE):
        ref[pl.ds(c, n_tok, stride=ROW_TILE), :] = val[:, c * LANES:(c + 1) * LANES]


def _tile_copy(src_hbm, src_tok, dst, dst_tok, sem):
    return pltpu.make_async_copy(src_hbm.at[pl.ds(src_tok * ROW_TILE, ROW_TILE), :],
                                 dst.at[pl.ds(dst_tok * ROW_TILE, ROW_TILE), :], sem)


def _wait_tiles(src_hbm, dst, dst_tok, sem, n_tok):
    pltpu.make_async_copy(src_hbm.at[pl.ds(0, n_tok * ROW_TILE), :],
                          dst.at[pl.ds(dst_tok * ROW_TILE, n_tok * ROW_TILE), :], sem).wait()


def _mm_kernel(x_ref, w_ref, o_ref):
    o_ref[...] = _bdot(x_ref[...], w_ref[...]).astype(o_ref.dtype)


def matmul(x, w, *, tm, tn, out_dtype=F32):
    m, k = x.shape
    n = w.shape[1]
    assert m % tm == 0 and n % tn == 0
    return pl.pallas_call(
        _mm_kernel,
        out_shape=jax.ShapeDtypeStruct((m, n), out_dtype),
        grid=(m // tm, n // tn),
        in_specs=[pl.BlockSpec((tm, k), lambda i, j: (i, 0)),
                  pl.BlockSpec((k, tn), lambda i, j: (0, j))],
        out_specs=pl.BlockSpec((tm, tn), lambda i, j: (i, j)),
        compiler_params=_cparams(("parallel", "parallel")),
        name="matmul",
    )(x, w)


def _mm_dn_kernel(*refs, n_lhs, token_tiles):
    lhs = refs[:n_lhs]
    ws = refs[n_lhs:2 * n_lhs]
    res_ref, g_ref, b_ref, o_ref = refs[2 * n_lhs:2 * n_lhs + 4]
    acc = _bdot(lhs[0][...], ws[0][...])
    for a, w in zip(lhs[1:], ws[1:]):
        acc = acc + _bdot(a[...], w[...])
    y = DEEPNORM_ALPHA * res_ref[...] + acc
    out = _layer_norm(y, g_ref[...], b_ref[...])
    o_ref[...] = out
    if token_tiles:
        _store_token_tiles(refs[2 * n_lhs + 4], out)


def matmul_deepnorm(lhs_list, w_list, res, g, b, *, tm, token_tiles=False):
    m, d = res.shape
    n_lhs = len(lhs_list)
    in_specs = ([pl.BlockSpec((tm, a.shape[1]), lambda i: (i, 0)) for a in lhs_list]
                + [pl.BlockSpec(w.shape, lambda i: (0, 0)) for w in w_list]
                + [pl.BlockSpec((tm, d), lambda i: (i, 0)),
                   pl.BlockSpec((1, d), lambda i: (0, 0)),
                   pl.BlockSpec((1, d), lambda i: (0, 0))])
    out_shape = [jax.ShapeDtypeStruct((m, d), F32)]
    out_specs = [pl.BlockSpec((tm, d), lambda i: (i, 0))]
    if token_tiles:
        out_shape.append(jax.ShapeDtypeStruct((m * ROW_TILE, LANES), F32))
        out_specs.append(pl.BlockSpec((tm * ROW_TILE, LANES), lambda i: (i, 0)))
    res_out = pl.pallas_call(
        functools.partial(_mm_dn_kernel, n_lhs=n_lhs, token_tiles=token_tiles),
        out_shape=tuple(out_shape),
        grid=(m // tm,),
        in_specs=in_specs,
        out_specs=tuple(out_specs),
        compiler_params=_cparams(("parallel",)),
        name="matmul_deepnorm",
    )(*lhs_list, *w_list, res, g.reshape(1, d), b.reshape(1, d))
    return res_out if token_tiles else res_out[0]


CONV_PAD = 32


def _conv_kernel(p_ref, hist_ref, w_ref, cb_ref, g_ref, b_ref, c_ref, st_ref, u_scr, *, nb, t_len, rc):
    hw = CONV_WIDTH - 1
    w = w_ref[...]
    for bb in range(nb):
        rows = pl.ds(bb * t_len, t_len)
        a = p_ref[rows, 0:CONV_CH]
        gt = p_ref[rows, CONV_CH:2 * CONV_CH]
        u = a * (1.0 / (1.0 + jnp.exp(-gt)))
        u_scr[0:CONV_PAD, :] = jnp.concatenate(
            [jnp.zeros((CONV_PAD - hw, CONV_CH), F32), hist_ref[bb]], axis=0)
        u_scr[CONV_PAD:CONV_PAD + t_len, :] = u
        st_ref[bb] = u_scr[CONV_PAD + t_len - hw:CONV_PAD + t_len, :]

        def chunk(ci, carry):
            r0 = pl.multiple_of(ci * rc, rc)
            blk = u_scr[pl.ds(r0, rc + CONV_PAD), :]
            acc = jnp.broadcast_to(cb_ref[...], (rc, CONV_CH))
            for j in range(CONV_WIDTH):
                off = j + CONV_PAD - hw
                acc = acc + w[j:j + 1, :] * blk[off:off + rc, :]
            y = _silu(_layer_norm(acc, g_ref[...], b_ref[...]))
            c_ref[pl.ds(bb * t_len + r0, rc), :] = y.astype(c_ref.dtype)
            return carry

        n_chunks = t_len // rc
        if n_chunks == 1:
            chunk(0, 0)
        else:
            lax.fori_loop(0, n_chunks, chunk, 0)


def conv_module(p, hist, conv_w, conv_b, ln_g, ln_b, *, row0, n_batch, t_len, nb, out_dtype):
    rc = min(64, t_len)
    blk_rows = nb * t_len
    assert row0 % blk_rows == 0 and n_batch % nb == 0
    base = row0 // blk_rows
    hw = CONV_WIDTH - 1
    c, st = pl.pallas_call(
        functools.partial(_conv_kernel, nb=nb, t_len=t_len, rc=rc),
        out_shape=(jax.ShapeDtypeStruct((n_batch * t_len, CONV_CH), out_dtype),
                   jax.ShapeDtypeStruct((n_batch, hw, CONV_CH), F32)),
        grid=(n_batch // nb,),
        in_specs=[pl.BlockSpec((blk_rows, 2 * CONV_CH), lambda i: (base + i, 0)),
                  pl.BlockSpec((nb, hw, CONV_CH), lambda i: (i, 0, 0)),
                  pl.BlockSpec((CONV_WIDTH, CONV_CH), lambda i: (0, 0)),
                  pl.BlockSpec((1, CONV_CH), lambda i: (0, 0)),
                  pl.BlockSpec((1, CONV_CH), lambda i: (0, 0)),
                  pl.BlockSpec((1, CONV_CH), lambda i: (0, 0))],
        out_specs=(pl.BlockSpec((blk_rows, CONV_CH), lambda i: (i, 0)),
                   pl.BlockSpec((nb, hw, CONV_CH), lambda i: (i, 0, 0))),
        scratch_shapes=[pltpu.VMEM((t_len + CONV_PAD, CONV_CH), F32)],
        compiler_params=_cparams(("parallel",)),
        name="conv_module",
    )(p, hist, conv_w, conv_b.reshape(1, -1), ln_g.reshape(1, -1), ln_b.reshape(1, -1))
    return c, st


def _bucket_thresholds():
    max_exact = NUM_BUCKETS // 2
    d = np.arange(0, MAX_DISTANCE + 1)
    val = (np.log(np.maximum(d, 1).astype(np.float32) / np.float32(max_exact))
           / np.float32(math.log(MAX_DISTANCE / max_exact)) * np.float32(NUM_BUCKETS - max_exact))
    inner = (d > max_exact) & (d < MAX_DISTANCE)
    assert np.all(np.abs(val[inner] - np.round(val[inner])) > 1e-3)
    bucket = np.where(d < max_exact, d, np.minimum(max_exact + val.astype(np.int32), NUM_BUCKETS - 1))
    return [int(np.argmax(bucket >= k)) for k in range(1, NUM_BUCKETS)]


def _bias_chain(dist, tab_at, thr):
    b = jnp.where(dist >= thr[0], tab_at(1), tab_at(0))
    for k in range(2, NUM_BUCKETS):
        b = jnp.where(dist >= thr[k - 1], tab_at(k), b)
    return b


def _moba_prompt_kernel(tab_ref, q_ref, k_ref, v_ref, o_ref,
                        kmean_scr, kbf_scr, vt_scr, d0_scr, d1_scr, sel_scr, qb_scr, m_scr, l_scr, acc_scr,
                        *, n_blk, thr):
    b = pl.program_id(0)
    i = pl.program_id(1)
    blk = MOBA_BLOCK
    hd = MOBA_HEAD_DIM
    scale = hd ** -0.5
    nt_dims = (((1,), (1,)), ((), ()))
    key = lax.broadcasted_iota(I32, (blk, blk), 0)
    qry = lax.broadcasted_iota(I32, (blk, blk), 1)

    @pl.when((b == 0) & (i == 0))
    def _():
        for h in range(MOBA_HEADS):
            tab_at = functools.partial(lambda k, hh: tab_ref[hh, k], hh=h)
            d0_scr[h] = _bias_chain(qry - key, tab_at, thr)
            d1_scr[h] = _bias_chain(blk + qry - key, tab_at, thr)

    @pl.when(i == 0)
    def _():
        for n in range(n_blk):
            kb = k_ref[n * blk:(n + 1) * blk, :]
            kmean_scr[n:n + 1, :] = jnp.sum(kb, axis=0, keepdims=True) * (1.0 / blk)
            kbf_scr[n * blk:(n + 1) * blk, :] = kb.astype(BF16)
            for pair in range(MOBA_HEADS // 2):
                vt_scr[n, pair * LANES:(pair + 1) * LANES, :] = (
                    v_ref[n * blk:(n + 1) * blk, pair * LANES:(pair + 1) * LANES].T.astype(BF16))

    q = q_ref[...]
    lane = lax.broadcasted_iota(I32, (blk, LANES), 1)
    rown = lax.broadcasted_iota(I32, (n_blk, blk), 0)
    rowd = lax.broadcasted_iota(I32, (LANES, blk), 0)
    r0 = pl.multiple_of(i * blk, blk)
    heads = [(h, slice((h // 2) * LANES, (h // 2 + 1) * LANES)) for h in range(MOBA_HEADS)]

    for h, cols in heads:
        sub = h % 2
        in_head = (lane >= sub * hd) & (lane < (sub + 1) * hd)
        qm = jnp.where(in_head, q[:, cols], 0.0)
        gate = _dot_hi(kmean_scr[:, cols], qm, nt_dims)
        rank = jnp.zeros((n_blk, blk), F32)
        for m in range(n_blk):
            gm = gate[m:m + 1, :]
            tie = jnp.where(rown > m, 1.0, 0.0)
            cnt = jnp.where(gm > gate, 1.0, jnp.where(gm == gate, tie, 0.0))
            rank = rank + jnp.where(m < i, cnt, 0.0)
        sel_scr[h] = jnp.where((rown < i) & (rank < MOBA_TOPK), 1.0, 0.0)
        qb = (qm * scale).astype(BF16)
        qb_scr[h] = qb
        s = lax.dot_general(kbf_scr[pl.ds(r0, blk), cols], qb, nt_dims,
                            preferred_element_type=F32) + d0_scr[h]
        s = jnp.where(key <= qry, s, NEG_INF)
        m_run = jnp.max(s, axis=0, keepdims=True)
        p = jnp.exp(s - m_run)
        m_scr[h] = m_run
        l_scr[h] = jnp.sum(p, axis=0, keepdims=True)
        acc_scr[h] = jnp.dot(vt_scr[i, cols, :], p.astype(BF16), preferred_element_type=F32)

    def merge_block(n, h, cols, s, shift):
        picked = sel_scr[h, pl.ds(n, 1), :] > 0.5
        m_run = m_scr[h]
        m_new = jnp.maximum(m_run, jnp.where(picked, jnp.max(s, axis=0, keepdims=True) + shift, NEG_INF))
        alpha = jnp.exp(m_run - m_new)
        p = jnp.exp(s - jnp.where(picked, m_new - shift, jnp.inf))
        m_scr[h] = m_new
        l_scr[h] = alpha * l_scr[h] + jnp.sum(p, axis=0, keepdims=True)
        acc_scr[h] = alpha * acc_scr[h] + jnp.dot(vt_scr[n, cols, :], p.astype(BF16), preferred_element_type=F32)

    @pl.when(i > 0)
    def _():
        rr = pl.multiple_of((i - 1) * blk, blk)
        for h, cols in heads:
            s = lax.dot_general(kbf_scr[pl.ds(rr, blk), cols], qb_scr[h], nt_dims,
                                preferred_element_type=F32) + d1_scr[h]
            merge_block(i - 1, h, cols, s, 0.0)

    def body(n, carry):
        rr = pl.multiple_of(n * blk, blk)
        for h, cols in heads:
            s = lax.dot_general(kbf_scr[pl.ds(rr, blk), cols], qb_scr[h], nt_dims, preferred_element_type=F32)
            merge_block(n, h, cols, s, tab_ref[h, NUM_BUCKETS - 1])
        return carry

    lax.fori_loop(0, i - 1, body, 0)

    outs = []
    for pair in range(MOBA_HEADS // 2):
        o0 = acc_scr[2 * pair] / l_scr[2 * pair]
        o1 = acc_scr[2 * pair + 1] / l_scr[2 * pair + 1]
        outs.append(jnp.where(rowd < hd, o0, o1).T)
    o_ref[...] = jnp.concatenate(outs, axis=1).astype(o_ref.dtype)


def moba_prompt(p, tab, *, n_batch, t_len, out_dtype):
    n_blk = t_len // MOBA_BLOCK
    return pl.pallas_call(
        functools.partial(_moba_prompt_kernel, n_blk=n_blk, thr=_bucket_thresholds()),
        out_shape=jax.ShapeDtypeStruct((n_batch * t_len, MOBA_WIDTH), out_dtype),
        grid=(n_batch, n_blk),
        in_specs=[pl.BlockSpec(memory_space=pltpu.SMEM),
                  pl.BlockSpec((MOBA_BLOCK, MOBA_WIDTH), lambda b, i: (b * n_blk + i, 2)),
                  pl.BlockSpec((t_len, MOBA_WIDTH), lambda b, i: (b, 3)),
                  pl.BlockSpec((t_len, MOBA_WIDTH), lambda b, i: (b, 4))],
        out_specs=pl.BlockSpec((MOBA_BLOCK, MOBA_WIDTH), lambda b, i: (b * n_blk + i, 0)),
        scratch_shapes=[pltpu.VMEM((n_blk, MOBA_WIDTH), F32),
                        pltpu.VMEM((t_len, MOBA_WIDTH), BF16),
                        pltpu.VMEM((n_blk, MOBA_WIDTH, MOBA_BLOCK), BF16),
                        pltpu.VMEM((MOBA_HEADS, MOBA_BLOCK, MOBA_BLOCK), F32),
                        pltpu.VMEM((MOBA_HEADS, MOBA_BLOCK, MOBA_BLOCK), F32),
                        pltpu.VMEM((MOBA_HEADS, n_blk, MOBA_BLOCK), F32),
                        pltpu.VMEM((MOBA_HEADS, MOBA_BLOCK, LANES), BF16),
                        pltpu.VMEM((MOBA_HEADS, 1, MOBA_BLOCK), F32),
                        pltpu.VMEM((MOBA_HEADS, 1, MOBA_BLOCK), F32),
                        pltpu.VMEM((MOBA_HEADS, LANES, MOBA_BLOCK), F32)],
        compiler_params=_cparams(("arbitrary", "arbitrary")),
        name="moba_prompt",
    )(tab, p, p, p)


def _moba_sample_kernel(pt_ref, *refs, n_blk, nbs, page, t_len, thr):
    npg = 2 * nbs
    k_refs = refs[:npg]
    v_refs = refs[npg:2 * npg]
    q_ref, kn_ref, vnt_ref, tab_ref, tabt_ref, o_ref, g_scr, m_scr, l_scr, o_scr, bl_scr = refs[2 * npg:]
    n = pl.program_id(1)
    nh, hd = MOBA_HEADS, MOBA_HEAD_DIM
    nrow = nh * t_len
    scale = hd ** -0.5
    lane = lax.broadcasted_iota(I32, (nrow, LANES), 1)
    row = lax.broadcasted_iota(I32, (nrow, LANES), 0)
    col_head = lax.broadcasted_iota(I32, (LANES, nrow), 1) // t_len
    tab = tab_ref[...]

    def tab_at(k):
        return tab[:, k:k + 1]

    @pl.when(n == 0)
    def _():
        g_scr[...] = jnp.zeros(g_scr.shape, F32)
        m_scr[...] = jnp.zeros(m_scr.shape, F32)
        l_scr[...] = jnp.zeros(l_scr.shape, F32)
        for half in range(2):
            bl_scr[half] = _bias_chain(MOBA_BLOCK + row % t_len - (half * page + lane), tab_at, thr)

    q = q_ref[0]
    qb = [(q[h] * scale).astype(BF16) for h in range(nh)]
    far = tab_at(NUM_BUCKETS - 1)
    for j in range(nbs):
        blk_idx = n * nbs + j
        at_blk = lane == blk_idx
        newest = blk_idx == n_blk - 1
        gsum = jnp.zeros((nrow, 1), F32)
        for half in range(2):
            kt = k_refs[2 * j + half][...].astype(BF16)
            vt = v_refs[2 * j + half][...].astype(BF16)
            s_raw = jnp.concatenate([jnp.dot(qb[h], kt[h], preferred_element_type=F32) for h in range(nh)], axis=0)
            gsum = gsum + jnp.sum(s_raw, axis=1, keepdims=True)
            s = s_raw + jnp.where(newest, bl_scr[half], far)
            m = jnp.max(s, axis=1, keepdims=True)
            p = jnp.exp(s - m)
            l = jnp.sum(p, axis=1, keepdims=True)
            pt = p.T
            o_t = jnp.zeros((hd, nrow), F32)
            for h in range(nh):
                o_t = o_t + jnp.dot(vt[h], jnp.where(col_head == h, pt, 0.0).astype(BF16),
                                    preferred_element_type=F32)
            m_scr[half] = jnp.where(at_blk, m, m_scr[half])
            l_scr[half] = jnp.where(at_blk, l, l_scr[half])
            o_scr[half, blk_idx] = o_t
        g_scr[...] = jnp.where(at_blk, gsum * (1.0 / (MOBA_BLOCK * scale)), g_scr[...])

    @pl.when(n == n_blk // nbs - 1)
    def _():
        gate = g_scr[...].T[:n_blk]
        rown = lax.broadcasted_iota(I32, (n_blk, nrow), 0)
        rank = jnp.zeros((n_blk, nrow), F32)
        for mm in range(n_blk):
            gm = gate[mm:mm + 1, :]
            tie = jnp.where(rown > mm, 1.0, 0.0)
            rank = rank + jnp.where(gm > gate, 1.0, jnp.where(gm == gate, tie, 0.0))
        sel = rank < MOBA_TOPK

        kn = kn_ref[0]
        vnt = vnt_ref[0]
        ch = col_head[:hd, :]
        sk = lax.broadcasted_iota(I32, (t_len, nrow), 0)
        tq = lax.broadcasted_iota(I32, (t_len, nrow), 1) % t_len
        tab_t = tabt_ref[...]
        s_rows = jnp.concatenate(
            [lax.dot_general(qb[h], kn[h].astype(BF16), (((1,), (1,)), ((), ())), preferred_element_type=F32)
             for h in range(nh)], axis=0)
        s_own = jnp.concatenate([s_rows, jnp.zeros((nrow, LANES - t_len), F32)], axis=1).T[:t_len]
        s_own = s_own + _bias_chain(tq - sk, lambda k: tab_t[k:k + 1, :], thr)
        s_own = jnp.where(sk <= tq, s_own, NEG_INF)
        m_o = jnp.max(s_own, axis=0, keepdims=True)
        p_o = jnp.exp(s_own - m_o)
        l_o = jnp.sum(p_o, axis=0, keepdims=True)
        o_o = jnp.zeros((hd, nrow), F32)
        for h in range(nh):
            o_o = o_o + jnp.dot(vnt[h].astype(BF16), jnp.where(ch[:t_len] == h, p_o, 0.0).astype(BF16),
                                preferred_element_type=F32)

        m0 = jnp.where(sel, m_scr[0].T[:n_blk], NEG_INF)
        m1 = jnp.where(sel, m_scr[1].T[:n_blk], NEG_INF)
        m_fin = jnp.maximum(jnp.max(jnp.maximum(m0, m1), axis=0, keepdims=True), m_o)
        w0 = jnp.where(sel, jnp.exp(m0 - m_fin), 0.0)
        w1 = jnp.where(sel, jnp.exp(m1 - m_fin), 0.0)
        w_o = jnp.exp(m_o - m_fin)
        l_fin = (jnp.sum(w0 * l_scr[0].T[:n_blk] + w1 * l_scr[1].T[:n_blk], axis=0, keepdims=True)
                 + w_o * l_o)
        acc = w_o * o_o
        for nn in range(n_blk):
            acc = acc + w0[nn:nn + 1, :] * o_scr[0, nn] + w1[nn:nn + 1, :] * o_scr[1, nn]
        o_ref[0] = acc / l_fin


MOBA_SAMPLE_BLOCKS_PER_STEP = 2


def moba_sample(page_table, cache_kt, cache_vt, q, kn, vn, tab_heads, *, layer_e, t_len):
    n_b, n_pages = page_table.shape
    page = cache_kt.shape[-1]
    nbs = MOBA_SAMPLE_BLOCKS_PER_STEP
    ppb = MOBA_BLOCK // page
    assert ppb == 2 and page == LANES and n_pages % (ppb * nbs) == 0
    n_blk = n_pages // ppb
    assert n_blk <= LANES
    nh, hd = MOBA_HEADS, MOBA_HEAD_DIM
    nrow = nh * t_len
    vnt = vn.transpose(0, 1, 3, 2)
    tab = jnp.pad(jnp.repeat(tab_heads, t_len, axis=0), ((0, 0), (0, LANES - NUM_BUCKETS)))
    tab_t = jnp.repeat(tab_heads.T, t_len, axis=1)

    def pspec(p):
        return pl.BlockSpec((None, None, nh, hd, page),
                            lambda b, n, pt: (pt[b, n * ppb * nbs + p], layer_e, 0, 0, 0))

    pages = [pspec(p) for p in range(ppb * nbs)]
    grid_spec = pltpu.PrefetchScalarGridSpec(
        num_scalar_prefetch=1,
        grid=(n_b, n_blk // nbs),
        in_specs=pages + pages + [
            pl.BlockSpec((1, nh, t_len, hd), lambda b, n, pt: (b, 0, 0, 0)),
            pl.BlockSpec((1, nh, t_len, hd), lambda b, n, pt: (b, 0, 0, 0)),
            pl.BlockSpec((1, nh, hd, t_len), lambda b, n, pt: (b, 0, 0, 0)),
            pl.BlockSpec(tab.shape, lambda b, n, pt: (0, 0)),
            pl.BlockSpec(tab_t.shape, lambda b, n, pt: (0, 0))],
        out_specs=pl.BlockSpec((1, hd, nrow), lambda b, n, pt: (b, 0, 0)),
        scratch_shapes=[pltpu.VMEM((nrow, LANES), F32), pltpu.VMEM((2, nrow, LANES), F32),
                        pltpu.VMEM((2, nrow, LANES), F32), pltpu.VMEM((2, n_blk, hd, nrow), F32),
                        pltpu.VMEM((2, nrow, LANES), F32)],
    )
    return pl.pallas_call(
        functools.partial(_moba_sample_kernel, n_blk=n_blk, nbs=nbs, page=page, t_len=t_len,
                          thr=_bucket_thresholds()),
        out_shape=jax.ShapeDtypeStruct((n_b, hd, nrow), F32),
        grid_spec=grid_spec,
        compiler_params=_cparams(("parallel", "arbitrary")),
        name="moba_sample",
    )(page_table, *([cache_kt] * (ppb * nbs)), *([cache_vt] * (ppb * nbs)), q, kn, vnt, tab, tab_t)


def _retention_kernel(q_ref, k_ref, v_ref, g_ref, cos_ref, sin_ref, dmask_ref, xi_ref, zeta_ref, gc_ref,
                      gng_ref, gnb_ref, *rest, nb, chunk, has_s0):
    if has_s0:
        s0_ref, o_ref, s_out_ref, s_scr = rest
    else:
        o_ref, s_out_ref, s_scr = rest
    c = pl.program_id(1)
    dk, dv = RET_DK, RET_DV
    half = dk // 2

    @pl.when(c == 0)
    def _():
        if has_s0:
            s_scr[...] = s0_ref[...]
        else:
            s_scr[...] = jnp.zeros(s_scr.shape, F32)

    cos = cos_ref[...]
    sin = sin_ref[...]

    def rot(x):
        x1 = x[:, :half]
        x2 = x[:, half:]
        return jnp.concatenate([x1 * cos - x2 * sin, x2 * cos + x1 * sin], axis=1)

    out_rows = []
    for bb in range(nb):
        rows = pl.ds(bb * chunk, chunk)
        out_heads = []
        for h in range(RET_HEADS):
            qh = rot(q_ref[rows, h * dk:(h + 1) * dk])
            kh = rot(k_ref[rows, h * dk:(h + 1) * dk]) * (dk ** -0.5)
            vh = v_ref[rows, h * dv:(h + 1) * dv]
            s = s_scr[bb, h]
            att = _bdot_nt(qh, kh) * dmask_ref[h]
            o = _bdot(att, vh) + _bdot(qh, s) * xi_ref[h]
            s_scr[bb, h] = s * gc_ref[h, 0:1, 0:1] + _bdot_tn(kh * zeta_ref[h], vh)
            mu = jnp.mean(o, axis=-1, keepdims=True)
            oc = o - mu
            var = jnp.mean(oc * oc, axis=-1, keepdims=True)
            on = oc * lax.rsqrt(var + LN_EPS) * gng_ref[:, h * dv:(h + 1) * dv] + gnb_ref[:, h * dv:(h + 1) * dv]
            gate = _silu(g_ref[rows, h * dv:(h + 1) * dv])
            out_heads.append(gate * on)
        out_rows.append(jnp.concatenate(out_heads, axis=1))
    o_ref[...] = jnp.concatenate(out_rows, axis=0).astype(o_ref.dtype)

    @pl.when(c == pl.num_programs(1) - 1)
    def _():
        s_out_ref[...] = s_scr[...]


def retention(p, cos, sin, consts, gn_g, gn_b, s0, *, row0, n_batch, t_len, nb, out_dtype):
    chunk = math.gcd(t_len, RET_CHUNK)
    n_chunk = t_len // chunk
    dmask, xi, zeta, gc = consts
    blk_rows = nb * chunk
    if nb > 1:
        assert n_chunk == 1
    assert row0 % blk_rows == 0
    base = row0 // blk_rows
    qk_w = RET_HEADS * RET_DK
    v_w = RET_HEADS * RET_DV
    has_s0 = s0 is not None

    def rmap(col):
        return lambda b, c: (base + b * n_chunk + c, col)

    in_specs = [pl.BlockSpec((blk_rows, qk_w), rmap(0)),
                pl.BlockSpec((blk_rows, qk_w), rmap(1)),
                pl.BlockSpec((blk_rows, v_w), rmap(1)),
                pl.BlockSpec((blk_rows, v_w), rmap(2)),
                pl.BlockSpec((chunk, RET_DK // 2), lambda b, c: (c, 0)),
                pl.BlockSpec((chunk, RET_DK // 2), lambda b, c: (c, 0)),
                pl.BlockSpec(dmask.shape, lambda b, c: (0, 0, 0)),
                pl.BlockSpec(xi.shape, lambda b, c: (0, 0, 0)),
                pl.BlockSpec(zeta.shape, lambda b, c: (0, 0, 0)),
                pl.BlockSpec(gc.shape, lambda b, c: (0, 0, 0)),
                pl.BlockSpec((1, v_w), lambda b, c: (0, 0)),
                pl.BlockSpec((1, v_w), lambda b, c: (0, 0))]
    args = [p, p, p, p, cos, sin, dmask, xi, zeta, gc, gn_g.reshape(1, -1), gn_b.reshape(1, -1)]
    s_spec = pl.BlockSpec((nb, RET_HEADS, RET_DK, RET_DV), lambda b, c: (b, 0, 0, 0))
    if has_s0:
        in_specs.append(s_spec)
        args.append(s0)
    o, s_out = pl.pallas_call(
        functools.partial(_retention_kernel, nb=nb, chunk=chunk, has_s0=has_s0),
        out_shape=(jax.ShapeDtypeStruct((n_batch * t_len, v_w), out_dtype),
                   jax.ShapeDtypeStruct((n_batch, RET_HEADS, RET_DK, RET_DV), F32)),
        grid=(n_batch // nb, n_chunk),
        in_specs=in_specs,
        out_specs=(pl.BlockSpec((blk_rows, v_w), lambda b, c: (b * n_chunk + c, 0)), s_spec),
        scratch_shapes=[pltpu.VMEM((nb, RET_HEADS, RET_DK, RET_DV), F32)],
        compiler_params=_cparams(("parallel", "arbitrary")),
        name="retention",
    )(*args)
    return o, s_out


def _retention_consts(chunk):
    h = jnp.arange(RET_HEADS, dtype=F32)
    lg = jnp.log1p(-jnp.exp2(-5.0 - h))
    idx = jnp.arange(chunk, dtype=F32)
    diff = idx[:, None] - idx[None, :]
    dmask = jnp.where(diff >= 0, jnp.exp(lg[:, None, None] * jnp.maximum(diff, 0.0)), 0.0)
    xi = jnp.exp(lg[:, None] * (idx[None, :] + 1.0))
    zeta = jnp.exp(lg[:, None] * (chunk - 1.0 - idx[None, :]))
    g_c = jnp.exp(lg * chunk)
    return (dmask,
            jnp.broadcast_to(xi[:, :, None], (RET_HEADS, chunk, RET_DV)),
            jnp.broadcast_to(zeta[:, :, None], (RET_HEADS, chunk, RET_DK)),
            jnp.broadcast_to(g_c[:, None, None], (RET_HEADS, 8, LANES)))


def _rotary_tables(pos):
    half = RET_DK // 2
    inv = 10000.0 ** (-jnp.linspace(0.0, 1.0, half, dtype=F32))
    ang = pos.astype(F32)[:, None] * inv[None, :]
    return jnp.cos(ang), jnp.sin(ang)


def _mem_attn_kernel(q_ref, mk_ref, mv_ref, o_ref, *, nb, tq):
    hd = MEM_HEAD_DIM
    scale = hd ** -0.5
    qf = q_ref[...].astype(F32)
    out_rows = []
    for bb in range(nb):
        out_heads = []
        for h in range(MEM_HEADS):
            cols = slice(h * hd, (h + 1) * hd)
            q = qf[bb * tq:(bb + 1) * tq, cols]
            s = _bdot_nt(q, mk_ref[bb, :, cols]) * scale
            m = jnp.max(s, axis=-1, keepdims=True)
            p = jnp.exp(s - m)
            p = p / jnp.sum(p, axis=-1, keepdims=True)
            out_heads.append(_bdot(p, mv_ref[bb, :, cols]))
        out_rows.append(jnp.concatenate(out_heads, axis=1))
    o_ref[...] = jnp.concatenate(out_rows, axis=0).astype(o_ref.dtype)


def mem_attention(q, mk, mv, *, row0, n_batch, t_len, nb, tq, out_dtype):
    n_t = t_len // tq
    blk_rows = nb * tq
    if nb > 1:
        assert n_t == 1
    assert row0 % blk_rows == 0
    base = row0 // blk_rows
    m_tok = mk.shape[1]
    return pl.pallas_call(
        functools.partial(_mem_attn_kernel, nb=nb, tq=tq),
        out_shape=jax.ShapeDtypeStruct((n_batch * t_len, D_MODEL), out_dtype),
        grid=(n_batch // nb, n_t),
        in_specs=[pl.BlockSpec((blk_rows, D_MODEL), lambda b, t: (base + b * n_t + t, 0)),
                  pl.BlockSpec((nb, m_tok, D_MODEL), lambda b, t: (b, 0, 0)),
                  pl.BlockSpec((nb, m_tok, D_MODEL), lambda b, t: (b, 0, 0))],
        out_specs=pl.BlockSpec((blk_rows, D_MODEL), lambda b, t: (b * n_t + t, 0)),
        compiler_params=_cparams(("parallel", "arbitrary")),
        name="mem_attention",
    )(q, mk, mv)


def _router_kernel(x_ref, w_ref, b_ref, tri_ref, wt_ref, ei_ref, cnt_ref, base_scr):
    @pl.when(pl.program_id(0) == 0)
    def _():
        base_scr[...] = jnp.zeros(base_scr.shape, F32)

    x = x_ref[...]
    logits = _dot_hi(x, w_ref[...]) + b_ref[...]
    tm = x.shape[0]
    lane = lax.broadcasted_iota(I32, (tm, LANES), 1)
    big = jnp.int32(LANES)

    def masked_softmax(mask):
        lm = jnp.where(mask, logits, NEG_INF)
        mx = jnp.max(lm, axis=-1, keepdims=True)
        e = jnp.exp(lm - mx)
        return e / jnp.sum(e, axis=-1, keepdims=True)

    g_prob = masked_softmax(lane < MOE_GROUPS)
    gp = jnp.max(g_prob, axis=-1, keepdims=True)
    gi = jnp.min(jnp.where(g_prob == gp, lane, big), axis=-1, keepdims=True)
    lo = MOE_GROUPS + gi * MOE_EPG
    in_group = (lane >= lo) & (lane < lo + MOE_EPG)
    e_prob = jnp.where(in_group, masked_softmax(in_group), -1.0)
    p1 = jnp.max(e_prob, axis=-1, keepdims=True)
    i1 = jnp.min(jnp.where(e_prob == p1, lane, big), axis=-1, keepdims=True)
    rest = jnp.where(lane == i1, -1.0, e_prob)
    p2 = jnp.max(rest, axis=-1, keepdims=True)
    i2 = jnp.min(jnp.where(rest == p2, lane, big), axis=-1, keepdims=True)
    tot = p1 + p2
    w1 = gp * (p1 / tot)
    w2 = gp * (p2 / tot)
    e1 = i1 - MOE_GROUPS
    e2 = i2 - MOE_GROUPS
    oh1 = jnp.where(lane == e1, 1.0, 0.0)
    oh2 = jnp.where(lane == e2, 1.0, 0.0)
    both = oh1 + oh2
    base = base_scr[0:1, :]
    before = jnp.dot(tri_ref[...], both.astype(BF16), preferred_element_type=F32) + base
    r1 = jnp.sum(oh1 * before, axis=-1, keepdims=True).astype(I32)
    r2 = jnp.sum(oh2 * before, axis=-1, keepdims=True).astype(I32)
    base_new = base + jnp.sum(both, axis=0, keepdims=True)
    base_scr[0:1, :] = base_new
    wt_ref[...] = jnp.where(lane == 0, w1, jnp.where(lane == 1, w2, 0.0))
    ei_ref[...] = jnp.where(lane == 0, e1, jnp.where(lane == 1, e2, jnp.where(lane == 2, r1,
                                                                              jnp.where(lane == 3, r2, 0))))
    cnt_ref[...] = jnp.broadcast_to(base_new, cnt_ref.shape)


def moe_router(x, w_all, b_all, *, tm):
    m, d = x.shape
    tri = jnp.asarray(np.tril(np.ones((tm, tm), np.float32), -1), BF16)
    return pl.pallas_call(
        _router_kernel,
        out_shape=(jax.ShapeDtypeStruct((m, LANES), F32), jax.ShapeDtypeStruct((m, LANES), I32),
                   jax.ShapeDtypeStruct((8, LANES), F32)),
        grid=(m // tm,),
        in_specs=[pl.BlockSpec((tm, d), lambda i: (i, 0)),
                  pl.BlockSpec((d, LANES), lambda i: (0, 0)),
                  pl.BlockSpec((1, LANES), lambda i: (0, 0)),
                  pl.BlockSpec((tm, tm), lambda i: (0, 0))],
        out_specs=(pl.BlockSpec((tm, LANES), lambda i: (i, 0)), pl.BlockSpec((tm, LANES), lambda i: (i, 0)),
                   pl.BlockSpec((8, LANES), lambda i: (0, 0))),
        scratch_shapes=[pltpu.VMEM((8, LANES), F32)],
        compiler_params=_cparams(("arbitrary",)),
        name="moe_router",
    )(x, w_all, b_all, tri)


def _moe_dispatch_kernel(dest_ref, x_ref, xs_in, xs_out, sem, *, tm):
    del xs_in

    def body(r, carry):
        _tile_copy(x_ref, r, xs_out, dest_ref[r], sem.at[0]).start()
        _tile_copy(x_ref, r, xs_out, dest_ref[tm + r], sem.at[0]).start()
        return carry

    lax.fori_loop(0, tm, body, 0, unroll=8)
    for _ in range(MOE_TOPK):
        _wait_tiles(x_ref, xs_out, 0, sem.at[0], tm)


def moe_dispatch(x8, dest, n_slots, *, tm):
    n_t = dest.shape[0]
    xs0 = jnp.zeros((n_slots * ROW_TILE, LANES), F32)
    return pl.pallas_call(
        functools.partial(_moe_dispatch_kernel, tm=tm),
        out_shape=jax.ShapeDtypeStruct(xs0.shape, F32),
        grid=(n_t,),
        in_specs=[pl.BlockSpec((None, None, 2 * tm), lambda i: (i, 0, 0), memory_space=pltpu.SMEM),
                  pl.BlockSpec((tm * ROW_TILE, LANES), lambda i: (i, 0)),
                  pl.BlockSpec(memory_space=pl.ANY)],
        out_specs=pl.BlockSpec(memory_space=pl.ANY),
        scratch_shapes=[pltpu.SemaphoreType.DMA((1,))],
        input_output_aliases={2: 0},
        compiler_params=_cparams(("arbitrary",)),
        name="moe_dispatch",
    )(dest.reshape(n_t, 1, 2 * tm), x8, xs0)


def _moe_ffn_kernel(te_ref, nt_ref, xs_ref, w1_ref, w3_ref, w2_ref, ys_ref):
    j = pl.program_id(0)

    @pl.when(j < nt_ref[0])
    def _():
        x = _load_token_tiles(xs_ref, 0, MOE_TILE).astype(BF16)
        h = _silu(_bdot(x, w1_ref[0])) * _bdot(x, w3_ref[0])
        _store_token_tiles(ys_ref, _bdot(h, w2_ref[0]))

    @pl.when(j >= nt_ref[0])
    def _():
        ys_ref[...] = jnp.zeros(ys_ref.shape, F32)


def moe_ffn(xs8, tile_expert, n_tiles, w1, w3, w2, *, layer):
    n_t = tile_expert.shape[0]
    d, dff = w1.shape[2], w1.shape[3]
    blk = MOE_TILE * ROW_TILE
    grid_spec = pltpu.PrefetchScalarGridSpec(
        num_scalar_prefetch=2,
        grid=(n_t,),
        in_specs=[pl.BlockSpec((blk, LANES), lambda j, te, nt: (jnp.minimum(j, nt[0] - 1), 0)),
                  pl.BlockSpec((None, 1, d, dff), lambda j, te, nt: (layer, te[j], 0, 0)),
                  pl.BlockSpec((None, 1, d, dff), lambda j, te, nt: (layer, te[j], 0, 0)),
                  pl.BlockSpec((None, 1, dff, d), lambda j, te, nt: (layer, te[j], 0, 0))],
        out_specs=pl.BlockSpec((blk, LANES), lambda j, te, nt: (j, 0)),
    )
    return pl.pallas_call(
        _moe_ffn_kernel,
        out_shape=jax.ShapeDtypeStruct(xs8.shape, F32),
        grid_spec=grid_spec,
        compiler_params=_cparams(("arbitrary",)),
        name="moe_ffn",
    )(tile_expert, n_tiles, xs8, w1, w3, w2)


def _moe_combine_kernel(d_cur, d_nxt, ys_hbm, x_ref, wt_ref, g_ref, b_ref, o_ref, ybuf, sem, *, tm):
    i = pl.program_id(0)
    n = pl.num_programs(0)
    slot = i % 2

    def issue(d_ref, s):
        def body(r, carry):
            _tile_copy(ys_hbm, d_ref[r], ybuf, s * 2 * tm + r, sem.at[s]).start()
            return carry
        lax.fori_loop(0, 2 * tm, body, 0, unroll=8)

    @pl.when(i == 0)
    def _():
        issue(d_cur, 0)

    @pl.when(i + 1 < n)
    def _():
        issue(d_nxt, 1 - slot)

    base_tok = slot * 2 * tm
    _wait_tiles(ys_hbm, ybuf, base_tok, sem.at[slot], 2 * tm)
    wt = wt_ref[...]
    y0 = _load_token_tiles(ybuf, base_tok * ROW_TILE, tm)
    y1 = _load_token_tiles(ybuf, (base_tok + tm) * ROW_TILE, tm)
    y = DEEPNORM_ALPHA * x_ref[...] + (wt[:, 0:1] * y0 + wt[:, 1:2] * y1)
    o_ref[...] = _layer_norm(y, g_ref[...], b_ref[...])


def moe_combine_deepnorm(ys8, dest, x, wts, g, b, *, tm):
    m, d = x.shape
    n_t = m // tm
    dest3 = dest.reshape(n_t, 1, 2 * tm)
    return pl.pallas_call(
        functools.partial(_moe_combine_kernel, tm=tm),
        out_shape=jax.ShapeDtypeStruct((m, d), F32),
        grid=(n_t,),
        in_specs=[pl.BlockSpec((None, None, 2 * tm), lambda i: (i, 0, 0), memory_space=pltpu.SMEM),
                  pl.BlockSpec((None, None, 2 * tm), lambda i: (jnp.minimum(i + 1, n_t - 1), 0, 0),
                               memory_space=pltpu.SMEM),
                  pl.BlockSpec(memory_space=pl.ANY),
                  pl.BlockSpec((tm, d), lambda i: (i, 0)),
                  pl.BlockSpec((tm, LANES), lambda i: (i, 0)),
                  pl.BlockSpec((1, d), lambda i: (0, 0)),
                  pl.BlockSpec((1, d), lambda i: (0, 0))],
        out_specs=pl.BlockSpec((tm, d), lambda i: (i, 0)),
        scratch_shapes=[pltpu.VMEM((2 * 2 * tm * ROW_TILE, LANES), F32), pltpu.SemaphoreType.DMA((2,))],
        compiler_params=_cparams(("arbitrary",)),
        name="moe_combine_deepnorm",
    )(dest3, dest3, ys8, x, wts, g.reshape(1, d), b.reshape(1, d))


def _moe_plan(eidx, rank, counts, n_tok, tm):
    n_tiles_max = n_tok * MOE_TOPK // MOE_TILE + MOE_EXPERTS
    tiles_per = (counts + MOE_TILE - 1) // MOE_TILE
    tile_end = jnp.cumsum(tiles_per)
    pad_off = (tile_end - tiles_per) * MOE_TILE
    onehot = eidx[:, :, None] == jnp.arange(MOE_EXPERTS, dtype=I32)[None, None, :]
    dest = jnp.sum(jnp.where(onehot, pad_off[None, None, :], 0), axis=-1) + rank
    tile_expert = jnp.minimum(
        jnp.sum((jnp.arange(n_tiles_max, dtype=I32)[:, None] >= tile_end[None, :]).astype(I32), axis=1),
        MOE_EXPERTS - 1)
    dest = dest.reshape(n_tok // tm, tm, MOE_TOPK).transpose(0, 2, 1).reshape(n_tok // tm, MOE_TOPK * tm)
    return tile_expert, tile_end[-1:].astype(I32), dest, n_tiles_max * MOE_TILE


def hier_moe_deepnorm(x, x8, w_all, b_all, w1, w3, w2, g, b, *, layer):
    n_tok = x.shape[0]
    wts, ei, cnt = moe_router(x, w_all, b_all, tm=640)
    counts = cnt[0, :MOE_EXPERTS].astype(I32)
    tile_expert, n_tiles, dest, n_slots = _moe_plan(ei[:, 0:2], ei[:, 2:4], counts, n_tok, MOE_TILE)
    xs8 = moe_dispatch(x8, dest, n_slots, tm=MOE_TILE)
    ys8 = moe_ffn(xs8, tile_expert, n_tiles, w1, w3, w2, layer=layer)
    return moe_combine_deepnorm(ys8, dest, x, wts, g, b, tm=MOE_TILE)


def kernel(x_prompt, x_sample, cache_moba_k, cache_moba_v, state_conv, state_ret, cache_mem_k, cache_mem_v,
           page_table, mem_prompt, rel_bias, ev_w_in, ev_conv_w, ev_conv_b, ev_conv_ln_g, ev_conv_ln_b,
           ev_w_out, od_w_in, od_gn_g, od_gn_b, od_w_out, mem_wq, mem_wk, mem_wv, mem_wo, ln_g, ln_b,
           moe_w_group, moe_b_group, moe_w_router, moe_b_router, moe_w1, moe_w3, moe_w2):
    bp, tp, d = x_prompt.shape
    bs, ts, _ = x_sample.shape
    n_p = bp * tp
    n_s = bs * ts
    n_all = n_p + n_s
    page = cache_moba_k.shape[1]
    past_len = page_table.shape[1] * page
    nh, hd = MOBA_HEADS, MOBA_HEAD_DIM
    assert tp % MOBA_BLOCK == 0 and past_len % MOBA_BLOCK == 0 and ts <= MOBA_BLOCK
    assert MOBA_BLOCK >= MAX_DISTANCE
    tm = 640
    assert n_all % tm == 0

    x = jnp.concatenate([x_prompt.reshape(n_p, d), x_sample.reshape(n_s, d)], axis=0)

    tab_heads = rel_bias.T
    cache_kt = cache_moba_k.transpose(0, 2, 3, 4, 1)
    cache_vt = cache_moba_v.transpose(0, 2, 3, 4, 1)

    pos_p = jnp.arange(tp, dtype=I32)
    pos_s = past_len + jnp.arange(ts, dtype=I32)
    cos_p, sin_p = _rotary_tables(pos_p)
    cos_s, sin_s = _rotary_tables(pos_s)
    ret_c_p = _retention_consts(math.gcd(tp, RET_CHUNK))
    ret_c_s = _retention_consts(math.gcd(ts, RET_CHUNK))

    outs = {}
    for layer in range(DEPTH):
        if layer % 2 == 0:
            e = layer // 2
            proj = matmul(x, ev_w_in[e].astype(BF16), tm=tm, tn=ev_w_in.shape[2])
            k_new = proj[:, 2 * CONV_CH + MOBA_WIDTH:2 * CONV_CH + 2 * MOBA_WIDTH]
            v_new = proj[:, 2 * CONV_CH + 2 * MOBA_WIDTH:]
            q_s = proj[n_p:, 2 * CONV_CH:2 * CONV_CH + MOBA_WIDTH]
            conv_args = (ev_conv_w[e], ev_conv_b[e], ev_conv_ln_g[e], ev_conv_ln_b[e])
            c_p, cst_p = conv_module(proj, jnp.zeros((bp, CONV_WIDTH - 1, CONV_CH), F32), *conv_args,
                                     row0=0, n_batch=bp, t_len=tp, nb=1, out_dtype=BF16)
            c_s, cst_s = conv_module(proj, state_conv[e], *conv_args,
                                     row0=n_p, n_batch=bs, t_len=ts, nb=bs, out_dtype=F32)
            a_p = moba_prompt(proj, tab_heads, n_batch=bp, t_len=tp, out_dtype=BF16)

            def by_head(a):
                return a.reshape(bs, ts, nh, hd).transpose(0, 2, 1, 3)

            a_s = moba_sample(page_table, cache_kt, cache_vt, by_head(q_s), by_head(k_new[n_p:]),
                              by_head(v_new[n_p:]), tab_heads, layer_e=e, t_len=ts)
            a_s = a_s.reshape(bs, hd, nh, ts).transpose(0, 3, 2, 1).reshape(n_s, MOBA_WIDTH).astype(BF16)
            c_all = jnp.concatenate([c_p, c_s.astype(BF16)], axis=0)
            a_all = jnp.concatenate([a_p, a_s], axis=0)
            w_out = ev_w_out[e].astype(BF16)
            x = matmul_deepnorm([c_all, a_all], [w_out[:CONV_CH], w_out[CONV_CH:]], x,
                                ln_g[layer, 0], ln_b[layer, 0], tm=tm)
            outs.setdefault("kp", []).append(k_new[:n_p].reshape(bp, tp, nh, hd))
            outs.setdefault("vp", []).append(v_new[:n_p].reshape(bp, tp, nh, hd))
            outs.setdefault("ks", []).append(k_new[n_p:].reshape(bs, ts, nh, hd))
            outs.setdefault("vs", []).append(v_new[n_p:].reshape(bs, ts, nh, hd))
            outs.setdefault("cp", []).append(cst_p)
            outs.setdefault("cs", []).append(cst_s)
        else:
            o = layer // 2
            proj = matmul(x, od_w_in[o].astype(BF16), tm=tm, tn=2048)
            r_p, st_p = retention(proj, cos_p, sin_p, ret_c_p, od_gn_g[o], od_gn_b[o], None,
                                  row0=0, n_batch=bp, t_len=tp, nb=1, out_dtype=BF16)
            r_s, st_s = retention(proj, cos_s, sin_s, ret_c_s, od_gn_g[o], od_gn_b[o], state_ret[o],
                                  row0=n_p, n_batch=bs, t_len=ts, nb=2, out_dtype=BF16)
            r_all = jnp.concatenate([r_p, r_s], axis=0)
            x = matmul_deepnorm([r_all], [od_w_out[o].astype(BF16)], x, ln_g[layer, 0], ln_b[layer, 0], tm=tm)
            outs.setdefault("sp", []).append(st_p)
            outs.setdefault("ss", []).append(st_s)

        m_tok = mem_prompt.shape[1]
        mem2 = mem_prompt.reshape(bp * m_tok, d)
        mk_p = matmul(mem2, mem_wk[layer].astype(BF16), tm=512, tn=d)
        mv_p = matmul(mem2, mem_wv[layer].astype(BF16), tm=512, tn=d)
        outs.setdefault("mk", []).append(mk_p.reshape(bp, m_tok, MEM_HEADS, MEM_HEAD_DIM))
        outs.setdefault("mv", []).append(mv_p.reshape(bp, m_tok, MEM_HEADS, MEM_HEAD_DIM))
        q = matmul(x, mem_wq[layer].astype(BF16), tm=tm, tn=d, out_dtype=BF16)
        o_p = mem_attention(q, mk_p.reshape(bp, m_tok, d), mv_p.reshape(bp, m_tok, d),
                            row0=0, n_batch=bp, t_len=tp, nb=1, tq=512, out_dtype=BF16)
        o_s = mem_attention(q, cache_mem_k[layer].reshape(bs, m_tok, d), cache_mem_v[layer].reshape(bs, m_tok, d),
                            row0=n_p, n_batch=bs, t_len=ts, nb=2, tq=ts, out_dtype=BF16)
        o_all = jnp.concatenate([o_p, o_s], axis=0)
        x, x8 = matmul_deepnorm([o_all], [mem_wo[layer].astype(BF16)], x, ln_g[layer, 1], ln_b[layer, 1], tm=tm,
                                token_tiles=True)

        w_all = jnp.zeros((d, LANES), F32)
        w_all = w_all.at[:, :MOE_GROUPS].set(moe_w_group[layer])
        w_all = w_all.at[:, MOE_GROUPS:MOE_GROUPS + MOE_EXPERTS].set(moe_w_router[layer])
        b_all = jnp.zeros((1, LANES), F32)
        b_all = b_all.at[0, :MOE_GROUPS].set(moe_b_group[layer])
        b_all = b_all.at[0, MOE_GROUPS:MOE_GROUPS + MOE_EXPERTS].set(moe_b_router[layer])
        x = hier_moe_deepnorm(x, x8, w_all, b_all, moe_w1, moe_w3, moe_w2, ln_g[layer, 2], ln_b[layer, 2],
                              layer=layer)

    y_prompt = x[:n_p].reshape(bp, tp, d)
    y_sample = x[n_p:].reshape(bs, ts, d)
    return (y_prompt, y_sample,
            jnp.stack(outs["kp"], axis=2), jnp.stack(outs["vp"], axis=2),
            jnp.stack(outs["ks"], axis=2), jnp.stack(outs["vs"], axis=2),
            jnp.stack(outs["cp"], axis=0), jnp.stack(outs["cs"], axis=0),
            jnp.stack(outs["sp"], axis=0), jnp.stack(outs["ss"], axis=0),
            jnp.stack(outs["mk"], axis=0), jnp.stack(outs["mv"], axis=0))
```

```python
import functools
import math

import numpy as np
import jax
import jax.numpy as jnp
from jax import lax
from jax.experimental import pallas as pl
from jax.experimental.pallas import tpu as pltpu

F32 = jnp.float32
BF16 = jnp.bfloat16
I32 = jnp.int32

D_MODEL = 1024
DEPTH = 2
CONV_CH = 512
CONV_WIDTH = 31
MOBA_HEADS = 8
MOBA_HEAD_DIM = 64
MOBA_WIDTH = 512
MOBA_BLOCK = 256
MOBA_TOPK = 3
NUM_BUCKETS = 32
MAX_DISTANCE = 128
RET_HEADS = 4
RET_DK = 256
RET_DV = 512
RET_CHUNK = 128
MEM_HEADS = 4
MEM_HEAD_DIM = 256
MOE_GROUPS = 4
MOE_EPG = 8
MOE_EXPERTS = 32
MOE_TOPK = 2
MOE_D_FF = 512
DEEPNORM_ALPHA = (2 * DEPTH) ** 0.25
LN_EPS = 1e-5

LANES = 128
VMEM_LIMIT = 56 * 1024 * 1024
MOE_TILE = 256
NEG_INF = float("-inf")


def _cparams(sem):
    return pltpu.CompilerParams(dimension_semantics=sem, vmem_limit_bytes=VMEM_LIMIT)


def _layer_norm(y, g, b):
    mu = jnp.mean(y, axis=-1, keepdims=True)
    yc = y - mu
    var = jnp.mean(yc * yc, axis=-1, keepdims=True)
    return yc * lax.rsqrt(var + LN_EPS) * g + b


def _silu(x):
    return x * (1.0 / (1.0 + jnp.exp(-x)))


def _bdot(a, b):
    return jnp.dot(a.astype(BF16), b.astype(BF16), preferred_element_type=F32)


def _bdot_nt(a, b):
    return lax.dot_general(a.astype(BF16), b.astype(BF16), (((1,), (1,)), ((), ())),
                           preferred_element_type=F32)


def _bdot_tn(a, b):
    return lax.dot_general(a.astype(BF16), b.astype(BF16), (((0,), (0,)), ((), ())),
                           preferred_element_type=F32)


def _split3(x):
    hi = x.astype(BF16)
    lo = (x - hi.astype(F32)).astype(BF16)
    return hi, lo


def _dot_hi(a, b, dims=(((1,), (0,)), ((), ()))):
    ah, al = _split3(a)
    bh, bl = _split3(b)
    dg = functools.partial(lax.dot_general, dimension_numbers=dims, preferred_element_type=F32)
    return dg(ah, bh) + (dg(al, bh) + dg(ah, bl))


ROW_TILE = D_MODEL // LANES


def _load_token_tiles(ref, start, n_tok):
    return jnp.concatenate([ref[pl.ds(start + c, n_tok, stride=ROW_TILE), :] for c in range(ROW_TILE)], axis=1)


def _store_token_tiles(ref, val):
    n_tok = val.shape[0]
    for c in range(ROW_TILE):
        ref[pl.ds(c, n_tok, stride=ROW_TILE), :] = val[:, c * LANES:(c + 1) * LANES]


def _tile_copy(src_hbm, src_tok, dst, dst_tok, sem):
    return pltpu.make_async_copy(src_hbm.at[pl.ds(src_tok * ROW_TILE, ROW_TILE), :],
                                 dst.at[pl.ds(dst_tok * ROW_TILE, ROW_TILE), :], sem)


def _wait_tiles(src_hbm, dst, dst_tok, sem, n_tok):
    pltpu.make_async_copy(src_hbm.at[pl.ds(0, n_tok * ROW_TILE), :],
                          dst.at[pl.ds(dst_tok * ROW_TILE, n_tok * ROW_TILE), :], sem).wait()


def _mm_kernel(x_ref, w_ref, o_ref):
    o_ref[...] = _bdot(x_ref[...], w_ref[...]).astype(o_ref.dtype)


def matmul(x, w, *, tm, tn, out_dtype=F32):
    m, k = x.shape
    n = w.shape[1]
    assert m % tm == 0 and n % tn == 0
    return pl.pallas_call(
        _mm_kernel,
        out_shape=jax.ShapeDtypeStruct((m, n), out_dtype),
        grid=(m // tm, n // tn),
        in_specs=[pl.BlockSpec((tm, k), lambda i, j: (i, 0)),
                  pl.BlockSpec((k, tn), lambda i, j: (0, j))],
        out_specs=pl.BlockSpec((tm, tn), lambda i, j: (i, j)),
        compiler_params=_cparams(("parallel", "parallel")),
        name="matmul",
    )(x, w)


def _mm_dn_kernel(*refs, n_lhs, token_tiles):
    lhs = refs[:n_lhs]
    ws = refs[n_lhs:2 * n_lhs]
    res_ref, g_ref, b_ref, o_ref = refs[2 * n_lhs:2 * n_lhs + 4]
    acc = _bdot(lhs[0][...], ws[0][...])
    for a, w in zip(lhs[1:], ws[1:]):
        acc = acc + _bdot(a[...], w[...])
    y = DEEPNORM_ALPHA * res_ref[...] + acc
    out = _layer_norm(y, g_ref[...], b_ref[...])
    o_ref[...] = out
    if token_tiles:
        _store_token_tiles(refs[2 * n_lhs + 4], out)


def matmul_deepnorm(lhs_list, w_list, res, g, b, *, tm, token_tiles=False):
    m, d = res.shape
    n_lhs = len(lhs_list)
    in_specs = ([pl.BlockSpec((tm, a.shape[1]), lambda i: (i, 0)) for a in lhs_list]
                + [pl.BlockSpec(w.shape, lambda i: (0, 0)) for w in w_list]
                + [pl.BlockSpec((tm, d), lambda i: (i, 0)),
                   pl.BlockSpec((1, d), lambda i: (0, 0)),
                   pl.BlockSpec((1, d), lambda i: (0, 0))])
    out_shape = [jax.ShapeDtypeStruct((m, d), F32)]
    out_specs = [pl.BlockSpec((tm, d), lambda i: (i, 0))]
    if token_tiles:
        out_shape.append(jax.ShapeDtypeStruct((m * ROW_TILE, LANES), F32))
        out_specs.append(pl.BlockSpec((tm * ROW_TILE, LANES), lambda i: (i, 0)))
    res_out = pl.pallas_call(
        functools.partial(_mm_dn_kernel, n_lhs=n_lhs, token_tiles=token_tiles),
        out_shape=tuple(out_shape),
        grid=(m // tm,),
        in_specs=in_specs,
        out_specs=tuple(out_specs),
        compiler_params=_cparams(("parallel",)),
        name="matmul_deepnorm",
    )(*lhs_list, *w_list, res, g.reshape(1, d), b.reshape(1, d))
    return res_out if token_tiles else res_out[0]


CONV_PAD = 32


def _conv_kernel(p_ref, hist_ref, w_ref, cb_ref, g_ref, b_ref, c_ref, st_ref, u_scr, *, nb, t_len, rc):
    hw = CONV_WIDTH - 1
    w = w_ref[...]
    for bb in range(nb):
        rows = pl.ds(bb * t_len, t_len)
        a = p_ref[rows, 0:CONV_CH]
        gt = p_ref[rows, CONV_CH:2 * CONV_CH]
        u = a * (1.0 / (1.0 + jnp.exp(-gt)))
        u_scr[0:CONV_PAD, :] = jnp.concatenate(
            [jnp.zeros((CONV_PAD - hw, CONV_CH), F32), hist_ref[bb]], axis=0)
        u_scr[CONV_PAD:CONV_PAD + t_len, :] = u
        st_ref[bb] = u_scr[CONV_PAD + t_len - hw:CONV_PAD + t_len, :]

        def chunk(ci, carry):
            r0 = pl.multiple_of(ci * rc, rc)
            blk = u_scr[pl.ds(r0, rc + CONV_PAD), :]
            acc = jnp.broadcast_to(cb_ref[...], (rc, CONV_CH))
            for j in range(CONV_WIDTH):
                off = j + CONV_PAD - hw
                acc = acc + w[j:j + 1, :] * blk[off:off + rc, :]
            y = _silu(_layer_norm(acc, g_ref[...], b_ref[...]))
            c_ref[pl.ds(bb * t_len + r0, rc), :] = y.astype(c_ref.dtype)
            return carry

        n_chunks = t_len // rc
        if n_chunks == 1:
            chunk(0, 0)
        else:
            lax.fori_loop(0, n_chunks, chunk, 0)


def conv_module(p, hist, conv_w, conv_b, ln_g, ln_b, *, row0, n_batch, t_len, nb, out_dtype):
    rc = min(64, t_len)
    blk_rows = nb * t_len
    assert row0 % blk_rows == 0 and n_batch % nb == 0
    base = row0 // blk_rows
    hw = CONV_WIDTH - 1
    c, st = pl.pallas_call(
        functools.partial(_conv_kernel, nb=nb, t_len=t_len, rc=rc),
        out_shape=(jax.ShapeDtypeStruct((n_batch * t_len, CONV_CH), out_dtype),
                   jax.ShapeDtypeStruct((n_batch, hw, CONV_CH), F32)),
        grid=(n_batch // nb,),
        in_specs=[pl.BlockSpec((blk_rows, 2 * CONV_CH), lambda i: (base + i, 0)),
                  pl.BlockSpec((nb, hw, CONV_CH), lambda i: (i, 0, 0)),
                  pl.BlockSpec((CONV_WIDTH, CONV_CH), lambda i: (0, 0)),
                  pl.BlockSpec((1, CONV_CH), lambda i: (0, 0)),
                  pl.BlockSpec((1, CONV_CH), lambda i: (0, 0)),
                  pl.BlockSpec((1, CONV_CH), lambda i: (0, 0))],
        out_specs=(pl.BlockSpec((blk_rows, CONV_CH), lambda i: (i, 0)),
                   pl.BlockSpec((nb, hw, CONV_CH), lambda i: (i, 0, 0))),
        scratch_shapes=[pltpu.VMEM((t_len + CONV_PAD, CONV_CH), F32)],
        compiler_params=_cparams(("parallel",)),
        name="conv_module",
    )(p, hist, conv_w, conv_b.reshape(1, -1), ln_g.reshape(1, -1), ln_b.reshape(1, -1))
    return c, st


def _bucket_thresholds():
    max_exact = NUM_BUCKETS // 2
    d = np.arange(0, MAX_DISTANCE + 1)
    val = (np.log(np.maximum(d, 1).astype(np.float32) / np.float32(max_exact))
           / np.float32(math.log(MAX_DISTANCE / max_exact)) * np.float32(NUM_BUCKETS - max_exact))
    inner = (d > max_exact) & (d < MAX_DISTANCE)
    assert np.all(np.abs(val[inner] - np.round(val[inner])) > 1e-3)
    bucket = np.where(d < max_exact, d, np.minimum(max_exact + val.astype(np.int32), NUM_BUCKETS - 1))
    return [int(np.argmax(bucket >= k)) for k in range(1, NUM_BUCKETS)]


def _bias_chain(dist, tab_at, thr):
    b = jnp.where(dist >= thr[0], tab_at(1), tab_at(0))
    for k in range(2, NUM_BUCKETS):
        b = jnp.where(dist >= thr[k - 1], tab_at(k), b)
    return b


def _moba_prompt_kernel(tab_ref, q_ref, k_ref, v_ref, o_ref,
                        kmean_scr, kbf_scr, vt_scr, d0_scr, d1_scr, far_scr, sel_scr, qb_scr, m_scr, l_scr, acc_scr,
                        *, n_blk, thr):
    b = pl.program_id(0)
    i = pl.program_id(1)
    blk = MOBA_BLOCK
    hd = MOBA_HEAD_DIM
    n_pair = MOBA_HEADS // 2
    scale = hd ** -0.5
    nt_dims = (((1,), (1,)), ((), ()))
    key = lax.broadcasted_iota(I32, (blk, blk), 0)
    qry = lax.broadcasted_iota(I32, (blk, blk), 1)
    causal = jnp.concatenate([key <= qry, key <= qry], axis=1)

    @pl.when((b == 0) & (i == 0))
    def _():
        for h in range(MOBA_HEADS):
            tab_at = functools.partial(lambda k, hh: tab_ref[hh, k], hh=h)
            half = slice((h % 2) * blk, (h % 2 + 1) * blk)
            d0_scr[h // 2, :, half] = _bias_chain(qry - key, tab_at, thr)
            d1_scr[h // 2, :, half] = _bias_chain(blk + qry - key, tab_at, thr)
            far_scr[h // 2, :, half] = jnp.full((1, blk), tab_ref[h, NUM_BUCKETS - 1], F32)

    @pl.when(i == 0)
    def _():
        for n in range(n_blk):
            kb = k_ref[n * blk:(n + 1) * blk, :]
            kmean_scr[n:n + 1, :] = jnp.sum(kb, axis=0, keepdims=True) * (1.0 / blk)
            kbf_scr[n * blk:(n + 1) * blk, :] = kb.astype(BF16)
            for pair in range(n_pair):
                vt_scr[n, pair * LANES:(pair + 1) * LANES, :] = (
                    v_ref[n * blk:(n + 1) * blk, pair * LANES:(pair + 1) * LANES].T.astype(BF16))

    q = q_ref[...]
    lane = lax.broadcasted_iota(I32, (blk, LANES), 1)
    rown = lax.broadcasted_iota(I32, (n_blk, blk), 0)
    rowd = lax.broadcasted_iota(I32, (LANES, blk), 0)
    r0 = pl.multiple_of(i * blk, blk)
    pairs = [(pr, slice(pr * LANES, (pr + 1) * LANES)) for pr in range(n_pair)]

    for pr, cols in pairs:
        sels = []
        qbs = []
        for sub in range(2):
            in_head = (lane >= sub * hd) & (lane < (sub + 1) * hd)
            qm = jnp.where(in_head, q[:, cols], 0.0)
            gate = _dot_hi(kmean_scr[:, cols], qm, nt_dims)
            rank = jnp.zeros((n_blk, blk), F32)
            for m in range(n_blk):
                gm = gate[m:m + 1, :]
                tie = jnp.where(rown > m, 1.0, 0.0)
                cnt = jnp.where(gm > gate, 1.0, jnp.where(gm == gate, tie, 0.0))
                rank = rank + jnp.where(m < i, cnt, 0.0)
            sels.append(jnp.where((rown < i) & (rank < MOBA_TOPK), 1.0, 0.0))
            qbs.append((qm * scale).astype(BF16))
        sel_scr[pr] = jnp.concatenate(sels, axis=1)
        qb = jnp.concatenate(qbs, axis=0)
        qb_scr[pr] = qb
        s = lax.dot_general(kbf_scr[pl.ds(r0, blk), cols], qb, nt_dims,
                            preferred_element_type=F32) + d0_scr[pr]
        s = jnp.where(causal, s, NEG_INF)
        m_run = jnp.max(s, axis=0, keepdims=True)
        p = jnp.exp(s - m_run)
        m_scr[pr] = m_run
        l_scr[pr] = jnp.sum(p, axis=0, keepdims=True)
        acc_scr[pr] = jnp.dot(vt_scr[i, cols, :], p.astype(BF16), preferred_element_type=F32)

    def merge_block(n, pr, cols, s, shift):
        picked = sel_scr[pr, pl.ds(n, 1), :] > 0.5
        m_run = m_scr[pr]
        m_new = jnp.maximum(m_run, jnp.where(picked, jnp.max(s, axis=0, keepdims=True) + shift, NEG_INF))
        alpha = jnp.exp(m_run - m_new)
        p = jnp.exp(s - jnp.where(picked, m_new - shift, jnp.inf))
        m_scr[pr] = m_new
        l_scr[pr] = alpha * l_scr[pr] + jnp.sum(p, axis=0, keepdims=True)
        acc_scr[pr] = alpha * acc_scr[pr] + jnp.dot(vt_scr[n, cols, :], p.astype(BF16), preferred_element_type=F32)

    @pl.when(i > 0)
    def _():
        rr = pl.multiple_of((i - 1) * blk, blk)
        for pr, cols in pairs:
            s = lax.dot_general(kbf_scr[pl.ds(rr, blk), cols], qb_scr[pr], nt_dims,
                                preferred_element_type=F32) + d1_scr[pr]
            merge_block(i - 1, pr, cols, s, 0.0)

    def body(n, carry):
        rr = pl.multiple_of(n * blk, blk)
        for pr, cols in pairs:
            s = lax.dot_general(kbf_scr[pl.ds(rr, blk), cols], qb_scr[pr], nt_dims, preferred_element_type=F32)
            merge_block(n, pr, cols, s, far_scr[pr])
        return carry

    lax.fori_loop(0, i - 1, body, 0)

    outs = []
    for pr, _ in pairs:
        o = acc_scr[pr] / l_scr[pr]
        outs.append(jnp.where(rowd < hd, o[:, :blk], o[:, blk:]).T)
    o_ref[...] = jnp.concatenate(outs, axis=1).astype(o_ref.dtype)


def moba_prompt(p, tab, *, n_batch, t_len, out_dtype):
    n_blk = t_len // MOBA_BLOCK
    n_pair = MOBA_HEADS // 2
    wide = 2 * MOBA_BLOCK
    return pl.pallas_call(
        functools.partial(_moba_prompt_kernel, n_blk=n_blk, thr=_bucket_thresholds()),
        out_shape=jax.ShapeDtypeStruct((n_batch * t_len, MOBA_WIDTH), out_dtype),
        grid=(n_batch, n_blk),
        in_specs=[pl.BlockSpec(memory_space=pltpu.SMEM),
                  pl.BlockSpec((MOBA_BLOCK, MOBA_WIDTH), lambda b, i: (b * n_blk + i, 2)),
                  pl.BlockSpec((t_len, MOBA_WIDTH), lambda b, i: (b, 3)),
                  pl.BlockSpec((t_len, MOBA_WIDTH), lambda b, i: (b, 4))],
        out_specs=pl.BlockSpec((MOBA_BLOCK, MOBA_WIDTH), lambda b, i: (b * n_blk + i, 0)),
        scratch_shapes=[pltpu.VMEM((n_blk, MOBA_WIDTH), F32),
                        pltpu.VMEM((t_len, MOBA_WIDTH), BF16),
                        pltpu.VMEM((n_blk, MOBA_WIDTH, MOBA_BLOCK), BF16),
                        pltpu.VMEM((n_pair, MOBA_BLOCK, wide), F32),
                        pltpu.VMEM((n_pair, MOBA_BLOCK, wide), F32),
                        pltpu.VMEM((n_pair, 1, wide), F32),
                        pltpu.VMEM((n_pair, n_blk, wide), F32),
                        pltpu.VMEM((n_pair, wide, LANES), BF16),
                        pltpu.VMEM((n_pair, 1, wide), F32),
                        pltpu.VMEM((n_pair, 1, wide), F32),
                        pltpu.VMEM((n_pair, LANES, wide), F32)],
        compiler_params=_cparams(("arbitrary", "arbitrary")),
        name="moba_prompt",
    )(tab, p, p, p)


def _moba_sample_kernel(pt_ref, *refs, n_blk, nbs, page, t_len, thr):
    npg = 2 * nbs
    k_refs = refs[:npg]
    v_refs = refs[npg:2 * npg]
    q_ref, qbd_ref, kn_ref, vn_ref, tab_ref, o_ref, g_scr, m_scr, l_scr, o_scr, bl_scr = refs[2 * npg:]
    n = pl.program_id(1)
    nh, hd = MOBA_HEADS, MOBA_HEAD_DIM
    nrow = nh * t_len
    scale = hd ** -0.5
    nt_dims = (((1,), (1,)), ((), ()))
    lane = lax.broadcasted_iota(I32, (nrow, LANES), 1)
    row = lax.broadcasted_iota(I32, (nrow, LANES), 0)
    tab = tab_ref[...]

    def tab_at(k):
        return tab[:, k:k + 1]

    @pl.when(n == 0)
    def _():
        g_scr[...] = jnp.zeros(g_scr.shape, F32)
        m_scr[...] = jnp.zeros(m_scr.shape, F32)
        l_scr[...] = jnp.zeros(l_scr.shape, F32)
        for half in range(2):
            bl_scr[:, half * page:(half + 1) * page] = _bias_chain(
                MOBA_BLOCK + row % t_len - (half * page + lane), tab_at, thr)

    qbd = (qbd_ref[0] * scale).astype(BF16)
    far = tab_at(NUM_BUCKETS - 1)
    bw = 2 * page
    kt = jnp.concatenate([r[...].reshape(nh * hd, page).astype(BF16) for r in k_refs], axis=1)
    vt = jnp.concatenate([r[...].reshape(nh * hd, page).astype(BF16) for r in v_refs], axis=1)
    s_all = jnp.dot(qbd, kt, preferred_element_type=F32)
    p_rows = []
    stats = []
    for j in range(nbs):
        blk_idx = n * nbs + j
        s_raw = s_all[:, j * bw:(j + 1) * bw]
        gate = jnp.sum(s_raw, axis=1, keepdims=True) * (1.0 / (MOBA_BLOCK * scale))
        s = s_raw + jnp.where(blk_idx == n_blk - 1, bl_scr[...], far)
        m = jnp.max(s, axis=1, keepdims=True)
        p = jnp.exp(s - m)
        l = jnp.sum(p, axis=1, keepdims=True)
        stats.append((blk_idx, gate, m, l))
        zl = jnp.zeros((nrow, j * bw), F32)
        zr = jnp.zeros((nrow, (nbs - 1 - j) * bw), F32)
        p_rows.append(jnp.concatenate([x for x in (zl, p, zr) if x.shape[1]], axis=1))
    p_bd = jnp.concatenate(p_rows, axis=0).astype(BF16)
    pv = lax.dot_general(p_bd, vt, nt_dims, preferred_element_type=F32)
    for j, (blk_idx, gate, m, l) in enumerate(stats):
        at_blk = lane == blk_idx
        g_scr[...] = jnp.where(at_blk, gate, g_scr[...])
        m_scr[...] = jnp.where(at_blk, m, m_scr[...])
        l_scr[...] = jnp.where(at_blk, l, l_scr[...])
        o_scr[blk_idx] = jnp.concatenate(
            [pv[j * nrow + h * t_len:j * nrow + (h + 1) * t_len, h * hd:(h + 1) * hd] for h in range(nh)], axis=0)

    @pl.when(n == n_blk // nbs - 1)
    def _():
        gate = g_scr[...]
        rank = jnp.zeros((nrow, LANES), F32)
        for mm in range(n_blk):
            gm = gate[:, mm:mm + 1]
            tie = jnp.where(lane > mm, 1.0, 0.0)
            rank = rank + jnp.where(gm > gate, 1.0, jnp.where(gm == gate, tie, 0.0))
        sel = (rank < MOBA_TOPK) & (lane < n_blk)

        q = q_ref[0]
        kn = kn_ref[0]
        vn = vn_ref[0]
        tq = lax.broadcasted_iota(I32, (nrow, t_len), 0) % t_len
        sk = lax.broadcasted_iota(I32, (nrow, t_len), 1)
        s_own = jnp.concatenate(
            [lax.dot_general((q[h] * scale).astype(BF16), kn[h].astype(BF16), nt_dims, preferred_element_type=F32)
             for h in range(nh)], axis=0) + _bias_chain(tq - sk, tab_at, thr)
        s_own = jnp.where(sk <= tq, s_own, NEG_INF)
        m_o = jnp.max(s_own, axis=1, keepdims=True)
        p_o = jnp.exp(s_own - m_o)
        l_o = jnp.sum(p_o, axis=1, keepdims=True)
        o_o = jnp.concatenate(
            [jnp.dot(p_o[h * t_len:(h + 1) * t_len].astype(BF16), vn[h].astype(BF16), preferred_element_type=F32)
             for h in range(nh)], axis=0)

        m_sel = jnp.where(sel, m_scr[...], NEG_INF)
        m_fin = jnp.maximum(jnp.max(m_sel, axis=1, keepdims=True), m_o)
        w = jnp.where(sel, jnp.exp(m_sel - m_fin), 0.0)
        w_o = jnp.exp(m_o - m_fin)
        l_fin = jnp.sum(w * l_scr[...], axis=1, keepdims=True) + w_o * l_o
        acc = w_o * o_o
        for nn in range(n_blk):
            acc = acc + w[:, nn:nn + 1] * o_scr[nn]
        o_ref[0] = acc / l_fin


MOBA_SAMPLE_BLOCKS_PER_STEP = 4


def moba_sample(page_table, cache_kt, cache_vt, q, kn, vn, tab_heads, *, layer_e, t_len):
    n_b, n_pages = page_table.shape
    page = cache_kt.shape[-1]
    nbs = MOBA_SAMPLE_BLOCKS_PER_STEP
    ppb = MOBA_BLOCK // page
    assert ppb == 2 and page == LANES and n_pages % (ppb * nbs) == 0
    n_blk = n_pages // ppb
    assert n_blk <= LANES
    nh, hd = MOBA_HEADS, MOBA_HEAD_DIM
    nrow = nh * t_len
    tab = jnp.pad(jnp.repeat(tab_heads, t_len, axis=0), ((0, 0), (0, LANES - NUM_BUCKETS)))
    qbd = (q[:, :, :, None, :] * jnp.eye(nh, dtype=F32)[None, :, None, :, None]).reshape(n_b, nrow, nh * hd)

    def pspec(p):
        return pl.BlockSpec((None, None, nh, hd, page),
                            lambda b, n, pt: (pt[b, n * ppb * nbs + p], layer_e, 0, 0, 0))

    def bspec():
        return pl.BlockSpec((1, nh, t_len, hd), lambda b, n, pt: (b, 0, 0, 0))

    pages = [pspec(p) for p in range(ppb * nbs)]
    grid_spec = pltpu.PrefetchScalarGridSpec(
        num_scalar_prefetch=1,
        grid=(n_b, n_blk // nbs),
        in_specs=pages + pages + [
            bspec(),
            pl.BlockSpec((1, nrow, nh * hd), lambda b, n, pt: (b, 0, 0)),
            bspec(), bspec(),
            pl.BlockSpec(tab.shape, lambda b, n, pt: (0, 0))],
        out_specs=pl.BlockSpec((1, nrow, hd), lambda b, n, pt: (b, 0, 0)),
        scratch_shapes=[pltpu.VMEM((nrow, LANES), F32), pltpu.VMEM((nrow, LANES), F32),
                        pltpu.VMEM((nrow, LANES), F32), pltpu.VMEM((n_blk, nrow, hd), F32),
                        pltpu.VMEM((nrow, MOBA_BLOCK), F32)],
    )
    return pl.pallas_call(
        functools.partial(_moba_sample_kernel, n_blk=n_blk, nbs=nbs, page=page, t_len=t_len,
                          thr=_bucket_thresholds()),
        out_shape=jax.ShapeDtypeStruct((n_b, nrow, hd), F32),
        grid_spec=grid_spec,
        compiler_params=_cparams(("parallel", "arbitrary")),
        name="moba_sample",
    )(page_table, *([cache_kt] * (ppb * nbs)), *([cache_vt] * (ppb * nbs)), q, qbd, kn, vn, tab)


def _retention_kernel(q_ref, k_ref, v_ref, g_ref, cos_ref, sin_ref, dmask_ref, xi_ref, zeta_ref, gc_ref,
                      gng_ref, gnb_ref, *rest, nb, chunk, has_s0):
    if has_s0:
        s0_ref, o_ref, s_out_ref, s_scr = rest
    else:
        o_ref, s_out_ref, s_scr = rest
    c = pl.program_id(1)
    dk, dv = RET_DK, RET_DV
    half = dk // 2

    @pl.when(c == 0)
    def _():
        if has_s0:
            s_scr[...] = s0_ref[...]
        else:
            s_scr[...] = jnp.zeros(s_scr.shape, F32)

    cos = cos_ref[...]
    sin = sin_ref[...]

    def rot(x):
        x1 = x[:, :half]
        x2 = x[:, half:]
        return jnp.concatenate([x1 * cos - x2 * sin, x2 * cos + x1 * sin], axis=1)

    out_rows = []
    for bb in range(nb):
        rows = pl.ds(bb * chunk, chunk)
        out_heads = []
        for h in range(RET_HEADS):
            qh = rot(q_ref[rows, h * dk:(h + 1) * dk])
            kh = rot(k_ref[rows, h * dk:(h + 1) * dk]) * (dk ** -0.5)
            vh = v_ref[rows, h * dv:(h + 1) * dv]
            s = s_scr[bb, h]
            att = _bdot_nt(qh, kh) * dmask_ref[h]
            o = _bdot(att, vh) + _bdot(qh, s) * xi_ref[h]
            s_scr[bb, h] = s * gc_ref[h, 0:1, 0:1] + _bdot_tn(kh * zeta_ref[h], vh)
            mu = jnp.mean(o, axis=-1, keepdims=True)
            oc = o - mu
            var = jnp.mean(oc * oc, axis=-1, keepdims=True)
            on = oc * lax.rsqrt(var + LN_EPS) * gng_ref[:, h * dv:(h + 1) * dv] + gnb_ref[:, h * dv:(h + 1) * dv]
            gate = _silu(g_ref[rows, h * dv:(h + 1) * dv])
            out_heads.append(gate * on)
        out_rows.append(jnp.concatenate(out_heads, axis=1))
    o_ref[...] = jnp.concatenate(out_rows, axis=0).astype(o_ref.dtype)

    @pl.when(c == pl.num_programs(1) - 1)
    def _():
        s_out_ref[...] = s_scr[...]


def retention(p, cos, sin, consts, gn_g, gn_b, s0, *, row0, n_batch, t_len, nb, out_dtype):
    chunk = math.gcd(t_len, RET_CHUNK)
    n_chunk = t_len // chunk
    dmask, xi, zeta, gc = consts
    blk_rows = nb * chunk
    if nb > 1:
        assert n_chunk == 1
    assert row0 % blk_rows == 0
    base = row0 // blk_rows
    qk_w = RET_HEADS * RET_DK
    v_w = RET_HEADS * RET_DV
    has_s0 = s0 is not None

    def rmap(col):
        return lambda b, c: (base + b * n_chunk + c, col)

    in_specs = [pl.BlockSpec((blk_rows, qk_w), rmap(0)),
                pl.BlockSpec((blk_rows, qk_w), rmap(1)),
                pl.BlockSpec((blk_rows, v_w), rmap(1)),
                pl.BlockSpec((blk_rows, v_w), rmap(2)),
                pl.BlockSpec((chunk, RET_DK // 2), lambda b, c: (c, 0)),
                pl.BlockSpec((chunk, RET_DK // 2), lambda b, c: (c, 0)),
                pl.BlockSpec(dmask.shape, lambda b, c: (0, 0, 0)),
                pl.BlockSpec(xi.shape, lambda b, c: (0, 0, 0)),
                pl.BlockSpec(zeta.shape, lambda b, c: (0, 0, 0)),
                pl.BlockSpec(gc.shape, lambda b, c: (0, 0, 0)),
                pl.BlockSpec((1, v_w), lambda b, c: (0, 0)),
                pl.BlockSpec((1, v_w), lambda b, c: (0, 0))]
    args = [p, p, p, p, cos, sin, dmask, xi, zeta, gc, gn_g.reshape(1, -1), gn_b.reshape(1, -1)]
    s_spec = pl.BlockSpec((nb, RET_HEADS, RET_DK, RET_DV), lambda b, c: (b, 0, 0, 0))
    if has_s0:
        in_specs.append(s_spec)
        args.append(s0)
    o, s_out = pl.pallas_call(
        functools.partial(_retention_kernel, nb=nb, chunk=chunk, has_s0=has_s0),
        out_shape=(jax.ShapeDtypeStruct((n_batch * t_len, v_w), out_dtype),
                   jax.ShapeDtypeStruct((n_batch, RET_HEADS, RET_DK, RET_DV), F32)),
        grid=(n_batch // nb, n_chunk),
        in_specs=in_specs,
        out_specs=(pl.BlockSpec((blk_rows, v_w), lambda b, c: (b * n_chunk + c, 0)), s_spec),
        scratch_shapes=[pltpu.VMEM((nb, RET_HEADS, RET_DK, RET_DV), F32)],
        compiler_params=_cparams(("parallel", "arbitrary")),
        name="retention",
    )(*args)
    return o, s_out


def _retention_consts(chunk):
    h = jnp.arange(RET_HEADS, dtype=F32)
    lg = jnp.log1p(-jnp.exp2(-5.0 - h))
    idx = jnp.arange(chunk, dtype=F32)
    diff = idx[:, None] - idx[None, :]
    dmask = jnp.where(diff >= 0, jnp.exp(lg[:, None, None] * jnp.maximum(diff, 0.0)), 0.0)
    xi = jnp.exp(lg[:, None] * (idx[None, :] + 1.0))
    zeta = jnp.exp(lg[:, None] * (chunk - 1.0 - idx[None, :]))
    g_c = jnp.exp(lg * chunk)
    return (dmask,
            jnp.broadcast_to(xi[:, :, None], (RET_HEADS, chunk, RET_DV)),
            jnp.broadcast_to(zeta[:, :, None], (RET_HEADS, chunk, RET_DK)),
            jnp.broadcast_to(g_c[:, None, None], (RET_HEADS, 8, LANES)))


def _rotary_tables(pos):
    half = RET_DK // 2
    inv = 10000.0 ** (-jnp.linspace(0.0, 1.0, half, dtype=F32))
    ang = pos.astype(F32)[:, None] * inv[None, :]
    return jnp.cos(ang), jnp.sin(ang)


def _mem_attn_kernel(q_ref, mk_ref, mv_ref, o_ref, *, nb, tq):
    hd = MEM_HEAD_DIM
    scale = hd ** -0.5
    qf = q_ref[...].astype(F32)
    out_rows = []
    for bb in range(nb):
        out_heads = []
        for h in range(MEM_HEADS):
            cols = slice(h * hd, (h + 1) * hd)
            q = qf[bb * tq:(bb + 1) * tq, cols]
            s = _bdot_nt(q, mk_ref[bb, :, cols]) * scale
            m = jnp.max(s, axis=-1, keepdims=True)
            p = jnp.exp(s - m)
            p = p / jnp.sum(p, axis=-1, keepdims=True)
            out_heads.append(_bdot(p, mv_ref[bb, :, cols]))
        out_rows.append(jnp.concatenate(out_heads, axis=1))
    o_ref[...] = jnp.concatenate(out_rows, axis=0).astype(o_ref.dtype)


def mem_attention(q, mk, mv, *, row0, n_batch, t_len, nb, tq, out_dtype):
    n_t = t_len // tq
    blk_rows = nb * tq
    if nb > 1:
        assert n_t == 1
    assert row0 % blk_rows == 0
    base = row0 // blk_rows
    m_tok = mk.shape[1]
    return pl.pallas_call(
        functools.partial(_mem_attn_kernel, nb=nb, tq=tq),
        out_shape=jax.ShapeDtypeStruct((n_batch * t_len, D_MODEL), out_dtype),
        grid=(n_batch // nb, n_t),
        in_specs=[pl.BlockSpec((blk_rows, D_MODEL), lambda b, t: (base + b * n_t + t, 0)),
                  pl.BlockSpec((nb, m_tok, D_MODEL), lambda b, t: (b, 0, 0)),
                  pl.BlockSpec((nb, m_tok, D_MODEL), lambda b, t: (b, 0, 0))],
        out_specs=pl.BlockSpec((blk_rows, D_MODEL), lambda b, t: (b * n_t + t, 0)),
        compiler_params=_cparams(("parallel", "arbitrary")),
        name="mem_attention",
    )(q, mk, mv)


def _router_kernel(x_ref, w_ref, b_ref, tri_ref, wt_ref, ei_ref, cnt_ref, base_scr):
    @pl.when(pl.program_id(0) == 0)
    def _():
        base_scr[...] = jnp.zeros(base_scr.shape, F32)

    x = x_ref[...]
    logits = _dot_hi(x, w_ref[...]) + b_ref[...]
    tm = x.shape[0]
    lane = lax.broadcasted_iota(I32, (tm, LANES), 1)
    big = jnp.int32(LANES)

    def masked_softmax(mask):
        lm = jnp.where(mask, logits, NEG_INF)
        mx = jnp.max(lm, axis=-1, keepdims=True)
        e = jnp.exp(lm - mx)
        return e / jnp.sum(e, axis=-1, keepdims=True)

    g_prob = masked_softmax(lane < MOE_GROUPS)
    gp = jnp.max(g_prob, axis=-1, keepdims=True)
    gi = jnp.min(jnp.where(g_prob == gp, lane, big), axis=-1, keepdims=True)
    lo = MOE_GROUPS + gi * MOE_EPG
    in_group = (lane >= lo) & (lane < lo + MOE_EPG)
    e_prob = jnp.where(in_group, masked_softmax(in_group), -1.0)
    p1 = jnp.max(e_prob, axis=-1, keepdims=True)
    i1 = jnp.min(jnp.where(e_prob == p1, lane, big), axis=-1, keepdims=True)
    rest = jnp.where(lane == i1, -1.0, e_prob)
    p2 = jnp.max(rest, axis=-1, keepdims=True)
    i2 = jnp.min(jnp.where(rest == p2, lane, big), axis=-1, keepdims=True)
    tot = p1 + p2
    w1 = gp * (p1 / tot)
    w2 = gp * (p2 / tot)
    e1 = i1 - MOE_GROUPS
    e2 = i2 - MOE_GROUPS
    oh1 = jnp.where(lane == e1, 1.0, 0.0)
    oh2 = jnp.where(lane == e2, 1.0, 0.0)
    both = oh1 + oh2
    base = base_scr[0:1, :]
    before = jnp.dot(tri_ref[...], both.astype(BF16), preferred_element_type=F32) + base
    r1 = jnp.sum(oh1 * before, axis=-1, keepdims=True).astype(I32)
    r2 = jnp.sum(oh2 * before, axis=-1, keepdims=True).astype(I32)
    base_new = base + jnp.sum(both, axis=0, keepdims=True)
    base_scr[0:1, :] = base_new
    wt_ref[...] = jnp.where(lane == 0, w1, jnp.where(lane == 1, w2, 0.0))
    ei_ref[...] = jnp.where(lane == 0, e1, jnp.where(lane == 1, e2, jnp.where(lane == 2, r1,
                                                                              jnp.where(lane == 3, r2, 0))))
    cnt_ref[...] = jnp.broadcast_to(base_new, cnt_ref.shape)


def moe_router(x, w_all, b_all, *, tm):
    m, d = x.shape
    tri = jnp.asarray(np.tril(np.ones((tm, tm), np.float32), -1), BF16)
    return pl.pallas_call(
        _router_kernel,
        out_shape=(jax.ShapeDtypeStruct((m, LANES), F32), jax.ShapeDtypeStruct((m, LANES), I32),
                   jax.ShapeDtypeStruct((8, LANES), F32)),
        grid=(m // tm,),
        in_specs=[pl.BlockSpec((tm, d), lambda i: (i, 0)),
                  pl.BlockSpec((d, LANES), lambda i: (0, 0)),
                  pl.BlockSpec((1, LANES), lambda i: (0, 0)),
                  pl.BlockSpec((tm, tm), lambda i: (0, 0))],
        out_specs=(pl.BlockSpec((tm, LANES), lambda i: (i, 0)), pl.BlockSpec((tm, LANES), lambda i: (i, 0)),
                   pl.BlockSpec((8, LANES), lambda i: (0, 0))),
        scratch_shapes=[pltpu.VMEM((8, LANES), F32)],
        compiler_params=_cparams(("arbitrary",)),
        name="moe_router",
    )(x, w_all, b_all, tri)


def _moe_dispatch_kernel(dest_ref, x_ref, xs_in, xs_out, sem, *, tm):
    del xs_in

    def body(r, carry):
        _tile_copy(x_ref, r, xs_out, dest_ref[r], sem.at[0]).start()
        _tile_copy(x_ref, r, xs_out, dest_ref[tm + r], sem.at[0]).start()
        return carry

    lax.fori_loop(0, tm, body, 0, unroll=8)
    for _ in range(MOE_TOPK):
        _wait_tiles(x_ref, xs_out, 0, sem.at[0], tm)


def moe_dispatch(x8, dest, n_slots, *, tm):
    n_t = dest.shape[0]
    xs0 = jnp.zeros((n_slots * ROW_TILE, LANES), F32)
    return pl.pallas_call(
        functools.partial(_moe_dispatch_kernel, tm=tm),
        out_shape=jax.ShapeDtypeStruct(xs0.shape, F32),
        grid=(n_t,),
        in_specs=[pl.BlockSpec((None, None, 2 * tm), lambda i: (i, 0, 0), memory_space=pltpu.SMEM),
                  pl.BlockSpec((tm * ROW_TILE, LANES), lambda i: (i, 0)),
                  pl.BlockSpec(memory_space=pl.ANY)],
        out_specs=pl.BlockSpec(memory_space=pl.ANY),
        scratch_shapes=[pltpu.SemaphoreType.DMA((1,))],
        input_output_aliases={2: 0},
        compiler_params=_cparams(("arbitrary",)),
        name="moe_dispatch",
    )(dest.reshape(n_t, 1, 2 * tm), x8, xs0)


def _moe_ffn_kernel(te_ref, nt_ref, xs_ref, w1_ref, w3_ref, w2_ref, ys_ref):
    j = pl.program_id(0)

    @pl.when(j < nt_ref[0])
    def _():
        x = _load_token_tiles(xs_ref, 0, MOE_TILE).astype(BF16)
        h = _silu(_bdot(x, w1_ref[0])) * _bdot(x, w3_ref[0])
        _store_token_tiles(ys_ref, _bdot(h, w2_ref[0]))

    @pl.when(j >= nt_ref[0])
    def _():
        ys_ref[...] = jnp.zeros(ys_ref.shape, F32)


def moe_ffn(xs8, tile_expert, n_tiles, w1, w3, w2, *, layer):
    n_t = tile_expert.shape[0]
    d, dff = w1.shape[2], w1.shape[3]
    blk = MOE_TILE * ROW_TILE
    grid_spec = pltpu.PrefetchScalarGridSpec(
        num_scalar_prefetch=2,
        grid=(n_t,),
        in_specs=[pl.BlockSpec((blk, LANES), lambda j, te, nt: (jnp.minimum(j, nt[0] - 1), 0)),
                  pl.BlockSpec((None, 1, d, dff), lambda j, te, nt: (layer, te[j], 0, 0)),
                  pl.BlockSpec((None, 1, d, dff), lambda j, te, nt: (layer, te[j], 0, 0)),
                  pl.BlockSpec((None, 1, dff, d), lambda j, te, nt: (layer, te[j], 0, 0))],
        out_specs=pl.BlockSpec((blk, LANES), lambda j, te, nt: (j, 0)),
    )
    return pl.pallas_call(
        _moe_ffn_kernel,
        out_shape=jax.ShapeDtypeStruct(xs8.shape, F32),
        grid_spec=grid_spec,
        compiler_params=_cparams(("arbitrary",)),
        name="moe_ffn",
    )(tile_expert, n_tiles, xs8, w1, w3, w2)


def _moe_combine_kernel(d_cur, d_nxt, ys_hbm, x_ref, wt_ref, g_ref, b_ref, o_ref, ybuf, sem, *, tm):
    i = pl.program_id(0)
    n = pl.num_programs(0)
    slot = i % 2

    def issue(d_ref, s):
        def body(r, carry):
            _tile_copy(ys_hbm, d_ref[r], ybuf, s * 2 * tm + r, sem.at[s]).start()
            return carry
        lax.fori_loop(0, 2 * tm, body, 0, unroll=8)

    @pl.when(i == 0)
    def _():
        issue(d_cur, 0)

    @pl.when(i + 1 < n)
    def _():
        issue(d_nxt, 1 - slot)

    base_tok = slot * 2 * tm
    _wait_tiles(ys_hbm, ybuf, base_tok, sem.at[slot], 2 * tm)
    wt = wt_ref[...]
    y0 = _load_token_tiles(ybuf, base_tok * ROW_TILE, tm)
    y1 = _load_token_tiles(ybuf, (base_tok + tm) * ROW_TILE, tm)
    y = DEEPNORM_ALPHA * x_ref[...] + (wt[:, 0:1] * y0 + wt[:, 1:2] * y1)
    o_ref[...] = _layer_norm(y, g_ref[...], b_ref[...])


def moe_combine_deepnorm(ys8, dest, x, wts, g, b, *, tm):
    m, d = x.shape
    n_t = m // tm
    dest3 = dest.reshape(n_t, 1, 2 * tm)
    return pl.pallas_call(
        functools.partial(_moe_combine_kernel, tm=tm),
        out_shape=jax.ShapeDtypeStruct((m, d), F32),
        grid=(n_t,),
        in_specs=[pl.BlockSpec((None, None, 2 * tm), lambda i: (i, 0, 0), memory_space=pltpu.SMEM),
                  pl.BlockSpec((None, None, 2 * tm), lambda i: (jnp.minimum(i + 1, n_t - 1), 0, 0),
                               memory_space=pltpu.SMEM),
                  pl.BlockSpec(memory_space=pl.ANY),
                  pl.BlockSpec((tm, d), lambda i: (i, 0)),
                  pl.BlockSpec((tm, LANES), lambda i: (i, 0)),
                  pl.BlockSpec((1, d), lambda i: (0, 0)),
                  pl.BlockSpec((1, d), lambda i: (0, 0))],
        out_specs=pl.BlockSpec((tm, d), lambda i: (i, 0)),
        scratch_shapes=[pltpu.VMEM((2 * 2 * tm * ROW_TILE, LANES), F32), pltpu.SemaphoreType.DMA((2,))],
        compiler_params=_cparams(("arbitrary",)),
        name="moe_combine_deepnorm",
    )(dest3, dest3, ys8, x, wts, g.reshape(1, d), b.reshape(1, d))


def _moe_plan(eidx, rank, counts, n_tok, tm):
    n_tiles_max = n_tok * MOE_TOPK // MOE_TILE + MOE_EXPERTS
    tiles_per = (counts + MOE_TILE - 1) // MOE_TILE
    tile_end = jnp.cumsum(tiles_per)
    pad_off = (tile_end - tiles_per) * MOE_TILE
    onehot = eidx[:, :, None] == jnp.arange(MOE_EXPERTS, dtype=I32)[None, None, :]
    dest = jnp.sum(jnp.where(onehot, pad_off[None, None, :], 0), axis=-1) + rank
    tile_expert = jnp.minimum(
        jnp.sum((jnp.arange(n_tiles_max, dtype=I32)[:, None] >= tile_end[None, :]).astype(I32), axis=1),
        MOE_EXPERTS - 1)
    dest = dest.reshape(n_tok // tm, tm, MOE_TOPK).transpose(0, 2, 1).reshape(n_tok // tm, MOE_TOPK * tm)
    return tile_expert, tile_end[-1:].astype(I32), dest, n_tiles_max * MOE_TILE


def hier_moe_deepnorm(x, x8, w_all, b_all, w1, w3, w2, g, b, *, layer):
    n_tok = x.shape[0]
    wts, ei, cnt = moe_router(x, w_all, b_all, tm=640)
    counts = cnt[0, :MOE_EXPERTS].astype(I32)
    tile_expert, n_tiles, dest, n_slots = _moe_plan(ei[:, 0:2], ei[:, 2:4], counts, n_tok, MOE_TILE)
    xs8 = moe_dispatch(x8, dest, n_slots, tm=MOE_TILE)
    ys8 = moe_ffn(xs8, tile_expert, n_tiles, w1, w3, w2, layer=layer)
    return moe_combine_deepnorm(ys8, dest, x, wts, g, b, tm=MOE_TILE)


def kernel(x_prompt, x_sample, cache_moba_k, cache_moba_v, state_conv, state_ret, cache_mem_k, cache_mem_v,
           page_table, mem_prompt, rel_bias, ev_w_in, ev_conv_w, ev_conv_b, ev_conv_ln_g, ev_conv_ln_b,
           ev_w_out, od_w_in, od_gn_g, od_gn_b, od_w_out, mem_wq, mem_wk, mem_wv, mem_wo, ln_g, ln_b,
           moe_w_group, moe_b_group, moe_w_router, moe_b_router, moe_w1, moe_w3, moe_w2):
    bp, tp, d = x_prompt.shape
    bs, ts, _ = x_sample.shape
    n_p = bp * tp
    n_s = bs * ts
    n_all = n_p + n_s
    page = cache_moba_k.shape[1]
    past_len = page_table.shape[1] * page
    nh, hd = MOBA_HEADS, MOBA_HEAD_DIM
    assert tp % MOBA_BLOCK == 0 and past_len % MOBA_BLOCK == 0 and ts <= MOBA_BLOCK
    assert MOBA_BLOCK >= MAX_DISTANCE
    tm = 640
    assert n_all % tm == 0

    x = jnp.concatenate([x_prompt.reshape(n_p, d), x_sample.reshape(n_s, d)], axis=0)

    tab_heads = rel_bias.T
    cache_kt = cache_moba_k.transpose(0, 2, 3, 4, 1)
    cache_vt = cache_moba_v.transpose(0, 2, 3, 4, 1)

    pos_p = jnp.arange(tp, dtype=I32)
    pos_s = past_len + jnp.arange(ts, dtype=I32)
    cos_p, sin_p = _rotary_tables(pos_p)
    cos_s, sin_s = _rotary_tables(pos_s)
    ret_c_p = _retention_consts(math.gcd(tp, RET_CHUNK))
    ret_c_s = _retention_consts(math.gcd(ts, RET_CHUNK))

    outs = {}
    for layer in range(DEPTH):
        if layer % 2 == 0:
            e = layer // 2
            proj = matmul(x, ev_w_in[e].astype(BF16), tm=tm, tn=ev_w_in.shape[2])
            k_new = proj[:, 2 * CONV_CH + MOBA_WIDTH:2 * CONV_CH + 2 * MOBA_WIDTH]
            v_new = proj[:, 2 * CONV_CH + 2 * MOBA_WIDTH:]
            q_s = proj[n_p:, 2 * CONV_CH:2 * CONV_CH + MOBA_WIDTH]
            conv_args = (ev_conv_w[e], ev_conv_b[e], ev_conv_ln_g[e], ev_conv_ln_b[e])
            c_p, cst_p = conv_module(proj, jnp.zeros((bp, CONV_WIDTH - 1, CONV_CH), F32), *conv_args,
                                     row0=0, n_batch=bp, t_len=tp, nb=1, out_dtype=BF16)
            c_s, cst_s = conv_module(proj, state_conv[e], *conv_args,
                                     row0=n_p, n_batch=bs, t_len=ts, nb=bs, out_dtype=F32)
            a_p = moba_prompt(proj, tab_heads, n_batch=bp, t_len=tp, out_dtype=BF16)

            def by_head(a):
                return a.reshape(bs, ts, nh, hd).transpose(0, 2, 1, 3)

            a_s = moba_sample(page_table, cache_kt, cache_vt, by_head(q_s), by_head(k_new[n_p:]),
                              by_head(v_new[n_p:]), tab_heads, layer_e=e, t_len=ts)
            a_s = a_s.reshape(bs, nh, ts, hd).transpose(0, 2, 1, 3).reshape(n_s, MOBA_WIDTH).astype(BF16)
            c_all = jnp.concatenate([c_p, c_s.astype(BF16)], axis=0)
            a_all = jnp.concatenate([a_p, a_s], axis=0)
            w_out = ev_w_out[e].astype(BF16)
            x = matmul_deepnorm([c_all, a_all], [w_out[:CONV_CH], w_out[CONV_CH:]], x,
                                ln_g[layer, 0], ln_b[layer, 0], tm=tm)
            outs.setdefault("kp", []).append(k_new[:n_p].reshape(bp, tp, nh, hd))
            outs.setdefault("vp", []).append(v_new[:n_p].reshape(bp, tp, nh, hd))
            outs.setdefault("ks", []).append(k_new[n_p:].reshape(bs, ts, nh, hd))
            outs.setdefault("vs", []).append(v_new[n_p:].reshape(bs, ts, nh, hd))
            outs.setdefault("cp", []).append(cst_p)
            outs.setdefault("cs", []).append(cst_s)
        else:
            o = layer // 2
            proj = matmul(x, od_w_in[o].astype(BF16), tm=tm, tn=2048)
            r_p, st_p = retention(proj, cos_p, sin_p, ret_c_p, od_gn_g[o], od_gn_b[o], None,
                                  row0=0, n_batch=bp, t_len=tp, nb=1, out_dtype=BF16)
            r_s, st_s = retention(proj, cos_s, sin_s, ret_c_s, od_gn_g[o], od_gn_b[o], state_ret[o],
                                  row0=n_p, n_batch=bs, t_len=ts, nb=2, out_dtype=BF16)
            r_all = jnp.concatenate([r_p, r_s], axis=0)
            x = matmul_deepnorm([r_all], [od_w_out[o].astype(BF16)], x, ln_g[layer, 0], ln_b[layer, 0], tm=tm)
            outs.setdefault("sp", []).append(st_p)
            outs.setdefault("ss", []).append(st_s)

        m_tok = mem_prompt.shape[1]
        mem2 = mem_prompt.reshape(bp * m_tok, d)
        mk_p = matmul(mem2, mem_wk[layer].astype(BF16), tm=512, tn=d)
        mv_p = matmul(mem2, mem_wv[layer].astype(BF16), tm=512, tn=d)
        outs.setdefault("mk", []).append(mk_p.reshape(bp, m_tok, MEM_HEADS, MEM_HEAD_DIM))
        outs.setdefault("mv", []).append(mv_p.reshape(bp, m_tok, MEM_HEADS, MEM_HEAD_DIM))
        q = matmul(x, mem_wq[layer].astype(BF16), tm=tm, tn=d, out_dtype=BF16)
        o_p = mem_attention(q, mk_p.reshape(bp, m_tok, d), mv_p.reshape(bp, m_tok, d),
                            row0=0, n_batch=bp, t_len=tp, nb=1, tq=512, out_dtype=BF16)
        o_s = mem_attention(q, cache_mem_k[layer].reshape(bs, m_tok, d), cache_mem_v[layer].reshape(bs, m_tok, d),
                            row0=n_p, n_batch=bs, t_len=ts, nb=2, tq=ts, out_dtype=BF16)
        o_all = jnp.concatenate([o_p, o_s], axis=0)
        x, x8 = matmul_deepnorm([o_all], [mem_wo[layer].astype(BF16)], x, ln_g[layer, 1], ln_b[layer, 1], tm=tm,
                                token_tiles=True)

        w_all = jnp.zeros((d, LANES), F32)
        w_all = w_all.at[:, :MOE_GROUPS].set(moe_w_group[layer])
        w_all = w_all.at[:, MOE_GROUPS:MOE_GROUPS + MOE_EXPERTS].set(moe_w_router[layer])
        b_all = jnp.zeros((1, LANES), F32)
        b_all = b_all.at[0, :MOE_GROUPS].set(moe_b_group[layer])
        b_all = b_all.at[0, MOE_GROUPS:MOE_GROUPS + MOE_EXPERTS].set(moe_b_router[layer])
        x = hier_moe_deepnorm(x, x8, w_all, b_all, moe_w1, moe_w3, moe_w2, ln_g[layer, 2], ln_b[layer, 2],
                              layer=layer)

    y_prompt = x[:n_p].reshape(bp, tp, d)
    y_sample = x[n_p:].reshape(bs, ts, d)
    return (y_prompt, y_sample,
            jnp.stack(outs["kp"], axis=2), jnp.stack(outs["vp"], axis=2),
            jnp.stack(outs["ks"], axis=2), jnp.stack(outs["vs"], axis=2),
            jnp.stack(outs["cp"], axis=0), jnp.stack(outs["cs"], axis=0),
            jnp.stack(outs["sp"], axis=0), jnp.stack(outs["ss"], axis=0),
            jnp.stack(outs["mk"], axis=0), jnp.stack(outs["mv"], axis=0))
```

```python
import functools
import math

import numpy as np
import jax
import jax.numpy as jnp
from jax import lax
from jax.experimental import pallas as pl
from jax.experimental.pallas import tpu as pltpu

F32 = jnp.float32
BF16 = jnp.bfloat16
I32 = jnp.int32

D_MODEL = 1024
DEPTH = 2
CONV_CH = 512
CONV_WIDTH = 31
MOBA_HEADS = 8
MOBA_HEAD_DIM = 64
MOBA_WIDTH = 512
MOBA_BLOCK = 256
MOBA_TOPK = 3
NUM_BUCKETS = 32
MAX_DISTANCE = 128
RET_HEADS = 4
RET_DK = 256
RET_DV = 512
RET_CHUNK = 128
MEM_HEADS = 4
MEM_HEAD_DIM = 256
MOE_GROUPS = 4
MOE_EPG = 8
MOE_EXPERTS = 32
MOE_TOPK = 2
MOE_D_FF = 512
DEEPNORM_ALPHA = (2 * DEPTH) ** 0.25
LN_EPS = 1e-5

LANES = 128
VMEM_LIMIT = 56 * 1024 * 1024
MOE_TILE = 256
NEG_INF = float("-inf")


def _cparams(sem):
    return pltpu.CompilerParams(dimension_semantics=sem, vmem_limit_bytes=VMEM_LIMIT)


def _layer_norm(y, g, b):
    mu = jnp.mean(y, axis=-1, keepdims=True)
    yc = y - mu
    var = jnp.mean(yc * yc, axis=-1, keepdims=True)
    return yc * lax.rsqrt(var + LN_EPS) * g + b


def _silu(x):
    return x * (1.0 / (1.0 + jnp.exp(-x)))


def _bdot(a, b):
    return jnp.dot(a.astype(BF16), b.astype(BF16), preferred_element_type=F32)


def _bdot_nt(a, b):
    return lax.dot_general(a.astype(BF16), b.astype(BF16), (((1,), (1,)), ((), ())),
                           preferred_element_type=F32)


def _bdot_tn(a, b):
    return lax.dot_general(a.astype(BF16), b.astype(BF16), (((0,), (0,)), ((), ())),
                           preferred_element_type=F32)


def _split3(x):
    hi = x.astype(BF16)
    lo = (x - hi.astype(F32)).astype(BF16)
    return hi, lo


def _dot_hi(a, b, dims=(((1,), (0,)), ((), ()))):
    ah, al = _split3(a)
    bh, bl = _split3(b)
    dg = functools.partial(lax.dot_general, dimension_numbers=dims, preferred_element_type=F32)
    return dg(ah, bh) + (dg(al, bh) + dg(ah, bl))


ROW_TILE = D_MODEL // LANES


def _load_token_tiles(ref, start, n_tok):
    return jnp.concatenate([ref[pl.ds(start + c, n_tok, stride=ROW_TILE), :] for c in range(ROW_TILE)], axis=1)


def _store_token_tiles(ref, val):
    n_tok = val.shape[0]
    for c in range(ROW_TILE):
        ref[pl.ds(c, n_tok, stride=ROW_TILE), :] = val[:, c * LANES:(c + 1) * LANES]


def _tile_copy(src_hbm, src_tok, dst, dst_tok, sem):
    return pltpu.make_async_copy(src_hbm.at[pl.ds(src_tok * ROW_TILE, ROW_TILE), :],
                                 dst.at[pl.ds(dst_tok * ROW_TILE, ROW_TILE), :], sem)


def _wait_tiles(src_hbm, dst, dst_tok, sem, n_tok):
    pltpu.make_async_copy(src_hbm.at[pl.ds(0, n_tok * ROW_TILE), :],
                          dst.at[pl.ds(dst_tok * ROW_TILE, n_tok * ROW_TILE), :], sem).wait()


def _mm_kernel(x_ref, w_ref, o_ref):
    o_ref[...] = _bdot(x_ref[...], w_ref[...]).astype(o_ref.dtype)


def matmul(x, w, *, tm, tn, out_dtype=F32):
    m, k = x.shape
    n = w.shape[1]
    assert m % tm == 0 and n % tn == 0
    return pl.pallas_call(
        _mm_kernel,
        out_shape=jax.ShapeDtypeStruct((m, n), out_dtype),
        grid=(m // tm, n // tn),
        in_specs=[pl.BlockSpec((tm, k), lambda i, j: (i, 0)),
                  pl.BlockSpec((k, tn), lambda i, j: (0, j))],
        out_specs=pl.BlockSpec((tm, tn), lambda i, j: (i, j)),
        compiler_params=_cparams(("parallel", "parallel")),
        name="matmul",
    )(x, w)


def _mm_dn_kernel(*refs, n_lhs, token_tiles, tail_rows):
    lhs = refs[:2 * n_lhs:2]
    tails = refs[1:2 * n_lhs:2]
    ws = refs[2 * n_lhs:3 * n_lhs]
    res_ref, g_ref, b_ref, o_ref = refs[3 * n_lhs:3 * n_lhs + 4]
    last = pl.program_id(0) == pl.num_programs(0) - 1

    def rows(a_ref, t_ref):
        a = a_ref[...]
        if tail_rows:
            a = jnp.where(last, jnp.concatenate([a[:a.shape[0] - tail_rows], t_ref[...]], axis=0), a)
        return a

    acc = _bdot(rows(lhs[0], tails[0]), ws[0][...])
    for a, t, w in zip(lhs[1:], tails[1:], ws[1:]):
        acc = acc + _bdot(rows(a, t), w[...])
    y = DEEPNORM_ALPHA * res_ref[...] + acc
    out = _layer_norm(y, g_ref[...], b_ref[...])
    o_ref[...] = out
    if token_tiles:
        _store_token_tiles(refs[3 * n_lhs + 4], out)


def matmul_deepnorm(lhs_list, w_list, res, g, b, *, tm, token_tiles=False):
    m, d = res.shape
    n_lhs = len(lhs_list)
    tail_rows = lhs_list[0][1].shape[0]
    for a, t in lhs_list:
        assert a.shape[0] + t.shape[0] == m and t.shape[0] == tail_rows and a.shape[1] == t.shape[1]
        assert 0 < tail_rows < tm and (m - tail_rows) // tm == m // tm - 1
    flat_lhs = [x for pair in lhs_list for x in pair]
    in_specs = ([spec for a, t in lhs_list
                 for spec in (pl.BlockSpec((tm, a.shape[1]), lambda i: (i, 0)),
                              pl.BlockSpec(t.shape, lambda i: (0, 0)))]
                + [pl.BlockSpec(w.shape, lambda i: (0, 0)) for w in w_list]
                + [pl.BlockSpec((tm, d), lambda i: (i, 0)),
                   pl.BlockSpec((1, d), lambda i: (0, 0)),
                   pl.BlockSpec((1, d), lambda i: (0, 0))])
    out_shape = [jax.ShapeDtypeStruct((m, d), F32)]
    out_specs = [pl.BlockSpec((tm, d), lambda i: (i, 0))]
    if token_tiles:
        out_shape.append(jax.ShapeDtypeStruct((m * ROW_TILE, LANES), F32))
        out_specs.append(pl.BlockSpec((tm * ROW_TILE, LANES), lambda i: (i, 0)))
    res_out = pl.pallas_call(
        functools.partial(_mm_dn_kernel, n_lhs=n_lhs, token_tiles=token_tiles, tail_rows=tail_rows),
        out_shape=tuple(out_shape),
        grid=(m // tm,),
        in_specs=in_specs,
        out_specs=tuple(out_specs),
        compiler_params=_cparams(("parallel",)),
        name="matmul_deepnorm",
    )(*flat_lhs, *w_list, res, g.reshape(1, d), b.reshape(1, d))
    return res_out if token_tiles else res_out[0]


CONV_PAD = 32


def _conv_kernel(p_ref, hist_ref, w_ref, cb_ref, g_ref, b_ref, c_ref, st_ref, u_scr, *, nb, t_len, rc):
    hw = CONV_WIDTH - 1
    w = w_ref[...]
    for bb in range(nb):
        rows = pl.ds(bb * t_len, t_len)
        a = p_ref[rows, 0:CONV_CH]
        gt = p_ref[rows, CONV_CH:2 * CONV_CH]
        u = a * (1.0 / (1.0 + jnp.exp(-gt)))
        u_scr[0:CONV_PAD, :] = jnp.concatenate(
            [jnp.zeros((CONV_PAD - hw, CONV_CH), F32), hist_ref[bb]], axis=0)
        u_scr[CONV_PAD:CONV_PAD + t_len, :] = u
        st_ref[bb] = u_scr[CONV_PAD + t_len - hw:CONV_PAD + t_len, :]

        def chunk(ci, carry):
            r0 = pl.multiple_of(ci * rc, rc)
            blk = u_scr[pl.ds(r0, rc + CONV_PAD), :]
            acc = jnp.broadcast_to(cb_ref[...], (rc, CONV_CH))
            for j in range(CONV_WIDTH):
                off = j + CONV_PAD - hw
                acc = acc + w[j:j + 1, :] * blk[off:off + rc, :]
            y = _silu(_layer_norm(acc, g_ref[...], b_ref[...]))
            c_ref[pl.ds(bb * t_len + r0, rc), :] = y.astype(c_ref.dtype)
            return carry

        n_chunks = t_len // rc
        if n_chunks == 1:
            chunk(0, 0)
        else:
            lax.fori_loop(0, n_chunks, chunk, 0)


def conv_module(p, hist, conv_w, conv_b, ln_g, ln_b, *, row0, n_batch, t_len, nb, out_dtype):
    rc = min(64, t_len)
    blk_rows = nb * t_len
    assert row0 % blk_rows == 0 and n_batch % nb == 0
    base = row0 // blk_rows
    hw = CONV_WIDTH - 1
    c, st = pl.pallas_call(
        functools.partial(_conv_kernel, nb=nb, t_len=t_len, rc=rc),
        out_shape=(jax.ShapeDtypeStruct((n_batch * t_len, CONV_CH), out_dtype),
                   jax.ShapeDtypeStruct((n_batch, hw, CONV_CH), F32)),
        grid=(n_batch // nb,),
        in_specs=[pl.BlockSpec((blk_rows, 2 * CONV_CH), lambda i: (base + i, 0)),
                  pl.BlockSpec((nb, hw, CONV_CH), lambda i: (i, 0, 0)),
                  pl.BlockSpec((CONV_WIDTH, CONV_CH), lambda i: (0, 0)),
                  pl.BlockSpec((1, CONV_CH), lambda i: (0, 0)),
                  pl.BlockSpec((1, CONV_CH), lambda i: (0, 0)),
                  pl.BlockSpec((1, CONV_CH), lambda i: (0, 0))],
        out_specs=(pl.BlockSpec((blk_rows, CONV_CH), lambda i: (i, 0)),
                   pl.BlockSpec((nb, hw, CONV_CH), lambda i: (i, 0, 0))),
        scratch_shapes=[pltpu.VMEM((t_len + CONV_PAD, CONV_CH), F32)],
        compiler_params=_cparams(("parallel",)),
        name="conv_module",
    )(p, hist, conv_w, conv_b.reshape(1, -1), ln_g.reshape(1, -1), ln_b.reshape(1, -1))
    return c, st


def _bucket_thresholds():
    max_exact = NUM_BUCKETS // 2
    d = np.arange(0, MAX_DISTANCE + 1)
    val = (np.log(np.maximum(d, 1).astype(np.float32) / np.float32(max_exact))
           / np.float32(math.log(MAX_DISTANCE / max_exact)) * np.float32(NUM_BUCKETS - max_exact))
    inner = (d > max_exact) & (d < MAX_DISTANCE)
    assert np.all(np.abs(val[inner] - np.round(val[inner])) > 1e-3)
    bucket = np.where(d < max_exact, d, np.minimum(max_exact + val.astype(np.int32), NUM_BUCKETS - 1))
    return [int(np.argmax(bucket >= k)) for k in range(1, NUM_BUCKETS)]


def _bias_chain(dist, tab_at, thr):
    b = jnp.where(dist >= thr[0], tab_at(1), tab_at(0))
    for k in range(2, NUM_BUCKETS):
        b = jnp.where(dist >= thr[k - 1], tab_at(k), b)
    return b


def _moba_prompt_kernel(tab_ref, q_ref, k_ref, v_ref, o_ref,
                        kmean_scr, kbf_scr, vt_scr, d0_scr, d1_scr, far_scr, sel_scr, qb_scr, m_scr, l_scr, acc_scr,
                        *, n_blk, thr):
    b = pl.program_id(0)
    i = pl.program_id(1)
    blk = MOBA_BLOCK
    hd = MOBA_HEAD_DIM
    n_pair = MOBA_HEADS // 2
    scale = hd ** -0.5
    nt_dims = (((1,), (1,)), ((), ()))
    key = lax.broadcasted_iota(I32, (blk, blk), 0)
    qry = lax.broadcasted_iota(I32, (blk, blk), 1)
    causal = jnp.concatenate([key <= qry, key <= qry], axis=1)

    @pl.when((b == 0) & (i == 0))
    def _():
        for h in range(MOBA_HEADS):
            tab_at = functools.partial(lambda k, hh: tab_ref[hh, k], hh=h)
            half = slice((h % 2) * blk, (h % 2 + 1) * blk)
            d0_scr[h // 2, :, half] = _bias_chain(qry - key, tab_at, thr)
            d1_scr[h // 2, :, half] = _bias_chain(blk + qry - key, tab_at, thr)
            far_scr[h // 2, :, half] = jnp.full((1, blk), tab_ref[h, NUM_BUCKETS - 1], F32)

    @pl.when(i == 0)
    def _():
        for n in range(n_blk):
            kb = k_ref[n * blk:(n + 1) * blk, :]
            kmean_scr[n:n + 1, :] = jnp.sum(kb, axis=0, keepdims=True) * (1.0 / blk)
            kbf_scr[n * blk:(n + 1) * blk, :] = kb.astype(BF16)
            for pair in range(n_pair):
                vt_scr[n, pair * LANES:(pair + 1) * LANES, :] = (
                    v_ref[n * blk:(n + 1) * blk, pair * LANES:(pair + 1) * LANES].T.astype(BF16))

    q = q_ref[...]
    lane = lax.broadcasted_iota(I32, (blk, LANES), 1)
    rown = lax.broadcasted_iota(I32, (n_blk, blk), 0)
    rowd = lax.broadcasted_iota(I32, (LANES, blk), 0)
    r0 = pl.multiple_of(i * blk, blk)
    pairs = [(pr, slice(pr * LANES, (pr + 1) * LANES)) for pr in range(n_pair)]

    for pr, cols in pairs:
        sels = []
        qbs = []
        for sub in range(2):
            in_head = (lane >= sub * hd) & (lane < (sub + 1) * hd)
            qm = jnp.where(in_head, q[:, cols], 0.0)
            gate = _dot_hi(kmean_scr[:, cols], qm, nt_dims)
            rank = jnp.zeros((n_blk, blk), F32)
            for m in range(n_blk):
                gm = gate[m:m + 1, :]
                tie = jnp.where(rown > m, 1.0, 0.0)
                cnt = jnp.where(gm > gate, 1.0, jnp.where(gm == gate, tie, 0.0))
                rank = rank + jnp.where(m < i, cnt, 0.0)
            sels.append(jnp.where((rown < i) & (rank < MOBA_TOPK), 1.0, 0.0))
            qbs.append((qm * scale).astype(BF16))
        sel_scr[pr] = jnp.concatenate(sels, axis=1)
        qb = jnp.concatenate(qbs, axis=0)
        qb_scr[pr] = qb
        s = lax.dot_general(kbf_scr[pl.ds(r0, blk), cols], qb, nt_dims,
                            preferred_element_type=F32) + d0_scr[pr]
        s = jnp.where(causal, s, NEG_INF)
        m_run = jnp.max(s, axis=0, keepdims=True)
        p = jnp.exp(s - m_run)
        m_scr[pr] = m_run
        l_scr[pr] = jnp.sum(p, axis=0, keepdims=True)
        acc_scr[pr] = jnp.dot(vt_scr[i, cols, :], p.astype(BF16), preferred_element_type=F32)

    def merge_block(n, pr, cols, s, shift):
        picked = sel_scr[pr, pl.ds(n, 1), :] > 0.5
        m_run = m_scr[pr]
        m_new = jnp.maximum(m_run, jnp.where(picked, jnp.max(s, axis=0, keepdims=True) + shift, NEG_INF))
        alpha = jnp.exp(m_run - m_new)
        p = jnp.exp(s - jnp.where(picked, m_new - shift, jnp.inf))
        m_scr[pr] = m_new
        l_scr[pr] = alpha * l_scr[pr] + jnp.sum(p, axis=0, keepdims=True)
        acc_scr[pr] = alpha * acc_scr[pr] + jnp.dot(vt_scr[n, cols, :], p.astype(BF16), preferred_element_type=F32)

    @pl.when(i > 0)
    def _():
        rr = pl.multiple_of((i - 1) * blk, blk)
        for pr, cols in pairs:
            s = lax.dot_general(kbf_scr[pl.ds(rr, blk), cols], qb_scr[pr], nt_dims,
                                preferred_element_type=F32) + d1_scr[pr]
            merge_block(i - 1, pr, cols, s, 0.0)

    def body(n, carry):
        rr = pl.multiple_of(n * blk, blk)
        for pr, cols in pairs:
            s = lax.dot_general(kbf_scr[pl.ds(rr, blk), cols], qb_scr[pr], nt_dims, preferred_element_type=F32)
            merge_block(n, pr, cols, s, far_scr[pr])
        return carry

    lax.fori_loop(0, i - 1, body, 0)

    outs = []
    for pr, _ in pairs:
        o = acc_scr[pr] / l_scr[pr]
        outs.append(jnp.where(rowd < hd, o[:, :blk], o[:, blk:]).T)
    o_ref[...] = jnp.concatenate(outs, axis=1).astype(o_ref.dtype)


def moba_prompt(p, tab, *, n_batch, t_len, out_dtype):
    n_blk = t_len // MOBA_BLOCK
    n_pair = MOBA_HEADS // 2
    wide = 2 * MOBA_BLOCK
    return pl.pallas_call(
        functools.partial(_moba_prompt_kernel, n_blk=n_blk, thr=_bucket_thresholds()),
        out_shape=jax.ShapeDtypeStruct((n_batch * t_len, MOBA_WIDTH), out_dtype),
        grid=(n_batch, n_blk),
        in_specs=[pl.BlockSpec(memory_space=pltpu.SMEM),
                  pl.BlockSpec((MOBA_BLOCK, MOBA_WIDTH), lambda b, i: (b * n_blk + i, 2)),
                  pl.BlockSpec((t_len, MOBA_WIDTH), lambda b, i: (b, 3)),
                  pl.BlockSpec((t_len, MOBA_WIDTH), lambda b, i: (b, 4))],
        out_specs=pl.BlockSpec((MOBA_BLOCK, MOBA_WIDTH), lambda b, i: (b * n_blk + i, 0)),
        scratch_shapes=[pltpu.VMEM((n_blk, MOBA_WIDTH), F32),
                        pltpu.VMEM((t_len, MOBA_WIDTH), BF16),
                        pltpu.VMEM((n_blk, MOBA_WIDTH, MOBA_BLOCK), BF16),
                        pltpu.VMEM((n_pair, MOBA_BLOCK, wide), F32),
                        pltpu.VMEM((n_pair, MOBA_BLOCK, wide), F32),
                        pltpu.VMEM((n_pair, 1, wide), F32),
                        pltpu.VMEM((n_pair, n_blk, wide), F32),
                        pltpu.VMEM((n_pair, wide, LANES), BF16),
                        pltpu.VMEM((n_pair, 1, wide), F32),
                        pltpu.VMEM((n_pair, 1, wide), F32),
                        pltpu.VMEM((n_pair, LANES, wide), F32)],
        compiler_params=_cparams(("arbitrary", "arbitrary")),
        name="moba_prompt",
    )(tab, p, p, p)


def _moba_sample_kernel(pt_ref, *refs, n_blk, nbs, page, t_len, thr):
    npg = 2 * nbs
    k_refs = refs[:npg]
    v_refs = refs[npg:2 * npg]
    q_ref, qbd_ref, kn_ref, vn_ref, tab_ref, o_ref, g_scr, m_scr, l_scr, o_scr, bl_scr = refs[2 * npg:]
    n = pl.program_id(1)
    nh, hd = MOBA_HEADS, MOBA_HEAD_DIM
    nrow = nh * t_len
    scale = hd ** -0.5
    nt_dims = (((1,), (1,)), ((), ()))
    lane = lax.broadcasted_iota(I32, (nrow, LANES), 1)
    row = lax.broadcasted_iota(I32, (nrow, LANES), 0)
    tab = tab_ref[...]

    def tab_at(k):
        return tab[:, k:k + 1]

    @pl.when(n == 0)
    def _():
        g_scr[...] = jnp.zeros(g_scr.shape, F32)
        m_scr[...] = jnp.zeros(m_scr.shape, F32)
        l_scr[...] = jnp.zeros(l_scr.shape, F32)
        for half in range(2):
            bl_scr[:, half * page:(half + 1) * page] = _bias_chain(
                MOBA_BLOCK + row % t_len - (half * page + lane), tab_at, thr)

    qbd = (qbd_ref[0] * scale).astype(BF16)
    far = tab_at(NUM_BUCKETS - 1)
    bw = 2 * page
    kt = jnp.concatenate([r[...].reshape(nh * hd, page).astype(BF16) for r in k_refs], axis=1)
    vt = jnp.concatenate([r[...].reshape(nh * hd, page).astype(BF16) for r in v_refs], axis=1)
    s_all = jnp.dot(qbd, kt, preferred_element_type=F32)
    p_rows = []
    stats = []
    for j in range(nbs):
        blk_idx = n * nbs + j
        s_raw = s_all[:, j * bw:(j + 1) * bw]
        gate = jnp.sum(s_raw, axis=1, keepdims=True) * (1.0 / (MOBA_BLOCK * scale))
        s = s_raw + jnp.where(blk_idx == n_blk - 1, bl_scr[...], far)
        m = jnp.max(s, axis=1, keepdims=True)
        p = jnp.exp(s - m)
        l = jnp.sum(p, axis=1, keepdims=True)
        stats.append((blk_idx, gate, m, l))
        zl = jnp.zeros((nrow, j * bw), F32)
        zr = jnp.zeros((nrow, (nbs - 1 - j) * bw), F32)
        p_rows.append(jnp.concatenate([x for x in (zl, p, zr) if x.shape[1]], axis=1))
    p_bd = jnp.concatenate(p_rows, axis=0).astype(BF16)
    pv = lax.dot_general(p_bd, vt, nt_dims, preferred_element_type=F32)
    for j, (blk_idx, gate, m, l) in enumerate(stats):
        at_blk = lane == blk_idx
        g_scr[...] = jnp.where(at_blk, gate, g_scr[...])
        m_scr[...] = jnp.where(at_blk, m, m_scr[...])
        l_scr[...] = jnp.where(at_blk, l, l_scr[...])
        o_scr[blk_idx] = jnp.concatenate(
            [pv[j * nrow + h * t_len:j * nrow + (h + 1) * t_len, h * hd:(h + 1) * hd] for h in range(nh)], axis=0)

    @pl.when(n == n_blk // nbs - 1)
    def _():
        gate = g_scr[...]
        rank = jnp.zeros((nrow, LANES), F32)
        for mm in range(n_blk):
            gm = gate[:, mm:mm + 1]
            tie = jnp.where(lane > mm, 1.0, 0.0)
            rank = rank + jnp.where(gm > gate, 1.0, jnp.where(gm == gate, tie, 0.0))
        sel = (rank < MOBA_TOPK) & (lane < n_blk)

        q = q_ref[0]
        kn = kn_ref[0]
        vn = vn_ref[0]
        tq = lax.broadcasted_iota(I32, (nrow, t_len), 0) % t_len
        sk = lax.broadcasted_iota(I32, (nrow, t_len), 1)
        s_own = jnp.concatenate(
            [lax.dot_general((q[h] * scale).astype(BF16), kn[h].astype(BF16), nt_dims, preferred_element_type=F32)
             for h in range(nh)], axis=0) + _bias_chain(tq - sk, tab_at, thr)
        s_own = jnp.where(sk <= tq, s_own, NEG_INF)
        m_o = jnp.max(s_own, axis=1, keepdims=True)
        p_o = jnp.exp(s_own - m_o)
        l_o = jnp.sum(p_o, axis=1, keepdims=True)
        o_o = jnp.concatenate(
            [jnp.dot(p_o[h * t_len:(h + 1) * t_len].astype(BF16), vn[h].astype(BF16), preferred_element_type=F32)
             for h in range(nh)], axis=0)

        m_sel = jnp.where(sel, m_scr[...], NEG_INF)
        m_fin = jnp.maximum(jnp.max(m_sel, axis=1, keepdims=True), m_o)
        w = jnp.where(sel, jnp.exp(m_sel - m_fin), 0.0)
        w_o = jnp.exp(m_o - m_fin)
        l_fin = jnp.sum(w * l_scr[...], axis=1, keepdims=True) + w_o * l_o
        acc = w_o * o_o
        for nn in range(n_blk):
            acc = acc + w[:, nn:nn + 1] * o_scr[nn]
        o_ref[0] = acc / l_fin


MOBA_SAMPLE_BLOCKS_PER_STEP = 4


def moba_sample(page_table, cache_kt, cache_vt, q, kn, vn, tab_heads, *, layer_e, t_len):
    n_b, n_pages = page_table.shape
    page = cache_kt.shape[-1]
    nbs = MOBA_SAMPLE_BLOCKS_PER_STEP
    ppb = MOBA_BLOCK // page
    assert ppb == 2 and page == LANES and n_pages % (ppb * nbs) == 0
    n_blk = n_pages // ppb
    assert n_blk <= LANES
    nh, hd = MOBA_HEADS, MOBA_HEAD_DIM
    nrow = nh * t_len
    tab = jnp.pad(jnp.repeat(tab_heads, t_len, axis=0), ((0, 0), (0, LANES - NUM_BUCKETS)))
    qbd = (q[:, :, :, None, :] * jnp.eye(nh, dtype=F32)[None, :, None, :, None]).reshape(n_b, nrow, nh * hd)

    def pspec(p):
        return pl.BlockSpec((None, None, nh, hd, page),
                            lambda b, n, pt: (pt[b, n * ppb * nbs + p], layer_e, 0, 0, 0))

    def bspec():
        return pl.BlockSpec((1, nh, t_len, hd), lambda b, n, pt: (b, 0, 0, 0))

    pages = [pspec(p) for p in range(ppb * nbs)]
    grid_spec = pltpu.PrefetchScalarGridSpec(
        num_scalar_prefetch=1,
        grid=(n_b, n_blk // nbs),
        in_specs=pages + pages + [
            bspec(),
            pl.BlockSpec((1, nrow, nh * hd), lambda b, n, pt: (b, 0, 0)),
            bspec(), bspec(),
            pl.BlockSpec(tab.shape, lambda b, n, pt: (0, 0))],
        out_specs=pl.BlockSpec((1, nrow, hd), lambda b, n, pt: (b, 0, 0)),
        scratch_shapes=[pltpu.VMEM((nrow, LANES), F32), pltpu.VMEM((nrow, LANES), F32),
                        pltpu.VMEM((nrow, LANES), F32), pltpu.VMEM((n_blk, nrow, hd), F32),
                        pltpu.VMEM((nrow, MOBA_BLOCK), F32)],
    )
    return pl.pallas_call(
        functools.partial(_moba_sample_kernel, n_blk=n_blk, nbs=nbs, page=page, t_len=t_len,
                          thr=_bucket_thresholds()),
        out_shape=jax.ShapeDtypeStruct((n_b, nrow, hd), F32),
        grid_spec=grid_spec,
        compiler_params=_cparams(("parallel", "arbitrary")),
        name="moba_sample",
    )(page_table, *([cache_kt] * (ppb * nbs)), *([cache_vt] * (ppb * nbs)), q, qbd, kn, vn, tab)


def _retention_kernel(q_ref, k_ref, v_ref, g_ref, cos_ref, sin_ref, dmask_ref, xi_ref, zeta_ref, gc_ref,
                      gng_ref, gnb_ref, *rest, nb, chunk, has_s0):
    if has_s0:
        s0_ref, o_ref, s_out_ref, s_scr = rest
    else:
        o_ref, s_out_ref, s_scr = rest
    c = pl.program_id(1)
    dk, dv = RET_DK, RET_DV
    half = dk // 2

    @pl.when(c == 0)
    def _():
        if has_s0:
            s_scr[...] = s0_ref[...]
        else:
            s_scr[...] = jnp.zeros(s_scr.shape, F32)

    cos = cos_ref[...]
    sin = sin_ref[...]

    def rot(x):
        x1 = x[:, :half]
        x2 = x[:, half:]
        return jnp.concatenate([x1 * cos - x2 * sin, x2 * cos + x1 * sin], axis=1)

    out_rows = []
    for bb in range(nb):
        rows = pl.ds(bb * chunk, chunk)
        out_heads = []
        for h in range(RET_HEADS):
            qh = rot(q_ref[rows, h * dk:(h + 1) * dk])
            kh = rot(k_ref[rows, h * dk:(h + 1) * dk]) * (dk ** -0.5)
            vh = v_ref[rows, h * dv:(h + 1) * dv]
            s = s_scr[bb, h]
            att = _bdot_nt(qh, kh) * dmask_ref[h]
            o = _bdot(att, vh) + _bdot(qh, s) * xi_ref[h]
            s_scr[bb, h] = s * gc_ref[h, 0:1, 0:1] + _bdot_tn(kh * zeta_ref[h], vh)
            mu = jnp.mean(o, axis=-1, keepdims=True)
            oc = o - mu
            var = jnp.mean(oc * oc, axis=-1, keepdims=True)
            on = oc * lax.rsqrt(var + LN_EPS) * gng_ref[:, h * dv:(h + 1) * dv] + gnb_ref[:, h * dv:(h + 1) * dv]
            gate = _silu(g_ref[rows, h * dv:(h + 1) * dv])
            out_heads.append(gate * on)
        out_rows.append(jnp.concatenate(out_heads, axis=1))
    o_ref[...] = jnp.concatenate(out_rows, axis=0).astype(o_ref.dtype)

    @pl.when(c == pl.num_programs(1) - 1)
    def _():
        s_out_ref[...] = s_scr[...]


def retention(p, cos, sin, consts, gn_g, gn_b, s0, *, row0, n_batch, t_len, nb, out_dtype):
    chunk = math.gcd(t_len, RET_CHUNK)
    n_chunk = t_len // chunk
    dmask, xi, zeta, gc = consts
    blk_rows = nb * chunk
    if nb > 1:
        assert n_chunk == 1
    assert row0 % blk_rows == 0
    base = row0 // blk_rows
    qk_w = RET_HEADS * RET_DK
    v_w = RET_HEADS * RET_DV
    has_s0 = s0 is not None

    def rmap(col):
        return lambda b, c: (base + b * n_chunk + c, col)

    in_specs = [pl.BlockSpec((blk_rows, qk_w), rmap(0)),
                pl.BlockSpec((blk_rows, qk_w), rmap(1)),
                pl.BlockSpec((blk_rows, v_w), rmap(1)),
                pl.BlockSpec((blk_rows, v_w), rmap(2)),
                pl.BlockSpec((chunk, RET_DK // 2), lambda b, c: (c, 0)),
                pl.BlockSpec((chunk, RET_DK // 2), lambda b, c: (c, 0)),
                pl.BlockSpec(dmask.shape, lambda b, c: (0, 0, 0)),
                pl.BlockSpec(xi.shape, lambda b, c: (0, 0, 0)),
                pl.BlockSpec(zeta.shape, lambda b, c: (0, 0, 0)),
                pl.BlockSpec(gc.shape, lambda b, c: (0, 0, 0)),
                pl.BlockSpec((1, v_w), lambda b, c: (0, 0)),
                pl.BlockSpec((1, v_w), lambda b, c: (0, 0))]
    args = [p, p, p, p, cos, sin, dmask, xi, zeta, gc, gn_g.reshape(1, -1), gn_b.reshape(1, -1)]
    s_spec = pl.BlockSpec((nb, RET_HEADS, RET_DK, RET_DV), lambda b, c: (b, 0, 0, 0))
    if has_s0:
        in_specs.append(s_spec)
        args.append(s0)
    o, s_out = pl.pallas_call(
        functools.partial(_retention_kernel, nb=nb, chunk=chunk, has_s0=has_s0),
        out_shape=(jax.ShapeDtypeStruct((n_batch * t_len, v_w), out_dtype),
                   jax.ShapeDtypeStruct((n_batch, RET_HEADS, RET_DK, RET_DV), F32)),
        grid=(n_batch // nb, n_chunk),
        in_specs=in_specs,
        out_specs=(pl.BlockSpec((blk_rows, v_w), lambda b, c: (b * n_chunk + c, 0)), s_spec),
        scratch_shapes=[pltpu.VMEM((nb, RET_HEADS, RET_DK, RET_DV), F32)],
        compiler_params=_cparams(("parallel", "arbitrary")),
        name="retention",
    )(*args)
    return o, s_out


def _retention_consts(chunk):
    h = jnp.arange(RET_HEADS, dtype=F32)
    lg = jnp.log1p(-jnp.exp2(-5.0 - h))
    idx = jnp.arange(chunk, dtype=F32)
    diff = idx[:, None] - idx[None, :]
    dmask = jnp.where(diff >= 0, jnp.exp(lg[:, None, None] * jnp.maximum(diff, 0.0)), 0.0)
    xi = jnp.exp(lg[:, None] * (idx[None, :] + 1.0))
    zeta = jnp.exp(lg[:, None] * (chunk - 1.0 - idx[None, :]))
    g_c = jnp.exp(lg * chunk)
    return (dmask,
            jnp.broadcast_to(xi[:, :, None], (RET_HEADS, chunk, RET_DV)),
            jnp.broadcast_to(zeta[:, :, None], (RET_HEADS, chunk, RET_DK)),
            jnp.broadcast_to(g_c[:, None, None], (RET_HEADS, 8, LANES)))


def _rotary_tables(pos):
    half = RET_DK // 2
    inv = 10000.0 ** (-jnp.linspace(0.0, 1.0, half, dtype=F32))
    ang = pos.astype(F32)[:, None] * inv[None, :]
    return jnp.cos(ang), jnp.sin(ang)


def _mem_attn_kernel(q_ref, mk_ref, mv_ref, o_ref, *, nb, tq):
    hd = MEM_HEAD_DIM
    scale = hd ** -0.5
    qf = q_ref[...].astype(F32)
    out_rows = []
    for bb in range(nb):
        out_heads = []
        for h in range(MEM_HEADS):
            cols = slice(h * hd, (h + 1) * hd)
            q = qf[bb * tq:(bb + 1) * tq, cols]
            s = _bdot_nt(q, mk_ref[bb, :, cols]) * scale
            m = jnp.max(s, axis=-1, keepdims=True)
            p = jnp.exp(s - m)
            p = p / jnp.sum(p, axis=-1, keepdims=True)
            out_heads.append(_bdot(p, mv_ref[bb, :, cols]))
        out_rows.append(jnp.concatenate(out_heads, axis=1))
    o_ref[...] = jnp.concatenate(out_rows, axis=0).astype(o_ref.dtype)


def mem_attention(q, mk, mv, *, layer, row0, n_batch, t_len, nb, tq, out_dtype):
    n_t = t_len // tq
    blk_rows = nb * tq
    if nb > 1:
        assert n_t == 1
    assert row0 % blk_rows == 0
    base = row0 // blk_rows
    m_tok = mk.shape[2]
    return pl.pallas_call(
        functools.partial(_mem_attn_kernel, nb=nb, tq=tq),
        out_shape=jax.ShapeDtypeStruct((n_batch * t_len, D_MODEL), out_dtype),
        grid=(n_batch // nb, n_t),
        in_specs=[pl.BlockSpec((blk_rows, D_MODEL), lambda b, t: (base + b * n_t + t, 0)),
                  pl.BlockSpec((None, nb, m_tok, D_MODEL), lambda b, t: (layer, b, 0, 0)),
                  pl.BlockSpec((None, nb, m_tok, D_MODEL), lambda b, t: (layer, b, 0, 0))],
        out_specs=pl.BlockSpec((blk_rows, D_MODEL), lambda b, t: (b * n_t + t, 0)),
        compiler_params=_cparams(("parallel", "arbitrary")),
        name="mem_attention",
    )(q, mk, mv)


def _router_kernel(x_ref, w_ref, b_ref, tri_ref, wt_ref, ei_ref, cnt_ref, base_scr):
    @pl.when(pl.program_id(0) == 0)
    def _():
        base_scr[...] = jnp.zeros(base_scr.shape, F32)

    x = x_ref[...]
    logits = _dot_hi(x, w_ref[...]) + b_ref[...]
    tm = x.shape[0]
    lane = lax.broadcasted_iota(I32, (tm, LANES), 1)
    big = jnp.int32(LANES)

    def masked_softmax(mask):
        lm = jnp.where(mask, logits, NEG_INF)
        mx = jnp.max(lm, axis=-1, keepdims=True)
        e = jnp.exp(lm - mx)
        return e / jnp.sum(e, axis=-1, keepdims=True)

    g_prob = masked_softmax(lane < MOE_GROUPS)
    gp = jnp.max(g_prob, axis=-1, keepdims=True)
    gi = jnp.min(jnp.where(g_prob == gp, lane, big), axis=-1, keepdims=True)
    lo = MOE_GROUPS + gi * MOE_EPG
    in_group = (lane >= lo) & (lane < lo + MOE_EPG)
    e_prob = jnp.where(in_group, masked_softmax(in_group), -1.0)
    p1 = jnp.max(e_prob, axis=-1, keepdims=True)
    i1 = jnp.min(jnp.where(e_prob == p1, lane, big), axis=-1, keepdims=True)
    rest = jnp.where(lane == i1, -1.0, e_prob)
    p2 = jnp.max(rest, axis=-1, keepdims=True)
    i2 = jnp.min(jnp.where(rest == p2, lane, big), axis=-1, keepdims=True)
    tot = p1 + p2
    w1 = gp * (p1 / tot)
    w2 = gp * (p2 / tot)
    e1 = i1 - MOE_GROUPS
    e2 = i2 - MOE_GROUPS
    oh1 = jnp.where(lane == e1, 1.0, 0.0)
    oh2 = jnp.where(lane == e2, 1.0, 0.0)
    both = oh1 + oh2
    base = base_scr[0:1, :]
    before = jnp.dot(tri_ref[...], both.astype(BF16), preferred_element_type=F32) + base
    r1 = jnp.sum(oh1 * before, axis=-1, keepdims=True).astype(I32)
    r2 = jnp.sum(oh2 * before, axis=-1, keepdims=True).astype(I32)
    base_new = base + jnp.sum(both, axis=0, keepdims=True)
    base_scr[0:1, :] = base_new
    wt_ref[...] = jnp.where(lane == 0, w1, jnp.where(lane == 1, w2, 0.0))
    ei_ref[...] = jnp.where(lane == 0, e1, jnp.where(lane == 1, e2, jnp.where(lane == 2, r1,
                                                                              jnp.where(lane == 3, r2, 0))))
    cnt_ref[...] = jnp.broadcast_to(base_new, cnt_ref.shape)


def moe_router(x, w_all, b_all, *, tm):
    m, d = x.shape
    tri = jnp.asarray(np.tril(np.ones((tm, tm), np.float32), -1), BF16)
    return pl.pallas_call(
        _router_kernel,
        out_shape=(jax.ShapeDtypeStruct((m, LANES), F32), jax.ShapeDtypeStruct((m, LANES), I32),
                   jax.ShapeDtypeStruct((8, LANES), F32)),
        grid=(m // tm,),
        in_specs=[pl.BlockSpec((tm, d), lambda i: (i, 0)),
                  pl.BlockSpec((d, LANES), lambda i: (0, 0)),
                  pl.BlockSpec((1, LANES), lambda i: (0, 0)),
                  pl.BlockSpec((tm, tm), lambda i: (0, 0))],
        out_specs=(pl.BlockSpec((tm, LANES), lambda i: (i, 0)), pl.BlockSpec((tm, LANES), lambda i: (i, 0)),
                   pl.BlockSpec((8, LANES), lambda i: (0, 0))),
        scratch_shapes=[pltpu.VMEM((8, LANES), F32)],
        compiler_params=_cparams(("arbitrary",)),
        name="moe_router",
    )(x, w_all, b_all, tri)


def _moe_dispatch_kernel(dest_ref, x_ref, xs_in, xs_out, sem, *, tm):
    del xs_in

    def body(r, carry):
        _tile_copy(x_ref, r, xs_out, dest_ref[r], sem.at[0]).start()
        _tile_copy(x_ref, r, xs_out, dest_ref[tm + r], sem.at[0]).start()
        return carry

    lax.fori_loop(0, tm, body, 0, unroll=8)
    for _ in range(MOE_TOPK):
        _wait_tiles(x_ref, xs_out, 0, sem.at[0], tm)


def moe_dispatch(x8, dest, n_slots, *, tm):
    n_t = dest.shape[0]
    xs0 = jnp.zeros((n_slots * ROW_TILE, LANES), F32)
    return pl.pallas_call(
        functools.partial(_moe_dispatch_kernel, tm=tm),
        out_shape=jax.ShapeDtypeStruct(xs0.shape, F32),
        grid=(n_t,),
        in_specs=[pl.BlockSpec((None, None, 2 * tm), lambda i: (i, 0, 0), memory_space=pltpu.SMEM),
                  pl.BlockSpec((tm * ROW_TILE, LANES), lambda i: (i, 0)),
                  pl.BlockSpec(memory_space=pl.ANY)],
        out_specs=pl.BlockSpec(memory_space=pl.ANY),
        scratch_shapes=[pltpu.SemaphoreType.DMA((1,))],
        input_output_aliases={2: 0},
        compiler_params=_cparams(("arbitrary",)),
        name="moe_dispatch",
    )(dest.reshape(n_t, 1, 2 * tm), x8, xs0)


def _moe_ffn_kernel(te_ref, nt_ref, first_ref, slot_ref, nxt_ref, xs_ref, w1_hbm, w3_hbm, w2_hbm, ys_ref,
                    wb1, wb3, wb2, sem, *, layer):
    j = pl.program_id(0)

    def weight_copies(e, s):
        return (pltpu.make_async_copy(w1_hbm.at[layer, e], wb1.at[s], sem.at[s]),
                pltpu.make_async_copy(w3_hbm.at[layer, e], wb3.at[s], sem.at[s]),
                pltpu.make_async_copy(w2_hbm.at[layer, e], wb2.at[s], sem.at[s]))

    @pl.when(j == 0)
    def _():
        for c in weight_copies(te_ref[0], slot_ref[0]):
            c.start()

    @pl.when(j < nt_ref[0])
    def _():
        s = slot_ref[j]

        @pl.when(first_ref[j] == 1)
        def _():
            for c in weight_copies(te_ref[j], s):
                c.wait()

            @pl.when(nxt_ref[j] >= 0)
            def _():
                for c in weight_copies(nxt_ref[j], 1 - s):
                    c.start()

        x = _load_token_tiles(xs_ref, 0, MOE_TILE).astype(BF16)
        h = _silu(_bdot(x, wb1[s])) * _bdot(x, wb3[s])
        _store_token_tiles(ys_ref, _bdot(h, wb2[s]))

    @pl.when(j >= nt_ref[0])
    def _():
        ys_ref[...] = jnp.zeros(ys_ref.shape, F32)


def moe_ffn(xs8, plan, w1, w3, w2, *, layer):
    tile_expert, n_tiles, first, slot, nxt = plan
    n_t = tile_expert.shape[0]
    d, dff = w1.shape[2], w1.shape[3]
    blk = MOE_TILE * ROW_TILE
    grid_spec = pltpu.PrefetchScalarGridSpec(
        num_scalar_prefetch=5,
        grid=(n_t,),
        in_specs=[pl.BlockSpec((blk, LANES), lambda j, te, nt, *_: (jnp.minimum(j, nt[0] - 1), 0)),
                  pl.BlockSpec(memory_space=pl.ANY),
                  pl.BlockSpec(memory_space=pl.ANY),
                  pl.BlockSpec(memory_space=pl.ANY)],
        out_specs=pl.BlockSpec((blk, LANES), lambda j, te, nt, *_: (j, 0)),
        scratch_shapes=[pltpu.VMEM((2, d, dff), F32), pltpu.VMEM((2, d, dff), F32), pltpu.VMEM((2, dff, d), F32),
                        pltpu.SemaphoreType.DMA((2,))],
    )
    return pl.pallas_call(
        functools.partial(_moe_ffn_kernel, layer=layer),
        out_shape=jax.ShapeDtypeStruct(xs8.shape, F32),
        grid_spec=grid_spec,
        compiler_params=_cparams(("arbitrary",)),
        name="moe_ffn",
    )(tile_expert, n_tiles, first, slot, nxt, xs8, w1, w3, w2)


def _moe_combine_kernel(d_cur, d_nxt, ys_hbm, x_ref, wt_ref, g_ref, b_ref, *rest, tm, lead_tiles):
    if lead_tiles is None:
        o_ref, ybuf, sem = rest
    else:
        o_ref, o_tail_ref, ybuf, sem = rest
    i = pl.program_id(0)
    n = pl.num_programs(0)
    slot = i % 2

    def issue(d_ref, s):
        def body(r, carry):
            _tile_copy(ys_hbm, d_ref[r], ybuf, s * 2 * tm + r, sem.at[s]).start()
            return carry
        lax.fori_loop(0, 2 * tm, body, 0, unroll=8)

    @pl.when(i == 0)
    def _():
        issue(d_cur, 0)

    @pl.when(i + 1 < n)
    def _():
        issue(d_nxt, 1 - slot)

    base_tok = slot * 2 * tm
    _wait_tiles(ys_hbm, ybuf, base_tok, sem.at[slot], 2 * tm)
    wt = wt_ref[...]
    y0 = _load_token_tiles(ybuf, base_tok * ROW_TILE, tm)
    y1 = _load_token_tiles(ybuf, (base_tok + tm) * ROW_TILE, tm)
    y = DEEPNORM_ALPHA * x_ref[...] + (wt[:, 0:1] * y0 + wt[:, 1:2] * y1)
    out = _layer_norm(y, g_ref[...], b_ref[...])
    if lead_tiles is None:
        o_ref[...] = out
    else:
        @pl.when(i < lead_tiles)
        def _():
            o_ref[...] = out

        @pl.when(i >= lead_tiles)
        def _():
            o_tail_ref[...] = out


def moe_combine_deepnorm(ys8, dest, x, wts, g, b, *, tm, lead_rows=None):
    m, d = x.shape
    n_t = m // tm
    dest3 = dest.reshape(n_t, 1, 2 * tm)
    if lead_rows is None:
        lead_tiles = None
        out_shape = jax.ShapeDtypeStruct((m, d), F32)
        out_specs = pl.BlockSpec((tm, d), lambda i: (i, 0))
    else:
        assert lead_rows % tm == 0 and m - lead_rows == tm
        lead_tiles = lead_rows // tm
        out_shape = (jax.ShapeDtypeStruct((lead_rows, d), F32), jax.ShapeDtypeStruct((tm, d), F32))
        out_specs = (pl.BlockSpec((tm, d), lambda i: (jnp.minimum(i, lead_tiles - 1), 0)),
                     pl.BlockSpec((tm, d), lambda i: (0, 0)))
    return pl.pallas_call(
        functools.partial(_moe_combine_kernel, tm=tm, lead_tiles=lead_tiles),
        out_shape=out_shape,
        grid=(n_t,),
        in_specs=[pl.BlockSpec((None, None, 2 * tm), lambda i: (i, 0, 0), memory_space=pltpu.SMEM),
                  pl.BlockSpec((None, None, 2 * tm), lambda i: (jnp.minimum(i + 1, n_t - 1), 0, 0),
                               memory_space=pltpu.SMEM),
                  pl.BlockSpec(memory_space=pl.ANY),
                  pl.BlockSpec((tm, d), lambda i: (i, 0)),
                  pl.BlockSpec((tm, LANES), lambda i: (i, 0)),
                  pl.BlockSpec((1, d), lambda i: (0, 0)),
                  pl.BlockSpec((1, d), lambda i: (0, 0))],
        out_specs=out_specs,
        scratch_shapes=[pltpu.VMEM((2 * 2 * tm * ROW_TILE, LANES), F32), pltpu.SemaphoreType.DMA((2,))],
        compiler_params=_cparams(("arbitrary",)),
        name="moe_combine_deepnorm",
    )(dest3, dest3, ys8, x, wts, g.reshape(1, d), b.reshape(1, d))


def _moe_plan(eidx, rank, counts, n_tok, tm):
    n_tiles_max = n_tok * MOE_TOPK // MOE_TILE + MOE_EXPERTS
    tiles_per = (counts + MOE_TILE - 1) // MOE_TILE
    tile_end = jnp.cumsum(tiles_per)
    pad_off = (tile_end - tiles_per) * MOE_TILE
    onehot = eidx[:, :, None] == jnp.arange(MOE_EXPERTS, dtype=I32)[None, None, :]
    dest = jnp.sum(jnp.where(onehot, pad_off[None, None, :], 0), axis=-1) + rank
    tile_expert = jnp.minimum(
        jnp.sum((jnp.arange(n_tiles_max, dtype=I32)[:, None] >= tile_end[None, :]).astype(I32), axis=1),
        MOE_EXPERTS - 1)
    dest = dest.reshape(n_tok // tm, tm, MOE_TOPK).transpose(0, 2, 1).reshape(n_tok // tm, MOE_TOPK * tm)
    experts = jnp.arange(MOE_EXPERTS, dtype=I32)
    nonempty = tiles_per > 0
    slot_e = (jnp.cumsum(nonempty.astype(I32)) - 1) % 2
    later = nonempty[None, :] & (experts[None, :] > experts[:, None])
    nxt_e = jnp.min(jnp.where(later, experts[None, :], MOE_EXPERTS), axis=1)
    nxt_e = jnp.where(nxt_e == MOE_EXPERTS, -1, nxt_e)
    tiles = jnp.arange(n_tiles_max, dtype=I32)
    tile_oh = tile_expert[:, None] == experts[None, :]
    first = jnp.any((tiles[:, None] == (tile_end - tiles_per)[None, :]) & nonempty[None, :], axis=1).astype(I32)
    slot = jnp.sum(jnp.where(tile_oh, slot_e[None, :], 0), axis=1).astype(I32)
    nxt = jnp.sum(jnp.where(tile_oh, nxt_e[None, :], 0), axis=1).astype(I32)
    ffn_plan = (tile_expert, tile_end[-1:].astype(I32), first, slot, nxt)
    return ffn_plan, dest, n_tiles_max * MOE_TILE


def hier_moe_deepnorm(x, x8, w_all, b_all, w1, w3, w2, g, b, *, layer, lead_rows=None):
    n_tok = x.shape[0]
    wts, ei, cnt = moe_router(x, w_all, b_all, tm=640)
    counts = cnt[0, :MOE_EXPERTS].astype(I32)
    ffn_plan, dest, n_slots = _moe_plan(ei[:, 0:2], ei[:, 2:4], counts, n_tok, MOE_TILE)
    xs8 = moe_dispatch(x8, dest, n_slots, tm=MOE_TILE)
    ys8 = moe_ffn(xs8, ffn_plan, w1, w3, w2, layer=layer)
    return moe_combine_deepnorm(ys8, dest, x, wts, g, b, tm=MOE_TILE, lead_rows=lead_rows)


def kernel(x_prompt, x_sample, cache_moba_k, cache_moba_v, state_conv, state_ret, cache_mem_k, cache_mem_v,
           page_table, mem_prompt, rel_bias, ev_w_in, ev_conv_w, ev_conv_b, ev_conv_ln_g, ev_conv_ln_b,
           ev_w_out, od_w_in, od_gn_g, od_gn_b, od_w_out, mem_wq, mem_wk, mem_wv, mem_wo, ln_g, ln_b,
           moe_w_group, moe_b_group, moe_w_router, moe_b_router, moe_w1, moe_w3, moe_w2):
    bp, tp, d = x_prompt.shape
    bs, ts, _ = x_sample.shape
    n_p = bp * tp
    n_s = bs * ts
    n_all = n_p + n_s
    page = cache_moba_k.shape[1]
    past_len = page_table.shape[1] * page
    nh, hd = MOBA_HEADS, MOBA_HEAD_DIM
    assert tp % MOBA_BLOCK == 0 and past_len % MOBA_BLOCK == 0 and ts <= MOBA_BLOCK
    assert MOBA_BLOCK >= MAX_DISTANCE
    tm = 640
    assert n_all % tm == 0

    x = jnp.concatenate([x_prompt.reshape(n_p, d), x_sample.reshape(n_s, d)], axis=0)

    tab_heads = rel_bias.T
    cache_kt = cache_moba_k.transpose(0, 2, 3, 4, 1)
    cache_vt = cache_moba_v.transpose(0, 2, 3, 4, 1)
    mem_k_cache = cache_mem_k.reshape(DEPTH, bs, cache_mem_k.shape[2], d)
    mem_v_cache = cache_mem_v.reshape(DEPTH, bs, cache_mem_v.shape[2], d)

    pos_p = jnp.arange(tp, dtype=I32)
    pos_s = past_len + jnp.arange(ts, dtype=I32)
    cos_p, sin_p = _rotary_tables(pos_p)
    cos_s, sin_s = _rotary_tables(pos_s)
    ret_c_p = _retention_consts(math.gcd(tp, RET_CHUNK))
    ret_c_s = _retention_consts(math.gcd(ts, RET_CHUNK))

    outs = {}
    for layer in range(DEPTH):
        if layer % 2 == 0:
            e = layer // 2
            proj = matmul(x, ev_w_in[e].astype(BF16), tm=tm, tn=ev_w_in.shape[2])
            k_new = proj[:, 2 * CONV_CH + MOBA_WIDTH:2 * CONV_CH + 2 * MOBA_WIDTH]
            v_new = proj[:, 2 * CONV_CH + 2 * MOBA_WIDTH:]
            q_s = proj[n_p:, 2 * CONV_CH:2 * CONV_CH + MOBA_WIDTH]
            conv_args = (ev_conv_w[e], ev_conv_b[e], ev_conv_ln_g[e], ev_conv_ln_b[e])
            c_p, cst_p = conv_module(proj, jnp.zeros((bp, CONV_WIDTH - 1, CONV_CH), F32), *conv_args,
                                     row0=0, n_batch=bp, t_len=tp, nb=1, out_dtype=BF16)
            c_s, cst_s = conv_module(proj, state_conv[e], *conv_args,
                                     row0=n_p, n_batch=bs, t_len=ts, nb=bs, out_dtype=F32)
            a_p = moba_prompt(proj, tab_heads, n_batch=bp, t_len=tp, out_dtype=BF16)

            def by_head(a):
                return a.reshape(bs, ts, nh, hd).transpose(0, 2, 1, 3)

            a_s = moba_sample(page_table, cache_kt, cache_vt, by_head(q_s), by_head(k_new[n_p:]),
                              by_head(v_new[n_p:]), tab_heads, layer_e=e, t_len=ts)
            a_s = a_s.reshape(bs, nh, ts, hd).transpose(0, 2, 1, 3).reshape(n_s, MOBA_WIDTH).astype(BF16)
            w_out = ev_w_out[e].astype(BF16)
            x = matmul_deepnorm([(c_p, c_s.astype(BF16)), (a_p, a_s)], [w_out[:CONV_CH], w_out[CONV_CH:]], x,
                                ln_g[layer, 0], ln_b[layer, 0], tm=tm)
            outs.setdefault("kp", []).append(k_new[:n_p].reshape(bp, tp, nh, hd))
            outs.setdefault("vp", []).append(v_new[:n_p].reshape(bp, tp, nh, hd))
            outs.setdefault("ks", []).append(k_new[n_p:].reshape(bs, ts, nh, hd))
            outs.setdefault("vs", []).append(v_new[n_p:].reshape(bs, ts, nh, hd))
            outs.setdefault("cp", []).append(cst_p)
            outs.setdefault("cs", []).append(cst_s)
        else:
            o = layer // 2
            proj = matmul(x, od_w_in[o].astype(BF16), tm=tm, tn=2048)
            r_p, st_p = retention(proj, cos_p, sin_p, ret_c_p, od_gn_g[o], od_gn_b[o], None,
                                  row0=0, n_batch=bp, t_len=tp, nb=1, out_dtype=BF16)
            r_s, st_s = retention(proj, cos_s, sin_s, ret_c_s, od_gn_g[o], od_gn_b[o], state_ret[o],
                                  row0=n_p, n_batch=bs, t_len=ts, nb=2, out_dtype=BF16)
            x = matmul_deepnorm([(r_p, r_s)], [od_w_out[o].astype(BF16)], x, ln_g[layer, 0], ln_b[layer, 0], tm=tm)
            outs.setdefault("sp", []).append(st_p)
            outs.setdefault("ss", []).append(st_s)

        m_tok = mem_prompt.shape[1]
        mem2 = mem_prompt.reshape(bp * m_tok, d)
        mk_p = matmul(mem2, mem_wk[layer].astype(BF16), tm=512, tn=d)
        mv_p = matmul(mem2, mem_wv[layer].astype(BF16), tm=512, tn=d)
        outs.setdefault("mk", []).append(mk_p.reshape(bp, m_tok, MEM_HEADS, MEM_HEAD_DIM))
        outs.setdefault("mv", []).append(mv_p.reshape(bp, m_tok, MEM_HEADS, MEM_HEAD_DIM))
        q = matmul(x, mem_wq[layer].astype(BF16), tm=tm, tn=d, out_dtype=BF16)
        o_p = mem_attention(q, mk_p.reshape(1, bp, m_tok, d), mv_p.reshape(1, bp, m_tok, d),
                            layer=0, row0=0, n_batch=bp, t_len=tp, nb=1, tq=512, out_dtype=BF16)
        o_s = mem_attention(q, mem_k_cache, mem_v_cache,
                            layer=layer, row0=n_p, n_batch=bs, t_len=ts, nb=2, tq=ts, out_dtype=BF16)
        x, x8 = matmul_deepnorm([(o_p, o_s)], [mem_wo[layer].astype(BF16)], x, ln_g[layer, 1], ln_b[layer, 1], tm=tm,
                                token_tiles=True)

        w_all = jnp.zeros((d, LANES), F32)
        w_all = w_all.at[:, :MOE_GROUPS].set(moe_w_group[layer])
        w_all = w_all.at[:, MOE_GROUPS:MOE_GROUPS + MOE_EXPERTS].set(moe_w_router[layer])
        b_all = jnp.zeros((1, LANES), F32)
        b_all = b_all.at[0, :MOE_GROUPS].set(moe_b_group[layer])
        b_all = b_all.at[0, MOE_GROUPS:MOE_GROUPS + MOE_EXPERTS].set(moe_b_router[layer])
        x = hier_moe_deepnorm(x, x8, w_all, b_all, moe_w1, moe_w3, moe_w2, ln_g[layer, 2], ln_b[layer, 2],
                              layer=layer, lead_rows=n_p if layer == DEPTH - 1 else None)

    y_prompt = x[0].reshape(bp, tp, d)
    y_sample = x[1].reshape(bs, ts, d)
    return (y_prompt, y_sample,
            jnp.stack(outs["kp"], axis=2), jnp.stack(outs["vp"], axis=2),
            jnp.stack(outs["ks"], axis=2), jnp.stack(outs["vs"], axis=2),
            jnp.stack(outs["cp"], axis=0), jnp.stack(outs["cs"], axis=0),
            jnp.stack(outs["sp"], axis=0), jnp.stack(outs["ss"], axis=0),
            jnp.stack(outs["mk"], axis=0), jnp.stack(outs["mv"], axis=0))
```

```python
import functools
import math

import numpy as np
import jax
import jax.numpy as jnp
from jax import lax
from jax.experimental import pallas as pl
from jax.experimental.pallas import tpu as pltpu

F32 = jnp.float32
BF16 = jnp.bfloat16
I32 = jnp.int32

D_MODEL = 1024
DEPTH = 2
CONV_CH = 512
CONV_WIDTH = 31
MOBA_HEADS = 8
MOBA_HEAD_DIM = 64
MOBA_WIDTH = 512
MOBA_BLOCK = 256
MOBA_TOPK = 3
NUM_BUCKETS = 32
MAX_DISTANCE = 128
RET_HEADS = 4
RET_DK = 256
RET_DV = 512
RET_CHUNK = 128
MEM_HEADS = 4
MEM_HEAD_DIM = 256
MOE_GROUPS = 4
MOE_EPG = 8
MOE_EXPERTS = 32
MOE_TOPK = 2
MOE_D_FF = 512
DEEPNORM_ALPHA = (2 * DEPTH) ** 0.25
LN_EPS = 1e-5

LANES = 128
VMEM_LIMIT = 56 * 1024 * 1024
MOE_TILE = 256
NEG_INF = float("-inf")


def _cparams(sem):
    return pltpu.CompilerParams(dimension_semantics=sem, vmem_limit_bytes=VMEM_LIMIT)


def _layer_norm(y, g, b):
    mu = jnp.mean(y, axis=-1, keepdims=True)
    yc = y - mu
    var = jnp.mean(yc * yc, axis=-1, keepdims=True)
    return yc * lax.rsqrt(var + LN_EPS) * g + b


def _silu(x):
    return x * (1.0 / (1.0 + jnp.exp(-x)))


def _bdot(a, b):
    return jnp.dot(a.astype(BF16), b.astype(BF16), preferred_element_type=F32)


def _bdot_nt(a, b):
    return lax.dot_general(a.astype(BF16), b.astype(BF16), (((1,), (1,)), ((), ())),
                           preferred_element_type=F32)


def _bdot_tn(a, b):
    return lax.dot_general(a.astype(BF16), b.astype(BF16), (((0,), (0,)), ((), ())),
                           preferred_element_type=F32)


def _split3(x):
    hi = x.astype(BF16)
    lo = (x - hi.astype(F32)).astype(BF16)
    return hi, lo


def _dot_hi(a, b, dims=(((1,), (0,)), ((), ()))):
    ah, al = _split3(a)
    bh, bl = _split3(b)
    dg = functools.partial(lax.dot_general, dimension_numbers=dims, preferred_element_type=F32)
    return dg(ah, bh) + (dg(al, bh) + dg(ah, bl))


U32 = jnp.uint32
ROW_TILE = D_MODEL // (2 * LANES)
HALF = D_MODEL // 2


def _bf16_bits(v):
    b = pltpu.bitcast(v, U32)
    return b + (jnp.uint32(0x7FFF) + ((b >> 16) & jnp.uint32(1)))


def _load_token_tiles(ref, start, n_tok):
    u = jnp.concatenate([ref[pl.ds(start + c, n_tok, stride=ROW_TILE), :] for c in range(ROW_TILE)], axis=1)
    lo = pltpu.bitcast(u << 16, F32)
    hi = pltpu.bitcast(u & jnp.uint32(0xFFFF0000), F32)
    return jnp.concatenate([lo, hi], axis=1)


def _store_token_tiles(ref, val):
    n_tok = val.shape[0]
    packed = (_bf16_bits(val[:, :HALF]) >> 16) | (_bf16_bits(val[:, HALF:]) & jnp.uint32(0xFFFF0000))
    for c in range(ROW_TILE):
        ref[pl.ds(c, n_tok, stride=ROW_TILE), :] = packed[:, c * LANES:(c + 1) * LANES]


def _tile_copy(src_hbm, src_tok, dst, dst_tok, sem):
    return pltpu.make_async_copy(src_hbm.at[pl.ds(src_tok * ROW_TILE, ROW_TILE), :],
                                 dst.at[pl.ds(dst_tok * ROW_TILE, ROW_TILE), :], sem)


def _wait_tiles(src_hbm, dst, dst_tok, sem, n_tok):
    pltpu.make_async_copy(src_hbm.at[pl.ds(0, n_tok * ROW_TILE), :],
                          dst.at[pl.ds(dst_tok * ROW_TILE, n_tok * ROW_TILE), :], sem).wait()


def _mm_kernel(x_ref, w_ref, o_ref):
    o_ref[...] = _bdot(x_ref[...], w_ref[...]).astype(o_ref.dtype)


def matmul(x, w, *, tm, tn, out_dtype=F32):
    m, k = x.shape
    n = w.shape[1]
    assert m % tm == 0 and n % tn == 0
    return pl.pallas_call(
        _mm_kernel,
        out_shape=jax.ShapeDtypeStruct((m, n), out_dtype),
        grid=(m // tm, n // tn),
        in_specs=[pl.BlockSpec((tm, k), lambda i, j: (i, 0)),
                  pl.BlockSpec((k, tn), lambda i, j: (0, j))],
        out_specs=pl.BlockSpec((tm, tn), lambda i, j: (i, j)),
        compiler_params=_cparams(("parallel", "parallel")),
        name="matmul",
    )(x, w)


def _mm_dn_kernel(*refs, n_lhs, token_tiles, tail_rows):
    lhs = refs[:2 * n_lhs:2]
    tails = refs[1:2 * n_lhs:2]
    ws = refs[2 * n_lhs:3 * n_lhs]
    res_ref, g_ref, b_ref, o_ref = refs[3 * n_lhs:3 * n_lhs + 4]
    last = pl.program_id(0) == pl.num_programs(0) - 1

    def rows(a_ref, t_ref):
        a = a_ref[...]
        if tail_rows:
            a = jnp.where(last, jnp.concatenate([a[:a.shape[0] - tail_rows], t_ref[...]], axis=0), a)
        return a

    acc = _bdot(rows(lhs[0], tails[0]), ws[0][...])
    for a, t, w in zip(lhs[1:], tails[1:], ws[1:]):
        acc = acc + _bdot(rows(a, t), w[...])
    y = DEEPNORM_ALPHA * res_ref[...] + acc
    out = _layer_norm(y, g_ref[...], b_ref[...])
    o_ref[...] = out
    if token_tiles:
        _store_token_tiles(refs[3 * n_lhs + 4], out)


def matmul_deepnorm(lhs_list, w_list, res, g, b, *, tm, token_tiles=False):
    m, d = res.shape
    n_lhs = len(lhs_list)
    tail_rows = lhs_list[0][1].shape[0]
    for a, t in lhs_list:
        assert a.shape[0] + t.shape[0] == m and t.shape[0] == tail_rows and a.shape[1] == t.shape[1]
        assert 0 < tail_rows < tm and (m - tail_rows) // tm == m // tm - 1
    flat_lhs = [x for pair in lhs_list for x in pair]
    in_specs = ([spec for a, t in lhs_list
                 for spec in (pl.BlockSpec((tm, a.shape[1]), lambda i: (i, 0)),
                              pl.BlockSpec(t.shape, lambda i: (0, 0)))]
                + [pl.BlockSpec(w.shape, lambda i: (0, 0)) for w in w_list]
                + [pl.BlockSpec((tm, d), lambda i: (i, 0)),
                   pl.BlockSpec((1, d), lambda i: (0, 0)),
                   pl.BlockSpec((1, d), lambda i: (0, 0))])
    out_shape = [jax.ShapeDtypeStruct((m, d), F32)]
    out_specs = [pl.BlockSpec((tm, d), lambda i: (i, 0))]
    if token_tiles:
        out_shape.append(jax.ShapeDtypeStruct((m * ROW_TILE, LANES), U32))
        out_specs.append(pl.BlockSpec((tm * ROW_TILE, LANES), lambda i: (i, 0)))
    res_out = pl.pallas_call(
        functools.partial(_mm_dn_kernel, n_lhs=n_lhs, token_tiles=token_tiles, tail_rows=tail_rows),
        out_shape=tuple(out_shape),
        grid=(m // tm,),
        in_specs=in_specs,
        out_specs=tuple(out_specs),
        compiler_params=_cparams(("parallel",)),
        name="matmul_deepnorm",
    )(*flat_lhs, *w_list, res, g.reshape(1, d), b.reshape(1, d))
    return res_out if token_tiles else res_out[0]


CONV_PAD = 32


def _conv_kernel(p_ref, hist_ref, w_ref, cb_ref, g_ref, b_ref, c_ref, st_ref, u_scr, sh_scr, *, nb, t_len, rc):
    hw = CONV_WIDTH - 1
    w = w_ref[...]
    for bb in range(nb):
        rows = pl.ds(bb * t_len, t_len)
        a = p_ref[rows, 0:CONV_CH]
        gt = p_ref[rows, CONV_CH:2 * CONV_CH]
        u = a * (1.0 / (1.0 + jnp.exp(-gt)))
        u_scr[0:CONV_PAD, :] = jnp.concatenate(
            [jnp.zeros((CONV_PAD - hw, CONV_CH), F32), hist_ref[bb]], axis=0)
        u_scr[CONV_PAD:CONV_PAD + t_len, :] = u
        st_ref[bb] = u_scr[CONV_PAD + t_len - hw:CONV_PAD + t_len, :]

        def chunk(ci, carry):
            r0 = pl.multiple_of(ci * rc, rc)
            span = rc + CONV_PAD - 8
            for r in range(1, 8):
                sh_scr[r - 1] = u_scr[pl.ds(r0, rc + CONV_PAD), :][r:r + span, :]
            acc = jnp.broadcast_to(cb_ref[...], (rc, CONV_CH))
            for j in range(CONV_WIDTH):
                off = j + CONV_PAD - hw
                r = off % 8
                if r == 0:
                    tap = u_scr[pl.ds(r0 + off, rc), :]
                else:
                    tap = sh_scr[r - 1, off - r:off - r + rc, :]
                acc = acc + w[j:j + 1, :] * tap
            y = _silu(_layer_norm(acc, g_ref[...], b_ref[...]))
            c_ref[pl.ds(bb * t_len + r0, rc), :] = y.astype(c_ref.dtype)
            return carry

        n_chunks = t_len // rc
        if n_chunks == 1:
            chunk(0, 0)
        else:
            lax.fori_loop(0, n_chunks, chunk, 0)


def conv_module(p, hist, conv_w, conv_b, ln_g, ln_b, *, row0, n_batch, t_len, nb, out_dtype):
    rc = min(256, t_len)
    blk_rows = nb * t_len
    assert row0 % blk_rows == 0 and n_batch % nb == 0
    base = row0 // blk_rows
    hw = CONV_WIDTH - 1
    c, st = pl.pallas_call(
        functools.partial(_conv_kernel, nb=nb, t_len=t_len, rc=rc),
        out_shape=(jax.ShapeDtypeStruct((n_batch * t_len, CONV_CH), out_dtype),
                   jax.ShapeDtypeStruct((n_batch, hw, CONV_CH), F32)),
        grid=(n_batch // nb,),
        in_specs=[pl.BlockSpec((blk_rows, 2 * CONV_CH), lambda i: (base + i, 0)),
                  pl.BlockSpec((nb, hw, CONV_CH), lambda i: (i, 0, 0)),
                  pl.BlockSpec((CONV_WIDTH, CONV_CH), lambda i: (0, 0)),
                  pl.BlockSpec((1, CONV_CH), lambda i: (0, 0)),
                  pl.BlockSpec((1, CONV_CH), lambda i: (0, 0)),
                  pl.BlockSpec((1, CONV_CH), lambda i: (0, 0))],
        out_specs=(pl.BlockSpec((blk_rows, CONV_CH), lambda i: (i, 0)),
                   pl.BlockSpec((nb, hw, CONV_CH), lambda i: (i, 0, 0))),
        scratch_shapes=[pltpu.VMEM((t_len + CONV_PAD, CONV_CH), F32),
                        pltpu.VMEM((7, rc + CONV_PAD - 8, CONV_CH), F32)],
        compiler_params=_cparams(("parallel",)),
        name="conv_module",
    )(p, hist, conv_w, conv_b.reshape(1, -1), ln_g.reshape(1, -1), ln_b.reshape(1, -1))
    return c, st


def _bucket_thresholds():
    max_exact = NUM_BUCKETS // 2
    d = np.arange(0, MAX_DISTANCE + 1)
    val = (np.log(np.maximum(d, 1).astype(np.float32) / np.float32(max_exact))
           / np.float32(math.log(MAX_DISTANCE / max_exact)) * np.float32(NUM_BUCKETS - max_exact))
    inner = (d > max_exact) & (d < MAX_DISTANCE)
    assert np.all(np.abs(val[inner] - np.round(val[inner])) > 1e-3)
    bucket = np.where(d < max_exact, d, np.minimum(max_exact + val.astype(np.int32), NUM_BUCKETS - 1))
    return [int(np.argmax(bucket >= k)) for k in range(1, NUM_BUCKETS)]


def _bias_chain(dist, tab_at, thr):
    b = jnp.where(dist >= thr[0], tab_at(1), tab_at(0))
    for k in range(2, NUM_BUCKETS):
        b = jnp.where(dist >= thr[k - 1], tab_at(k), b)
    return b


def _moba_prompt_kernel(tab_ref, q_ref, k_ref, v_ref, o_ref,
                        kmean_scr, kbf_scr, vt_scr, d0_scr, d1_scr, far_scr, sel_scr, qb_scr, m_scr, l_scr, acc_scr,
                        *, n_blk, thr):
    b = pl.program_id(0)
    i = pl.program_id(1)
    blk = MOBA_BLOCK
    hd = MOBA_HEAD_DIM
    n_pair = MOBA_HEADS // 2
    scale = hd ** -0.5
    nt_dims = (((1,), (1,)), ((), ()))
    key = lax.broadcasted_iota(I32, (blk, blk), 0)
    qry = lax.broadcasted_iota(I32, (blk, blk), 1)
    causal = jnp.concatenate([key <= qry, key <= qry], axis=1)

    @pl.when((b == 0) & (i == 0))
    def _():
        for h in range(MOBA_HEADS):
            tab_at = functools.partial(lambda k, hh: tab_ref[hh, k], hh=h)
            half = slice((h % 2) * blk, (h % 2 + 1) * blk)
            d0_scr[h // 2, :, half] = _bias_chain(qry - key, tab_at, thr)
            d1_scr[h // 2, :, half] = _bias_chain(blk + qry - key, tab_at, thr)
            far_scr[h // 2, :, half] = jnp.full((1, blk), tab_ref[h, NUM_BUCKETS - 1], F32)

    @pl.when(i == 0)
    def _():
        for n in range(n_blk):
            kb = k_ref[n * blk:(n + 1) * blk, :]
            kmean_scr[n:n + 1, :] = jnp.sum(kb, axis=0, keepdims=True) * (1.0 / blk)
            kbf_scr[n * blk:(n + 1) * blk, :] = kb.astype(BF16)
            for pair in range(n_pair):
                vt_scr[n, pair * LANES:(pair + 1) * LANES, :] = (
                    v_ref[n * blk:(n + 1) * blk, pair * LANES:(pair + 1) * LANES].T.astype(BF16))

    q = q_ref[...]
    lane = lax.broadcasted_iota(I32, (blk, LANES), 1)
    rown = lax.broadcasted_iota(I32, (n_blk, blk), 0)
    rowd = lax.broadcasted_iota(I32, (LANES, blk), 0)
    r0 = pl.multiple_of(i * blk, blk)
    pairs = [(pr, slice(pr * LANES, (pr + 1) * LANES)) for pr in range(n_pair)]

    for pr, cols in pairs:
        sels = []
        qbs = []
        for sub in range(2):
            in_head = (lane >= sub * hd) & (lane < (sub + 1) * hd)
            qm = jnp.where(in_head, q[:, cols], 0.0)
            gate = _dot_hi(kmean_scr[:, cols], qm, nt_dims)
            rank = jnp.zeros((n_blk, blk), F32)
            for m in range(n_blk):
                gm = gate[m:m + 1, :]
                tie = jnp.where(rown > m, 1.0, 0.0)
                cnt = jnp.where(gm > gate, 1.0, jnp.where(gm == gate, tie, 0.0))
                rank = rank + jnp.where(m < i, cnt, 0.0)
            sels.append(jnp.where((rown < i) & (rank < MOBA_TOPK), 1.0, 0.0))
            qbs.append((qm * scale).astype(BF16))
        sel_scr[pr] = jnp.concatenate(sels, axis=1)
        qb = jnp.concatenate(qbs, axis=0)
        qb_scr[pr] = qb
        s = lax.dot_general(kbf_scr[pl.ds(r0, blk), cols], qb, nt_dims,
                            preferred_element_type=F32) + d0_scr[pr]
        s = jnp.where(causal, s, NEG_INF)
        m_run = jnp.max(s, axis=0, keepdims=True)
        p = jnp.exp(s - m_run)
        m_scr[pr] = m_run
        l_scr[pr] = jnp.sum(p, axis=0, keepdims=True)
        acc_scr[pr] = jnp.dot(vt_scr[i, cols, :], p.astype(BF16), preferred_element_type=F32)

    def merge_block(n, pr, cols, s, shift):
        picked = sel_scr[pr, pl.ds(n, 1), :] > 0.5
        m_run = m_scr[pr]
        m_new = jnp.maximum(m_run, jnp.where(picked, jnp.max(s, axis=0, keepdims=True) + shift, NEG_INF))
        alpha = jnp.exp(m_run - m_new)
        p = jnp.exp(s - jnp.where(picked, m_new - shift, jnp.inf))
        m_scr[pr] = m_new
        l_scr[pr] = alpha * l_scr[pr] + jnp.sum(p, axis=0, keepdims=True)
        acc_scr[pr] = alpha * acc_scr[pr] + jnp.dot(vt_scr[n, cols, :], p.astype(BF16), preferred_element_type=F32)

    @pl.when(i > 0)
    def _():
        rr = pl.multiple_of((i - 1) * blk, blk)
        for pr, cols in pairs:
            s = lax.dot_general(kbf_scr[pl.ds(rr, blk), cols], qb_scr[pr], nt_dims,
                                preferred_element_type=F32) + d1_scr[pr]
            merge_block(i - 1, pr, cols, s, 0.0)

    def body(n, carry):
        rr = pl.multiple_of(n * blk, blk)
        for pr, cols in pairs:
            s = lax.dot_general(kbf_scr[pl.ds(rr, blk), cols], qb_scr[pr], nt_dims, preferred_element_type=F32)
            merge_block(n, pr, cols, s, far_scr[pr])
        return carry

    lax.fori_loop(0, i - 1, body, 0)

    outs = []
    for pr, _ in pairs:
        o = acc_scr[pr] / l_scr[pr]
        outs.append(jnp.where(rowd < hd, o[:, :blk], o[:, blk:]).T)
    o_ref[...] = jnp.concatenate(outs, axis=1).astype(o_ref.dtype)


def moba_prompt(p, tab, *, n_batch, t_len, out_dtype):
    n_blk = t_len // MOBA_BLOCK
    n_pair = MOBA_HEADS // 2
    wide = 2 * MOBA_BLOCK
    return pl.pallas_call(
        functools.partial(_moba_prompt_kernel, n_blk=n_blk, thr=_bucket_thresholds()),
        out_shape=jax.ShapeDtypeStruct((n_batch * t_len, MOBA_WIDTH), out_dtype),
        grid=(n_batch, n_blk),
        in_specs=[pl.BlockSpec(memory_space=pltpu.SMEM),
                  pl.BlockSpec((MOBA_BLOCK, MOBA_WIDTH), lambda b, i: (b * n_blk + i, 2)),
                  pl.BlockSpec((t_len, MOBA_WIDTH), lambda b, i: (b, 3)),
                  pl.BlockSpec((t_len, MOBA_WIDTH), lambda b, i: (b, 4))],
        out_specs=pl.BlockSpec((MOBA_BLOCK, MOBA_WIDTH), lambda b, i: (b * n_blk + i, 0)),
        scratch_shapes=[pltpu.VMEM((n_blk, MOBA_WIDTH), F32),
                        pltpu.VMEM((t_len, MOBA_WIDTH), BF16),
                        pltpu.VMEM((n_blk, MOBA_WIDTH, MOBA_BLOCK), BF16),
                        pltpu.VMEM((n_pair, MOBA_BLOCK, wide), F32),
                        pltpu.VMEM((n_pair, MOBA_BLOCK, wide), F32),
                        pltpu.VMEM((n_pair, 1, wide), F32),
                        pltpu.VMEM((n_pair, n_blk, wide), F32),
                        pltpu.VMEM((n_pair, wide, LANES), BF16),
                        pltpu.VMEM((n_pair, 1, wide), F32),
                        pltpu.VMEM((n_pair, 1, wide), F32),
                        pltpu.VMEM((n_pair, LANES, wide), F32)],
        compiler_params=_cparams(("arbitrary", "arbitrary")),
        name="moba_prompt",
    )(tab, p, p, p)


def _moba_sample_kernel(pt_ref, *refs, n_blk, nbs, page, t_len, thr):
    npg = 2 * nbs
    k_refs = refs[:npg]
    v_refs = refs[npg:2 * npg]
    q_ref, qbd_ref, kn_ref, vn_ref, tab_ref, o_ref, g_scr, m_scr, l_scr, o_scr, bl_scr = refs[2 * npg:]
    n = pl.program_id(1)
    nh, hd = MOBA_HEADS, MOBA_HEAD_DIM
    nrow = nh * t_len
    scale = hd ** -0.5
    nt_dims = (((1,), (1,)), ((), ()))
    lane = lax.broadcasted_iota(I32, (nrow, LANES), 1)
    row = lax.broadcasted_iota(I32, (nrow, LANES), 0)
    tab = tab_ref[...]

    def tab_at(k):
        return tab[:, k:k + 1]

    @pl.when(n == 0)
    def _():
        g_scr[...] = jnp.zeros(g_scr.shape, F32)
        m_scr[...] = jnp.zeros(m_scr.shape, F32)
        l_scr[...] = jnp.zeros(l_scr.shape, F32)
        for half in range(2):
            bl_scr[:, half * page:(half + 1) * page] = _bias_chain(
                MOBA_BLOCK + row % t_len - (half * page + lane), tab_at, thr)

    qbd = (qbd_ref[0] * scale).astype(BF16)
    far = tab_at(NUM_BUCKETS - 1)
    bw = 2 * page
    kt = jnp.concatenate([r[...].reshape(nh * hd, page).astype(BF16) for r in k_refs], axis=1)
    vt = jnp.concatenate([r[...].reshape(nh * hd, page).astype(BF16) for r in v_refs], axis=1)
    s_all = jnp.dot(qbd, kt, preferred_element_type=F32)
    p_rows = []
    stats = []
    for j in range(nbs):
        blk_idx = n * nbs + j
        s_raw = s_all[:, j * bw:(j + 1) * bw]
        gate = jnp.sum(s_raw, axis=1, keepdims=True) * (1.0 / (MOBA_BLOCK * scale))
        s = s_raw + jnp.where(blk_idx == n_blk - 1, bl_scr[...], far)
        m = jnp.max(s, axis=1, keepdims=True)
        p = jnp.exp(s - m)
        l = jnp.sum(p, axis=1, keepdims=True)
        stats.append((blk_idx, gate, m, l))
        zl = jnp.zeros((nrow, j * bw), F32)
        zr = jnp.zeros((nrow, (nbs - 1 - j) * bw), F32)
        p_rows.append(jnp.concatenate([x for x in (zl, p, zr) if x.shape[1]], axis=1))
    p_bd = jnp.concatenate(p_rows, axis=0).astype(BF16)
    pv = lax.dot_general(p_bd, vt, nt_dims, preferred_element_type=F32)
    for j, (blk_idx, gate, m, l) in enumerate(stats):
        at_blk = lane == blk_idx
        g_scr[...] = jnp.where(at_blk, gate, g_scr[...])
        m_scr[...] = jnp.where(at_blk, m, m_scr[...])
        l_scr[...] = jnp.where(at_blk, l, l_scr[...])
        o_scr[blk_idx] = jnp.concatenate(
            [pv[j * nrow + h * t_len:j * nrow + (h + 1) * t_len, h * hd:(h + 1) * hd] for h in range(nh)], axis=0)

    @pl.when(n == n_blk // nbs - 1)
    def _():
        gate = g_scr[...]
        rank = jnp.zeros((nrow, LANES), F32)
        for mm in range(n_blk):
            gm = gate[:, mm:mm + 1]
            tie = jnp.where(lane > mm, 1.0, 0.0)
            rank = rank + jnp.where(gm > gate, 1.0, jnp.where(gm == gate, tie, 0.0))
        sel = (rank < MOBA_TOPK) & (lane < n_blk)

        q = q_ref[0]
        kn = kn_ref[0]
        vn = vn_ref[0]
        tq = lax.broadcasted_iota(I32, (nrow, t_len), 0) % t_len
        sk = lax.broadcasted_iota(I32, (nrow, t_len), 1)
        s_own = jnp.concatenate(
            [lax.dot_general((q[h] * scale).astype(BF16), kn[h].astype(BF16), nt_dims, preferred_element_type=F32)
             for h in range(nh)], axis=0) + _bias_chain(tq - sk, tab_at, thr)
        s_own = jnp.where(sk <= tq, s_own, NEG_INF)
        m_o = jnp.max(s_own, axis=1, keepdims=True)
        p_o = jnp.exp(s_own - m_o)
        l_o = jnp.sum(p_o, axis=1, keepdims=True)
        o_o = jnp.concatenate(
            [jnp.dot(p_o[h * t_len:(h + 1) * t_len].astype(BF16), vn[h].astype(BF16), preferred_element_type=F32)
             for h in range(nh)], axis=0)

        m_sel = jnp.where(sel, m_scr[...], NEG_INF)
        m_fin = jnp.maximum(jnp.max(m_sel, axis=1, keepdims=True), m_o)
        w = jnp.where(sel, jnp.exp(m_sel - m_fin), 0.0)
        w_o = jnp.exp(m_o - m_fin)
        l_fin = jnp.sum(w * l_scr[...], axis=1, keepdims=True) + w_o * l_o
        acc = w_o * o_o
        for nn in range(n_blk):
            acc = acc + w[:, nn:nn + 1] * o_scr[nn]
        o_ref[0] = acc / l_fin


MOBA_SAMPLE_BLOCKS_PER_STEP = 4


def moba_sample(page_table, cache_kt, cache_vt, q, kn, vn, tab_heads, *, layer_e, t_len):
    n_b, n_pages = page_table.shape
    page = cache_kt.shape[-1]
    nbs = MOBA_SAMPLE_BLOCKS_PER_STEP
    ppb = MOBA_BLOCK // page
    assert ppb == 2 and page == LANES and n_pages % (ppb * nbs) == 0
    n_blk = n_pages // ppb
    assert n_blk <= LANES
    nh, hd = MOBA_HEADS, MOBA_HEAD_DIM
    nrow = nh * t_len
    tab = jnp.pad(jnp.repeat(tab_heads, t_len, axis=0), ((0, 0), (0, LANES - NUM_BUCKETS)))
    qbd = (q[:, :, :, None, :] * jnp.eye(nh, dtype=F32)[None, :, None, :, None]).reshape(n_b, nrow, nh * hd)

    def pspec(p):
        return pl.BlockSpec((None, None, nh, hd, page),
                            lambda b, n, pt: (pt[b, n * ppb * nbs + p], layer_e, 0, 0, 0))

    def bspec():
        return pl.BlockSpec((1, nh, t_len, hd), lambda b, n, pt: (b, 0, 0, 0))

    pages = [pspec(p) for p in range(ppb * nbs)]
    grid_spec = pltpu.PrefetchScalarGridSpec(
        num_scalar_prefetch=1,
        grid=(n_b, n_blk // nbs),
        in_specs=pages + pages + [
            bspec(),
            pl.BlockSpec((1, nrow, nh * hd), lambda b, n, pt: (b, 0, 0)),
            bspec(), bspec(),
            pl.BlockSpec(tab.shape, lambda b, n, pt: (0, 0))],
        out_specs=pl.BlockSpec((1, nrow, hd), lambda b, n, pt: (b, 0, 0)),
        scratch_shapes=[pltpu.VMEM((nrow, LANES), F32), pltpu.VMEM((nrow, LANES), F32),
                        pltpu.VMEM((nrow, LANES), F32), pltpu.VMEM((n_blk, nrow, hd), F32),
                        pltpu.VMEM((nrow, MOBA_BLOCK), F32)],
    )
    return pl.pallas_call(
        functools.partial(_moba_sample_kernel, n_blk=n_blk, nbs=nbs, page=page, t_len=t_len,
                          thr=_bucket_thresholds()),
        out_shape=jax.ShapeDtypeStruct((n_b, nrow, hd), F32),
        grid_spec=grid_spec,
        compiler_params=_cparams(("parallel", "arbitrary")),
        name="moba_sample",
    )(page_table, *([cache_kt] * (ppb * nbs)), *([cache_vt] * (ppb * nbs)), q, qbd, kn, vn, tab)


def _retention_kernel(q_ref, k_ref, v_ref, g_ref, cos_ref, sin_ref, dmask_ref, xi_ref, zeta_ref, gc_ref,
                      gng_ref, gnb_ref, *rest, nb, chunk, has_s0):
    if has_s0:
        s0_ref, o_ref, s_out_ref, s_scr = rest
    else:
        o_ref, s_out_ref, s_scr = rest
    c = pl.program_id(1)
    dk, dv = RET_DK, RET_DV
    half = dk // 2

    @pl.when(c == 0)
    def _():
        if has_s0:
            s_scr[...] = s0_ref[...]
        else:
            s_scr[...] = jnp.zeros(s_scr.shape, F32)

    cos = cos_ref[...]
    sin = sin_ref[...]

    def rot(x):
        x1 = x[:, :half]
        x2 = x[:, half:]
        return jnp.concatenate([x1 * cos - x2 * sin, x2 * cos + x1 * sin], axis=1)

    out_rows = []
    for bb in range(nb):
        rows = pl.ds(bb * chunk, chunk)
        out_heads = []
        for h in range(RET_HEADS):
            qh = rot(q_ref[rows, h * dk:(h + 1) * dk])
            kh = rot(k_ref[rows, h * dk:(h + 1) * dk]) * (dk ** -0.5)
            vh = v_ref[rows, h * dv:(h + 1) * dv]
            s = s_scr[bb, h]
            att = _bdot_nt(qh, kh) * dmask_ref[h]
            o = _bdot(att, vh) + _bdot(qh, s) * xi_ref[h]
            s_scr[bb, h] = s * gc_ref[h, 0:1, 0:1] + _bdot_tn(kh * zeta_ref[h], vh)
            mu = jnp.mean(o, axis=-1, keepdims=True)
            oc = o - mu
            var = jnp.mean(oc * oc, axis=-1, keepdims=True)
            on = oc * lax.rsqrt(var + LN_EPS) * gng_ref[:, h * dv:(h + 1) * dv] + gnb_ref[:, h * dv:(h + 1) * dv]
            gate = _silu(g_ref[rows, h * dv:(h + 1) * dv])
            out_heads.append(gate * on)
        out_rows.append(jnp.concatenate(out_heads, axis=1))
    o_ref[...] = jnp.concatenate(out_rows, axis=0).astype(o_ref.dtype)

    @pl.when(c == pl.num_programs(1) - 1)
    def _():
        s_out_ref[...] = s_scr[...]


def retention(p, cos, sin, consts, gn_g, gn_b, s0, *, row0, n_batch, t_len, nb, out_dtype):
    chunk = math.gcd(t_len, RET_CHUNK)
    n_chunk = t_len // chunk
    dmask, xi, zeta, gc = consts
    blk_rows = nb * chunk
    if nb > 1:
        assert n_chunk == 1
    assert row0 % blk_rows == 0
    base = row0 // blk_rows
    qk_w = RET_HEADS * RET_DK
    v_w = RET_HEADS * RET_DV
    has_s0 = s0 is not None

    def rmap(col):
        return lambda b, c: (base + b * n_chunk + c, col)

    in_specs = [pl.BlockSpec((blk_rows, qk_w), rmap(0)),
                pl.BlockSpec((blk_rows, qk_w), rmap(1)),
                pl.BlockSpec((blk_rows, v_w), rmap(1)),
                pl.BlockSpec((blk_rows, v_w), rmap(2)),
                pl.BlockSpec((chunk, RET_DK // 2), lambda b, c: (c, 0)),
                pl.BlockSpec((chunk, RET_DK // 2), lambda b, c: (c, 0)),
                pl.BlockSpec(dmask.shape, lambda b, c: (0, 0, 0)),
                pl.BlockSpec(xi.shape, lambda b, c: (0, 0, 0)),
                pl.BlockSpec(zeta.shape, lambda b, c: (0, 0, 0)),
                pl.BlockSpec(gc.shape, lambda b, c: (0, 0, 0)),
                pl.BlockSpec((1, v_w), lambda b, c: (0, 0)),
                pl.BlockSpec((1, v_w), lambda b, c: (0, 0))]
    args = [p, p, p, p, cos, sin, dmask, xi, zeta, gc, gn_g.reshape(1, -1), gn_b.reshape(1, -1)]
    s_spec = pl.BlockSpec((nb, RET_HEADS, RET_DK, RET_DV), lambda b, c: (b, 0, 0, 0))
    if has_s0:
        in_specs.append(s_spec)
        args.append(s0)
    o, s_out = pl.pallas_call(
        functools.partial(_retention_kernel, nb=nb, chunk=chunk, has_s0=has_s0),
        out_shape=(jax.ShapeDtypeStruct((n_batch * t_len, v_w), out_dtype),
                   jax.ShapeDtypeStruct((n_batch, RET_HEADS, RET_DK, RET_DV), F32)),
        grid=(n_batch // nb, n_chunk),
        in_specs=in_specs,
        out_specs=(pl.BlockSpec((blk_rows, v_w), lambda b, c: (b * n_chunk + c, 0)), s_spec),
        scratch_shapes=[pltpu.VMEM((nb, RET_HEADS, RET_DK, RET_DV), F32)],
        compiler_params=_cparams(("parallel", "arbitrary")),
        name="retention",
    )(*args)
    return o, s_out


def _retention_consts(chunk):
    h = jnp.arange(RET_HEADS, dtype=F32)
    lg = jnp.log1p(-jnp.exp2(-5.0 - h))
    idx = jnp.arange(chunk, dtype=F32)
    diff = idx[:, None] - idx[None, :]
    dmask = jnp.where(diff >= 0, jnp.exp(lg[:, None, None] * jnp.maximum(diff, 0.0)), 0.0)
    xi = jnp.exp(lg[:, None] * (idx[None, :] + 1.0))
    zeta = jnp.exp(lg[:, None] * (chunk - 1.0 - idx[None, :]))
    g_c = jnp.exp(lg * chunk)
    return (dmask,
            jnp.broadcast_to(xi[:, :, None], (RET_HEADS, chunk, RET_DV)),
            jnp.broadcast_to(zeta[:, :, None], (RET_HEADS, chunk, RET_DK)),
            jnp.broadcast_to(g_c[:, None, None], (RET_HEADS, 8, LANES)))


def _rotary_tables(pos):
    half = RET_DK // 2
    inv = 10000.0 ** (-jnp.linspace(0.0, 1.0, half, dtype=F32))
    ang = pos.astype(F32)[:, None] * inv[None, :]
    return jnp.cos(ang), jnp.sin(ang)


def _mem_attn_kernel(q_ref, mk_ref, mv_ref, o_ref, *, nb, tq):
    hd = MEM_HEAD_DIM
    scale = hd ** -0.5
    qf = q_ref[...].astype(F32)
    out_rows = []
    for bb in range(nb):
        out_heads = []
        for h in range(MEM_HEADS):
            cols = slice(h * hd, (h + 1) * hd)
            q = qf[bb * tq:(bb + 1) * tq, cols]
            s = _bdot_nt(q, mk_ref[bb, :, cols]) * scale
            m = jnp.max(s, axis=-1, keepdims=True)
            p = jnp.exp(s - m)
            p = p / jnp.sum(p, axis=-1, keepdims=True)
            out_heads.append(_bdot(p, mv_ref[bb, :, cols]))
        out_rows.append(jnp.concatenate(out_heads, axis=1))
    o_ref[...] = jnp.concatenate(out_rows, axis=0).astype(o_ref.dtype)


def mem_attention(q, mk, mv, *, layer, row0, n_batch, t_len, nb, tq, out_dtype):
    n_t = t_len // tq
    blk_rows = nb * tq
    if nb > 1:
        assert n_t == 1
    assert row0 % blk_rows == 0
    base = row0 // blk_rows
    m_tok = mk.shape[2]
    return pl.pallas_call(
        functools.partial(_mem_attn_kernel, nb=nb, tq=tq),
        out_shape=jax.ShapeDtypeStruct((n_batch * t_len, D_MODEL), out_dtype),
        grid=(n_batch // nb, n_t),
        in_specs=[pl.BlockSpec((blk_rows, D_MODEL), lambda b, t: (base + b * n_t + t, 0)),
                  pl.BlockSpec((None, nb, m_tok, D_MODEL), lambda b, t: (layer, b, 0, 0)),
                  pl.BlockSpec((None, nb, m_tok, D_MODEL), lambda b, t: (layer, b, 0, 0))],
        out_specs=pl.BlockSpec((blk_rows, D_MODEL), lambda b, t: (b * n_t + t, 0)),
        compiler_params=_cparams(("parallel", "arbitrary")),
        name="mem_attention",
    )(q, mk, mv)


def _router_kernel(x_ref, w_ref, b_ref, tri_ref, wt_ref, ei_ref, cnt_ref, base_scr):
    @pl.when(pl.program_id(0) == 0)
    def _():
        base_scr[...] = jnp.zeros(base_scr.shape, F32)

    x = x_ref[...]
    logits = _dot_hi(x, w_ref[...]) + b_ref[...]
    tm = x.shape[0]
    lane = lax.broadcasted_iota(I32, (tm, LANES), 1)
    big = jnp.int32(LANES)

    def masked_softmax(mask):
        lm = jnp.where(mask, logits, NEG_INF)
        mx = jnp.max(lm, axis=-1, keepdims=True)
        e = jnp.exp(lm - mx)
        return e / jnp.sum(e, axis=-1, keepdims=True)

    g_prob = masked_softmax(lane < MOE_GROUPS)
    gp = jnp.max(g_prob, axis=-1, keepdims=True)
    gi = jnp.min(jnp.where(g_prob == gp, lane, big), axis=-1, keepdims=True)
    lo = MOE_GROUPS + gi * MOE_EPG
    in_group = (lane >= lo) & (lane < lo + MOE_EPG)
    e_prob = jnp.where(in_group, masked_softmax(in_group), -1.0)
    p1 = jnp.max(e_prob, axis=-1, keepdims=True)
    i1 = jnp.min(jnp.where(e_prob == p1, lane, big), axis=-1, keepdims=True)
    rest = jnp.where(lane == i1, -1.0, e_prob)
    p2 = jnp.max(rest, axis=-1, keepdims=True)
    i2 = jnp.min(jnp.where(rest == p2, lane, big), axis=-1, keepdims=True)
    tot = p1 + p2
    w1 = gp * (p1 / tot)
    w2 = gp * (p2 / tot)
    e1 = i1 - MOE_GROUPS
    e2 = i2 - MOE_GROUPS
    oh1 = jnp.where(lane == e1, 1.0, 0.0)
    oh2 = jnp.where(lane == e2, 1.0, 0.0)
    both = oh1 + oh2
    base = base_scr[0:1, :]
    before = jnp.dot(tri_ref[...], both.astype(BF16), preferred_element_type=F32) + base
    r1 = jnp.sum(oh1 * before, axis=-1, keepdims=True).astype(I32)
    r2 = jnp.sum(oh2 * before, axis=-1, keepdims=True).astype(I32)
    base_new = base + jnp.sum(both, axis=0, keepdims=True)
    base_scr[0:1, :] = base_new
    wt_ref[...] = jnp.where(lane == 0, w1, jnp.where(lane == 1, w2, 0.0))
    ei_ref[...] = jnp.where(lane == 0, e1, jnp.where(lane == 1, e2, jnp.where(lane == 2, r1,
                                                                              jnp.where(lane == 3, r2, 0))))
    cnt_ref[...] = jnp.broadcast_to(base_new, cnt_ref.shape)


def moe_router(x, w_all, b_all, *, tm):
    m, d = x.shape
    tri = jnp.asarray(np.tril(np.ones((tm, tm), np.float32), -1), BF16)
    return pl.pallas_call(
        _router_kernel,
        out_shape=(jax.ShapeDtypeStruct((m, LANES), F32), jax.ShapeDtypeStruct((m, LANES), I32),
                   jax.ShapeDtypeStruct((8, LANES), F32)),
        grid=(m // tm,),
        in_specs=[pl.BlockSpec((tm, d), lambda i: (i, 0)),
                  pl.BlockSpec((d, LANES), lambda i: (0, 0)),
                  pl.BlockSpec((1, LANES), lambda i: (0, 0)),
                  pl.BlockSpec((tm, tm), lambda i: (0, 0))],
        out_specs=(pl.BlockSpec((tm, LANES), lambda i: (i, 0)), pl.BlockSpec((tm, LANES), lambda i: (i, 0)),
                   pl.BlockSpec((8, LANES), lambda i: (0, 0))),
        scratch_shapes=[pltpu.VMEM((8, LANES), F32)],
        compiler_params=_cparams(("arbitrary",)),
        name="moe_router",
    )(x, w_all, b_all, tri)


def _moe_dispatch_kernel(dest_ref, x_ref, xs_in, xs_out, sem, *, tm):
    del xs_in

    def body(r, carry):
        _tile_copy(x_ref, r, xs_out, dest_ref[r], sem.at[0]).start()
        _tile_copy(x_ref, r, xs_out, dest_ref[tm + r], sem.at[0]).start()
        return carry

    lax.fori_loop(0, tm, body, 0, unroll=8)
    for _ in range(MOE_TOPK):
        _wait_tiles(x_ref, xs_out, 0, sem.at[0], tm)


def moe_dispatch(x8, dest, n_slots, *, tm):
    n_t = dest.shape[0]
    xs0 = jnp.zeros((n_slots * ROW_TILE, LANES), U32)
    return pl.pallas_call(
        functools.partial(_moe_dispatch_kernel, tm=tm),
        out_shape=jax.ShapeDtypeStruct(xs0.shape, U32),
        grid=(n_t,),
        in_specs=[pl.BlockSpec((None, None, 2 * tm), lambda i: (i, 0, 0), memory_space=pltpu.SMEM),
                  pl.BlockSpec((tm * ROW_TILE, LANES), lambda i: (i, 0)),
                  pl.BlockSpec(memory_space=pl.ANY)],
        out_specs=pl.BlockSpec(memory_space=pl.ANY),
        scratch_shapes=[pltpu.SemaphoreType.DMA((1,))],
        input_output_aliases={2: 0},
        compiler_params=_cparams(("arbitrary",)),
        name="moe_dispatch",
    )(dest.reshape(n_t, 1, 2 * tm), x8, xs0)


def _moe_ffn_kernel(te_ref, nt_ref, first_ref, slot_ref, nxt_ref, xs_ref, w1_hbm, w3_hbm, w2_hbm, ys_ref,
                    wb1, wb3, wb2, sem, *, layer):
    j = pl.program_id(0)

    def weight_copies(e, s):
        return (pltpu.make_async_copy(w1_hbm.at[layer, e], wb1.at[s], sem.at[s]),
                pltpu.make_async_copy(w3_hbm.at[layer, e], wb3.at[s], sem.at[s]),
                pltpu.make_async_copy(w2_hbm.at[layer, e], wb2.at[s], sem.at[s]))

    @pl.when(j == 0)
    def _():
        for c in weight_copies(te_ref[0], slot_ref[0]):
            c.start()

    @pl.when(j < nt_ref[0])
    def _():
        s = slot_ref[j]

        @pl.when(first_ref[j] == 1)
        def _():
            for c in weight_copies(te_ref[j], s):
                c.wait()

            @pl.when(nxt_ref[j] >= 0)
            def _():
                for c in weight_copies(nxt_ref[j], 1 - s):
                    c.start()

        x = _load_token_tiles(xs_ref, 0, MOE_TILE).astype(BF16)
        h = _silu(_bdot(x, wb1[s])) * _bdot(x, wb3[s])
        _store_token_tiles(ys_ref, _bdot(h, wb2[s]))

    @pl.when(j >= nt_ref[0])
    def _():
        ys_ref[...] = jnp.zeros(ys_ref.shape, U32)


def moe_ffn(xs8, plan, w1, w3, w2, *, layer):
    tile_expert, n_tiles, first, slot, nxt = plan
    n_t = tile_expert.shape[0]
    d, dff = w1.shape[2], w1.shape[3]
    blk = MOE_TILE * ROW_TILE
    grid_spec = pltpu.PrefetchScalarGridSpec(
        num_scalar_prefetch=5,
        grid=(n_t,),
        in_specs=[pl.BlockSpec((blk, LANES), lambda j, te, nt, *_: (jnp.minimum(j, nt[0] - 1), 0)),
                  pl.BlockSpec(memory_space=pl.ANY),
                  pl.BlockSpec(memory_space=pl.ANY),
                  pl.BlockSpec(memory_space=pl.ANY)],
        out_specs=pl.BlockSpec((blk, LANES), lambda j, te, nt, *_: (j, 0)),
        scratch_shapes=[pltpu.VMEM((2, d, dff), F32), pltpu.VMEM((2, d, dff), F32), pltpu.VMEM((2, dff, d), F32),
                        pltpu.SemaphoreType.DMA((2,))],
    )
    return pl.pallas_call(
        functools.partial(_moe_ffn_kernel, layer=layer),
        out_shape=jax.ShapeDtypeStruct(xs8.shape, U32),
        grid_spec=grid_spec,
        compiler_params=_cparams(("arbitrary",)),
        name="moe_ffn",
    )(tile_expert, n_tiles, first, slot, nxt, xs8, w1, w3, w2)


def _moe_combine_kernel(d_cur, d_nxt, ys_hbm, x_ref, wt_ref, g_ref, b_ref, *rest, tm, lead_tiles):
    if lead_tiles is None:
        o_ref, ybuf, sem = rest
    else:
        o_ref, o_tail_ref, ybuf, sem = rest
    i = pl.program_id(0)
    n = pl.num_programs(0)
    slot = i % 2

    def issue(d_ref, s):
        def body(r, carry):
            _tile_copy(ys_hbm, d_ref[r], ybuf, s * 2 * tm + r, sem.at[s]).start()
            return carry
        lax.fori_loop(0, 2 * tm, body, 0, unroll=8)

    @pl.when(i == 0)
    def _():
        issue(d_cur, 0)

    @pl.when(i + 1 < n)
    def _():
        issue(d_nxt, 1 - slot)

    base_tok = slot * 2 * tm
    _wait_tiles(ys_hbm, ybuf, base_tok, sem.at[slot], 2 * tm)
    wt = wt_ref[...]
    y0 = _load_token_tiles(ybuf, base_tok * ROW_TILE, tm)
    y1 = _load_token_tiles(ybuf, (base_tok + tm) * ROW_TILE, tm)
    y = DEEPNORM_ALPHA * x_ref[...] + (wt[:, 0:1] * y0 + wt[:, 1:2] * y1)
    out = _layer_norm(y, g_ref[...], b_ref[...])
    if lead_tiles is None:
        o_ref[...] = out
    else:
        @pl.when(i < lead_tiles)
        def _():
            o_ref[...] = out

        @pl.when(i >= lead_tiles)
        def _():
            o_tail_ref[...] = out


def moe_combine_deepnorm(ys8, dest, x, wts, g, b, *, tm, lead_rows=None):
    m, d = x.shape
    n_t = m // tm
    dest3 = dest.reshape(n_t, 1, 2 * tm)
    if lead_rows is None:
        lead_tiles = None
        out_shape = jax.ShapeDtypeStruct((m, d), F32)
        out_specs = pl.BlockSpec((tm, d), lambda i: (i, 0))
    else:
        assert lead_rows % tm == 0 and m - lead_rows == tm
        lead_tiles = lead_rows // tm
        out_shape = (jax.ShapeDtypeStruct((lead_rows, d), F32), jax.ShapeDtypeStruct((tm, d), F32))
        out_specs = (pl.BlockSpec((tm, d), lambda i: (jnp.minimum(i, lead_tiles - 1), 0)),
                     pl.BlockSpec((tm, d), lambda i: (0, 0)))
    return pl.pallas_call(
        functools.partial(_moe_combine_kernel, tm=tm, lead_tiles=lead_tiles),
        out_shape=out_shape,
        grid=(n_t,),
        in_specs=[pl.BlockSpec((None, None, 2 * tm), lambda i: (i, 0, 0), memory_space=pltpu.SMEM),
                  pl.BlockSpec((None, None, 2 * tm), lambda i: (jnp.minimum(i + 1, n_t - 1), 0, 0),
                               memory_space=pltpu.SMEM),
                  pl.BlockSpec(memory_space=pl.ANY),
                  pl.BlockSpec((tm, d), lambda i: (i, 0)),
                  pl.BlockSpec((tm, LANES), lambda i: (i, 0)),
                  pl.BlockSpec((1, d), lambda i: (0, 0)),
                  pl.BlockSpec((1, d), lambda i: (0, 0))],
        out_specs=out_specs,
        scratch_shapes=[pltpu.VMEM((2 * 2 * tm * ROW_TILE, LANES), U32), pltpu.SemaphoreType.DMA((2,))],
        compiler_params=_cparams(("arbitrary",)),
        name="moe_combine_deepnorm",
    )(dest3, dest3, ys8, x, wts, g.reshape(1, d), b.reshape(1, d))


def _moe_plan(eidx, rank, counts, n_tok, tm):
    n_tiles_max = n_tok * MOE_TOPK // MOE_TILE + MOE_EXPERTS
    tiles_per = (counts + MOE_TILE - 1) // MOE_TILE
    tile_end = jnp.cumsum(tiles_per)
    pad_off = (tile_end - tiles_per) * MOE_TILE
    onehot = eidx[:, :, None] == jnp.arange(MOE_EXPERTS, dtype=I32)[None, None, :]
    dest = jnp.sum(jnp.where(onehot, pad_off[None, None, :], 0), axis=-1) + rank
    tile_expert = jnp.minimum(
        jnp.sum((jnp.arange(n_tiles_max, dtype=I32)[:, None] >= tile_end[None, :]).astype(I32), axis=1),
        MOE_EXPERTS - 1)
    dest = dest.reshape(n_tok // tm, tm, MOE_TOPK).transpose(0, 2, 1).reshape(n_tok // tm, MOE_TOPK * tm)
    experts = jnp.arange(MOE_EXPERTS, dtype=I32)
    nonempty = tiles_per > 0
    slot_e = (jnp.cumsum(nonempty.astype(I32)) - 1) % 2
    later = nonempty[None, :] & (experts[None, :] > experts[:, None])
    nxt_e = jnp.min(jnp.where(later, experts[None, :], MOE_EXPERTS), axis=1)
    nxt_e = jnp.where(nxt_e == MOE_EXPERTS, -1, nxt_e)
    tiles = jnp.arange(n_tiles_max, dtype=I32)
    tile_oh = tile_expert[:, None] == experts[None, :]
    first = jnp.any((tiles[:, None] == (tile_end - tiles_per)[None, :]) & nonempty[None, :], axis=1).astype(I32)
    slot = jnp.sum(jnp.where(tile_oh, slot_e[None, :], 0), axis=1).astype(I32)
    nxt = jnp.sum(jnp.where(tile_oh, nxt_e[None, :], 0), axis=1).astype(I32)
    ffn_plan = (tile_expert, tile_end[-1:].astype(I32), first, slot, nxt)
    return ffn_plan, dest, n_tiles_max * MOE_TILE


def hier_moe_deepnorm(x, x8, w_all, b_all, w1, w3, w2, g, b, *, layer, lead_rows=None):
    n_tok = x.shape[0]
    wts, ei, cnt = moe_router(x, w_all, b_all, tm=640)
    counts = cnt[0, :MOE_EXPERTS].astype(I32)
    ffn_plan, dest, n_slots = _moe_plan(ei[:, 0:2], ei[:, 2:4], counts, n_tok, MOE_TILE)
    xs8 = moe_dispatch(x8, dest, n_slots, tm=MOE_TILE)
    ys8 = moe_ffn(xs8, ffn_plan, w1, w3, w2, layer=layer)
    return moe_combine_deepnorm(ys8, dest, x, wts, g, b, tm=MOE_TILE, lead_rows=lead_rows)


def kernel(x_prompt, x_sample, cache_moba_k, cache_moba_v, state_conv, state_ret, cache_mem_k, cache_mem_v,
           page_table, mem_prompt, rel_bias, ev_w_in, ev_conv_w, ev_conv_b, ev_conv_ln_g, ev_conv_ln_b,
           ev_w_out, od_w_in, od_gn_g, od_gn_b, od_w_out, mem_wq, mem_wk, mem_wv, mem_wo, ln_g, ln_b,
           moe_w_group, moe_b_group, moe_w_router, moe_b_router, moe_w1, moe_w3, moe_w2):
    bp, tp, d = x_prompt.shape
    bs, ts, _ = x_sample.shape
    n_p = bp * tp
    n_s = bs * ts
    n_all = n_p + n_s
    page = cache_moba_k.shape[1]
    past_len = page_table.shape[1] * page
    nh, hd = MOBA_HEADS, MOBA_HEAD_DIM
    assert tp % MOBA_BLOCK == 0 and past_len % MOBA_BLOCK == 0 and ts <= MOBA_BLOCK
    assert MOBA_BLOCK >= MAX_DISTANCE
    tm = 640
    assert n_all % tm == 0

    x = jnp.concatenate([x_prompt.reshape(n_p, d), x_sample.reshape(n_s, d)], axis=0)

    tab_heads = rel_bias.T
    cache_kt = cache_moba_k.transpose(0, 2, 3, 4, 1)
    cache_vt = cache_moba_v.transpose(0, 2, 3, 4, 1)
    mem_k_cache = cache_mem_k.reshape(DEPTH, bs, cache_mem_k.shape[2], d)
    mem_v_cache = cache_mem_v.reshape(DEPTH, bs, cache_mem_v.shape[2], d)

    pos_p = jnp.arange(tp, dtype=I32)
    pos_s = past_len + jnp.arange(ts, dtype=I32)
    cos_p, sin_p = _rotary_tables(pos_p)
    cos_s, sin_s = _rotary_tables(pos_s)
    ret_c_p = _retention_consts(math.gcd(tp, RET_CHUNK))
    ret_c_s = _retention_consts(math.gcd(ts, RET_CHUNK))

    outs = {}
    for layer in range(DEPTH):
        if layer % 2 == 0:
            e = layer // 2
            proj = matmul(x, ev_w_in[e].astype(BF16), tm=tm, tn=ev_w_in.shape[2])
            k_new = proj[:, 2 * CONV_CH + MOBA_WIDTH:2 * CONV_CH + 2 * MOBA_WIDTH]
            v_new = proj[:, 2 * CONV_CH + 2 * MOBA_WIDTH:]
            q_s = proj[n_p:, 2 * CONV_CH:2 * CONV_CH + MOBA_WIDTH]
            conv_args = (ev_conv_w[e], ev_conv_b[e], ev_conv_ln_g[e], ev_conv_ln_b[e])
            c_p, cst_p = conv_module(proj, jnp.zeros((bp, CONV_WIDTH - 1, CONV_CH), F32), *conv_args,
                                     row0=0, n_batch=bp, t_len=tp, nb=1, out_dtype=BF16)
            c_s, cst_s = conv_module(proj, state_conv[e], *conv_args,
                                     row0=n_p, n_batch=bs, t_len=ts, nb=bs, out_dtype=F32)
            a_p = moba_prompt(proj, tab_heads, n_batch=bp, t_len=tp, out_dtype=BF16)

            def by_head(a):
                return a.reshape(bs, ts, nh, hd).transpose(0, 2, 1, 3)

            a_s = moba_sample(page_table, cache_kt, cache_vt, by_head(q_s), by_head(k_new[n_p:]),
                              by_head(v_new[n_p:]), tab_heads, layer_e=e, t_len=ts)
            a_s = a_s.reshape(bs, nh, ts, hd).transpose(0, 2, 1, 3).reshape(n_s, MOBA_WIDTH).astype(BF16)
            w_out = ev_w_out[e].astype(BF16)
            x = matmul_deepnorm([(c_p, c_s.astype(BF16)), (a_p, a_s)], [w_out[:CONV_CH], w_out[CONV_CH:]], x,
                                ln_g[layer, 0], ln_b[layer, 0], tm=tm)
            outs.setdefault("kp", []).append(k_new[:n_p].reshape(bp, tp, nh, hd))
            outs.setdefault("vp", []).append(v_new[:n_p].reshape(bp, tp, nh, hd))
            outs.setdefault("ks", []).append(k_new[n_p:].reshape(bs, ts, nh, hd))
            outs.setdefault("vs", []).append(v_new[n_p:].reshape(bs, ts, nh, hd))
            outs.setdefault("cp", []).append(cst_p)
            outs.setdefault("cs", []).append(cst_s)
        else:
            o = layer // 2
            proj = matmul(x, od_w_in[o].astype(BF16), tm=2 * tm, tn=2048)
            r_p, st_p = retention(proj, cos_p, sin_p, ret_c_p, od_gn_g[o], od_gn_b[o], None,
                                  row0=0, n_batch=bp, t_len=tp, nb=1, out_dtype=BF16)
            r_s, st_s = retention(proj, cos_s, sin_s, ret_c_s, od_gn_g[o], od_gn_b[o], state_ret[o],
                                  row0=n_p, n_batch=bs, t_len=ts, nb=2, out_dtype=BF16)
            x = matmul_deepnorm([(r_p, r_s)], [od_w_out[o].astype(BF16)], x, ln_g[layer, 0], ln_b[layer, 0], tm=tm)
            outs.setdefault("sp", []).append(st_p)
            outs.setdefault("ss", []).append(st_s)

        m_tok = mem_prompt.shape[1]
        mem2 = mem_prompt.reshape(bp * m_tok, d)
        mk_p = matmul(mem2, mem_wk[layer].astype(BF16), tm=512, tn=d)
        mv_p = matmul(mem2, mem_wv[layer].astype(BF16), tm=512, tn=d)
        outs.setdefault("mk", []).append(mk_p.reshape(bp, m_tok, MEM_HEADS, MEM_HEAD_DIM))
        outs.setdefault("mv", []).append(mv_p.reshape(bp, m_tok, MEM_HEADS, MEM_HEAD_DIM))
        q = matmul(x, mem_wq[layer].astype(BF16), tm=tm, tn=d, out_dtype=BF16)
        o_p = mem_attention(q, mk_p.reshape(1, bp, m_tok, d), mv_p.reshape(1, bp, m_tok, d),
                            layer=0, row0=0, n_batch=bp, t_len=tp, nb=1, tq=512, out_dtype=BF16)
        o_s = mem_attention(q, mem_k_cache, mem_v_cache,
                            layer=layer, row0=n_p, n_batch=bs, t_len=ts, nb=2, tq=ts, out_dtype=BF16)
        x, x8 = matmul_deepnorm([(o_p, o_s)], [mem_wo[layer].astype(BF16)], x, ln_g[layer, 1], ln_b[layer, 1], tm=tm,
                                token_tiles=True)

        w_all = jnp.zeros((d, LANES), F32)
        w_all = w_all.at[:, :MOE_GROUPS].set(moe_w_group[layer])
        w_all = w_all.at[:, MOE_GROUPS:MOE_GROUPS + MOE_EXPERTS].set(moe_w_router[layer])
        b_all = jnp.zeros((1, LANES), F32)
        b_all = b_all.at[0, :MOE_GROUPS].set(moe_b_group[layer])
        b_all = b_all.at[0, MOE_GROUPS:MOE_GROUPS + MOE_EXPERTS].set(moe_b_router[layer])
        x = hier_moe_deepnorm(x, x8, w_all, b_all, moe_w1, moe_w3, moe_w2, ln_g[layer, 2], ln_b[layer, 2],
                              layer=layer, lead_rows=n_p if layer == DEPTH - 1 else None)

    y_prompt = x[0].reshape(bp, tp, d)
    y_sample = x[1].reshape(bs, ts, d)
    return (y_prompt, y_sample,
            jnp.stack(outs["kp"], axis=2), jnp.stack(outs["vp"], axis=2),
            jnp.stack(outs["ks"], axis=2), jnp.stack(outs["vs"], axis=2),
            jnp.stack(outs["cp"], axis=0), jnp.stack(outs["cs"], axis=0),
            jnp.stack(outs["sp"], axis=0), jnp.stack(outs["ss"], axis=0),
            jnp.stack(outs["mk"], axis=0), jnp.stack(outs["mv"], axis=0))
```

```python
import functools
import math

import numpy as np
import jax
import jax.numpy as jnp
from jax import lax
from jax.experimental import pallas as pl
from jax.experimental.pallas import tpu as pltpu

F32 = jnp.float32
BF16 = jnp.bfloat16
I32 = jnp.int32

D_MODEL = 1024
DEPTH = 2
CONV_CH = 512
CONV_WIDTH = 31
MOBA_HEADS = 8
MOBA_HEAD_DIM = 64
MOBA_WIDTH = 512
MOBA_BLOCK = 256
MOBA_TOPK = 3
NUM_BUCKETS = 32
MAX_DISTANCE = 128
RET_HEADS = 4
RET_DK = 256
RET_DV = 512
RET_CHUNK = 128
MEM_HEADS = 4
MEM_HEAD_DIM = 256
MOE_GROUPS = 4
MOE_EPG = 8
MOE_EXPERTS = 32
MOE_TOPK = 2
MOE_D_FF = 512
DEEPNORM_ALPHA = (2 * DEPTH) ** 0.25
LN_EPS = 1e-5

LANES = 128
VMEM_LIMIT = 56 * 1024 * 1024
MOE_TILE = 256
DMA_UNROLL = 8
NEG_INF = float("-inf")


def _cparams(sem):
    return pltpu.CompilerParams(dimension_semantics=sem, vmem_limit_bytes=VMEM_LIMIT)


def _layer_norm(y, g, b):
    mu = jnp.mean(y, axis=-1, keepdims=True)
    yc = y - mu
    var = jnp.mean(yc * yc, axis=-1, keepdims=True)
    return yc * lax.rsqrt(var + LN_EPS) * g + b


def _silu(x):
    return x * (1.0 / (1.0 + jnp.exp(-x)))


def _bdot(a, b):
    return jnp.dot(a.astype(BF16), b.astype(BF16), preferred_element_type=F32)


def _bdot_nt(a, b):
    return lax.dot_general(a.astype(BF16), b.astype(BF16), (((1,), (1,)), ((), ())),
                           preferred_element_type=F32)


def _bdot_tn(a, b):
    return lax.dot_general(a.astype(BF16), b.astype(BF16), (((0,), (0,)), ((), ())),
                           preferred_element_type=F32)


def _split3(x):
    hi = x.astype(BF16)
    lo = (x - hi.astype(F32)).astype(BF16)
    return hi, lo


def _dot_hi(a, b, dims=(((1,), (0,)), ((), ()))):
    ah, al = _split3(a)
    bh, bl = _split3(b)
    dg = functools.partial(lax.dot_general, dimension_numbers=dims, preferred_element_type=F32)
    return dg(ah, bh) + (dg(al, bh) + dg(ah, bl))


U32 = jnp.uint32
ROW_TILE = D_MODEL // (2 * LANES)
HALF = D_MODEL // 2


def _bf16_bits(v):
    b = pltpu.bitcast(v, U32)
    return b + (jnp.uint32(0x7FFF) + ((b >> 16) & jnp.uint32(1)))


def _load_token_tiles(ref, start, n_tok):
    u = jnp.concatenate([ref[pl.ds(start + c, n_tok, stride=ROW_TILE), :] for c in range(ROW_TILE)], axis=1)
    lo = pltpu.bitcast(u << 16, F32)
    hi = pltpu.bitcast(u & jnp.uint32(0xFFFF0000), F32)
    return jnp.concatenate([lo, hi], axis=1)


def _store_token_tiles(ref, val):
    n_tok = val.shape[0]
    packed = (_bf16_bits(val[:, :HALF]) >> 16) | (_bf16_bits(val[:, HALF:]) & jnp.uint32(0xFFFF0000))
    for c in range(ROW_TILE):
        ref[pl.ds(c, n_tok, stride=ROW_TILE), :] = packed[:, c * LANES:(c + 1) * LANES]


def _tile_copy(src_hbm, src_tok, dst, dst_tok, sem):
    return pltpu.make_async_copy(src_hbm.at[pl.ds(src_tok * ROW_TILE, ROW_TILE), :],
                                 dst.at[pl.ds(dst_tok * ROW_TILE, ROW_TILE), :], sem)


def _wait_tiles(src_hbm, dst, dst_tok, sem, n_tok):
    pltpu.make_async_copy(src_hbm.at[pl.ds(0, n_tok * ROW_TILE), :],
                          dst.at[pl.ds(dst_tok * ROW_TILE, n_tok * ROW_TILE), :], sem).wait()


def _mm_kernel(x_ref, w_ref, o_ref):
    o_ref[...] = _bdot(x_ref[...], w_ref[...]).astype(o_ref.dtype)


def matmul(x, w, *, tm, tn, out_dtype=F32):
    m, k = x.shape
    n = w.shape[1]
    assert m % tm == 0 and n % tn == 0
    return pl.pallas_call(
        _mm_kernel,
        out_shape=jax.ShapeDtypeStruct((m, n), out_dtype),
        grid=(m // tm, n // tn),
        in_specs=[pl.BlockSpec((tm, k), lambda i, j: (i, 0)),
                  pl.BlockSpec((k, tn), lambda i, j: (0, j))],
        out_specs=pl.BlockSpec((tm, tn), lambda i, j: (i, j)),
        compiler_params=_cparams(("parallel", "parallel")),
        name="matmul",
    )(x, w)


def _mm_dn_kernel(*refs, n_lhs, token_tiles, tail_rows):
    lhs = refs[:2 * n_lhs:2]
    tails = refs[1:2 * n_lhs:2]
    ws = refs[2 * n_lhs:3 * n_lhs]
    res_ref, g_ref, b_ref, o_ref = refs[3 * n_lhs:3 * n_lhs + 4]
    last = pl.program_id(0) == pl.num_programs(0) - 1

    def rows(a_ref, t_ref):
        a = a_ref[...]
        if tail_rows:
            a = jnp.where(last, jnp.concatenate([a[:a.shape[0] - tail_rows], t_ref[...]], axis=0), a)
        return a

    acc = _bdot(rows(lhs[0], tails[0]), ws[0][...])
    for a, t, w in zip(lhs[1:], tails[1:], ws[1:]):
        acc = acc + _bdot(rows(a, t), w[...])
    y = DEEPNORM_ALPHA * res_ref[...] + acc
    out = _layer_norm(y, g_ref[...], b_ref[...])
    o_ref[...] = out
    if token_tiles:
        _store_token_tiles(refs[3 * n_lhs + 4], out)


def matmul_deepnorm(lhs_list, w_list, res, g, b, *, tm, token_tiles=False):
    m, d = res.shape
    n_lhs = len(lhs_list)
    tail_rows = lhs_list[0][1].shape[0]
    for a, t in lhs_list:
        assert a.shape[0] + t.shape[0] == m and t.shape[0] == tail_rows and a.shape[1] == t.shape[1]
        assert 0 < tail_rows < tm and (m - tail_rows) // tm == m // tm - 1
    flat_lhs = [x for pair in lhs_list for x in pair]
    in_specs = ([spec for a, t in lhs_list
                 for spec in (pl.BlockSpec((tm, a.shape[1]), lambda i: (i, 0)),
                              pl.BlockSpec(t.shape, lambda i: (0, 0)))]
                + [pl.BlockSpec(w.shape, lambda i: (0, 0)) for w in w_list]
                + [pl.BlockSpec((tm, d), lambda i: (i, 0)),
                   pl.BlockSpec((1, d), lambda i: (0, 0)),
                   pl.BlockSpec((1, d), lambda i: (0, 0))])
    out_shape = [jax.ShapeDtypeStruct((m, d), F32)]
    out_specs = [pl.BlockSpec((tm, d), lambda i: (i, 0))]
    if token_tiles:
        out_shape.append(jax.ShapeDtypeStruct((m * ROW_TILE, LANES), U32))
        out_specs.append(pl.BlockSpec((tm * ROW_TILE, LANES), lambda i: (i, 0)))
    res_out = pl.pallas_call(
        functools.partial(_mm_dn_kernel, n_lhs=n_lhs, token_tiles=token_tiles, tail_rows=tail_rows),
        out_shape=tuple(out_shape),
        grid=(m // tm,),
        in_specs=in_specs,
        out_specs=tuple(out_specs),
        compiler_params=_cparams(("parallel",)),
        name="matmul_deepnorm",
    )(*flat_lhs, *w_list, res, g.reshape(1, d), b.reshape(1, d))
    return res_out if token_tiles else res_out[0]


CONV_PAD = 32


def _conv_kernel(p_ref, hist_ref, w_ref, cb_ref, g_ref, b_ref, c_ref, st_ref, u_scr, sh_scr, *, nb, t_len, rc):
    hw = CONV_WIDTH - 1
    w = w_ref[...]
    for bb in range(nb):
        rows = pl.ds(bb * t_len, t_len)
        a = p_ref[rows, 0:CONV_CH]
        gt = p_ref[rows, CONV_CH:2 * CONV_CH]
        u = a * (1.0 / (1.0 + jnp.exp(-gt)))
        u_scr[0:CONV_PAD, :] = jnp.concatenate(
            [jnp.zeros((CONV_PAD - hw, CONV_CH), F32), hist_ref[bb]], axis=0)
        u_scr[CONV_PAD:CONV_PAD + t_len, :] = u
        st_ref[bb] = u_scr[CONV_PAD + t_len - hw:CONV_PAD + t_len, :]

        def chunk(ci, carry):
            r0 = pl.multiple_of(ci * rc, rc)
            span = rc + CONV_PAD - 8
            for r in range(1, 8):
                sh_scr[r - 1] = u_scr[pl.ds(r0, rc + CONV_PAD), :][r:r + span, :]
            acc = jnp.broadcast_to(cb_ref[...], (rc, CONV_CH))
            for j in range(CONV_WIDTH):
                off = j + CONV_PAD - hw
                r = off % 8
                if r == 0:
                    tap = u_scr[pl.ds(r0 + off, rc), :]
                else:
                    tap = sh_scr[r - 1, off - r:off - r + rc, :]
                acc = acc + w[j:j + 1, :] * tap
            y = _silu(_layer_norm(acc, g_ref[...], b_ref[...]))
            c_ref[pl.ds(bb * t_len + r0, rc), :] = y.astype(c_ref.dtype)
            return carry

        n_chunks = t_len // rc
        if n_chunks == 1:
            chunk(0, 0)
        else:
            lax.fori_loop(0, n_chunks, chunk, 0)


def conv_module(p, hist, conv_w, conv_b, ln_g, ln_b, *, row0, n_batch, t_len, nb, out_dtype):
    rc = min(256, t_len)
    blk_rows = nb * t_len
    assert row0 % blk_rows == 0 and n_batch % nb == 0
    base = row0 // blk_rows
    hw = CONV_WIDTH - 1
    c, st = pl.pallas_call(
        functools.partial(_conv_kernel, nb=nb, t_len=t_len, rc=rc),
        out_shape=(jax.ShapeDtypeStruct((n_batch * t_len, CONV_CH), out_dtype),
                   jax.ShapeDtypeStruct((n_batch, hw, CONV_CH), F32)),
        grid=(n_batch // nb,),
        in_specs=[pl.BlockSpec((blk_rows, 2 * CONV_CH), lambda i: (base + i, 0)),
                  pl.BlockSpec((nb, hw, CONV_CH), lambda i: (i, 0, 0)),
                  pl.BlockSpec((CONV_WIDTH, CONV_CH), lambda i: (0, 0)),
                  pl.BlockSpec((1, CONV_CH), lambda i: (0, 0)),
                  pl.BlockSpec((1, CONV_CH), lambda i: (0, 0)),
                  pl.BlockSpec((1, CONV_CH), lambda i: (0, 0))],
        out_specs=(pl.BlockSpec((blk_rows, CONV_CH), lambda i: (i, 0)),
                   pl.BlockSpec((nb, hw, CONV_CH), lambda i: (i, 0, 0))),
        scratch_shapes=[pltpu.VMEM((t_len + CONV_PAD, CONV_CH), F32),
                        pltpu.VMEM((7, rc + CONV_PAD - 8, CONV_CH), F32)],
        compiler_params=_cparams(("parallel",)),
        name="conv_module",
    )(p, hist, conv_w, conv_b.reshape(1, -1), ln_g.reshape(1, -1), ln_b.reshape(1, -1))
    return c, st


def _bucket_thresholds():
    max_exact = NUM_BUCKETS // 2
    d = np.arange(0, MAX_DISTANCE + 1)
    val = (np.log(np.maximum(d, 1).astype(np.float32) / np.float32(max_exact))
           / np.float32(math.log(MAX_DISTANCE / max_exact)) * np.float32(NUM_BUCKETS - max_exact))
    inner = (d > max_exact) & (d < MAX_DISTANCE)
    assert np.all(np.abs(val[inner] - np.round(val[inner])) > 1e-3)
    bucket = np.where(d < max_exact, d, np.minimum(max_exact + val.astype(np.int32), NUM_BUCKETS - 1))
    return [int(np.argmax(bucket >= k)) for k in range(1, NUM_BUCKETS)]


def _bias_chain(dist, tab_at, thr):
    b = jnp.where(dist >= thr[0], tab_at(1), tab_at(0))
    for k in range(2, NUM_BUCKETS):
        b = jnp.where(dist >= thr[k - 1], tab_at(k), b)
    return b


def _moba_prompt_kernel(tab_ref, q_ref, k_ref, v_ref, o_ref,
                        kmean_scr, kbf_scr, vt_scr, d0_scr, d1_scr, far_scr, sel_scr, qb_scr, m_scr, l_scr, acc_scr,
                        *, n_blk, thr):
    b = pl.program_id(0)
    i = pl.program_id(1)
    blk = MOBA_BLOCK
    hd = MOBA_HEAD_DIM
    n_pair = MOBA_HEADS // 2
    scale = hd ** -0.5
    nt_dims = (((1,), (1,)), ((), ()))
    key = lax.broadcasted_iota(I32, (blk, blk), 0)
    qry = lax.broadcasted_iota(I32, (blk, blk), 1)
    causal = jnp.concatenate([key <= qry, key <= qry], axis=1)

    @pl.when((b == 0) & (i == 0))
    def _():
        for h in range(MOBA_HEADS):
            tab_at = functools.partial(lambda k, hh: tab_ref[hh, k], hh=h)
            half = slice((h % 2) * blk, (h % 2 + 1) * blk)
            d0_scr[h // 2, :, half] = _bias_chain(qry - key, tab_at, thr)
            d1_scr[h // 2, :, half] = _bias_chain(blk + qry - key, tab_at, thr)
            far_scr[h // 2, :, half] = jnp.full((1, blk), tab_ref[h, NUM_BUCKETS - 1], F32)

    @pl.when(i == 0)
    def _():
        for n in range(n_blk):
            kb = k_ref[n * blk:(n + 1) * blk, :]
            kmean_scr[n:n + 1, :] = jnp.sum(kb, axis=0, keepdims=True) * (1.0 / blk)
            kbf_scr[n * blk:(n + 1) * blk, :] = kb.astype(BF16)
            for pair in range(n_pair):
                vt_scr[n, pair * LANES:(pair + 1) * LANES, :] = (
                    v_ref[n * blk:(n + 1) * blk, pair * LANES:(pair + 1) * LANES].T.astype(BF16))

    q = q_ref[...]
    lane = lax.broadcasted_iota(I32, (blk, LANES), 1)
    rown = lax.broadcasted_iota(I32, (n_blk, blk), 0)
    rowd = lax.broadcasted_iota(I32, (LANES, blk), 0)
    r0 = pl.multiple_of(i * blk, blk)
    pairs = [(pr, slice(pr * LANES, (pr + 1) * LANES)) for pr in range(n_pair)]

    for pr, cols in pairs:
        sels = []
        qbs = []
        for sub in range(2):
            in_head = (lane >= sub * hd) & (lane < (sub + 1) * hd)
            qm = jnp.where(in_head, q[:, cols], 0.0)
            gate = _dot_hi(kmean_scr[:, cols], qm, nt_dims)
            rank = jnp.zeros((n_blk, blk), F32)
            for m in range(n_blk):
                gm = gate[m:m + 1, :]
                tie = jnp.where(rown > m, 1.0, 0.0)
                cnt = jnp.where(gm > gate, 1.0, jnp.where(gm == gate, tie, 0.0))
                rank = rank + jnp.where(m < i, cnt, 0.0)
            sels.append(jnp.where((rown < i) & (rank < MOBA_TOPK), 1.0, 0.0))
            qbs.append((qm * scale).astype(BF16))
        sel_scr[pr] = jnp.concatenate(sels, axis=1)
        qb = jnp.concatenate(qbs, axis=0)
        qb_scr[pr] = qb
        s = lax.dot_general(kbf_scr[pl.ds(r0, blk), cols], qb, nt_dims,
                            preferred_element_type=F32) + d0_scr[pr]
        s = jnp.where(causal, s, NEG_INF)
        m_run = jnp.max(s, axis=0, keepdims=True)
        p = jnp.exp(s - m_run)
        m_scr[pr] = m_run
        l_scr[pr] = jnp.sum(p, axis=0, keepdims=True)
        acc_scr[pr] = jnp.dot(vt_scr[i, cols, :], p.astype(BF16), preferred_element_type=F32)

    def merge_block(n, pr, cols, s, shift):
        picked = sel_scr[pr, pl.ds(n, 1), :] > 0.5
        m_run = m_scr[pr]
        m_new = jnp.maximum(m_run, jnp.where(picked, jnp.max(s, axis=0, keepdims=True) + shift, NEG_INF))
        alpha = jnp.exp(m_run - m_new)
        p = jnp.exp(s - jnp.where(picked, m_new - shift, jnp.inf))
        m_scr[pr] = m_new
        l_scr[pr] = alpha * l_scr[pr] + jnp.sum(p, axis=0, keepdims=True)
        acc_scr[pr] = alpha * acc_scr[pr] + jnp.dot(vt_scr[n, cols, :], p.astype(BF16), preferred_element_type=F32)

    @pl.when(i > 0)
    def _():
        rr = pl.multiple_of((i - 1) * blk, blk)
        for pr, cols in pairs:
            s = lax.dot_general(kbf_scr[pl.ds(rr, blk), cols], qb_scr[pr], nt_dims,
                                preferred_element_type=F32) + d1_scr[pr]
            merge_block(i - 1, pr, cols, s, 0.0)

    def far_scores(n, pr, cols):
        rr = pl.multiple_of(n * blk, blk)
        return lax.dot_general(kbf_scr[pl.ds(rr, blk), cols], qb_scr[pr], nt_dims, preferred_element_type=F32)

    def merge_two(n0, n1, pr, cols, s0, s1, shift):
        pk0 = sel_scr[pr, pl.ds(n0, 1), :] > 0.5
        pk1 = sel_scr[pr, pl.ds(n1, 1), :] > 0.5
        m_run = m_scr[pr]
        bm0 = jnp.where(pk0, jnp.max(s0, axis=0, keepdims=True) + shift, NEG_INF)
        bm1 = jnp.where(pk1, jnp.max(s1, axis=0, keepdims=True) + shift, NEG_INF)
        m_new = jnp.maximum(m_run, jnp.maximum(bm0, bm1))
        alpha = jnp.exp(m_run - m_new)
        p0 = jnp.exp(s0 - jnp.where(pk0, m_new - shift, jnp.inf))
        p1 = jnp.exp(s1 - jnp.where(pk1, m_new - shift, jnp.inf))
        m_scr[pr] = m_new
        l_scr[pr] = (alpha * l_scr[pr] + jnp.sum(p0, axis=0, keepdims=True)) + jnp.sum(p1, axis=0, keepdims=True)
        acc_scr[pr] = (alpha * acc_scr[pr]
                       + jnp.dot(vt_scr[n0, cols, :], p0.astype(BF16), preferred_element_type=F32)
                       + jnp.dot(vt_scr[n1, cols, :], p1.astype(BF16), preferred_element_type=F32))

    n_far = jnp.maximum(i - 1, 0)

    def body(n2, carry):
        for pr, cols in pairs:
            merge_two(2 * n2, 2 * n2 + 1, pr, cols, far_scores(2 * n2, pr, cols),
                      far_scores(2 * n2 + 1, pr, cols), far_scr[pr])
        return carry

    lax.fori_loop(0, n_far // 2, body, 0)

    @pl.when(n_far % 2 == 1)
    def _():
        for pr, cols in pairs:
            merge_block(n_far - 1, pr, cols, far_scores(n_far - 1, pr, cols), far_scr[pr])

    outs = []
    for pr, _ in pairs:
        o = acc_scr[pr] / l_scr[pr]
        outs.append(jnp.where(rowd < hd, o[:, :blk], o[:, blk:]).T)
    o_ref[...] = jnp.concatenate(outs, axis=1).astype(o_ref.dtype)


def moba_prompt(p, tab, *, n_batch, t_len, out_dtype):
    n_blk = t_len // MOBA_BLOCK
    n_pair = MOBA_HEADS // 2
    wide = 2 * MOBA_BLOCK
    return pl.pallas_call(
        functools.partial(_moba_prompt_kernel, n_blk=n_blk, thr=_bucket_thresholds()),
        out_shape=jax.ShapeDtypeStruct((n_batch * t_len, MOBA_WIDTH), out_dtype),
        grid=(n_batch, n_blk),
        in_specs=[pl.BlockSpec(memory_space=pltpu.SMEM),
                  pl.BlockSpec((MOBA_BLOCK, MOBA_WIDTH), lambda b, i: (b * n_blk + i, 2)),
                  pl.BlockSpec((t_len, MOBA_WIDTH), lambda b, i: (b, 3)),
                  pl.BlockSpec((t_len, MOBA_WIDTH), lambda b, i: (b, 4))],
        out_specs=pl.BlockSpec((MOBA_BLOCK, MOBA_WIDTH), lambda b, i: (b * n_blk + i, 0)),
        scratch_shapes=[pltpu.VMEM((n_blk, MOBA_WIDTH), F32),
                        pltpu.VMEM((t_len, MOBA_WIDTH), BF16),
                        pltpu.VMEM((n_blk, MOBA_WIDTH, MOBA_BLOCK), BF16),
                        pltpu.VMEM((n_pair, MOBA_BLOCK, wide), F32),
                        pltpu.VMEM((n_pair, MOBA_BLOCK, wide), F32),
                        pltpu.VMEM((n_pair, 1, wide), F32),
                        pltpu.VMEM((n_pair, n_blk, wide), F32),
                        pltpu.VMEM((n_pair, wide, LANES), BF16),
                        pltpu.VMEM((n_pair, 1, wide), F32),
                        pltpu.VMEM((n_pair, 1, wide), F32),
                        pltpu.VMEM((n_pair, LANES, wide), F32)],
        compiler_params=_cparams(("arbitrary", "arbitrary")),
        name="moba_prompt",
    )(tab, p, p, p)


def _moba_sample_kernel(pt_ref, *refs, n_blk, nbs, page, t_len, thr):
    npg = 2 * nbs
    k_refs = refs[:npg]
    v_refs = refs[npg:2 * npg]
    q_ref, qbd_ref, kn_ref, vn_ref, tab_ref, o_ref, g_scr, m_scr, l_scr, o_scr, bl_scr = refs[2 * npg:]
    n = pl.program_id(1)
    nh, hd = MOBA_HEADS, MOBA_HEAD_DIM
    nrow = nh * t_len
    scale = hd ** -0.5
    nt_dims = (((1,), (1,)), ((), ()))
    lane = lax.broadcasted_iota(I32, (nrow, LANES), 1)
    row = lax.broadcasted_iota(I32, (nrow, LANES), 0)
    tab = tab_ref[...]

    def tab_at(k):
        return tab[:, k:k + 1]

    @pl.when(n == 0)
    def _():
        g_scr[...] = jnp.zeros(g_scr.shape, F32)
        m_scr[...] = jnp.zeros(m_scr.shape, F32)
        l_scr[...] = jnp.zeros(l_scr.shape, F32)
        for half in range(2):
            bl_scr[:, half * page:(half + 1) * page] = _bias_chain(
                MOBA_BLOCK + row % t_len - (half * page + lane), tab_at, thr)

    qbd = (qbd_ref[0] * scale).astype(BF16)
    far = tab_at(NUM_BUCKETS - 1)
    bw = 2 * page
    kt = jnp.concatenate([r[...].reshape(nh * hd, page).astype(BF16) for r in k_refs], axis=1)
    vt = jnp.concatenate([r[...].reshape(nh * hd, page).astype(BF16) for r in v_refs], axis=1)
    s_all = jnp.dot(qbd, kt, preferred_element_type=F32)
    p_rows = []
    stats = []
    for j in range(nbs):
        blk_idx = n * nbs + j
        s_raw = s_all[:, j * bw:(j + 1) * bw]
        gate = jnp.sum(s_raw, axis=1, keepdims=True) * (1.0 / (MOBA_BLOCK * scale))
        s = s_raw + jnp.where(blk_idx == n_blk - 1, bl_scr[...], far)
        m = jnp.max(s, axis=1, keepdims=True)
        p = jnp.exp(s - m)
        l = jnp.sum(p, axis=1, keepdims=True)
        stats.append((blk_idx, gate, m, l))
        zl = jnp.zeros((nrow, j * bw), F32)
        zr = jnp.zeros((nrow, (nbs - 1 - j) * bw), F32)
        p_rows.append(jnp.concatenate([x for x in (zl, p, zr) if x.shape[1]], axis=1))
    p_bd = jnp.concatenate(p_rows, axis=0).astype(BF16)
    pv = lax.dot_general(p_bd, vt, nt_dims, preferred_element_type=F32)
    for j, (blk_idx, gate, m, l) in enumerate(stats):
        at_blk = lane == blk_idx
        g_scr[...] = jnp.where(at_blk, gate, g_scr[...])
        m_scr[...] = jnp.where(at_blk, m, m_scr[...])
        l_scr[...] = jnp.where(at_blk, l, l_scr[...])
        o_scr[blk_idx] = jnp.concatenate(
            [pv[j * nrow + h * t_len:j * nrow + (h + 1) * t_len, h * hd:(h + 1) * hd] for h in range(nh)], axis=0)

    @pl.when(n == n_blk // nbs - 1)
    def _():
        gate = g_scr[...]
        rank = jnp.zeros((nrow, LANES), F32)
        for mm in range(n_blk):
            gm = gate[:, mm:mm + 1]
            tie = jnp.where(lane > mm, 1.0, 0.0)
            rank = rank + jnp.where(gm > gate, 1.0, jnp.where(gm == gate, tie, 0.0))
        sel = (rank < MOBA_TOPK) & (lane < n_blk)

        q = q_ref[0]
        kn = kn_ref[0]
        vn = vn_ref[0]
        tq = lax.broadcasted_iota(I32, (nrow, t_len), 0) % t_len
        sk = lax.broadcasted_iota(I32, (nrow, t_len), 1)
        s_own = jnp.concatenate(
            [lax.dot_general((q[h] * scale).astype(BF16), kn[h].astype(BF16), nt_dims, preferred_element_type=F32)
             for h in range(nh)], axis=0) + _bias_chain(tq - sk, tab_at, thr)
        s_own = jnp.where(sk <= tq, s_own, NEG_INF)
        m_o = jnp.max(s_own, axis=1, keepdims=True)
        p_o = jnp.exp(s_own - m_o)
        l_o = jnp.sum(p_o, axis=1, keepdims=True)
        o_o = jnp.concatenate(
            [jnp.dot(p_o[h * t_len:(h + 1) * t_len].astype(BF16), vn[h].astype(BF16), preferred_element_type=F32)
             for h in range(nh)], axis=0)

        m_sel = jnp.where(sel, m_scr[...], NEG_INF)
        m_fin = jnp.maximum(jnp.max(m_sel, axis=1, keepdims=True), m_o)
        w = jnp.where(sel, jnp.exp(m_sel - m_fin), 0.0)
        w_o = jnp.exp(m_o - m_fin)
        l_fin = jnp.sum(w * l_scr[...], axis=1, keepdims=True) + w_o * l_o
        acc = w_o * o_o
        for nn in range(n_blk):
            acc = acc + w[:, nn:nn + 1] * o_scr[nn]
        o_ref[0] = acc / l_fin


MOBA_SAMPLE_BLOCKS_PER_STEP = 8


def moba_sample(page_table, cache_kt, cache_vt, q, kn, vn, tab_heads, *, layer_e, t_len):
    n_b, n_pages = page_table.shape
    page = cache_kt.shape[-1]
    nbs = MOBA_SAMPLE_BLOCKS_PER_STEP
    ppb = MOBA_BLOCK // page
    assert ppb == 2 and page == LANES and n_pages % (ppb * nbs) == 0
    n_blk = n_pages // ppb
    assert n_blk <= LANES
    nh, hd = MOBA_HEADS, MOBA_HEAD_DIM
    nrow = nh * t_len
    tab = jnp.pad(jnp.repeat(tab_heads, t_len, axis=0), ((0, 0), (0, LANES - NUM_BUCKETS)))
    qbd = (q[:, :, :, None, :] * jnp.eye(nh, dtype=F32)[None, :, None, :, None]).reshape(n_b, nrow, nh * hd)

    def pspec(p):
        return pl.BlockSpec((None, None, nh, hd, page),
                            lambda b, n, pt: (pt[b, n * ppb * nbs + p], layer_e, 0, 0, 0))

    def bspec():
        return pl.BlockSpec((1, nh, t_len, hd), lambda b, n, pt: (b, 0, 0, 0))

    pages = [pspec(p) for p in range(ppb * nbs)]
    grid_spec = pltpu.PrefetchScalarGridSpec(
        num_scalar_prefetch=1,
        grid=(n_b, n_blk // nbs),
        in_specs=pages + pages + [
            bspec(),
            pl.BlockSpec((1, nrow, nh * hd), lambda b, n, pt: (b, 0, 0)),
            bspec(), bspec(),
            pl.BlockSpec(tab.shape, lambda b, n, pt: (0, 0))],
        out_specs=pl.BlockSpec((1, nrow, hd), lambda b, n, pt: (b, 0, 0)),
        scratch_shapes=[pltpu.VMEM((nrow, LANES), F32), pltpu.VMEM((nrow, LANES), F32),
                        pltpu.VMEM((nrow, LANES), F32), pltpu.VMEM((n_blk, nrow, hd), F32),
                        pltpu.VMEM((nrow, MOBA_BLOCK), F32)],
    )
    return pl.pallas_call(
        functools.partial(_moba_sample_kernel, n_blk=n_blk, nbs=nbs, page=page, t_len=t_len,
                          thr=_bucket_thresholds()),
        out_shape=jax.ShapeDtypeStruct((n_b, nrow, hd), F32),
        grid_spec=grid_spec,
        compiler_params=_cparams(("parallel", "arbitrary")),
        name="moba_sample",
    )(page_table, *([cache_kt] * (ppb * nbs)), *([cache_vt] * (ppb * nbs)), q, qbd, kn, vn, tab)


def _retention_kernel(q_ref, k_ref, v_ref, g_ref, cos_ref, sin_ref, dmask_ref, xi_ref, zeta_ref, gc_ref,
                      gng_ref, gnb_ref, *rest, nb, chunk, has_s0):
    if has_s0:
        s0_ref, o_ref, s_out_ref, s_scr = rest
    else:
        o_ref, s_out_ref, s_scr = rest
    c = pl.program_id(1)
    dk, dv = RET_DK, RET_DV
    half = dk // 2

    @pl.when(c == 0)
    def _():
        if has_s0:
            s_scr[...] = s0_ref[...]
        else:
            s_scr[...] = jnp.zeros(s_scr.shape, F32)

    cos = cos_ref[...]
    sin = sin_ref[...]

    def rot(x):
        x1 = x[:, :half]
        x2 = x[:, half:]
        return jnp.concatenate([x1 * cos - x2 * sin, x2 * cos + x1 * sin], axis=1)

    out_rows = []
    for bb in range(nb):
        rows = pl.ds(bb * chunk, chunk)
        out_heads = []
        for h in range(RET_HEADS):
            qh = rot(q_ref[rows, h * dk:(h + 1) * dk])
            kh = rot(k_ref[rows, h * dk:(h + 1) * dk]) * (dk ** -0.5)
            vh = v_ref[rows, h * dv:(h + 1) * dv]
            s = s_scr[bb, h]
            att = _bdot_nt(qh, kh) * dmask_ref[h]
            o = _bdot(att, vh) + _bdot(qh, s) * xi_ref[h]
            s_scr[bb, h] = s * gc_ref[h, 0:1, 0:1] + _bdot_tn(kh * zeta_ref[h], vh)
            mu = jnp.mean(o, axis=-1, keepdims=True)
            oc = o - mu
            var = jnp.mean(oc * oc, axis=-1, keepdims=True)
            on = oc * lax.rsqrt(var + LN_EPS) * gng_ref[:, h * dv:(h + 1) * dv] + gnb_ref[:, h * dv:(h + 1) * dv]
            gate = _silu(g_ref[rows, h * dv:(h + 1) * dv])
            out_heads.append(gate * on)
        out_rows.append(jnp.concatenate(out_heads, axis=1))
    o_ref[...] = jnp.concatenate(out_rows, axis=0).astype(o_ref.dtype)

    @pl.when(c == pl.num_programs(1) - 1)
    def _():
        s_out_ref[...] = s_scr[...]


def retention(p, cos, sin, consts, gn_g, gn_b, s0, *, row0, n_batch, t_len, nb, out_dtype):
    chunk = math.gcd(t_len, RET_CHUNK)
    n_chunk = t_len // chunk
    dmask, xi, zeta, gc = consts
    blk_rows = nb * chunk
    if nb > 1:
        assert n_chunk == 1
    assert row0 % blk_rows == 0
    base = row0 // blk_rows
    qk_w = RET_HEADS * RET_DK
    v_w = RET_HEADS * RET_DV
    has_s0 = s0 is not None

    def rmap(col):
        return lambda b, c: (base + b * n_chunk + c, col)

    in_specs = [pl.BlockSpec((blk_rows, qk_w), rmap(0)),
                pl.BlockSpec((blk_rows, qk_w), rmap(1)),
                pl.BlockSpec((blk_rows, v_w), rmap(1)),
                pl.BlockSpec((blk_rows, v_w), rmap(2)),
                pl.BlockSpec((chunk, RET_DK // 2), lambda b, c: (c, 0)),
                pl.BlockSpec((chunk, RET_DK // 2), lambda b, c: (c, 0)),
                pl.BlockSpec(dmask.shape, lambda b, c: (0, 0, 0)),
                pl.BlockSpec(xi.shape, lambda b, c: (0, 0, 0)),
                pl.BlockSpec(zeta.shape, lambda b, c: (0, 0, 0)),
                pl.BlockSpec(gc.shape, lambda b, c: (0, 0, 0)),
                pl.BlockSpec((1, v_w), lambda b, c: (0, 0)),
                pl.BlockSpec((1, v_w), lambda b, c: (0, 0))]
    args = [p, p, p, p, cos, sin, dmask, xi, zeta, gc, gn_g.reshape(1, -1), gn_b.reshape(1, -1)]
    s_spec = pl.BlockSpec((nb, RET_HEADS, RET_DK, RET_DV), lambda b, c: (b, 0, 0, 0))
    if has_s0:
        in_specs.append(s_spec)
        args.append(s0)
    o, s_out = pl.pallas_call(
        functools.partial(_retention_kernel, nb=nb, chunk=chunk, has_s0=has_s0),
        out_shape=(jax.ShapeDtypeStruct((n_batch * t_len, v_w), out_dtype),
                   jax.ShapeDtypeStruct((n_batch, RET_HEADS, RET_DK, RET_DV), F32)),
        grid=(n_batch // nb, n_chunk),
        in_specs=in_specs,
        out_specs=(pl.BlockSpec((blk_rows, v_w), lambda b, c: (b * n_chunk + c, 0)), s_spec),
        scratch_shapes=[pltpu.VMEM((nb, RET_HEADS, RET_DK, RET_DV), F32)],
        compiler_params=_cparams(("parallel", "arbitrary")),
        name="retention",
    )(*args)
    return o, s_out


def _retention_consts(chunk):
    h = jnp.arange(RET_HEADS, dtype=F32)
    lg = jnp.log1p(-jnp.exp2(-5.0 - h))
    idx = jnp.arange(chunk, dtype=F32)
    diff = idx[:, None] - idx[None, :]
    dmask = jnp.where(diff >= 0, jnp.exp(lg[:, None, None] * jnp.maximum(diff, 0.0)), 0.0)
    xi = jnp.exp(lg[:, None] * (idx[None, :] + 1.0))
    zeta = jnp.exp(lg[:, None] * (chunk - 1.0 - idx[None, :]))
    g_c = jnp.exp(lg * chunk)
    return (dmask,
            jnp.broadcast_to(xi[:, :, None], (RET_HEADS, chunk, RET_DV)),
            jnp.broadcast_to(zeta[:, :, None], (RET_HEADS, chunk, RET_DK)),
            jnp.broadcast_to(g_c[:, None, None], (RET_HEADS, 8, LANES)))


def _rotary_tables(pos):
    half = RET_DK // 2
    inv = 10000.0 ** (-jnp.linspace(0.0, 1.0, half, dtype=F32))
    ang = pos.astype(F32)[:, None] * inv[None, :]
    return jnp.cos(ang), jnp.sin(ang)


def _mem_attn_kernel(q_ref, mk_ref, mv_ref, o_ref, *, nb, tq):
    hd = MEM_HEAD_DIM
    scale = hd ** -0.5
    qf = q_ref[...].astype(F32)
    out_rows = []
    for bb in range(nb):
        out_heads = []
        for h in range(MEM_HEADS):
            cols = slice(h * hd, (h + 1) * hd)
            q = qf[bb * tq:(bb + 1) * tq, cols]
            s = _bdot_nt(q, mk_ref[bb, :, cols]) * scale
            m = jnp.max(s, axis=-1, keepdims=True)
            p = jnp.exp(s - m)
            p = p / jnp.sum(p, axis=-1, keepdims=True)
            out_heads.append(_bdot(p, mv_ref[bb, :, cols]))
        out_rows.append(jnp.concatenate(out_heads, axis=1))
    o_ref[...] = jnp.concatenate(out_rows, axis=0).astype(o_ref.dtype)


def mem_attention(q, mk, mv, *, layer, row0, n_batch, t_len, nb, tq, out_dtype):
    n_t = t_len // tq
    blk_rows = nb * tq
    if nb > 1:
        assert n_t == 1
    assert row0 % blk_rows == 0
    base = row0 // blk_rows
    m_tok = mk.shape[2]
    return pl.pallas_call(
        functools.partial(_mem_attn_kernel, nb=nb, tq=tq),
        out_shape=jax.ShapeDtypeStruct((n_batch * t_len, D_MODEL), out_dtype),
        grid=(n_batch // nb, n_t),
        in_specs=[pl.BlockSpec((blk_rows, D_MODEL), lambda b, t: (base + b * n_t + t, 0)),
                  pl.BlockSpec((None, nb, m_tok, D_MODEL), lambda b, t: (layer, b, 0, 0)),
                  pl.BlockSpec((None, nb, m_tok, D_MODEL), lambda b, t: (layer, b, 0, 0))],
        out_specs=pl.BlockSpec((blk_rows, D_MODEL), lambda b, t: (b * n_t + t, 0)),
        compiler_params=_cparams(("parallel", "arbitrary")),
        name="mem_attention",
    )(q, mk, mv)


def _router_kernel(x_ref, w_ref, b_ref, tri_ref, wt_ref, ei_ref, cnt_ref, base_scr):
    @pl.when(pl.program_id(0) == 0)
    def _():
        base_scr[...] = jnp.zeros(base_scr.shape, F32)

    x = x_ref[...]
    logits = _dot_hi(x, w_ref[...]) + b_ref[...]
    tm = x.shape[0]
    lane = lax.broadcasted_iota(I32, (tm, LANES), 1)
    big = jnp.int32(LANES)

    def masked_softmax(mask):
        lm = jnp.where(mask, logits, NEG_INF)
        mx = jnp.max(lm, axis=-1, keepdims=True)
        e = jnp.exp(lm - mx)
        return e / jnp.sum(e, axis=-1, keepdims=True)

    g_prob = masked_softmax(lane < MOE_GROUPS)
    gp = jnp.max(g_prob, axis=-1, keepdims=True)
    gi = jnp.min(jnp.where(g_prob == gp, lane, big), axis=-1, keepdims=True)
    lo = MOE_GROUPS + gi * MOE_EPG
    in_group = (lane >= lo) & (lane < lo + MOE_EPG)
    e_prob = jnp.where(in_group, masked_softmax(in_group), -1.0)
    p1 = jnp.max(e_prob, axis=-1, keepdims=True)
    i1 = jnp.min(jnp.where(e_prob == p1, lane, big), axis=-1, keepdims=True)
    rest = jnp.where(lane == i1, -1.0, e_prob)
    p2 = jnp.max(rest, axis=-1, keepdims=True)
    i2 = jnp.min(jnp.where(rest == p2, lane, big), axis=-1, keepdims=True)
    tot = p1 + p2
    w1 = gp * (p1 / tot)
    w2 = gp * (p2 / tot)
    e1 = i1 - MOE_GROUPS
    e2 = i2 - MOE_GROUPS
    oh1 = jnp.where(lane == e1, 1.0, 0.0)
    oh2 = jnp.where(lane == e2, 1.0, 0.0)
    both = oh1 + oh2
    base = base_scr[0:1, :]
    before = jnp.dot(tri_ref[...], both.astype(BF16), preferred_element_type=F32) + base
    r1 = jnp.sum(oh1 * before, axis=-1, keepdims=True).astype(I32)
    r2 = jnp.sum(oh2 * before, axis=-1, keepdims=True).astype(I32)
    base_new = base + jnp.sum(both, axis=0, keepdims=True)
    base_scr[0:1, :] = base_new
    wt_ref[...] = jnp.where(lane == 0, w1, jnp.where(lane == 1, w2, 0.0))
    ei_ref[...] = jnp.where(lane == 0, e1, jnp.where(lane == 1, e2, jnp.where(lane == 2, r1,
                                                                              jnp.where(lane == 3, r2, 0))))
    cnt_ref[...] = jnp.broadcast_to(base_new, cnt_ref.shape)


def moe_router(x, w_all, b_all, *, tm):
    m, d = x.shape
    tri = jnp.asarray(np.tril(np.ones((tm, tm), np.float32), -1), BF16)
    return pl.pallas_call(
        _router_kernel,
        out_shape=(jax.ShapeDtypeStruct((m, LANES), F32), jax.ShapeDtypeStruct((m, LANES), I32),
                   jax.ShapeDtypeStruct((8, LANES), F32)),
        grid=(m // tm,),
        in_specs=[pl.BlockSpec((tm, d), lambda i: (i, 0)),
                  pl.BlockSpec((d, LANES), lambda i: (0, 0)),
                  pl.BlockSpec((1, LANES), lambda i: (0, 0)),
                  pl.BlockSpec((tm, tm), lambda i: (0, 0))],
        out_specs=(pl.BlockSpec((tm, LANES), lambda i: (i, 0)), pl.BlockSpec((tm, LANES), lambda i: (i, 0)),
                   pl.BlockSpec((8, LANES), lambda i: (0, 0))),
        scratch_shapes=[pltpu.VMEM((8, LANES), F32)],
        compiler_params=_cparams(("arbitrary",)),
        name="moe_router",
    )(x, w_all, b_all, tri)


def _moe_dispatch_kernel(dest_ref, x_ref, xs_in, xs_out, sem, *, tm):
    del xs_in

    def body(g, carry):
        for u in range(DMA_UNROLL):
            r = g * DMA_UNROLL + u
            _tile_copy(x_ref, r, xs_out, dest_ref[r], sem.at[0]).start(priority=0)
            _tile_copy(x_ref, r, xs_out, dest_ref[tm + r], sem.at[0]).start(priority=1)
        return carry

    lax.fori_loop(0, tm // DMA_UNROLL, body, 0)
    for _ in range(MOE_TOPK):
        _wait_tiles(x_ref, xs_out, 0, sem.at[0], tm)


def moe_dispatch(x8, dest, n_slots, *, tm):
    n_t = dest.shape[0]
    xs0 = jnp.zeros((n_slots * ROW_TILE, LANES), U32)
    return pl.pallas_call(
        functools.partial(_moe_dispatch_kernel, tm=tm),
        out_shape=jax.ShapeDtypeStruct(xs0.shape, U32),
        grid=(n_t,),
        in_specs=[pl.BlockSpec((None, None, 2 * tm), lambda i: (i, 0, 0), memory_space=pltpu.SMEM),
                  pl.BlockSpec((tm * ROW_TILE, LANES), lambda i: (i, 0)),
                  pl.BlockSpec(memory_space=pl.ANY)],
        out_specs=pl.BlockSpec(memory_space=pl.ANY),
        scratch_shapes=[pltpu.SemaphoreType.DMA((1,))],
        input_output_aliases={2: 0},
        compiler_params=_cparams(("arbitrary",)),
        name="moe_dispatch",
    )(dest.reshape(n_t, 1, 2 * tm), x8, xs0)


def _moe_ffn_kernel(te_ref, nt_ref, first_ref, slot_ref, nxt_ref, xs_ref, w1_hbm, w3_hbm, w2_hbm, ys_ref,
                    wb1, wb3, wb2, sem, *, layer):
    j = pl.program_id(0)

    def weight_copies(e, s):
        return (pltpu.make_async_copy(w1_hbm.at[layer, e], wb1.at[s], sem.at[s]),
                pltpu.make_async_copy(w3_hbm.at[layer, e], wb3.at[s], sem.at[s]),
                pltpu.make_async_copy(w2_hbm.at[layer, e], wb2.at[s], sem.at[s]))

    @pl.when(j == 0)
    def _():
        for c in weight_copies(te_ref[0], slot_ref[0]):
            c.start()

    @pl.when(j < nt_ref[0])
    def _():
        s = slot_ref[j]

        @pl.when(first_ref[j] == 1)
        def _():
            for c in weight_copies(te_ref[j], s):
                c.wait()

            @pl.when(nxt_ref[j] >= 0)
            def _():
                for c in weight_copies(nxt_ref[j], 1 - s):
                    c.start()

        x = _load_token_tiles(xs_ref, 0, MOE_TILE).astype(BF16)
        h = _silu(_bdot(x, wb1[s])) * _bdot(x, wb3[s])
        _store_token_tiles(ys_ref, _bdot(h, wb2[s]))

    @pl.when(j >= nt_ref[0])
    def _():
        ys_ref[...] = jnp.zeros(ys_ref.shape, U32)


def moe_ffn(xs8, plan, w1, w3, w2, *, layer):
    tile_expert, n_tiles, first, slot, nxt = plan
    n_t = tile_expert.shape[0]
    d, dff = w1.shape[2], w1.shape[3]
    blk = MOE_TILE * ROW_TILE
    grid_spec = pltpu.PrefetchScalarGridSpec(
        num_scalar_prefetch=5,
        grid=(n_t,),
        in_specs=[pl.BlockSpec((blk, LANES), lambda j, te, nt, *_: (jnp.minimum(j, nt[0] - 1), 0)),
                  pl.BlockSpec(memory_space=pl.ANY),
                  pl.BlockSpec(memory_space=pl.ANY),
                  pl.BlockSpec(memory_space=pl.ANY)],
        out_specs=pl.BlockSpec((blk, LANES), lambda j, te, nt, *_: (j, 0)),
        scratch_shapes=[pltpu.VMEM((2, d, dff), F32), pltpu.VMEM((2, d, dff), F32), pltpu.VMEM((2, dff, d), F32),
                        pltpu.SemaphoreType.DMA((2,))],
    )
    return pl.pallas_call(
        functools.partial(_moe_ffn_kernel, layer=layer),
        out_shape=jax.ShapeDtypeStruct(xs8.shape, U32),
        grid_spec=grid_spec,
        compiler_params=_cparams(("arbitrary",)),
        name="moe_ffn",
    )(tile_expert, n_tiles, first, slot, nxt, xs8, w1, w3, w2)


def _moe_combine_kernel(d_cur, d_nxt, ys_hbm, x_ref, wt_ref, g_ref, b_ref, *rest, tm, lead_tiles):
    if lead_tiles is None:
        o_ref, ybuf, sem = rest
    else:
        o_ref, o_tail_ref, ybuf, sem = rest
    i = pl.program_id(0)
    n = pl.num_programs(0)
    slot = i % 2

    def issue(d_ref, s):
        def body(g, carry):
            for u in range(DMA_UNROLL):
                r = g * DMA_UNROLL + u
                _tile_copy(ys_hbm, d_ref[r], ybuf, s * 2 * tm + r, sem.at[s]).start(priority=u % 2)
            return carry
        lax.fori_loop(0, 2 * tm // DMA_UNROLL, body, 0)

    @pl.when(i == 0)
    def _():
        issue(d_cur, 0)

    @pl.when(i + 1 < n)
    def _():
        issue(d_nxt, 1 - slot)

    base_tok = slot * 2 * tm
    _wait_tiles(ys_hbm, ybuf, base_tok, sem.at[slot], 2 * tm)
    wt = wt_ref[...]
    y0 = _load_token_tiles(ybuf, base_tok * ROW_TILE, tm)
    y1 = _load_token_tiles(ybuf, (base_tok + tm) * ROW_TILE, tm)
    y = DEEPNORM_ALPHA * x_ref[...] + (wt[:, 0:1] * y0 + wt[:, 1:2] * y1)
    out = _layer_norm(y, g_ref[...], b_ref[...])
    if lead_tiles is None:
        o_ref[...] = out
    else:
        @pl.when(i < lead_tiles)
        def _():
            o_ref[...] = out

        @pl.when(i >= lead_tiles)
        def _():
            o_tail_ref[...] = out


def moe_combine_deepnorm(ys8, dest, x, wts, g, b, *, tm, lead_rows=None):
    m, d = x.shape
    n_t = m // tm
    dest3 = dest.reshape(n_t, 1, 2 * tm)
    if lead_rows is None:
        lead_tiles = None
        out_shape = jax.ShapeDtypeStruct((m, d), F32)
        out_specs = pl.BlockSpec((tm, d), lambda i: (i, 0))
    else:
        assert lead_rows % tm == 0 and m - lead_rows == tm
        lead_tiles = lead_rows // tm
        out_shape = (jax.ShapeDtypeStruct((lead_rows, d), F32), jax.ShapeDtypeStruct((tm, d), F32))
        out_specs = (pl.BlockSpec((tm, d), lambda i: (jnp.minimum(i, lead_tiles - 1), 0)),
                     pl.BlockSpec((tm, d), lambda i: (0, 0)))
    return pl.pallas_call(
        functools.partial(_moe_combine_kernel, tm=tm, lead_tiles=lead_tiles),
        out_shape=out_shape,
        grid=(n_t,),
        in_specs=[pl.BlockSpec((None, None, 2 * tm), lambda i: (i, 0, 0), memory_space=pltpu.SMEM),
                  pl.BlockSpec((None, None, 2 * tm), lambda i: (jnp.minimum(i + 1, n_t - 1), 0, 0),
                               memory_space=pltpu.SMEM),
                  pl.BlockSpec(memory_space=pl.ANY),
                  pl.BlockSpec((tm, d), lambda i: (i, 0)),
                  pl.BlockSpec((tm, LANES), lambda i: (i, 0)),
                  pl.BlockSpec((1, d), lambda i: (0, 0)),
                  pl.BlockSpec((1, d), lambda i: (0, 0))],
        out_specs=out_specs,
        scratch_shapes=[pltpu.VMEM((2 * 2 * tm * ROW_TILE, LANES), U32), pltpu.SemaphoreType.DMA((2,))],
        compiler_params=_cparams(("arbitrary",)),
        name="moe_combine_deepnorm",
    )(dest3, dest3, ys8, x, wts, g.reshape(1, d), b.reshape(1, d))


def _moe_plan(eidx, rank, counts, n_tok, tm):
    n_tiles_max = n_tok * MOE_TOPK // MOE_TILE + MOE_EXPERTS
    tiles_per = (counts + MOE_TILE - 1) // MOE_TILE
    tile_end = jnp.cumsum(tiles_per)
    pad_off = (tile_end - tiles_per) * MOE_TILE
    onehot = eidx[:, :, None] == jnp.arange(MOE_EXPERTS, dtype=I32)[None, None, :]
    dest = jnp.sum(jnp.where(onehot, pad_off[None, None, :], 0), axis=-1) + rank
    tile_expert = jnp.minimum(
        jnp.sum((jnp.arange(n_tiles_max, dtype=I32)[:, None] >= tile_end[None, :]).astype(I32), axis=1),
        MOE_EXPERTS - 1)
    dest = dest.reshape(n_tok // tm, tm, MOE_TOPK).transpose(0, 2, 1).reshape(n_tok // tm, MOE_TOPK * tm)
    experts = jnp.arange(MOE_EXPERTS, dtype=I32)
    nonempty = tiles_per > 0
    slot_e = (jnp.cumsum(nonempty.astype(I32)) - 1) % 2
    later = nonempty[None, :] & (experts[None, :] > experts[:, None])
    nxt_e = jnp.min(jnp.where(later, experts[None, :], MOE_EXPERTS), axis=1)
    nxt_e = jnp.where(nxt_e == MOE_EXPERTS, -1, nxt_e)
    tiles = jnp.arange(n_tiles_max, dtype=I32)
    tile_oh = tile_expert[:, None] == experts[None, :]
    first = jnp.any((tiles[:, None] == (tile_end - tiles_per)[None, :]) & nonempty[None, :], axis=1).astype(I32)
    slot = jnp.sum(jnp.where(tile_oh, slot_e[None, :], 0), axis=1).astype(I32)
    nxt = jnp.sum(jnp.where(tile_oh, nxt_e[None, :], 0), axis=1).astype(I32)
    ffn_plan = (tile_expert, tile_end[-1:].astype(I32), first, slot, nxt)
    return ffn_plan, dest, n_tiles_max * MOE_TILE


def hier_moe_deepnorm(x, x8, w_all, b_all, w1, w3, w2, g, b, *, layer, lead_rows=None):
    n_tok = x.shape[0]
    wts, ei, cnt = moe_router(x, w_all, b_all, tm=640)
    counts = cnt[0, :MOE_EXPERTS].astype(I32)
    ffn_plan, dest, n_slots = _moe_plan(ei[:, 0:2], ei[:, 2:4], counts, n_tok, MOE_TILE)
    xs8 = moe_dispatch(x8, dest, n_slots, tm=MOE_TILE)
    ys8 = moe_ffn(xs8, ffn_plan, w1, w3, w2, layer=layer)
    return moe_combine_deepnorm(ys8, dest, x, wts, g, b, tm=MOE_TILE, lead_rows=lead_rows)


def kernel(x_prompt, x_sample, cache_moba_k, cache_moba_v, state_conv, state_ret, cache_mem_k, cache_mem_v,
           page_table, mem_prompt, rel_bias, ev_w_in, ev_conv_w, ev_conv_b, ev_conv_ln_g, ev_conv_ln_b,
           ev_w_out, od_w_in, od_gn_g, od_gn_b, od_w_out, mem_wq, mem_wk, mem_wv, mem_wo, ln_g, ln_b,
           moe_w_group, moe_b_group, moe_w_router, moe_b_router, moe_w1, moe_w3, moe_w2):
    bp, tp, d = x_prompt.shape
    bs, ts, _ = x_sample.shape
    n_p = bp * tp
    n_s = bs * ts
    n_all = n_p + n_s
    page = cache_moba_k.shape[1]
    past_len = page_table.shape[1] * page
    nh, hd = MOBA_HEADS, MOBA_HEAD_DIM
    assert tp % MOBA_BLOCK == 0 and past_len % MOBA_BLOCK == 0 and ts <= MOBA_BLOCK
    assert MOBA_BLOCK >= MAX_DISTANCE
    tm = 640
    assert n_all % tm == 0

    x = jnp.concatenate([x_prompt.reshape(n_p, d), x_sample.reshape(n_s, d)], axis=0)

    tab_heads = rel_bias.T
    cache_kt = cache_moba_k.transpose(0, 2, 3, 4, 1)
    cache_vt = cache_moba_v.transpose(0, 2, 3, 4, 1)
    mem_k_cache = cache_mem_k.reshape(DEPTH, bs, cache_mem_k.shape[2], d)
    mem_v_cache = cache_mem_v.reshape(DEPTH, bs, cache_mem_v.shape[2], d)

    pos_p = jnp.arange(tp, dtype=I32)
    pos_s = past_len + jnp.arange(ts, dtype=I32)
    cos_p, sin_p = _rotary_tables(pos_p)
    cos_s, sin_s = _rotary_tables(pos_s)
    ret_c_p = _retention_consts(math.gcd(tp, RET_CHUNK))
    ret_c_s = _retention_consts(math.gcd(ts, RET_CHUNK))

    outs = {}
    for layer in range(DEPTH):
        if layer % 2 == 0:
            e = layer // 2
            proj = matmul(x, ev_w_in[e].astype(BF16), tm=tm, tn=ev_w_in.shape[2])
            k_new = proj[:, 2 * CONV_CH + MOBA_WIDTH:2 * CONV_CH + 2 * MOBA_WIDTH]
            v_new = proj[:, 2 * CONV_CH + 2 * MOBA_WIDTH:]
            q_s = proj[n_p:, 2 * CONV_CH:2 * CONV_CH + MOBA_WIDTH]
            conv_args = (ev_conv_w[e], ev_conv_b[e], ev_conv_ln_g[e], ev_conv_ln_b[e])
            c_p, cst_p = conv_module(proj, jnp.zeros((bp, CONV_WIDTH - 1, CONV_CH), F32), *conv_args,
                                     row0=0, n_batch=bp, t_len=tp, nb=1, out_dtype=BF16)
            c_s, cst_s = conv_module(proj, state_conv[e], *conv_args,
                                     row0=n_p, n_batch=bs, t_len=ts, nb=bs, out_dtype=F32)
            a_p = moba_prompt(proj, tab_heads, n_batch=bp, t_len=tp, out_dtype=BF16)

            def by_head(a):
                return a.reshape(bs, ts, nh, hd).transpose(0, 2, 1, 3)

            a_s = moba_sample(page_table, cache_kt, cache_vt, by_head(q_s), by_head(k_new[n_p:]),
                              by_head(v_new[n_p:]), tab_heads, layer_e=e, t_len=ts)
            a_s = a_s.reshape(bs, nh, ts, hd).transpose(0, 2, 1, 3).reshape(n_s, MOBA_WIDTH).astype(BF16)
            w_out = ev_w_out[e].astype(BF16)
            x = matmul_deepnorm([(c_p, c_s.astype(BF16)), (a_p, a_s)], [w_out[:CONV_CH], w_out[CONV_CH:]], x,
                                ln_g[layer, 0], ln_b[layer, 0], tm=tm)
            outs.setdefault("kp", []).append(k_new[:n_p].reshape(bp, tp, nh, hd))
            outs.setdefault("vp", []).append(v_new[:n_p].reshape(bp, tp, nh, hd))
            outs.setdefault("ks", []).append(k_new[n_p:].reshape(bs, ts, nh, hd))
            outs.setdefault("vs", []).append(v_new[n_p:].reshape(bs, ts, nh, hd))
            outs.setdefault("cp", []).append(cst_p)
            outs.setdefault("cs", []).append(cst_s)
        else:
            o = layer // 2
            proj = matmul(x, od_w_in[o].astype(BF16), tm=2 * tm, tn=2048)
            r_p, st_p = retention(proj, cos_p, sin_p, ret_c_p, od_gn_g[o], od_gn_b[o], None,
                                  row0=0, n_batch=bp, t_len=tp, nb=1, out_dtype=BF16)
            r_s, st_s = retention(proj, cos_s, sin_s, ret_c_s, od_gn_g[o], od_gn_b[o], state_ret[o],
                                  row0=n_p, n_batch=bs, t_len=ts, nb=2, out_dtype=BF16)
            x = matmul_deepnorm([(r_p, r_s)], [od_w_out[o].astype(BF16)], x, ln_g[layer, 0], ln_b[layer, 0], tm=tm)
            outs.setdefault("sp", []).append(st_p)
            outs.setdefault("ss", []).append(st_s)

        m_tok = mem_prompt.shape[1]
        mem2 = mem_prompt.reshape(bp * m_tok, d)
        mk_p = matmul(mem2, mem_wk[layer].astype(BF16), tm=512, tn=d)
        mv_p = matmul(mem2, mem_wv[layer].astype(BF16), tm=512, tn=d)
        outs.setdefault("mk", []).append(mk_p.reshape(bp, m_tok, MEM_HEADS, MEM_HEAD_DIM))
        outs.setdefault("mv", []).append(mv_p.reshape(bp, m_tok, MEM_HEADS, MEM_HEAD_DIM))
        q = matmul(x, mem_wq[layer].astype(BF16), tm=tm, tn=d, out_dtype=BF16)
        o_p = mem_attention(q, mk_p.reshape(1, bp, m_tok, d), mv_p.reshape(1, bp, m_tok, d),
                            layer=0, row0=0, n_batch=bp, t_len=tp, nb=1, tq=512, out_dtype=BF16)
        o_s = mem_attention(q, mem_k_cache, mem_v_cache,
                            layer=layer, row0=n_p, n_batch=bs, t_len=ts, nb=2, tq=ts, out_dtype=BF16)
        x, x8 = matmul_deepnorm([(o_p, o_s)], [mem_wo[layer].astype(BF16)], x, ln_g[layer, 1], ln_b[layer, 1], tm=tm,
                                token_tiles=True)

        w_all = jnp.zeros((d, LANES), F32)
        w_all = w_all.at[:, :MOE_GROUPS].set(moe_w_group[layer])
        w_all = w_all.at[:, MOE_GROUPS:MOE_GROUPS + MOE_EXPERTS].set(moe_w_router[layer])
        b_all = jnp.zeros((1, LANES), F32)
        b_all = b_all.at[0, :MOE_GROUPS].set(moe_b_group[layer])
        b_all = b_all.at[0, MOE_GROUPS:MOE_GROUPS + MOE_EXPERTS].set(moe_b_router[layer])
        x = hier_moe_deepnorm(x, x8, w_all, b_all, moe_w1, moe_w3, moe_w2, ln_g[layer, 2], ln_b[layer, 2],
                              layer=layer, lead_rows=n_p if layer == DEPTH - 1 else None)

    y_prompt = x[0].reshape(bp, tp, d)
    y_sample = x[1].reshape(bs, ts, d)
    return (y_prompt, y_sample,
            jnp.stack(outs["kp"], axis=2), jnp.stack(outs["vp"], axis=2),
            jnp.stack(outs["ks"], axis=2), jnp.stack(outs["vs"], axis=2),
            jnp.stack(outs["cp"], axis=0), jnp.stack(outs["cs"], axis=0),
            jnp.stack(outs["sp"], axis=0), jnp.stack(outs["ss"], axis=0),
            jnp.stack(outs["mk"], axis=0), jnp.stack(outs["mv"], axis=0))
```

```python
import functools
import math

import numpy as np
import jax
import jax.numpy as jnp
from jax import lax
from jax.experimental import pallas as pl
from jax.experimental.pallas import tpu as pltpu

F32 = jnp.float32
BF16 = jnp.bfloat16
I32 = jnp.int32

D_MODEL = 1024
DEPTH = 2
CONV_CH = 512
CONV_WIDTH = 31
MOBA_HEADS = 8
MOBA_HEAD_DIM = 64
MOBA_WIDTH = 512
MOBA_BLOCK = 256
MOBA_TOPK = 3
NUM_BUCKETS = 32
MAX_DISTANCE = 128
RET_HEADS = 4
RET_DK = 256
RET_DV = 512
RET_CHUNK = 128
MEM_HEADS = 4
MEM_HEAD_DIM = 256
MOE_GROUPS = 4
MOE_EPG = 8
MOE_EXPERTS = 32
MOE_TOPK = 2
MOE_D_FF = 512
DEEPNORM_ALPHA = (2 * DEPTH) ** 0.25
LN_EPS = 1e-5

LANES = 128
VMEM_LIMIT = 56 * 1024 * 1024
MOE_TILE = 384
TOKEN_TILE = 256
DMA_UNROLL = 8
NEG_INF = float("-inf")


def _cparams(sem):
    return pltpu.CompilerParams(dimension_semantics=sem, vmem_limit_bytes=VMEM_LIMIT)


def _layer_norm(y, g, b):
    mu = jnp.mean(y, axis=-1, keepdims=True)
    yc = y - mu
    var = jnp.mean(yc * yc, axis=-1, keepdims=True)
    return yc * lax.rsqrt(var + LN_EPS) * g + b


def _silu(x):
    return x * (1.0 / (1.0 + jnp.exp(-x)))


def _bdot(a, b):
    return jnp.dot(a.astype(BF16), b.astype(BF16), preferred_element_type=F32)


def _bdot_nt(a, b):
    return lax.dot_general(a.astype(BF16), b.astype(BF16), (((1,), (1,)), ((), ())),
                           preferred_element_type=F32)


def _bdot_tn(a, b):
    return lax.dot_general(a.astype(BF16), b.astype(BF16), (((0,), (0,)), ((), ())),
                           preferred_element_type=F32)


def _split3(x):
    hi = x.astype(BF16)
    lo = (x - hi.astype(F32)).astype(BF16)
    return hi, lo


def _dot_hi(a, b, dims=(((1,), (0,)), ((), ()))):
    ah, al = _split3(a)
    bh, bl = _split3(b)
    dg = functools.partial(lax.dot_general, dimension_numbers=dims, preferred_element_type=F32)
    return dg(ah, bh) + (dg(al, bh) + dg(ah, bl))


U32 = jnp.uint32
ROW_TILE = D_MODEL // (2 * LANES)
HALF = D_MODEL // 2


def _bf16_bits(v):
    b = pltpu.bitcast(v, U32)
    return b + (jnp.uint32(0x7FFF) + ((b >> 16) & jnp.uint32(1)))


def _load_token_tiles(ref, start, n_tok):
    u = jnp.concatenate([ref[pl.ds(start + c, n_tok, stride=ROW_TILE), :] for c in range(ROW_TILE)], axis=1)
    lo = pltpu.bitcast(u << 16, F32)
    hi = pltpu.bitcast(u & jnp.uint32(0xFFFF0000), F32)
    return jnp.concatenate([lo, hi], axis=1)


def _store_token_tiles(ref, val):
    n_tok = val.shape[0]
    packed = (_bf16_bits(val[:, :HALF]) >> 16) | (_bf16_bits(val[:, HALF:]) & jnp.uint32(0xFFFF0000))
    for c in range(ROW_TILE):
        ref[pl.ds(c, n_tok, stride=ROW_TILE), :] = packed[:, c * LANES:(c + 1) * LANES]


def _tile_copy(src_hbm, src_tok, dst, dst_tok, sem):
    return pltpu.make_async_copy(src_hbm.at[pl.ds(src_tok * ROW_TILE, ROW_TILE), :],
                                 dst.at[pl.ds(dst_tok * ROW_TILE, ROW_TILE), :], sem)


def _wait_tiles(src_hbm, dst, dst_tok, sem, n_tok):
    pltpu.make_async_copy(src_hbm.at[pl.ds(0, n_tok * ROW_TILE), :],
                          dst.at[pl.ds(dst_tok * ROW_TILE, n_tok * ROW_TILE), :], sem).wait()


def _mm_kernel(x_ref, w_ref, o_ref):
    o_ref[...] = _bdot(x_ref[...], w_ref[...]).astype(o_ref.dtype)


def matmul(x, w, *, tm, tn, out_dtype=F32):
    m, k = x.shape
    n = w.shape[1]
    assert m % tm == 0 and n % tn == 0
    return pl.pallas_call(
        _mm_kernel,
        out_shape=jax.ShapeDtypeStruct((m, n), out_dtype),
        grid=(m // tm, n // tn),
        in_specs=[pl.BlockSpec((tm, k), lambda i, j: (i, 0)),
                  pl.BlockSpec((k, tn), lambda i, j: (0, j))],
        out_specs=pl.BlockSpec((tm, tn), lambda i, j: (i, j)),
        compiler_params=_cparams(("parallel", "parallel")),
        name="matmul",
    )(x, w)


def _mm_dn_kernel(*refs, n_lhs, token_tiles, tail_rows):
    lhs = refs[:2 * n_lhs:2]
    tails = refs[1:2 * n_lhs:2]
    ws = refs[2 * n_lhs:3 * n_lhs]
    res_ref, g_ref, b_ref, o_ref = refs[3 * n_lhs:3 * n_lhs + 4]
    last = pl.program_id(0) == pl.num_programs(0) - 1

    def rows(a_ref, t_ref):
        a = a_ref[...]
        if tail_rows:
            a = jnp.where(last, jnp.concatenate([a[:a.shape[0] - tail_rows], t_ref[...]], axis=0), a)
        return a

    acc = _bdot(rows(lhs[0], tails[0]), ws[0][...])
    for a, t, w in zip(lhs[1:], tails[1:], ws[1:]):
        acc = acc + _bdot(rows(a, t), w[...])
    y = DEEPNORM_ALPHA * res_ref[...] + acc
    out = _layer_norm(y, g_ref[...], b_ref[...])
    o_ref[...] = out
    if token_tiles:
        _store_token_tiles(refs[3 * n_lhs + 4], out)


def matmul_deepnorm(lhs_list, w_list, res, g, b, *, tm, token_tiles=False):
    m, d = res.shape
    n_lhs = len(lhs_list)
    tail_rows = lhs_list[0][1].shape[0]
    for a, t in lhs_list:
        assert a.shape[0] + t.shape[0] == m and t.shape[0] == tail_rows and a.shape[1] == t.shape[1]
        assert 0 < tail_rows < tm and (m - tail_rows) // tm == m // tm - 1
    flat_lhs = [x for pair in lhs_list for x in pair]
    in_specs = ([spec for a, t in lhs_list
                 for spec in (pl.BlockSpec((tm, a.shape[1]), lambda i: (i, 0)),
                              pl.BlockSpec(t.shape, lambda i: (0, 0)))]
                + [pl.BlockSpec(w.shape, lambda i: (0, 0)) for w in w_list]
                + [pl.BlockSpec((tm, d), lambda i: (i, 0)),
                   pl.BlockSpec((1, d), lambda i: (0, 0)),
                   pl.BlockSpec((1, d), lambda i: (0, 0))])
    out_shape = [jax.ShapeDtypeStruct((m, d), F32)]
    out_specs = [pl.BlockSpec((tm, d), lambda i: (i, 0))]
    if token_tiles:
        out_shape.append(jax.ShapeDtypeStruct((m * ROW_TILE, LANES), U32))
        out_specs.append(pl.BlockSpec((tm * ROW_TILE, LANES), lambda i: (i, 0)))
    res_out = pl.pallas_call(
        functools.partial(_mm_dn_kernel, n_lhs=n_lhs, token_tiles=token_tiles, tail_rows=tail_rows),
        out_shape=tuple(out_shape),
        grid=(m // tm,),
        in_specs=in_specs,
        out_specs=tuple(out_specs),
        compiler_params=_cparams(("parallel",)),
        name="matmul_deepnorm",
    )(*flat_lhs, *w_list, res, g.reshape(1, d), b.reshape(1, d))
    return res_out if token_tiles else res_out[0]


CONV_PAD = 32


def _conv_kernel(p_ref, hist_ref, w_ref, cb_ref, g_ref, b_ref, c_ref, st_ref, u_scr, sh_scr, *, nb, t_len, rc):
    hw = CONV_WIDTH - 1
    w = w_ref[...]
    for bb in range(nb):
        rows = pl.ds(bb * t_len, t_len)
        a = p_ref[rows, 0:CONV_CH]
        gt = p_ref[rows, CONV_CH:2 * CONV_CH]
        u = a * (1.0 / (1.0 + jnp.exp(-gt)))
        u_scr[0:CONV_PAD, :] = jnp.concatenate(
            [jnp.zeros((CONV_PAD - hw, CONV_CH), F32), hist_ref[bb]], axis=0)
        u_scr[CONV_PAD:CONV_PAD + t_len, :] = u
        st_ref[bb] = u_scr[CONV_PAD + t_len - hw:CONV_PAD + t_len, :]

        def chunk(ci, carry):
            r0 = pl.multiple_of(ci * rc, rc)
            span = rc + CONV_PAD - 8
            for r in range(1, 8):
                sh_scr[r - 1] = u_scr[pl.ds(r0, rc + CONV_PAD), :][r:r + span, :]
            acc = jnp.broadcast_to(cb_ref[...], (rc, CONV_CH))
            for j in range(CONV_WIDTH):
                off = j + CONV_PAD - hw
                r = off % 8
                if r == 0:
                    tap = u_scr[pl.ds(r0 + off, rc), :]
                else:
                    tap = sh_scr[r - 1, off - r:off - r + rc, :]
                acc = acc + w[j:j + 1, :] * tap
            y = _silu(_layer_norm(acc, g_ref[...], b_ref[...]))
            c_ref[pl.ds(bb * t_len + r0, rc), :] = y.astype(c_ref.dtype)
            return carry

        n_chunks = t_len // rc
        if n_chunks == 1:
            chunk(0, 0)
        else:
            lax.fori_loop(0, n_chunks, chunk, 0)


def conv_module(p, hist, conv_w, conv_b, ln_g, ln_b, *, row0, n_batch, t_len, nb, out_dtype):
    rc = min(256, t_len)
    blk_rows = nb * t_len
    assert row0 % blk_rows == 0 and n_batch % nb == 0
    base = row0 // blk_rows
    hw = CONV_WIDTH - 1
    c, st = pl.pallas_call(
        functools.partial(_conv_kernel, nb=nb, t_len=t_len, rc=rc),
        out_shape=(jax.ShapeDtypeStruct((n_batch * t_len, CONV_CH), out_dtype),
                   jax.ShapeDtypeStruct((n_batch, hw, CONV_CH), F32)),
        grid=(n_batch // nb,),
        in_specs=[pl.BlockSpec((blk_rows, 2 * CONV_CH), lambda i: (base + i, 0)),
                  pl.BlockSpec((nb, hw, CONV_CH), lambda i: (i, 0, 0)),
                  pl.BlockSpec((CONV_WIDTH, CONV_CH), lambda i: (0, 0)),
                  pl.BlockSpec((1, CONV_CH), lambda i: (0, 0)),
                  pl.BlockSpec((1, CONV_CH), lambda i: (0, 0)),
                  pl.BlockSpec((1, CONV_CH), lambda i: (0, 0))],
        out_specs=(pl.BlockSpec((blk_rows, CONV_CH), lambda i: (i, 0)),
                   pl.BlockSpec((nb, hw, CONV_CH), lambda i: (i, 0, 0))),
        scratch_shapes=[pltpu.VMEM((t_len + CONV_PAD, CONV_CH), F32),
                        pltpu.VMEM((7, rc + CONV_PAD - 8, CONV_CH), F32)],
        compiler_params=_cparams(("parallel",)),
        name="conv_module",
    )(p, hist, conv_w, conv_b.reshape(1, -1), ln_g.reshape(1, -1), ln_b.reshape(1, -1))
    return c, st


def _bucket_thresholds():
    max_exact = NUM_BUCKETS // 2
    d = np.arange(0, MAX_DISTANCE + 1)
    val = (np.log(np.maximum(d, 1).astype(np.float32) / np.float32(max_exact))
           / np.float32(math.log(MAX_DISTANCE / max_exact)) * np.float32(NUM_BUCKETS - max_exact))
    inner = (d > max_exact) & (d < MAX_DISTANCE)
    assert np.all(np.abs(val[inner] - np.round(val[inner])) > 1e-3)
    bucket = np.where(d < max_exact, d, np.minimum(max_exact + val.astype(np.int32), NUM_BUCKETS - 1))
    return [int(np.argmax(bucket >= k)) for k in range(1, NUM_BUCKETS)]


def _bias_chain(dist, tab_at, thr):
    b = jnp.where(dist >= thr[0], tab_at(1), tab_at(0))
    for k in range(2, NUM_BUCKETS):
        b = jnp.where(dist >= thr[k - 1], tab_at(k), b)
    return b


def _moba_prompt_kernel(tab_ref, q_ref, k_ref, v_ref, o_ref,
                        kmean_scr, kbf_scr, vt_scr, d0_scr, d1_scr, far_scr, sel_scr, qb_scr, m_scr, l_scr, acc_scr,
                        *, n_blk, thr):
    b = pl.program_id(0)
    i = pl.program_id(1)
    blk = MOBA_BLOCK
    hd = MOBA_HEAD_DIM
    n_pair = MOBA_HEADS // 2
    scale = hd ** -0.5
    nt_dims = (((1,), (1,)), ((), ()))
    key = lax.broadcasted_iota(I32, (blk, blk), 0)
    qry = lax.broadcasted_iota(I32, (blk, blk), 1)
    causal = jnp.concatenate([key <= qry, key <= qry], axis=1)

    @pl.when((b == 0) & (i == 0))
    def _():
        for h in range(MOBA_HEADS):
            tab_at = functools.partial(lambda k, hh: tab_ref[hh, k], hh=h)
            half = slice((h % 2) * blk, (h % 2 + 1) * blk)
            d0_scr[h // 2, :, half] = _bias_chain(qry - key, tab_at, thr)
            d1_scr[h // 2, :, half] = _bias_chain(blk + qry - key, tab_at, thr)
            far_scr[h // 2, :, half] = jnp.full((1, blk), tab_ref[h, NUM_BUCKETS - 1], F32)

    @pl.when(i == 0)
    def _():
        for n in range(n_blk):
            kb = k_ref[n * blk:(n + 1) * blk, :]
            kmean_scr[n:n + 1, :] = jnp.sum(kb, axis=0, keepdims=True) * (1.0 / blk)
            kbf_scr[n * blk:(n + 1) * blk, :] = kb.astype(BF16)
            for pair in range(n_pair):
                vt_scr[n, pair * LANES:(pair + 1) * LANES, :] = (
                    v_ref[n * blk:(n + 1) * blk, pair * LANES:(pair + 1) * LANES].T.astype(BF16))

    q = q_ref[...]
    lane = lax.broadcasted_iota(I32, (blk, LANES), 1)
    rown = lax.broadcasted_iota(I32, (n_blk, blk), 0)
    rowd = lax.broadcasted_iota(I32, (LANES, blk), 0)
    r0 = pl.multiple_of(i * blk, blk)
    pairs = [(pr, slice(pr * LANES, (pr + 1) * LANES)) for pr in range(n_pair)]

    for pr, cols in pairs:
        sels = []
        qbs = []
        for sub in range(2):
            in_head = (lane >= sub * hd) & (lane < (sub + 1) * hd)
            qm = jnp.where(in_head, q[:, cols], 0.0)
            gate = _dot_hi(kmean_scr[:, cols], qm, nt_dims)
            rank = jnp.zeros((n_blk, blk), F32)
            for m in range(n_blk):
                gm = gate[m:m + 1, :]
                tie = jnp.where(rown > m, 1.0, 0.0)
                cnt = jnp.where(gm > gate, 1.0, jnp.where(gm == gate, tie, 0.0))
                rank = rank + jnp.where(m < i, cnt, 0.0)
            sels.append(jnp.where((rown < i) & (rank < MOBA_TOPK), 1.0, 0.0))
            qbs.append((qm * scale).astype(BF16))
        sel_scr[pr] = jnp.concatenate(sels, axis=1)
        qb = jnp.concatenate(qbs, axis=0)
        qb_scr[pr] = qb
        s = lax.dot_general(kbf_scr[pl.ds(r0, blk), cols], qb, nt_dims,
                            preferred_element_type=F32) + d0_scr[pr]
        s = jnp.where(causal, s, NEG_INF)
        m_run = jnp.max(s, axis=0, keepdims=True)
        p = jnp.exp(s - m_run)
        m_scr[pr] = m_run
        l_scr[pr] = jnp.sum(p, axis=0, keepdims=True)
        acc_scr[pr] = jnp.dot(vt_scr[i, cols, :], p.astype(BF16), preferred_element_type=F32)

    def merge_block(n, pr, cols, s, shift):
        picked = sel_scr[pr, pl.ds(n, 1), :] > 0.5
        m_run = m_scr[pr]
        m_new = jnp.maximum(m_run, jnp.where(picked, jnp.max(s, axis=0, keepdims=True) + shift, NEG_INF))
        alpha = jnp.exp(m_run - m_new)
        p = jnp.exp(s - jnp.where(picked, m_new - shift, jnp.inf))
        m_scr[pr] = m_new
        l_scr[pr] = alpha * l_scr[pr] + jnp.sum(p, axis=0, keepdims=True)
        acc_scr[pr] = alpha * acc_scr[pr] + jnp.dot(vt_scr[n, cols, :], p.astype(BF16), preferred_element_type=F32)

    @pl.when(i > 0)
    def _():
        rr = pl.multiple_of((i - 1) * blk, blk)
        for pr, cols in pairs:
            s = lax.dot_general(kbf_scr[pl.ds(rr, blk), cols], qb_scr[pr], nt_dims,
                                preferred_element_type=F32) + d1_scr[pr]
            merge_block(i - 1, pr, cols, s, 0.0)

    def far_scores(n, pr, cols):
        rr = pl.multiple_of(n * blk, blk)
        return lax.dot_general(kbf_scr[pl.ds(rr, blk), cols], qb_scr[pr], nt_dims, preferred_element_type=F32)

    def merge_two(n0, n1, pr, cols, s0, s1, shift):
        pk0 = sel_scr[pr, pl.ds(n0, 1), :] > 0.5
        pk1 = sel_scr[pr, pl.ds(n1, 1), :] > 0.5
        m_run = m_scr[pr]
        bm0 = jnp.where(pk0, jnp.max(s0, axis=0, keepdims=True) + shift, NEG_INF)
        bm1 = jnp.where(pk1, jnp.max(s1, axis=0, keepdims=True) + shift, NEG_INF)
        m_new = jnp.maximum(m_run, jnp.maximum(bm0, bm1))
        alpha = jnp.exp(m_run - m_new)
        p0 = jnp.exp(s0 - jnp.where(pk0, m_new - shift, jnp.inf))
        p1 = jnp.exp(s1 - jnp.where(pk1, m_new - shift, jnp.inf))
        m_scr[pr] = m_new
        l_scr[pr] = (alpha * l_scr[pr] + jnp.sum(p0, axis=0, keepdims=True)) + jnp.sum(p1, axis=0, keepdims=True)
        acc_scr[pr] = (alpha * acc_scr[pr]
                       + jnp.dot(vt_scr[n0, cols, :], p0.astype(BF16), preferred_element_type=F32)
                       + jnp.dot(vt_scr[n1, cols, :], p1.astype(BF16), preferred_element_type=F32))

    n_far = jnp.maximum(i - 1, 0)

    def body(n2, carry):
        for pr, cols in pairs:
            merge_two(2 * n2, 2 * n2 + 1, pr, cols, far_scores(2 * n2, pr, cols),
                      far_scores(2 * n2 + 1, pr, cols), far_scr[pr])
        return carry

    lax.fori_loop(0, n_far // 2, body, 0)

    @pl.when(n_far % 2 == 1)
    def _():
        for pr, cols in pairs:
            merge_block(n_far - 1, pr, cols, far_scores(n_far - 1, pr, cols), far_scr[pr])

    outs = []
    for pr, _ in pairs:
        o = acc_scr[pr] / l_scr[pr]
        outs.append(jnp.where(rowd < hd, o[:, :blk], o[:, blk:]).T)
    o_ref[...] = jnp.concatenate(outs, axis=1).astype(o_ref.dtype)


def moba_prompt(p, tab, *, n_batch, t_len, out_dtype):
    n_blk = t_len // MOBA_BLOCK
    n_pair = MOBA_HEADS // 2
    wide = 2 * MOBA_BLOCK
    return pl.pallas_call(
        functools.partial(_moba_prompt_kernel, n_blk=n_blk, thr=_bucket_thresholds()),
        out_shape=jax.ShapeDtypeStruct((n_batch * t_len, MOBA_WIDTH), out_dtype),
        grid=(n_batch, n_blk),
        in_specs=[pl.BlockSpec(memory_space=pltpu.SMEM),
                  pl.BlockSpec((MOBA_BLOCK, MOBA_WIDTH), lambda b, i: (b * n_blk + i, 2)),
                  pl.BlockSpec((t_len, MOBA_WIDTH), lambda b, i: (b, 3)),
                  pl.BlockSpec((t_len, MOBA_WIDTH), lambda b, i: (b, 4))],
        out_specs=pl.BlockSpec((MOBA_BLOCK, MOBA_WIDTH), lambda b, i: (b * n_blk + i, 0)),
        scratch_shapes=[pltpu.VMEM((n_blk, MOBA_WIDTH), F32),
                        pltpu.VMEM((t_len, MOBA_WIDTH), BF16),
                        pltpu.VMEM((n_blk, MOBA_WIDTH, MOBA_BLOCK), BF16),
                        pltpu.VMEM((n_pair, MOBA_BLOCK, wide), F32),
                        pltpu.VMEM((n_pair, MOBA_BLOCK, wide), F32),
                        pltpu.VMEM((n_pair, 1, wide), F32),
                        pltpu.VMEM((n_pair, n_blk, wide), F32),
                        pltpu.VMEM((n_pair, wide, LANES), BF16),
                        pltpu.VMEM((n_pair, 1, wide), F32),
                        pltpu.VMEM((n_pair, 1, wide), F32),
                        pltpu.VMEM((n_pair, LANES, wide), F32)],
        compiler_params=_cparams(("arbitrary", "arbitrary")),
        name="moba_prompt",
    )(tab, p, p, p)


def _moba_sample_kernel(pt_ref, *refs, n_blk, nbs, page, t_len, thr):
    npg = 2 * nbs
    k_refs = refs[:npg]
    v_refs = refs[npg:2 * npg]
    q_ref, qbd_ref, kn_ref, vn_ref, tab_ref, o_ref, g_scr, m_scr, l_scr, o_scr, bl_scr = refs[2 * npg:]
    n = pl.program_id(1)
    nh, hd = MOBA_HEADS, MOBA_HEAD_DIM
    nrow = nh * t_len
    scale = hd ** -0.5
    nt_dims = (((1,), (1,)), ((), ()))
    lane = lax.broadcasted_iota(I32, (nrow, LANES), 1)
    row = lax.broadcasted_iota(I32, (nrow, LANES), 0)
    tab = tab_ref[...]

    def tab_at(k):
        return tab[:, k:k + 1]

    @pl.when(n == 0)
    def _():
        g_scr[...] = jnp.zeros(g_scr.shape, F32)
        m_scr[...] = jnp.zeros(m_scr.shape, F32)
        l_scr[...] = jnp.zeros(l_scr.shape, F32)
        for half in range(2):
            bl_scr[:, half * page:(half + 1) * page] = _bias_chain(
                MOBA_BLOCK + row % t_len - (half * page + lane), tab_at, thr)

    qbd = (qbd_ref[0] * scale).astype(BF16)
    far = tab_at(NUM_BUCKETS - 1)
    bw = 2 * page
    kt = jnp.concatenate([r[...].reshape(nh * hd, page).astype(BF16) for r in k_refs], axis=1)
    vt = jnp.concatenate([r[...].reshape(nh * hd, page).astype(BF16) for r in v_refs], axis=1)
    s_all = jnp.dot(qbd, kt, preferred_element_type=F32)
    p_rows = []
    stats = []
    for j in range(nbs):
        blk_idx = n * nbs + j
        s_raw = s_all[:, j * bw:(j + 1) * bw]
        gate = jnp.sum(s_raw, axis=1, keepdims=True) * (1.0 / (MOBA_BLOCK * scale))
        s = s_raw + jnp.where(blk_idx == n_blk - 1, bl_scr[...], far)
        m = jnp.max(s, axis=1, keepdims=True)
        p = jnp.exp(s - m)
        l = jnp.sum(p, axis=1, keepdims=True)
        stats.append((blk_idx, gate, m, l))
        zl = jnp.zeros((nrow, j * bw), F32)
        zr = jnp.zeros((nrow, (nbs - 1 - j) * bw), F32)
        p_rows.append(jnp.concatenate([x for x in (zl, p, zr) if x.shape[1]], axis=1))
    p_bd = jnp.concatenate(p_rows, axis=0).astype(BF16)
    pv = lax.dot_general(p_bd, vt, nt_dims, preferred_element_type=F32)
    for j, (blk_idx, gate, m, l) in enumerate(stats):
        at_blk = lane == blk_idx
        g_scr[...] = jnp.where(at_blk, gate, g_scr[...])
        m_scr[...] = jnp.where(at_blk, m, m_scr[...])
        l_scr[...] = jnp.where(at_blk, l, l_scr[...])
        o_scr[blk_idx] = jnp.concatenate(
            [pv[j * nrow + h * t_len:j * nrow + (h + 1) * t_len, h * hd:(h + 1) * hd] for h in range(nh)], axis=0)

    @pl.when(n == n_blk // nbs - 1)
    def _():
        gate = g_scr[...]
        rank = jnp.zeros((nrow, LANES), F32)
        for mm in range(n_blk):
            gm = gate[:, mm:mm + 1]
            tie = jnp.where(lane > mm, 1.0, 0.0)
            rank = rank + jnp.where(gm > gate, 1.0, jnp.where(gm == gate, tie, 0.0))
        sel = (rank < MOBA_TOPK) & (lane < n_blk)

        q = q_ref[0]
        kn = kn_ref[0]
        vn = vn_ref[0]
        tq = lax.broadcasted_iota(I32, (nrow, t_len), 0) % t_len
        sk = lax.broadcasted_iota(I32, (nrow, t_len), 1)
        s_own = jnp.concatenate(
            [lax.dot_general((q[h] * scale).astype(BF16), kn[h].astype(BF16), nt_dims, preferred_element_type=F32)
             for h in range(nh)], axis=0) + _bias_chain(tq - sk, tab_at, thr)
        s_own = jnp.where(sk <= tq, s_own, NEG_INF)
        m_o = jnp.max(s_own, axis=1, keepdims=True)
        p_o = jnp.exp(s_own - m_o)
        l_o = jnp.sum(p_o, axis=1, keepdims=True)
        o_o = jnp.concatenate(
            [jnp.dot(p_o[h * t_len:(h + 1) * t_len].astype(BF16), vn[h].astype(BF16), preferred_element_type=F32)
             for h in range(nh)], axis=0)

        m_sel = jnp.where(sel, m_scr[...], NEG_INF)
        m_fin = jnp.maximum(jnp.max(m_sel, axis=1, keepdims=True), m_o)
        w = jnp.where(sel, jnp.exp(m_sel - m_fin), 0.0)
        w_o = jnp.exp(m_o - m_fin)
        l_fin = jnp.sum(w * l_scr[...], axis=1, keepdims=True) + w_o * l_o
        acc = w_o * o_o
        for nn in range(n_blk):
            acc = acc + w[:, nn:nn + 1] * o_scr[nn]
        o_ref[0] = acc / l_fin


MOBA_SAMPLE_BLOCKS_PER_STEP = 8


def moba_sample(page_table, cache_kt, cache_vt, q, kn, vn, tab_heads, *, layer_e, t_len):
    n_b, n_pages = page_table.shape
    page = cache_kt.shape[-1]
    nbs = MOBA_SAMPLE_BLOCKS_PER_STEP
    ppb = MOBA_BLOCK // page
    assert ppb == 2 and page == LANES and n_pages % (ppb * nbs) == 0
    n_blk = n_pages // ppb
    assert n_blk <= LANES
    nh, hd = MOBA_HEADS, MOBA_HEAD_DIM
    nrow = nh * t_len
    tab = jnp.pad(jnp.repeat(tab_heads, t_len, axis=0), ((0, 0), (0, LANES - NUM_BUCKETS)))
    qbd = (q[:, :, :, None, :] * jnp.eye(nh, dtype=F32)[None, :, None, :, None]).reshape(n_b, nrow, nh * hd)

    def pspec(p):
        return pl.BlockSpec((None, None, nh, hd, page),
                            lambda b, n, pt: (pt[b, n * ppb * nbs + p], layer_e, 0, 0, 0))

    def bspec():
        return pl.BlockSpec((1, nh, t_len, hd), lambda b, n, pt: (b, 0, 0, 0))

    pages = [pspec(p) for p in range(ppb * nbs)]
    grid_spec = pltpu.PrefetchScalarGridSpec(
        num_scalar_prefetch=1,
        grid=(n_b, n_blk // nbs),
        in_specs=pages + pages + [
            bspec(),
            pl.BlockSpec((1, nrow, nh * hd), lambda b, n, pt: (b, 0, 0)),
            bspec(), bspec(),
            pl.BlockSpec(tab.shape, lambda b, n, pt: (0, 0))],
        out_specs=pl.BlockSpec((1, nrow, hd), lambda b, n, pt: (b, 0, 0)),
        scratch_shapes=[pltpu.VMEM((nrow, LANES), F32), pltpu.VMEM((nrow, LANES), F32),
                        pltpu.VMEM((nrow, LANES), F32), pltpu.VMEM((n_blk, nrow, hd), F32),
                        pltpu.VMEM((nrow, MOBA_BLOCK), F32)],
    )
    return pl.pallas_call(
        functools.partial(_moba_sample_kernel, n_blk=n_blk, nbs=nbs, page=page, t_len=t_len,
                          thr=_bucket_thresholds()),
        out_shape=jax.ShapeDtypeStruct((n_b, nrow, hd), F32),
        grid_spec=grid_spec,
        compiler_params=_cparams(("parallel", "arbitrary")),
        name="moba_sample",
    )(page_table, *([cache_kt] * (ppb * nbs)), *([cache_vt] * (ppb * nbs)), q, qbd, kn, vn, tab)


def _retention_kernel(q_ref, k_ref, v_ref, g_ref, cos_ref, sin_ref, dmask_ref, xi_ref, zeta_ref, gc_ref,
                      gng_ref, gnb_ref, *rest, nb, chunk, has_s0):
    if has_s0:
        s0_ref, o_ref, s_out_ref, s_scr = rest
    else:
        o_ref, s_out_ref, s_scr = rest
    c = pl.program_id(1)
    dk, dv = RET_DK, RET_DV
    half = dk // 2

    @pl.when(c == 0)
    def _():
        if has_s0:
            s_scr[...] = s0_ref[...]
        else:
            s_scr[...] = jnp.zeros(s_scr.shape, F32)

    cos = cos_ref[...]
    sin = sin_ref[...]

    def rot(x):
        x1 = x[:, :half]
        x2 = x[:, half:]
        return jnp.concatenate([x1 * cos - x2 * sin, x2 * cos + x1 * sin], axis=1)

    out_rows = []
    for bb in range(nb):
        rows = pl.ds(bb * chunk, chunk)
        out_heads = []
        for h in range(RET_HEADS):
            qh = rot(q_ref[rows, h * dk:(h + 1) * dk])
            kh = rot(k_ref[rows, h * dk:(h + 1) * dk]) * (dk ** -0.5)
            vh = v_ref[rows, h * dv:(h + 1) * dv]
            s = s_scr[bb, h]
            att = _bdot_nt(qh, kh) * dmask_ref[h]
            o = _bdot(att, vh) + _bdot(qh, s) * xi_ref[h]
            s_scr[bb, h] = s * gc_ref[h, 0:1, 0:1] + _bdot_tn(kh * zeta_ref[h], vh)
            mu = jnp.mean(o, axis=-1, keepdims=True)
            oc = o - mu
            var = jnp.mean(oc * oc, axis=-1, keepdims=True)
            on = oc * lax.rsqrt(var + LN_EPS) * gng_ref[:, h * dv:(h + 1) * dv] + gnb_ref[:, h * dv:(h + 1) * dv]
            gate = _silu(g_ref[rows, h * dv:(h + 1) * dv])
            out_heads.append(gate * on)
        out_rows.append(jnp.concatenate(out_heads, axis=1))
    o_ref[...] = jnp.concatenate(out_rows, axis=0).astype(o_ref.dtype)

    @pl.when(c == pl.num_programs(1) - 1)
    def _():
        s_out_ref[...] = s_scr[...]


def retention(p, cos, sin, consts, gn_g, gn_b, s0, *, row0, n_batch, t_len, nb, out_dtype):
    chunk = math.gcd(t_len, RET_CHUNK)
    n_chunk = t_len // chunk
    dmask, xi, zeta, gc = consts
    blk_rows = nb * chunk
    if nb > 1:
        assert n_chunk == 1
    assert row0 % blk_rows == 0
    base = row0 // blk_rows
    qk_w = RET_HEADS * RET_DK
    v_w = RET_HEADS * RET_DV
    has_s0 = s0 is not None

    def rmap(col):
        return lambda b, c: (base + b * n_chunk + c, col)

    in_specs = [pl.BlockSpec((blk_rows, qk_w), rmap(0)),
                pl.BlockSpec((blk_rows, qk_w), rmap(1)),
                pl.BlockSpec((blk_rows, v_w), rmap(1)),
                pl.BlockSpec((blk_rows, v_w), rmap(2)),
                pl.BlockSpec((chunk, RET_DK // 2), lambda b, c: (c, 0)),
                pl.BlockSpec((chunk, RET_DK // 2), lambda b, c: (c, 0)),
                pl.BlockSpec(dmask.shape, lambda b, c: (0, 0, 0)),
                pl.BlockSpec(xi.shape, lambda b, c: (0, 0, 0)),
                pl.BlockSpec(zeta.shape, lambda b, c: (0, 0, 0)),
                pl.BlockSpec(gc.shape, lambda b, c: (0, 0, 0)),
                pl.BlockSpec((1, v_w), lambda b, c: (0, 0)),
                pl.BlockSpec((1, v_w), lambda b, c: (0, 0))]
    args = [p, p, p, p, cos, sin, dmask, xi, zeta, gc, gn_g.reshape(1, -1), gn_b.reshape(1, -1)]
    s_spec = pl.BlockSpec((nb, RET_HEADS, RET_DK, RET_DV), lambda b, c: (b, 0, 0, 0))
    if has_s0:
        in_specs.append(s_spec)
        args.append(s0)
    o, s_out = pl.pallas_call(
        functools.partial(_retention_kernel, nb=nb, chunk=chunk, has_s0=has_s0),
        out_shape=(jax.ShapeDtypeStruct((n_batch * t_len, v_w), out_dtype),
                   jax.ShapeDtypeStruct((n_batch, RET_HEADS, RET_DK, RET_DV), F32)),
        grid=(n_batch // nb, n_chunk),
        in_specs=in_specs,
        out_specs=(pl.BlockSpec((blk_rows, v_w), lambda b, c: (b * n_chunk + c, 0)), s_spec),
        scratch_shapes=[pltpu.VMEM((nb, RET_HEADS, RET_DK, RET_DV), F32)],
        compiler_params=_cparams(("parallel", "arbitrary")),
        name="retention",
    )(*args)
    return o, s_out


def _retention_consts(chunk):
    h = jnp.arange(RET_HEADS, dtype=F32)
    lg = jnp.log1p(-jnp.exp2(-5.0 - h))
    idx = jnp.arange(chunk, dtype=F32)
    diff = idx[:, None] - idx[None, :]
    dmask = jnp.where(diff >= 0, jnp.exp(lg[:, None, None] * jnp.maximum(diff, 0.0)), 0.0)
    xi = jnp.exp(lg[:, None] * (idx[None, :] + 1.0))
    zeta = jnp.exp(lg[:, None] * (chunk - 1.0 - idx[None, :]))
    g_c = jnp.exp(lg * chunk)
    return (dmask,
            jnp.broadcast_to(xi[:, :, None], (RET_HEADS, chunk, RET_DV)),
            jnp.broadcast_to(zeta[:, :, None], (RET_HEADS, chunk, RET_DK)),
            jnp.broadcast_to(g_c[:, None, None], (RET_HEADS, 8, LANES)))


def _rotary_tables(pos):
    half = RET_DK // 2
    inv = 10000.0 ** (-jnp.linspace(0.0, 1.0, half, dtype=F32))
    ang = pos.astype(F32)[:, None] * inv[None, :]
    return jnp.cos(ang), jnp.sin(ang)


def _mem_attn_kernel(q_ref, mk_ref, mv_ref, o_ref, *, nb, tq):
    hd = MEM_HEAD_DIM
    scale = hd ** -0.5
    qf = q_ref[...].astype(F32)
    out_rows = []
    for bb in range(nb):
        out_heads = []
        for h in range(MEM_HEADS):
            cols = slice(h * hd, (h + 1) * hd)
            q = qf[bb * tq:(bb + 1) * tq, cols]
            s = _bdot_nt(q, mk_ref[bb, :, cols]) * scale
            m = jnp.max(s, axis=-1, keepdims=True)
            p = jnp.exp(s - m)
            p = p / jnp.sum(p, axis=-1, keepdims=True)
            out_heads.append(_bdot(p, mv_ref[bb, :, cols]))
        out_rows.append(jnp.concatenate(out_heads, axis=1))
    o_ref[...] = jnp.concatenate(out_rows, axis=0).astype(o_ref.dtype)


def mem_attention(q, mk, mv, *, layer, row0, n_batch, t_len, nb, tq, out_dtype):
    n_t = t_len // tq
    blk_rows = nb * tq
    if nb > 1:
        assert n_t == 1
    assert row0 % blk_rows == 0
    base = row0 // blk_rows
    m_tok = mk.shape[2]
    return pl.pallas_call(
        functools.partial(_mem_attn_kernel, nb=nb, tq=tq),
        out_shape=jax.ShapeDtypeStruct((n_batch * t_len, D_MODEL), out_dtype),
        grid=(n_batch // nb, n_t),
        in_specs=[pl.BlockSpec((blk_rows, D_MODEL), lambda b, t: (base + b * n_t + t, 0)),
                  pl.BlockSpec((None, nb, m_tok, D_MODEL), lambda b, t: (layer, b, 0, 0)),
                  pl.BlockSpec((None, nb, m_tok, D_MODEL), lambda b, t: (layer, b, 0, 0))],
        out_specs=pl.BlockSpec((blk_rows, D_MODEL), lambda b, t: (b * n_t + t, 0)),
        compiler_params=_cparams(("parallel", "arbitrary")),
        name="mem_attention",
    )(q, mk, mv)


def _mem_attn_cache_kernel(qt_ref, mk_ref, mv_ref, o_ref, *, nb, tq):
    nh, hd = MEM_HEADS, MEM_HEAD_DIM
    ncol = nh * tq
    m_tok = mk_ref.shape[1]
    scale = hd ** -0.5
    r = lax.broadcasted_iota(I32, (m_tok * nh, ncol), 0)
    c = lax.broadcasted_iota(I32, (m_tok * nh, ncol), 1)
    match = (r % nh) == (c // tq)
    for bb in range(nb):
        k2 = mk_ref[bb].reshape(m_tok * nh, hd)
        v2 = mv_ref[bb].reshape(m_tok * nh, hd)
        s = jnp.where(match, _bdot(k2, qt_ref[bb]) * scale, NEG_INF)
        m = jnp.max(s, axis=0, keepdims=True)
        p = jnp.exp(s - m)
        p = p / jnp.sum(p, axis=0, keepdims=True)
        o_ref[bb] = _bdot_tn(v2, p)


def mem_attention_cache(qt, mk5, mv5, *, layer, nb, tq):
    n_b = qt.shape[0]
    m_tok, nh, hd = mk5.shape[2:]
    ncol = nh * tq
    return pl.pallas_call(
        functools.partial(_mem_attn_cache_kernel, nb=nb, tq=tq),
        out_shape=jax.ShapeDtypeStruct((n_b, hd, ncol), F32),
        grid=(n_b // nb,),
        in_specs=[pl.BlockSpec((nb, hd, ncol), lambda b: (b, 0, 0)),
                  pl.BlockSpec((None, nb, m_tok, nh, hd), lambda b: (layer, b, 0, 0, 0)),
                  pl.BlockSpec((None, nb, m_tok, nh, hd), lambda b: (layer, b, 0, 0, 0))],
        out_specs=pl.BlockSpec((nb, hd, ncol), lambda b: (b, 0, 0)),
        compiler_params=_cparams(("parallel",)),
        name="mem_attention_cache",
    )(qt, mk5, mv5)


def _router_kernel(x_ref, w_ref, b_ref, tri_ref, wt_ref, ei_ref, cnt_ref, base_scr):
    @pl.when(pl.program_id(0) == 0)
    def _():
        base_scr[...] = jnp.zeros(base_scr.shape, F32)

    x = x_ref[...]
    logits = _dot_hi(x, w_ref[...]) + b_ref[...]
    tm = x.shape[0]
    lane = lax.broadcasted_iota(I32, (tm, LANES), 1)
    big = jnp.int32(LANES)

    def masked_softmax(mask):
        lm = jnp.where(mask, logits, NEG_INF)
        mx = jnp.max(lm, axis=-1, keepdims=True)
        e = jnp.exp(lm - mx)
        return e / jnp.sum(e, axis=-1, keepdims=True)

    g_prob = masked_softmax(lane < MOE_GROUPS)
    gp = jnp.max(g_prob, axis=-1, keepdims=True)
    gi = jnp.min(jnp.where(g_prob == gp, lane, big), axis=-1, keepdims=True)
    lo = MOE_GROUPS + gi * MOE_EPG
    in_group = (lane >= lo) & (lane < lo + MOE_EPG)
    e_prob = jnp.where(in_group, masked_softmax(in_group), -1.0)
    p1 = jnp.max(e_prob, axis=-1, keepdims=True)
    i1 = jnp.min(jnp.where(e_prob == p1, lane, big), axis=-1, keepdims=True)
    rest = jnp.where(lane == i1, -1.0, e_prob)
    p2 = jnp.max(rest, axis=-1, keepdims=True)
    i2 = jnp.min(jnp.where(rest == p2, lane, big), axis=-1, keepdims=True)
    tot = p1 + p2
    w1 = gp * (p1 / tot)
    w2 = gp * (p2 / tot)
    e1 = i1 - MOE_GROUPS
    e2 = i2 - MOE_GROUPS
    oh1 = jnp.where(lane == e1, 1.0, 0.0)
    oh2 = jnp.where(lane == e2, 1.0, 0.0)
    both = oh1 + oh2
    base = base_scr[0:1, :]
    before = jnp.dot(tri_ref[...], both.astype(BF16), preferred_element_type=F32) + base
    r1 = jnp.sum(oh1 * before, axis=-1, keepdims=True).astype(I32)
    r2 = jnp.sum(oh2 * before, axis=-1, keepdims=True).astype(I32)
    base_new = base + jnp.sum(both, axis=0, keepdims=True)
    base_scr[0:1, :] = base_new
    wt_ref[...] = jnp.where(lane == 0, w1, jnp.where(lane == 1, w2, 0.0))
    ei_ref[...] = jnp.where(lane == 0, e1, jnp.where(lane == 1, e2, jnp.where(lane == 2, r1,
                                                                              jnp.where(lane == 3, r2, 0))))
    cnt_ref[...] = jnp.broadcast_to(base_new, cnt_ref.shape)


def moe_router(x, w_all, b_all, *, tm):
    m, d = x.shape
    tri = jnp.asarray(np.tril(np.ones((tm, tm), np.float32), -1), BF16)
    return pl.pallas_call(
        _router_kernel,
        out_shape=(jax.ShapeDtypeStruct((m, LANES), F32), jax.ShapeDtypeStruct((m, LANES), I32),
                   jax.ShapeDtypeStruct((8, LANES), F32)),
        grid=(m // tm,),
        in_specs=[pl.BlockSpec((tm, d), lambda i: (i, 0)),
                  pl.BlockSpec((d, LANES), lambda i: (0, 0)),
                  pl.BlockSpec((1, LANES), lambda i: (0, 0)),
                  pl.BlockSpec((tm, tm), lambda i: (0, 0))],
        out_specs=(pl.BlockSpec((tm, LANES), lambda i: (i, 0)), pl.BlockSpec((tm, LANES), lambda i: (i, 0)),
                   pl.BlockSpec((8, LANES), lambda i: (0, 0))),
        scratch_shapes=[pltpu.VMEM((8, LANES), F32)],
        compiler_params=_cparams(("arbitrary",)),
        name="moe_router",
    )(x, w_all, b_all, tri)


def _moe_dispatch_kernel(dest_ref, x_ref, xs_in, xs_out, sem, *, tm):
    del xs_in

    def body(g, carry):
        for u in range(DMA_UNROLL):
            r = g * DMA_UNROLL + u
            _tile_copy(x_ref, r, xs_out, dest_ref[r], sem.at[0]).start(priority=0)
            _tile_copy(x_ref, r, xs_out, dest_ref[tm + r], sem.at[0]).start(priority=1)
        return carry

    lax.fori_loop(0, tm // DMA_UNROLL, body, 0)
    for _ in range(MOE_TOPK):
        _wait_tiles(x_ref, xs_out, 0, sem.at[0], tm)


def moe_dispatch(x8, dest, n_slots, *, tm):
    n_t = dest.shape[0]
    xs0 = jnp.zeros((n_slots * ROW_TILE, LANES), U32)
    return pl.pallas_call(
        functools.partial(_moe_dispatch_kernel, tm=tm),
        out_shape=jax.ShapeDtypeStruct(xs0.shape, U32),
        grid=(n_t,),
        in_specs=[pl.BlockSpec((None, None, 2 * tm), lambda i: (i, 0, 0), memory_space=pltpu.SMEM),
                  pl.BlockSpec((tm * ROW_TILE, LANES), lambda i: (i, 0)),
                  pl.BlockSpec(memory_space=pl.ANY)],
        out_specs=pl.BlockSpec(memory_space=pl.ANY),
        scratch_shapes=[pltpu.SemaphoreType.DMA((1,))],
        input_output_aliases={2: 0},
        compiler_params=_cparams(("arbitrary",)),
        name="moe_dispatch",
    )(dest.reshape(n_t, 1, 2 * tm), x8, xs0)


def _moe_ffn_kernel(te_ref, nt_ref, first_ref, slot_ref, nxt_ref, xs_ref, w1_hbm, w3_hbm, w2_hbm, ys_ref,
                    wb1, wb3, wb2, sem, *, layer):
    j = pl.program_id(0)

    def weight_copies(e, s):
        return (pltpu.make_async_copy(w1_hbm.at[layer, e], wb1.at[s], sem.at[s]),
                pltpu.make_async_copy(w3_hbm.at[layer, e], wb3.at[s], sem.at[s]),
                pltpu.make_async_copy(w2_hbm.at[layer, e], wb2.at[s], sem.at[s]))

    @pl.when(j == 0)
    def _():
        for c in weight_copies(te_ref[0], slot_ref[0]):
            c.start()

    @pl.when(j < nt_ref[0])
    def _():
        s = slot_ref[j]

        @pl.when(first_ref[j] == 1)
        def _():
            for c in weight_copies(te_ref[j], s):
                c.wait()

            @pl.when(nxt_ref[j] >= 0)
            def _():
                for c in weight_copies(nxt_ref[j], 1 - s):
                    c.start()

        x = _load_token_tiles(xs_ref, 0, MOE_TILE).astype(BF16)
        h = _silu(_bdot(x, wb1[s])) * _bdot(x, wb3[s])
        _store_token_tiles(ys_ref, _bdot(h, wb2[s]))

    @pl.when(j >= nt_ref[0])
    def _():
        ys_ref[...] = jnp.zeros(ys_ref.shape, U32)


def moe_ffn(xs8, plan, w1, w3, w2, *, layer):
    tile_expert, n_tiles, first, slot, nxt = plan
    n_t = tile_expert.shape[0]
    d, dff = w1.shape[2], w1.shape[3]
    blk = MOE_TILE * ROW_TILE
    grid_spec = pltpu.PrefetchScalarGridSpec(
        num_scalar_prefetch=5,
        grid=(n_t,),
        in_specs=[pl.BlockSpec((blk, LANES), lambda j, te, nt, *_: (jnp.minimum(j, nt[0] - 1), 0)),
                  pl.BlockSpec(memory_space=pl.ANY),
                  pl.BlockSpec(memory_space=pl.ANY),
                  pl.BlockSpec(memory_space=pl.ANY)],
        out_specs=pl.BlockSpec((blk, LANES), lambda j, te, nt, *_: (j, 0)),
        scratch_shapes=[pltpu.VMEM((2, d, dff), F32), pltpu.VMEM((2, d, dff), F32), pltpu.VMEM((2, dff, d), F32),
                        pltpu.SemaphoreType.DMA((2,))],
    )
    return pl.pallas_call(
        functools.partial(_moe_ffn_kernel, layer=layer),
        out_shape=jax.ShapeDtypeStruct(xs8.shape, U32),
        grid_spec=grid_spec,
        compiler_params=_cparams(("arbitrary",)),
        name="moe_ffn",
    )(tile_expert, n_tiles, first, slot, nxt, xs8, w1, w3, w2)


def _moe_combine_kernel(d_cur, d_nxt, ys_hbm, x_ref, wt_ref, g_ref, b_ref, *rest, tm, lead_tiles):
    if lead_tiles is None:
        o_ref, ybuf, sem = rest
    else:
        o_ref, o_tail_ref, ybuf, sem = rest
    i = pl.program_id(0)
    n = pl.num_programs(0)
    slot = i % 2

    def issue(d_ref, s):
        def body(g, carry):
            for u in range(DMA_UNROLL):
                r = g * DMA_UNROLL + u
                _tile_copy(ys_hbm, d_ref[r], ybuf, s * 2 * tm + r, sem.at[s]).start(priority=u % 2)
            return carry
        lax.fori_loop(0, 2 * tm // DMA_UNROLL, body, 0)

    @pl.when(i == 0)
    def _():
        issue(d_cur, 0)

    @pl.when(i + 1 < n)
    def _():
        issue(d_nxt, 1 - slot)

    base_tok = slot * 2 * tm
    _wait_tiles(ys_hbm, ybuf, base_tok, sem.at[slot], 2 * tm)
    wt = wt_ref[...]
    y0 = _load_token_tiles(ybuf, base_tok * ROW_TILE, tm)
    y1 = _load_token_tiles(ybuf, (base_tok + tm) * ROW_TILE, tm)
    y = DEEPNORM_ALPHA * x_ref[...] + (wt[:, 0:1] * y0 + wt[:, 1:2] * y1)
    out = _layer_norm(y, g_ref[...], b_ref[...])
    if lead_tiles is None:
        o_ref[...] = out
    else:
        @pl.when(i < lead_tiles)
        def _():
            o_ref[...] = out

        @pl.when(i >= lead_tiles)
        def _():
            o_tail_ref[...] = out


def moe_combine_deepnorm(ys8, dest, x, wts, g, b, *, tm, lead_rows=None):
    m, d = x.shape
    n_t = m // tm
    dest3 = dest.reshape(n_t, 1, 2 * tm)
    if lead_rows is None:
        lead_tiles = None
        out_shape = jax.ShapeDtypeStruct((m, d), F32)
        out_specs = pl.BlockSpec((tm, d), lambda i: (i, 0))
    else:
        assert lead_rows % tm == 0 and m - lead_rows == tm
        lead_tiles = lead_rows // tm
        out_shape = (jax.ShapeDtypeStruct((lead_rows, d), F32), jax.ShapeDtypeStruct((tm, d), F32))
        out_specs = (pl.BlockSpec((tm, d), lambda i: (jnp.minimum(i, lead_tiles - 1), 0)),
                     pl.BlockSpec((tm, d), lambda i: (0, 0)))
    return pl.pallas_call(
        functools.partial(_moe_combine_kernel, tm=tm, lead_tiles=lead_tiles),
        out_shape=out_shape,
        grid=(n_t,),
        in_specs=[pl.BlockSpec((None, None, 2 * tm), lambda i: (i, 0, 0), memory_space=pltpu.SMEM),
                  pl.BlockSpec((None, None, 2 * tm), lambda i: (jnp.minimum(i + 1, n_t - 1), 0, 0),
                               memory_space=pltpu.SMEM),
                  pl.BlockSpec(memory_space=pl.ANY),
                  pl.BlockSpec((tm, d), lambda i: (i, 0)),
                  pl.BlockSpec((tm, LANES), lambda i: (i, 0)),
                  pl.BlockSpec((1, d), lambda i: (0, 0)),
                  pl.BlockSpec((1, d), lambda i: (0, 0))],
        out_specs=out_specs,
        scratch_shapes=[pltpu.VMEM((2 * 2 * tm * ROW_TILE, LANES), U32), pltpu.SemaphoreType.DMA((2,))],
        compiler_params=_cparams(("arbitrary",)),
        name="moe_combine_deepnorm",
    )(dest3, dest3, ys8, x, wts, g.reshape(1, d), b.reshape(1, d))


def _moe_plan(eidx, rank, counts, n_tok, tm):
    n_tiles_max = n_tok * MOE_TOPK // MOE_TILE + MOE_EXPERTS
    tiles_per = (counts + MOE_TILE - 1) // MOE_TILE
    tile_end = jnp.cumsum(tiles_per)
    pad_off = (tile_end - tiles_per) * MOE_TILE
    onehot = eidx[:, :, None] == jnp.arange(MOE_EXPERTS, dtype=I32)[None, None, :]
    dest = jnp.sum(jnp.where(onehot, pad_off[None, None, :], 0), axis=-1) + rank
    tile_expert = jnp.minimum(
        jnp.sum((jnp.arange(n_tiles_max, dtype=I32)[:, None] >= tile_end[None, :]).astype(I32), axis=1),
        MOE_EXPERTS - 1)
    dest = dest.reshape(n_tok // tm, tm, MOE_TOPK).transpose(0, 2, 1).reshape(n_tok // tm, MOE_TOPK * tm)
    experts = jnp.arange(MOE_EXPERTS, dtype=I32)
    nonempty = tiles_per > 0
    slot_e = (jnp.cumsum(nonempty.astype(I32)) - 1) % 2
    later = nonempty[None, :] & (experts[None, :] > experts[:, None])
    nxt_e = jnp.min(jnp.where(later, experts[None, :], MOE_EXPERTS), axis=1)
    nxt_e = jnp.where(nxt_e == MOE_EXPERTS, -1, nxt_e)
    tiles = jnp.arange(n_tiles_max, dtype=I32)
    tile_oh = tile_expert[:, None] == experts[None, :]
    first = jnp.any((tiles[:, None] == (tile_end - tiles_per)[None, :]) & nonempty[None, :], axis=1).astype(I32)
    slot = jnp.sum(jnp.where(tile_oh, slot_e[None, :], 0), axis=1).astype(I32)
    nxt = jnp.sum(jnp.where(tile_oh, nxt_e[None, :], 0), axis=1).astype(I32)
    ffn_plan = (tile_expert, tile_end[-1:].astype(I32), first, slot, nxt)
    return ffn_plan, dest, n_tiles_max * MOE_TILE


def hier_moe_deepnorm(x, x8, w_all, b_all, w1, w3, w2, g, b, *, layer, lead_rows=None):
    n_tok = x.shape[0]
    wts, ei, cnt = moe_router(x, w_all, b_all, tm=640)
    counts = cnt[0, :MOE_EXPERTS].astype(I32)
    ffn_plan, dest, n_slots = _moe_plan(ei[:, 0:2], ei[:, 2:4], counts, n_tok, TOKEN_TILE)
    xs8 = moe_dispatch(x8, dest, n_slots, tm=TOKEN_TILE)
    ys8 = moe_ffn(xs8, ffn_plan, w1, w3, w2, layer=layer)
    return moe_combine_deepnorm(ys8, dest, x, wts, g, b, tm=TOKEN_TILE, lead_rows=lead_rows)


def kernel(x_prompt, x_sample, cache_moba_k, cache_moba_v, state_conv, state_ret, cache_mem_k, cache_mem_v,
           page_table, mem_prompt, rel_bias, ev_w_in, ev_conv_w, ev_conv_b, ev_conv_ln_g, ev_conv_ln_b,
           ev_w_out, od_w_in, od_gn_g, od_gn_b, od_w_out, mem_wq, mem_wk, mem_wv, mem_wo, ln_g, ln_b,
           moe_w_group, moe_b_group, moe_w_router, moe_b_router, moe_w1, moe_w3, moe_w2):
    bp, tp, d = x_prompt.shape
    bs, ts, _ = x_sample.shape
    n_p = bp * tp
    n_s = bs * ts
    n_all = n_p + n_s
    page = cache_moba_k.shape[1]
    past_len = page_table.shape[1] * page
    nh, hd = MOBA_HEADS, MOBA_HEAD_DIM
    assert tp % MOBA_BLOCK == 0 and past_len % MOBA_BLOCK == 0 and ts <= MOBA_BLOCK
    assert MOBA_BLOCK >= MAX_DISTANCE
    tm = 640
    assert n_all % tm == 0

    x = jnp.concatenate([x_prompt.reshape(n_p, d), x_sample.reshape(n_s, d)], axis=0)

    tab_heads = rel_bias.T
    cache_kt = cache_moba_k.transpose(0, 2, 3, 4, 1)
    cache_vt = cache_moba_v.transpose(0, 2, 3, 4, 1)

    pos_p = jnp.arange(tp, dtype=I32)
    pos_s = past_len + jnp.arange(ts, dtype=I32)
    cos_p, sin_p = _rotary_tables(pos_p)
    cos_s, sin_s = _rotary_tables(pos_s)
    ret_c_p = _retention_consts(math.gcd(tp, RET_CHUNK))
    ret_c_s = _retention_consts(math.gcd(ts, RET_CHUNK))

    outs = {}
    for layer in range(DEPTH):
        if layer % 2 == 0:
            e = layer // 2
            proj = matmul(x, ev_w_in[e].astype(BF16), tm=tm, tn=ev_w_in.shape[2])
            k_new = proj[:, 2 * CONV_CH + MOBA_WIDTH:2 * CONV_CH + 2 * MOBA_WIDTH]
            v_new = proj[:, 2 * CONV_CH + 2 * MOBA_WIDTH:]
            q_s = proj[n_p:, 2 * CONV_CH:2 * CONV_CH + MOBA_WIDTH]
            conv_args = (ev_conv_w[e], ev_conv_b[e], ev_conv_ln_g[e], ev_conv_ln_b[e])
            c_p, cst_p = conv_module(proj, jnp.zeros((bp, CONV_WIDTH - 1, CONV_CH), F32), *conv_args,
                                     row0=0, n_batch=bp, t_len=tp, nb=1, out_dtype=BF16)
            c_s, cst_s = conv_module(proj, state_conv[e], *conv_args,
                                     row0=n_p, n_batch=bs, t_len=ts, nb=bs, out_dtype=F32)
            a_p = moba_prompt(proj, tab_heads, n_batch=bp, t_len=tp, out_dtype=BF16)

            def by_head(a):
                return a.reshape(bs, ts, nh, hd).transpose(0, 2, 1, 3)

            a_s = moba_sample(page_table, cache_kt, cache_vt, by_head(q_s), by_head(k_new[n_p:]),
                              by_head(v_new[n_p:]), tab_heads, layer_e=e, t_len=ts)
            a_s = a_s.reshape(bs, nh, ts, hd).transpose(0, 2, 1, 3).reshape(n_s, MOBA_WIDTH).astype(BF16)
            w_out = ev_w_out[e].astype(BF16)
            x = matmul_deepnorm([(c_p, c_s.astype(BF16)), (a_p, a_s)], [w_out[:CONV_CH], w_out[CONV_CH:]], x,
                                ln_g[layer, 0], ln_b[layer, 0], tm=tm)
            outs.setdefault("kp", []).append(k_new[:n_p].reshape(bp, tp, nh, hd))
            outs.setdefault("vp", []).append(v_new[:n_p].reshape(bp, tp, nh, hd))
            outs.setdefault("ks", []).append(k_new[n_p:].reshape(bs, ts, nh, hd))
            outs.setdefault("vs", []).append(v_new[n_p:].reshape(bs, ts, nh, hd))
            outs.setdefault("cp", []).append(cst_p)
            outs.setdefault("cs", []).append(cst_s)
        else:
            o = layer // 2
            proj = matmul(x, od_w_in[o].astype(BF16), tm=2 * tm, tn=2048)
            r_p, st_p = retention(proj, cos_p, sin_p, ret_c_p, od_gn_g[o], od_gn_b[o], None,
                                  row0=0, n_batch=bp, t_len=tp, nb=1, out_dtype=BF16)
            r_s, st_s = retention(proj, cos_s, sin_s, ret_c_s, od_gn_g[o], od_gn_b[o], state_ret[o],
                                  row0=n_p, n_batch=bs, t_len=ts, nb=2, out_dtype=BF16)
            x = matmul_deepnorm([(r_p, r_s)], [od_w_out[o].astype(BF16)], x, ln_g[layer, 0], ln_b[layer, 0], tm=tm)
            outs.setdefault("sp", []).append(st_p)
            outs.setdefault("ss", []).append(st_s)

        m_tok = mem_prompt.shape[1]
        mem2 = mem_prompt.reshape(bp * m_tok, d)
        mk_p = matmul(mem2, mem_wk[layer].astype(BF16), tm=512, tn=d)
        mv_p = matmul(mem2, mem_wv[layer].astype(BF16), tm=512, tn=d)
        outs.setdefault("mk", []).append(mk_p.reshape(bp, m_tok, MEM_HEADS, MEM_HEAD_DIM))
        outs.setdefault("mv", []).append(mv_p.reshape(bp, m_tok, MEM_HEADS, MEM_HEAD_DIM))
        q = matmul(x, mem_wq[layer].astype(BF16), tm=tm, tn=d, out_dtype=BF16)
        o_p = mem_attention(q, mk_p.reshape(1, bp, m_tok, d), mv_p.reshape(1, bp, m_tok, d),
                            layer=0, row0=0, n_batch=bp, t_len=tp, nb=1, tq=512, out_dtype=BF16)
        qt_s = q[n_p:].astype(F32).reshape(bs, ts, MEM_HEADS, MEM_HEAD_DIM).transpose(0, 3, 2, 1)
        o_s = mem_attention_cache(qt_s.reshape(bs, MEM_HEAD_DIM, MEM_HEADS * ts), cache_mem_k, cache_mem_v,
                                  layer=layer, nb=2, tq=ts)
        o_s = o_s.reshape(bs, MEM_HEAD_DIM, MEM_HEADS, ts).transpose(0, 3, 2, 1).reshape(n_s, d).astype(BF16)
        x, x8 = matmul_deepnorm([(o_p, o_s)], [mem_wo[layer].astype(BF16)], x, ln_g[layer, 1], ln_b[layer, 1], tm=tm,
                                token_tiles=True)

        w_all = jnp.zeros((d, LANES), F32)
        w_all = w_all.at[:, :MOE_GROUPS].set(moe_w_group[layer])
        w_all = w_all.at[:, MOE_GROUPS:MOE_GROUPS + MOE_EXPERTS].set(moe_w_router[layer])
        b_all = jnp.zeros((1, LANES), F32)
        b_all = b_all.at[0, :MOE_GROUPS].set(moe_b_group[layer])
        b_all = b_all.at[0, MOE_GROUPS:MOE_GROUPS + MOE_EXPERTS].set(moe_b_router[layer])
        x = hier_moe_deepnorm(x, x8, w_all, b_all, moe_w1, moe_w3, moe_w2, ln_g[layer, 2], ln_b[layer, 2],
                              layer=layer, lead_rows=n_p if layer == DEPTH - 1 else None)

    y_prompt = x[0].reshape(bp, tp, d)
    y_sample = x[1].reshape(bs, ts, d)
    return (y_prompt, y_sample,
            jnp.stack(outs["kp"], axis=2), jnp.stack(outs["vp"], axis=2),
            jnp.stack(outs["ks"], axis=2), jnp.stack(outs["vs"], axis=2),
            jnp.stack(outs["cp"], axis=0), jnp.stack(outs["cs"], axis=0),
            jnp.stack(outs["sp"], axis=0), jnp.stack(outs["ss"], axis=0),
            jnp.stack(outs["mk"], axis=0), jnp.stack(outs["mv"], axis=0))
```

```python
import functools
import math

import numpy as np
import jax
import jax.numpy as jnp
from jax import lax
from jax.experimental import pallas as pl
from jax.experimental.pallas import tpu as pltpu

F32 = jnp.float32
BF16 = jnp.bfloat16
I32 = jnp.int32

D_MODEL = 1024
DEPTH = 2
CONV_CH = 512
CONV_WIDTH = 31
MOBA_HEADS = 8
MOBA_HEAD_DIM = 64
MOBA_WIDTH = 512
MOBA_BLOCK = 256
MOBA_TOPK = 3
NUM_BUCKETS = 32
MAX_DISTANCE = 128
RET_HEADS = 4
RET_DK = 256
RET_DV = 512
RET_CHUNK = 128
MEM_HEADS = 4
MEM_HEAD_DIM = 256
MOE_GROUPS = 4
MOE_EPG = 8
MOE_EXPERTS = 32
MOE_TOPK = 2
MOE_D_FF = 512
DEEPNORM_ALPHA = (2 * DEPTH) ** 0.25
LN_EPS = 1e-5

LANES = 128
VMEM_LIMIT = 56 * 1024 * 1024
MOE_TILE = 384
TOKEN_TILE = 256
DMA_UNROLL = 8
NEG_INF = float("-inf")


def _cparams(sem):
    return pltpu.CompilerParams(dimension_semantics=sem, vmem_limit_bytes=VMEM_LIMIT)


def _layer_norm(y, g, b):
    mu = jnp.mean(y, axis=-1, keepdims=True)
    yc = y - mu
    var = jnp.mean(yc * yc, axis=-1, keepdims=True)
    return yc * lax.rsqrt(var + LN_EPS) * g + b


def _silu(x):
    return x * (1.0 / (1.0 + jnp.exp(-x)))


def _bdot(a, b):
    return jnp.dot(a.astype(BF16), b.astype(BF16), preferred_element_type=F32)


def _bdot_nt(a, b):
    return lax.dot_general(a.astype(BF16), b.astype(BF16), (((1,), (1,)), ((), ())),
                           preferred_element_type=F32)


def _bdot_tn(a, b):
    return lax.dot_general(a.astype(BF16), b.astype(BF16), (((0,), (0,)), ((), ())),
                           preferred_element_type=F32)


def _split3(x):
    hi = x.astype(BF16)
    lo = (x - hi.astype(F32)).astype(BF16)
    return hi, lo


def _dot_hi(a, b, dims=(((1,), (0,)), ((), ()))):
    ah, al = _split3(a)
    bh, bl = _split3(b)
    dg = functools.partial(lax.dot_general, dimension_numbers=dims, preferred_element_type=F32)
    return dg(ah, bh) + (dg(al, bh) + dg(ah, bl))


U32 = jnp.uint32
ROW_TILE = D_MODEL // (2 * LANES)
HALF = D_MODEL // 2


def _bf16_bits(v):
    b = pltpu.bitcast(v, U32)
    return b + (jnp.uint32(0x7FFF) + ((b >> 16) & jnp.uint32(1)))


def _load_token_tiles(ref, start, n_tok):
    u = jnp.concatenate([ref[pl.ds(start + c, n_tok, stride=ROW_TILE), :] for c in range(ROW_TILE)], axis=1)
    lo = pltpu.bitcast(u << 16, F32)
    hi = pltpu.bitcast(u & jnp.uint32(0xFFFF0000), F32)
    return jnp.concatenate([lo, hi], axis=1)


def _store_token_tiles(ref, val):
    n_tok = val.shape[0]
    packed = (_bf16_bits(val[:, :HALF]) >> 16) | (_bf16_bits(val[:, HALF:]) & jnp.uint32(0xFFFF0000))
    for c in range(ROW_TILE):
        ref[pl.ds(c, n_tok, stride=ROW_TILE), :] = packed[:, c * LANES:(c + 1) * LANES]


def _tile_copy(src_hbm, src_tok, dst, dst_tok, sem):
    return pltpu.make_async_copy(src_hbm.at[pl.ds(src_tok * ROW_TILE, ROW_TILE), :],
                                 dst.at[pl.ds(dst_tok * ROW_TILE, ROW_TILE), :], sem)


def _wait_tiles(src_hbm, dst, dst_tok, sem, n_tok):
    pltpu.make_async_copy(src_hbm.at[pl.ds(0, n_tok * ROW_TILE), :],
                          dst.at[pl.ds(dst_tok * ROW_TILE, n_tok * ROW_TILE), :], sem).wait()


def _mm_kernel(x_ref, w_ref, o_ref):
    o_ref[...] = _bdot(x_ref[...], w_ref[...]).astype(o_ref.dtype)


def matmul(x, w, *, tm, tn, out_dtype=F32):
    m, k = x.shape
    n = w.shape[1]
    assert m % tm == 0 and n % tn == 0
    return pl.pallas_call(
        _mm_kernel,
        out_shape=jax.ShapeDtypeStruct((m, n), out_dtype),
        grid=(m // tm, n // tn),
        in_specs=[pl.BlockSpec((tm, k), lambda i, j: (i, 0)),
                  pl.BlockSpec((k, tn), lambda i, j: (0, j))],
        out_specs=pl.BlockSpec((tm, tn), lambda i, j: (i, j)),
        compiler_params=_cparams(("parallel", "parallel")),
        name="matmul",
    )(x, w)


def _mm_dn_kernel(*refs, n_lhs, token_tiles, tail_rows):
    lhs = refs[:2 * n_lhs:2]
    tails = refs[1:2 * n_lhs:2]
    ws = refs[2 * n_lhs:3 * n_lhs]
    res_ref, g_ref, b_ref, o_ref = refs[3 * n_lhs:3 * n_lhs + 4]
    last = pl.program_id(0) == pl.num_programs(0) - 1

    def rows(a_ref, t_ref):
        a = a_ref[...]
        if tail_rows:
            a = jnp.where(last, jnp.concatenate([a[:a.shape[0] - tail_rows], t_ref[...]], axis=0), a)
        return a

    acc = _bdot(rows(lhs[0], tails[0]), ws[0][...])
    for a, t, w in zip(lhs[1:], tails[1:], ws[1:]):
        acc = acc + _bdot(rows(a, t), w[...])
    y = DEEPNORM_ALPHA * res_ref[...] + acc
    out = _layer_norm(y, g_ref[...], b_ref[...])
    o_ref[...] = out
    if token_tiles:
        _store_token_tiles(refs[3 * n_lhs + 4], out)


def matmul_deepnorm(lhs_list, w_list, res, g, b, *, tm, token_tiles=False):
    m, d = res.shape
    n_lhs = len(lhs_list)
    tail_rows = lhs_list[0][1].shape[0]
    for a, t in lhs_list:
        assert a.shape[0] + t.shape[0] == m and t.shape[0] == tail_rows and a.shape[1] == t.shape[1]
        assert 0 < tail_rows < tm and (m - tail_rows) // tm == m // tm - 1
    flat_lhs = [x for pair in lhs_list for x in pair]
    in_specs = ([spec for a, t in lhs_list
                 for spec in (pl.BlockSpec((tm, a.shape[1]), lambda i: (i, 0)),
                              pl.BlockSpec(t.shape, lambda i: (0, 0)))]
                + [pl.BlockSpec(w.shape, lambda i: (0, 0)) for w in w_list]
                + [pl.BlockSpec((tm, d), lambda i: (i, 0)),
                   pl.BlockSpec((1, d), lambda i: (0, 0)),
                   pl.BlockSpec((1, d), lambda i: (0, 0))])
    out_shape = [jax.ShapeDtypeStruct((m, d), F32)]
    out_specs = [pl.BlockSpec((tm, d), lambda i: (i, 0))]
    if token_tiles:
        out_shape.append(jax.ShapeDtypeStruct((m * ROW_TILE, LANES), U32))
        out_specs.append(pl.BlockSpec((tm * ROW_TILE, LANES), lambda i: (i, 0)))
    res_out = pl.pallas_call(
        functools.partial(_mm_dn_kernel, n_lhs=n_lhs, token_tiles=token_tiles, tail_rows=tail_rows),
        out_shape=tuple(out_shape),
        grid=(m // tm,),
        in_specs=in_specs,
        out_specs=tuple(out_specs),
        compiler_params=_cparams(("parallel",)),
        name="matmul_deepnorm",
    )(*flat_lhs, *w_list, res, g.reshape(1, d), b.reshape(1, d))
    return res_out if token_tiles else res_out[0]


CONV_PAD = 32


def _conv_kernel(p_ref, hist_ref, w_ref, cb_ref, g_ref, b_ref, c_ref, st_ref, u_scr, sh_scr, *, nb, t_len, rc):
    hw = CONV_WIDTH - 1
    w = w_ref[...]
    for bb in range(nb):
        rows = pl.ds(bb * t_len, t_len)
        a = p_ref[rows, 0:CONV_CH]
        gt = p_ref[rows, CONV_CH:2 * CONV_CH]
        u = a * (1.0 / (1.0 + jnp.exp(-gt)))
        u_scr[0:CONV_PAD, :] = jnp.concatenate(
            [jnp.zeros((CONV_PAD - hw, CONV_CH), F32), hist_ref[bb]], axis=0)
        u_scr[CONV_PAD:CONV_PAD + t_len, :] = u
        st_ref[bb] = u_scr[CONV_PAD + t_len - hw:CONV_PAD + t_len, :]

        def chunk(ci, carry):
            r0 = pl.multiple_of(ci * rc, rc)
            span = rc + CONV_PAD - 8
            for r in range(1, 8):
                sh_scr[r - 1] = u_scr[pl.ds(r0, rc + CONV_PAD), :][r:r + span, :]
            acc = jnp.broadcast_to(cb_ref[...], (rc, CONV_CH))
            for j in range(CONV_WIDTH):
                off = j + CONV_PAD - hw
                r = off % 8
                if r == 0:
                    tap = u_scr[pl.ds(r0 + off, rc), :]
                else:
                    tap = sh_scr[r - 1, off - r:off - r + rc, :]
                acc = acc + w[j:j + 1, :] * tap
            y = _silu(_layer_norm(acc, g_ref[...], b_ref[...]))
            c_ref[pl.ds(bb * t_len + r0, rc), :] = y.astype(c_ref.dtype)
            return carry

        n_chunks = t_len // rc
        if n_chunks == 1:
            chunk(0, 0)
        else:
            lax.fori_loop(0, n_chunks, chunk, 0)


def conv_module(p, hist, conv_w, conv_b, ln_g, ln_b, *, row0, n_batch, t_len, nb, out_dtype):
    rc = min(256, t_len)
    blk_rows = nb * t_len
    assert row0 % blk_rows == 0 and n_batch % nb == 0
    base = row0 // blk_rows
    hw = CONV_WIDTH - 1
    c, st = pl.pallas_call(
        functools.partial(_conv_kernel, nb=nb, t_len=t_len, rc=rc),
        out_shape=(jax.ShapeDtypeStruct((n_batch * t_len, CONV_CH), out_dtype),
                   jax.ShapeDtypeStruct((n_batch, hw, CONV_CH), F32)),
        grid=(n_batch // nb,),
        in_specs=[pl.BlockSpec((blk_rows, 2 * CONV_CH), lambda i: (base + i, 0)),
                  pl.BlockSpec((nb, hw, CONV_CH), lambda i: (i, 0, 0)),
                  pl.BlockSpec((CONV_WIDTH, CONV_CH), lambda i: (0, 0)),
                  pl.BlockSpec((1, CONV_CH), lambda i: (0, 0)),
                  pl.BlockSpec((1, CONV_CH), lambda i: (0, 0)),
                  pl.BlockSpec((1, CONV_CH), lambda i: (0, 0))],
        out_specs=(pl.BlockSpec((blk_rows, CONV_CH), lambda i: (i, 0)),
                   pl.BlockSpec((nb, hw, CONV_CH), lambda i: (i, 0, 0))),
        scratch_shapes=[pltpu.VMEM((t_len + CONV_PAD, CONV_CH), F32),
                        pltpu.VMEM((7, rc + CONV_PAD - 8, CONV_CH), F32)],
        compiler_params=_cparams(("parallel",)),
        name="conv_module",
    )(p, hist, conv_w, conv_b.reshape(1, -1), ln_g.reshape(1, -1), ln_b.reshape(1, -1))
    return c, st


def _bucket_thresholds():
    max_exact = NUM_BUCKETS // 2
    d = np.arange(0, MAX_DISTANCE + 1)
    val = (np.log(np.maximum(d, 1).astype(np.float32) / np.float32(max_exact))
           / np.float32(math.log(MAX_DISTANCE / max_exact)) * np.float32(NUM_BUCKETS - max_exact))
    inner = (d > max_exact) & (d < MAX_DISTANCE)
    assert np.all(np.abs(val[inner] - np.round(val[inner])) > 1e-3)
    bucket = np.where(d < max_exact, d, np.minimum(max_exact + val.astype(np.int32), NUM_BUCKETS - 1))
    return [int(np.argmax(bucket >= k)) for k in range(1, NUM_BUCKETS)]


def _bias_chain(dist, tab_at, thr):
    b = jnp.where(dist >= thr[0], tab_at(1), tab_at(0))
    for k in range(2, NUM_BUCKETS):
        b = jnp.where(dist >= thr[k - 1], tab_at(k), b)
    return b


def _moba_prompt_kernel(tab_ref, q_ref, k_ref, v_ref, o_ref,
                        kmean_scr, kbf_scr, vt_scr, d0_scr, d1_scr, far_scr, sel_scr, qb_scr, m_scr, l_scr, acc_scr,
                        *, n_blk, thr):
    b = pl.program_id(0)
    i = pl.program_id(1)
    blk = MOBA_BLOCK
    hd = MOBA_HEAD_DIM
    n_pair = MOBA_HEADS // 2
    scale = hd ** -0.5
    nt_dims = (((1,), (1,)), ((), ()))
    key = lax.broadcasted_iota(I32, (blk, blk), 0)
    qry = lax.broadcasted_iota(I32, (blk, blk), 1)
    causal = jnp.concatenate([key <= qry, key <= qry], axis=1)

    @pl.when((b == 0) & (i == 0))
    def _():
        for h in range(MOBA_HEADS):
            tab_at = functools.partial(lambda k, hh: tab_ref[hh, k], hh=h)
            half = slice((h % 2) * blk, (h % 2 + 1) * blk)
            d0_scr[h // 2, :, half] = _bias_chain(qry - key, tab_at, thr)
            d1_scr[h // 2, :, half] = _bias_chain(blk + qry - key, tab_at, thr)
            far_scr[h // 2, :, half] = jnp.full((1, blk), tab_ref[h, NUM_BUCKETS - 1], F32)

    @pl.when(i == 0)
    def _():
        for n in range(n_blk):
            kb = k_ref[n * blk:(n + 1) * blk, :]
            kmean_scr[n:n + 1, :] = jnp.sum(kb, axis=0, keepdims=True) * (1.0 / blk)
            kbf_scr[n * blk:(n + 1) * blk, :] = kb.astype(BF16)
            for pair in range(n_pair):
                vt_scr[n, pair * LANES:(pair + 1) * LANES, :] = (
                    v_ref[n * blk:(n + 1) * blk, pair * LANES:(pair + 1) * LANES].T.astype(BF16))

    q = q_ref[...]
    lane = lax.broadcasted_iota(I32, (blk, LANES), 1)
    rown = lax.broadcasted_iota(I32, (n_blk, 2 * blk), 0)
    rowd = lax.broadcasted_iota(I32, (LANES, blk), 0)
    r0 = pl.multiple_of(i * blk, blk)
    pairs = [(pr, slice(pr * LANES, (pr + 1) * LANES)) for pr in range(n_pair)]

    for pr, cols in pairs:
        qm = jnp.concatenate([jnp.where((lane >= sub * hd) & (lane < (sub + 1) * hd), q[:, cols], 0.0)
                              for sub in range(2)], axis=0)
        gate = _dot_hi(kmean_scr[:, cols], qm, nt_dims)
        rank = jnp.zeros((n_blk, 2 * blk), F32)
        for m in range(n_blk):
            gm = gate[m:m + 1, :]
            tie = jnp.where(rown > m, 1.0, 0.0)
            cnt = jnp.where(gm > gate, 1.0, jnp.where(gm == gate, tie, 0.0))
            rank = rank + jnp.where(m < i, cnt, 0.0)
        sel_scr[pr] = jnp.where((rown < i) & (rank < MOBA_TOPK), 1.0, 0.0)
        qb = (qm * scale).astype(BF16)
        qb_scr[pr] = qb
        s = lax.dot_general(kbf_scr[pl.ds(r0, blk), cols], qb, nt_dims,
                            preferred_element_type=F32) + d0_scr[pr]
        s = jnp.where(causal, s, NEG_INF)
        m_run = jnp.max(s, axis=0, keepdims=True)
        p = jnp.exp(s - m_run)
        m_scr[pr] = m_run
        l_scr[pr] = jnp.sum(p, axis=0, keepdims=True)
        acc_scr[pr] = jnp.dot(vt_scr[i, cols, :], p.astype(BF16), preferred_element_type=F32)

    def merge_block(n, pr, cols, s, shift):
        picked = sel_scr[pr, pl.ds(n, 1), :] > 0.5
        m_run = m_scr[pr]
        m_new = jnp.maximum(m_run, jnp.where(picked, jnp.max(s, axis=0, keepdims=True) + shift, NEG_INF))
        alpha = jnp.exp(m_run - m_new)
        p = jnp.exp(s - jnp.where(picked, m_new - shift, jnp.inf))
        m_scr[pr] = m_new
        l_scr[pr] = alpha * l_scr[pr] + jnp.sum(p, axis=0, keepdims=True)
        acc_scr[pr] = alpha * acc_scr[pr] + jnp.dot(vt_scr[n, cols, :], p.astype(BF16), preferred_element_type=F32)

    @pl.when(i > 0)
    def _():
        rr = pl.multiple_of((i - 1) * blk, blk)
        for pr, cols in pairs:
            s = lax.dot_general(kbf_scr[pl.ds(rr, blk), cols], qb_scr[pr], nt_dims,
                                preferred_element_type=F32) + d1_scr[pr]
            merge_block(i - 1, pr, cols, s, 0.0)

    def far_scores(n, pr, cols):
        rr = pl.multiple_of(n * blk, blk)
        return lax.dot_general(kbf_scr[pl.ds(rr, blk), cols], qb_scr[pr], nt_dims, preferred_element_type=F32)

    def merge_two(n0, n1, pr, cols, s0, s1, shift):
        pk0 = sel_scr[pr, pl.ds(n0, 1), :] > 0.5
        pk1 = sel_scr[pr, pl.ds(n1, 1), :] > 0.5
        m_run = m_scr[pr]
        bm0 = jnp.where(pk0, jnp.max(s0, axis=0, keepdims=True) + shift, NEG_INF)
        bm1 = jnp.where(pk1, jnp.max(s1, axis=0, keepdims=True) + shift, NEG_INF)
        m_new = jnp.maximum(m_run, jnp.maximum(bm0, bm1))
        alpha = jnp.exp(m_run - m_new)
        p0 = jnp.exp(s0 - jnp.where(pk0, m_new - shift, jnp.inf))
        p1 = jnp.exp(s1 - jnp.where(pk1, m_new - shift, jnp.inf))
        m_scr[pr] = m_new
        l_scr[pr] = (alpha * l_scr[pr] + jnp.sum(p0, axis=0, keepdims=True)) + jnp.sum(p1, axis=0, keepdims=True)
        acc_scr[pr] = (alpha * acc_scr[pr]
                       + jnp.dot(vt_scr[n0, cols, :], p0.astype(BF16), preferred_element_type=F32)
                       + jnp.dot(vt_scr[n1, cols, :], p1.astype(BF16), preferred_element_type=F32))

    n_far = jnp.maximum(i - 1, 0)

    def body(n2, carry):
        for pr, cols in pairs:
            merge_two(2 * n2, 2 * n2 + 1, pr, cols, far_scores(2 * n2, pr, cols),
                      far_scores(2 * n2 + 1, pr, cols), far_scr[pr])
        return carry

    lax.fori_loop(0, n_far // 2, body, 0)

    @pl.when(n_far % 2 == 1)
    def _():
        for pr, cols in pairs:
            merge_block(n_far - 1, pr, cols, far_scores(n_far - 1, pr, cols), far_scr[pr])

    outs = []
    for pr, _ in pairs:
        o = acc_scr[pr] / l_scr[pr]
        outs.append(jnp.where(rowd < hd, o[:, :blk], o[:, blk:]).T)
    o_ref[...] = jnp.concatenate(outs, axis=1).astype(o_ref.dtype)


def moba_prompt(p, tab, *, n_batch, t_len, out_dtype):
    n_blk = t_len // MOBA_BLOCK
    n_pair = MOBA_HEADS // 2
    wide = 2 * MOBA_BLOCK
    return pl.pallas_call(
        functools.partial(_moba_prompt_kernel, n_blk=n_blk, thr=_bucket_thresholds()),
        out_shape=jax.ShapeDtypeStruct((n_batch * t_len, MOBA_WIDTH), out_dtype),
        grid=(n_batch, n_blk),
        in_specs=[pl.BlockSpec(memory_space=pltpu.SMEM),
                  pl.BlockSpec((MOBA_BLOCK, MOBA_WIDTH), lambda b, i: (b * n_blk + i, 2)),
                  pl.BlockSpec((t_len, MOBA_WIDTH), lambda b, i: (b, 3)),
                  pl.BlockSpec((t_len, MOBA_WIDTH), lambda b, i: (b, 4))],
        out_specs=pl.BlockSpec((MOBA_BLOCK, MOBA_WIDTH), lambda b, i: (b * n_blk + i, 0)),
        scratch_shapes=[pltpu.VMEM((n_blk, MOBA_WIDTH), F32),
                        pltpu.VMEM((t_len, MOBA_WIDTH), BF16),
                        pltpu.VMEM((n_blk, MOBA_WIDTH, MOBA_BLOCK), BF16),
                        pltpu.VMEM((n_pair, MOBA_BLOCK, wide), F32),
                        pltpu.VMEM((n_pair, MOBA_BLOCK, wide), F32),
                        pltpu.VMEM((n_pair, 1, wide), F32),
                        pltpu.VMEM((n_pair, n_blk, wide), F32),
                        pltpu.VMEM((n_pair, wide, LANES), BF16),
                        pltpu.VMEM((n_pair, 1, wide), F32),
                        pltpu.VMEM((n_pair, 1, wide), F32),
                        pltpu.VMEM((n_pair, LANES, wide), F32)],
        compiler_params=_cparams(("arbitrary", "arbitrary")),
        name="moba_prompt",
    )(tab, p, p, p)


def _moba_sample_kernel(pt_ref, *refs, n_blk, nbs, page, t_len, thr):
    npg = 2 * nbs
    k_refs = refs[:npg]
    v_refs = refs[npg:2 * npg]
    q_ref, qbd_ref, kn_ref, vn_ref, tab_ref, o_ref, g_scr, m_scr, l_scr, o_scr, bl_scr = refs[2 * npg:]
    n = pl.program_id(1)
    nh, hd = MOBA_HEADS, MOBA_HEAD_DIM
    nrow = nh * t_len
    scale = hd ** -0.5
    nt_dims = (((1,), (1,)), ((), ()))
    lane = lax.broadcasted_iota(I32, (nrow, LANES), 1)
    row = lax.broadcasted_iota(I32, (nrow, LANES), 0)
    tab = tab_ref[...]

    def tab_at(k):
        return tab[:, k:k + 1]

    @pl.when(n == 0)
    def _():
        g_scr[...] = jnp.zeros(g_scr.shape, F32)
        m_scr[...] = jnp.zeros(m_scr.shape, F32)
        l_scr[...] = jnp.zeros(l_scr.shape, F32)
        for half in range(2):
            bl_scr[:, half * page:(half + 1) * page] = _bias_chain(
                MOBA_BLOCK + row % t_len - (half * page + lane), tab_at, thr)

    qbd = (qbd_ref[0] * scale).astype(BF16)
    far = tab_at(NUM_BUCKETS - 1)
    bw = 2 * page
    kt = jnp.concatenate([r[...].reshape(nh * hd, page).astype(BF16) for r in k_refs], axis=1)
    vt = jnp.concatenate([r[...].reshape(nh * hd, page).astype(BF16) for r in v_refs], axis=1)
    s_all = jnp.dot(qbd, kt, preferred_element_type=F32)
    p_rows = []
    stats = []
    for j in range(nbs):
        blk_idx = n * nbs + j
        s_raw = s_all[:, j * bw:(j + 1) * bw]
        gate = jnp.sum(s_raw, axis=1, keepdims=True) * (1.0 / (MOBA_BLOCK * scale))
        s = s_raw + jnp.where(blk_idx == n_blk - 1, bl_scr[...], far)
        m = jnp.max(s, axis=1, keepdims=True)
        p = jnp.exp(s - m)
        l = jnp.sum(p, axis=1, keepdims=True)
        stats.append((blk_idx, gate, m, l))
        zl = jnp.zeros((nrow, j * bw), F32)
        zr = jnp.zeros((nrow, (nbs - 1 - j) * bw), F32)
        p_rows.append(jnp.concatenate([x for x in (zl, p, zr) if x.shape[1]], axis=1))
    p_bd = jnp.concatenate(p_rows, axis=0).astype(BF16)
    pv = lax.dot_general(p_bd, vt, nt_dims, preferred_element_type=F32)
    for j, (blk_idx, gate, m, l) in enumerate(stats):
        at_blk = lane == blk_idx
        g_scr[...] = jnp.where(at_blk, gate, g_scr[...])
        m_scr[...] = jnp.where(at_blk, m, m_scr[...])
        l_scr[...] = jnp.where(at_blk, l, l_scr[...])
        o_scr[blk_idx] = jnp.concatenate(
            [pv[j * nrow + h * t_len:j * nrow + (h + 1) * t_len, h * hd:(h + 1) * hd] for h in range(nh)], axis=0)

    @pl.when(n == n_blk // nbs - 1)
    def _():
        gate = g_scr[...]
        rank = jnp.zeros((nrow, LANES), F32)
        for mm in range(n_blk):
            gm = gate[:, mm:mm + 1]
            tie = jnp.where(lane > mm, 1.0, 0.0)
            rank = rank + jnp.where(gm > gate, 1.0, jnp.where(gm == gate, tie, 0.0))
        sel = (rank < MOBA_TOPK) & (lane < n_blk)

        q = q_ref[0]
        kn = kn_ref[0]
        vn = vn_ref[0]
        tq = lax.broadcasted_iota(I32, (nrow, t_len), 0) % t_len
        sk = lax.broadcasted_iota(I32, (nrow, t_len), 1)
        s_own = jnp.concatenate(
            [lax.dot_general((q[h] * scale).astype(BF16), kn[h].astype(BF16), nt_dims, preferred_element_type=F32)
             for h in range(nh)], axis=0) + _bias_chain(tq - sk, tab_at, thr)
        s_own = jnp.where(sk <= tq, s_own, NEG_INF)
        m_o = jnp.max(s_own, axis=1, keepdims=True)
        p_o = jnp.exp(s_own - m_o)
        l_o = jnp.sum(p_o, axis=1, keepdims=True)
        o_o = jnp.concatenate(
            [jnp.dot(p_o[h * t_len:(h + 1) * t_len].astype(BF16), vn[h].astype(BF16), preferred_element_type=F32)
             for h in range(nh)], axis=0)

        m_sel = jnp.where(sel, m_scr[...], NEG_INF)
        m_fin = jnp.maximum(jnp.max(m_sel, axis=1, keepdims=True), m_o)
        w = jnp.where(sel, jnp.exp(m_sel - m_fin), 0.0)
        w_o = jnp.exp(m_o - m_fin)
        l_fin = jnp.sum(w * l_scr[...], axis=1, keepdims=True) + w_o * l_o
        acc = w_o * o_o
        for nn in range(n_blk):
            acc = acc + w[:, nn:nn + 1] * o_scr[nn]
        o_ref[0] = acc / l_fin


MOBA_SAMPLE_BLOCKS_PER_STEP = 8


def moba_sample(page_table, cache_kt, cache_vt, q, kn, vn, tab_heads, *, layer_e, t_len):
    n_b, n_pages = page_table.shape
    page = cache_kt.shape[-1]
    nbs = MOBA_SAMPLE_BLOCKS_PER_STEP
    ppb = MOBA_BLOCK // page
    assert ppb == 2 and page == LANES and n_pages % (ppb * nbs) == 0
    n_blk = n_pages // ppb
    assert n_blk <= LANES
    nh, hd = MOBA_HEADS, MOBA_HEAD_DIM
    nrow = nh * t_len
    tab = jnp.pad(jnp.repeat(tab_heads, t_len, axis=0), ((0, 0), (0, LANES - NUM_BUCKETS)))
    qbd = (q[:, :, :, None, :] * jnp.eye(nh, dtype=F32)[None, :, None, :, None]).reshape(n_b, nrow, nh * hd)

    def pspec(p):
        return pl.BlockSpec((None, None, nh, hd, page),
                            lambda b, n, pt: (pt[b, n * ppb * nbs + p], layer_e, 0, 0, 0))

    def bspec():
        return pl.BlockSpec((1, nh, t_len, hd), lambda b, n, pt: (b, 0, 0, 0))

    pages = [pspec(p) for p in range(ppb * nbs)]
    grid_spec = pltpu.PrefetchScalarGridSpec(
        num_scalar_prefetch=1,
        grid=(n_b, n_blk // nbs),
        in_specs=pages + pages + [
            bspec(),
            pl.BlockSpec((1, nrow, nh * hd), lambda b, n, pt: (b, 0, 0)),
            bspec(), bspec(),
            pl.BlockSpec(tab.shape, lambda b, n, pt: (0, 0))],
        out_specs=pl.BlockSpec((1, nrow, hd), lambda b, n, pt: (b, 0, 0)),
        scratch_shapes=[pltpu.VMEM((nrow, LANES), F32), pltpu.VMEM((nrow, LANES), F32),
                        pltpu.VMEM((nrow, LANES), F32), pltpu.VMEM((n_blk, nrow, hd), F32),
                        pltpu.VMEM((nrow, MOBA_BLOCK), F32)],
    )
    return pl.pallas_call(
        functools.partial(_moba_sample_kernel, n_blk=n_blk, nbs=nbs, page=page, t_len=t_len,
                          thr=_bucket_thresholds()),
        out_shape=jax.ShapeDtypeStruct((n_b, nrow, hd), F32),
        grid_spec=grid_spec,
        compiler_params=_cparams(("parallel", "arbitrary")),
        name="moba_sample",
    )(page_table, *([cache_kt] * (ppb * nbs)), *([cache_vt] * (ppb * nbs)), q, qbd, kn, vn, tab)


def _retention_kernel(q_ref, k_ref, v_ref, g_ref, cos_ref, sin_ref, dmask_ref, xi_ref, zeta_ref, gc_ref,
                      gng_ref, gnb_ref, *rest, nb, chunk, has_s0):
    if has_s0:
        s0_ref, o_ref, s_out_ref, s_scr = rest
    else:
        o_ref, s_out_ref, s_scr = rest
    c = pl.program_id(1)
    dk, dv = RET_DK, RET_DV
    half = dk // 2

    @pl.when(c == 0)
    def _():
        if has_s0:
            s_scr[...] = s0_ref[...]
        else:
            s_scr[...] = jnp.zeros(s_scr.shape, F32)

    cos = cos_ref[...]
    sin = sin_ref[...]

    def rot(x):
        x1 = x[:, :half]
        x2 = x[:, half:]
        return jnp.concatenate([x1 * cos - x2 * sin, x2 * cos + x1 * sin], axis=1)

    out_rows = []
    for bb in range(nb):
        rows = pl.ds(bb * chunk, chunk)
        out_heads = []
        for h in range(RET_HEADS):
            qh = rot(q_ref[rows, h * dk:(h + 1) * dk])
            kh = rot(k_ref[rows, h * dk:(h + 1) * dk]) * (dk ** -0.5)
            vh = v_ref[rows, h * dv:(h + 1) * dv]
            s = s_scr[bb, h]
            att = _bdot_nt(qh, kh) * dmask_ref[h]
            o = _bdot(att, vh) + _bdot(qh, s) * xi_ref[h]
            s_scr[bb, h] = s * gc_ref[h, 0:1, 0:1] + _bdot_tn(kh * zeta_ref[h], vh)
            mu = jnp.mean(o, axis=-1, keepdims=True)
            oc = o - mu
            var = jnp.mean(oc * oc, axis=-1, keepdims=True)
            on = oc * lax.rsqrt(var + LN_EPS) * gng_ref[:, h * dv:(h + 1) * dv] + gnb_ref[:, h * dv:(h + 1) * dv]
            gate = _silu(g_ref[rows, h * dv:(h + 1) * dv])
            out_heads.append(gate * on)
        out_rows.append(jnp.concatenate(out_heads, axis=1))
    o_ref[...] = jnp.concatenate(out_rows, axis=0).astype(o_ref.dtype)

    @pl.when(c == pl.num_programs(1) - 1)
    def _():
        s_out_ref[...] = s_scr[...]


def retention(p, cos, sin, consts, gn_g, gn_b, s0, *, row0, n_batch, t_len, nb, out_dtype):
    chunk = math.gcd(t_len, RET_CHUNK)
    n_chunk = t_len // chunk
    dmask, xi, zeta, gc = consts
    blk_rows = nb * chunk
    if nb > 1:
        assert n_chunk == 1
    assert row0 % blk_rows == 0
    base = row0 // blk_rows
    qk_w = RET_HEADS * RET_DK
    v_w = RET_HEADS * RET_DV
    has_s0 = s0 is not None

    def rmap(col):
        return lambda b, c: (base + b * n_chunk + c, col)

    in_specs = [pl.BlockSpec((blk_rows, qk_w), rmap(0)),
                pl.BlockSpec((blk_rows, qk_w), rmap(1)),
                pl.BlockSpec((blk_rows, v_w), rmap(1)),
                pl.BlockSpec((blk_rows, v_w), rmap(2)),
                pl.BlockSpec((chunk, RET_DK // 2), lambda b, c: (c, 0)),
                pl.BlockSpec((chunk, RET_DK // 2), lambda b, c: (c, 0)),
                pl.BlockSpec(dmask.shape, lambda b, c: (0, 0, 0)),
                pl.BlockSpec(xi.shape, lambda b, c: (0, 0, 0)),
                pl.BlockSpec(zeta.shape, lambda b, c: (0, 0, 0)),
                pl.BlockSpec(gc.shape, lambda b, c: (0, 0, 0)),
                pl.BlockSpec((1, v_w), lambda b, c: (0, 0)),
                pl.BlockSpec((1, v_w), lambda b, c: (0, 0))]
    args = [p, p, p, p, cos, sin, dmask, xi, zeta, gc, gn_g.reshape(1, -1), gn_b.reshape(1, -1)]
    s_spec = pl.BlockSpec((nb, RET_HEADS, RET_DK, RET_DV), lambda b, c: (b, 0, 0, 0))
    if has_s0:
        in_specs.append(s_spec)
        args.append(s0)
    o, s_out = pl.pallas_call(
        functools.partial(_retention_kernel, nb=nb, chunk=chunk, has_s0=has_s0),
        out_shape=(jax.ShapeDtypeStruct((n_batch * t_len, v_w), out_dtype),
                   jax.ShapeDtypeStruct((n_batch, RET_HEADS, RET_DK, RET_DV), F32)),
        grid=(n_batch // nb, n_chunk),
        in_specs=in_specs,
        out_specs=(pl.BlockSpec((blk_rows, v_w), lambda b, c: (b * n_chunk + c, 0)), s_spec),
        scratch_shapes=[pltpu.VMEM((nb, RET_HEADS, RET_DK, RET_DV), F32)],
        compiler_params=_cparams(("parallel", "arbitrary")),
        name="retention",
    )(*args)
    return o, s_out


def _retention_consts(chunk):
    h = jnp.arange(RET_HEADS, dtype=F32)
    lg = jnp.log1p(-jnp.exp2(-5.0 - h))
    idx = jnp.arange(chunk, dtype=F32)
    diff = idx[:, None] - idx[None, :]
    dmask = jnp.where(diff >= 0, jnp.exp(lg[:, None, None] * jnp.maximum(diff, 0.0)), 0.0)
    xi = jnp.exp(lg[:, None] * (idx[None, :] + 1.0))
    zeta = jnp.exp(lg[:, None] * (chunk - 1.0 - idx[None, :]))
    g_c = jnp.exp(lg * chunk)
    return (dmask,
            jnp.broadcast_to(xi[:, :, None], (RET_HEADS, chunk, RET_DV)),
            jnp.broadcast_to(zeta[:, :, None], (RET_HEADS, chunk, RET_DK)),
            jnp.broadcast_to(g_c[:, None, None], (RET_HEADS, 8, LANES)))


def _rotary_tables(pos):
    half = RET_DK // 2
    inv = 10000.0 ** (-jnp.linspace(0.0, 1.0, half, dtype=F32))
    ang = pos.astype(F32)[:, None] * inv[None, :]
    return jnp.cos(ang), jnp.sin(ang)


def _mem_attn_kernel(q_ref, mk_ref, mv_ref, o_ref, *, nb, tq):
    hd = MEM_HEAD_DIM
    scale = hd ** -0.5
    qf = q_ref[...].astype(F32)
    out_rows = []
    for bb in range(nb):
        out_heads = []
        for h in range(MEM_HEADS):
            cols = slice(h * hd, (h + 1) * hd)
            q = qf[bb * tq:(bb + 1) * tq, cols]
            s = _bdot_nt(q, mk_ref[bb, :, cols]) * scale
            m = jnp.max(s, axis=-1, keepdims=True)
            p = jnp.exp(s - m)
            p = p / jnp.sum(p, axis=-1, keepdims=True)
            out_heads.append(_bdot(p, mv_ref[bb, :, cols]))
        out_rows.append(jnp.concatenate(out_heads, axis=1))
    o_ref[...] = jnp.concatenate(out_rows, axis=0).astype(o_ref.dtype)


def mem_attention(q, mk, mv, *, layer, row0, n_batch, t_len, nb, tq, out_dtype):
    n_t = t_len // tq
    blk_rows = nb * tq
    if nb > 1:
        assert n_t == 1
    assert row0 % blk_rows == 0
    base = row0 // blk_rows
    m_tok = mk.shape[2]
    return pl.pallas_call(
        functools.partial(_mem_attn_kernel, nb=nb, tq=tq),
        out_shape=jax.ShapeDtypeStruct((n_batch * t_len, D_MODEL), out_dtype),
        grid=(n_batch // nb, n_t),
        in_specs=[pl.BlockSpec((blk_rows, D_MODEL), lambda b, t: (base + b * n_t + t, 0)),
                  pl.BlockSpec((None, nb, m_tok, D_MODEL), lambda b, t: (layer, b, 0, 0)),
                  pl.BlockSpec((None, nb, m_tok, D_MODEL), lambda b, t: (layer, b, 0, 0))],
        out_specs=pl.BlockSpec((blk_rows, D_MODEL), lambda b, t: (b * n_t + t, 0)),
        compiler_params=_cparams(("parallel", "arbitrary")),
        name="mem_attention",
    )(q, mk, mv)


def _mem_attn_cache_kernel(qt_ref, mk_ref, mv_ref, o_ref, *, nb, tq):
    nh, hd = MEM_HEADS, MEM_HEAD_DIM
    ncol = nh * tq
    m_tok = mk_ref.shape[1]
    scale = hd ** -0.5
    r = lax.broadcasted_iota(I32, (m_tok * nh, ncol), 0)
    c = lax.broadcasted_iota(I32, (m_tok * nh, ncol), 1)
    match = (r % nh) == (c // tq)
    for bb in range(nb):
        k2 = mk_ref[bb].reshape(m_tok * nh, hd)
        v2 = mv_ref[bb].reshape(m_tok * nh, hd)
        s = jnp.where(match, _bdot(k2, qt_ref[bb]) * scale, NEG_INF)
        m = jnp.max(s, axis=0, keepdims=True)
        p = jnp.exp(s - m)
        p = p / jnp.sum(p, axis=0, keepdims=True)
        o_ref[bb] = _bdot_tn(v2, p)


def mem_attention_cache(qt, mk5, mv5, *, layer, nb, tq):
    n_b = qt.shape[0]
    m_tok, nh, hd = mk5.shape[2:]
    ncol = nh * tq
    return pl.pallas_call(
        functools.partial(_mem_attn_cache_kernel, nb=nb, tq=tq),
        out_shape=jax.ShapeDtypeStruct((n_b, hd, ncol), F32),
        grid=(n_b // nb,),
        in_specs=[pl.BlockSpec((nb, hd, ncol), lambda b: (b, 0, 0)),
                  pl.BlockSpec((None, nb, m_tok, nh, hd), lambda b: (layer, b, 0, 0, 0)),
                  pl.BlockSpec((None, nb, m_tok, nh, hd), lambda b: (layer, b, 0, 0, 0))],
        out_specs=pl.BlockSpec((nb, hd, ncol), lambda b: (b, 0, 0)),
        compiler_params=_cparams(("parallel",)),
        name="mem_attention_cache",
    )(qt, mk5, mv5)


def _router_kernel(x_ref, w_ref, b_ref, tri_ref, wt_ref, ei_ref, cnt_ref, base_scr):
    @pl.when(pl.program_id(0) == 0)
    def _():
        base_scr[...] = jnp.zeros(base_scr.shape, F32)

    x = x_ref[...]
    logits = _dot_hi(x, w_ref[...]) + b_ref[...]
    tm = x.shape[0]
    lane = lax.broadcasted_iota(I32, (tm, LANES), 1)
    big = jnp.int32(LANES)

    def masked_softmax(mask):
        lm = jnp.where(mask, logits, NEG_INF)
        mx = jnp.max(lm, axis=-1, keepdims=True)
        e = jnp.exp(lm - mx)
        return e / jnp.sum(e, axis=-1, keepdims=True)

    g_prob = masked_softmax(lane < MOE_GROUPS)
    gp = jnp.max(g_prob, axis=-1, keepdims=True)
    gi = jnp.min(jnp.where(g_prob == gp, lane, big), axis=-1, keepdims=True)
    lo = MOE_GROUPS + gi * MOE_EPG
    in_group = (lane >= lo) & (lane < lo + MOE_EPG)
    e_prob = jnp.where(in_group, masked_softmax(in_group), -1.0)
    p1 = jnp.max(e_prob, axis=-1, keepdims=True)
    i1 = jnp.min(jnp.where(e_prob == p1, lane, big), axis=-1, keepdims=True)
    rest = jnp.where(lane == i1, -1.0, e_prob)
    p2 = jnp.max(rest, axis=-1, keepdims=True)
    i2 = jnp.min(jnp.where(rest == p2, lane, big), axis=-1, keepdims=True)
    tot = p1 + p2
    w1 = gp * (p1 / tot)
    w2 = gp * (p2 / tot)
    e1 = i1 - MOE_GROUPS
    e2 = i2 - MOE_GROUPS
    oh1 = jnp.where(lane == e1, 1.0, 0.0)
    oh2 = jnp.where(lane == e2, 1.0, 0.0)
    both = oh1 + oh2
    base = base_scr[0:1, :]
    before = jnp.dot(tri_ref[...], both.astype(BF16), preferred_element_type=F32) + base
    r1 = jnp.sum(oh1 * before, axis=-1, keepdims=True).astype(I32)
    r2 = jnp.sum(oh2 * before, axis=-1, keepdims=True).astype(I32)
    base_new = base + jnp.sum(both, axis=0, keepdims=True)
    base_scr[0:1, :] = base_new
    wt_ref[...] = jnp.where(lane == 0, w1, jnp.where(lane == 1, w2, 0.0))
    ei_ref[...] = jnp.where(lane == 0, e1, jnp.where(lane == 1, e2, jnp.where(lane == 2, r1,
                                                                              jnp.where(lane == 3, r2, 0))))
    cnt_ref[...] = jnp.broadcast_to(base_new, cnt_ref.shape)


def moe_router(x, w_all, b_all, *, tm):
    m, d = x.shape
    tri = jnp.asarray(np.tril(np.ones((tm, tm), np.float32), -1), BF16)
    return pl.pallas_call(
        _router_kernel,
        out_shape=(jax.ShapeDtypeStruct((m, LANES), F32), jax.ShapeDtypeStruct((m, LANES), I32),
                   jax.ShapeDtypeStruct((8, LANES), F32)),
        grid=(m // tm,),
        in_specs=[pl.BlockSpec((tm, d), lambda i: (i, 0)),
                  pl.BlockSpec((d, LANES), lambda i: (0, 0)),
                  pl.BlockSpec((1, LANES), lambda i: (0, 0)),
                  pl.BlockSpec((tm, tm), lambda i: (0, 0))],
        out_specs=(pl.BlockSpec((tm, LANES), lambda i: (i, 0)), pl.BlockSpec((tm, LANES), lambda i: (i, 0)),
                   pl.BlockSpec((8, LANES), lambda i: (0, 0))),
        scratch_shapes=[pltpu.VMEM((8, LANES), F32)],
        compiler_params=_cparams(("arbitrary",)),
        name="moe_router",
    )(x, w_all, b_all, tri)


def _moe_dispatch_kernel(dest_ref, x_ref, xs_in, xs_out, sem, *, tm):
    del xs_in

    def body(g, carry):
        for u in range(DMA_UNROLL):
            r = g * DMA_UNROLL + u
            _tile_copy(x_ref, r, xs_out, dest_ref[r], sem.at[0]).start(priority=0)
            _tile_copy(x_ref, r, xs_out, dest_ref[tm + r], sem.at[0]).start(priority=1)
        return carry

    lax.fori_loop(0, tm // DMA_UNROLL, body, 0)
    for _ in range(MOE_TOPK):
        _wait_tiles(x_ref, xs_out, 0, sem.at[0], tm)


def moe_dispatch(x8, dest, n_slots, *, tm):
    n_t = dest.shape[0]
    xs0 = jnp.zeros((n_slots * ROW_TILE, LANES), U32)
    return pl.pallas_call(
        functools.partial(_moe_dispatch_kernel, tm=tm),
        out_shape=jax.ShapeDtypeStruct(xs0.shape, U32),
        grid=(n_t,),
        in_specs=[pl.BlockSpec((None, None, 2 * tm), lambda i: (i, 0, 0), memory_space=pltpu.SMEM),
                  pl.BlockSpec((tm * ROW_TILE, LANES), lambda i: (i, 0)),
                  pl.BlockSpec(memory_space=pl.ANY)],
        out_specs=pl.BlockSpec(memory_space=pl.ANY),
        scratch_shapes=[pltpu.SemaphoreType.DMA((1,))],
        input_output_aliases={2: 0},
        compiler_params=_cparams(("arbitrary",)),
        name="moe_dispatch",
    )(dest.reshape(n_t, 1, 2 * tm), x8, xs0)


def _moe_ffn_kernel(te_ref, nt_ref, first_ref, slot_ref, nxt_ref, xs_ref, w1_hbm, w3_hbm, w2_hbm, ys_ref,
                    wb1, wb3, wb2, sem, *, layer):
    j = pl.program_id(0)

    def weight_copies(e, s):
        return (pltpu.make_async_copy(w1_hbm.at[layer, e], wb1.at[s], sem.at[s]),
                pltpu.make_async_copy(w3_hbm.at[layer, e], wb3.at[s], sem.at[s]),
                pltpu.make_async_copy(w2_hbm.at[layer, e], wb2.at[s], sem.at[s]))

    @pl.when(j == 0)
    def _():
        for c in weight_copies(te_ref[0], slot_ref[0]):
            c.start()

    @pl.when(j < nt_ref[0])
    def _():
        s = slot_ref[j]

        @pl.when(first_ref[j] == 1)
        def _():
            for c in weight_copies(te_ref[j], s):
                c.wait()

            @pl.when(nxt_ref[j] >= 0)
            def _():
                for c in weight_copies(nxt_ref[j], 1 - s):
                    c.start()

        x = _load_token_tiles(xs_ref, 0, MOE_TILE).astype(BF16)
        h = _silu(_bdot(x, wb1[s])) * _bdot(x, wb3[s])
        _store_token_tiles(ys_ref, _bdot(h, wb2[s]))

    @pl.when(j >= nt_ref[0])
    def _():
        ys_ref[...] = jnp.zeros(ys_ref.shape, U32)


def moe_ffn(xs8, plan, w1, w3, w2, *, layer):
    tile_expert, n_tiles, first, slot, nxt = plan
    n_t = tile_expert.shape[0]
    d, dff = w1.shape[2], w1.shape[3]
    blk = MOE_TILE * ROW_TILE
    grid_spec = pltpu.PrefetchScalarGridSpec(
        num_scalar_prefetch=5,
        grid=(n_t,),
        in_specs=[pl.BlockSpec((blk, LANES), lambda j, te, nt, *_: (jnp.minimum(j, nt[0] - 1), 0)),
                  pl.BlockSpec(memory_space=pl.ANY),
                  pl.BlockSpec(memory_space=pl.ANY),
                  pl.BlockSpec(memory_space=pl.ANY)],
        out_specs=pl.BlockSpec((blk, LANES), lambda j, te, nt, *_: (j, 0)),
        scratch_shapes=[pltpu.VMEM((2, d, dff), F32), pltpu.VMEM((2, d, dff), F32), pltpu.VMEM((2, dff, d), F32),
                        pltpu.SemaphoreType.DMA((2,))],
    )
    return pl.pallas_call(
        functools.partial(_moe_ffn_kernel, layer=layer),
        out_shape=jax.ShapeDtypeStruct(xs8.shape, U32),
        grid_spec=grid_spec,
        compiler_params=_cparams(("arbitrary",)),
        name="moe_ffn",
    )(tile_expert, n_tiles, first, slot, nxt, xs8, w1, w3, w2)


def _moe_combine_kernel(d_cur, d_nxt, ys_hbm, x_ref, wt_ref, g_ref, b_ref, *rest, tm, lead_tiles):
    if lead_tiles is None:
        o_ref, ybuf, sem = rest
    else:
        o_ref, o_tail_ref, ybuf, sem = rest
    i = pl.program_id(0)
    n = pl.num_programs(0)
    slot = i % 2

    def issue(d_ref, s):
        def body(g, carry):
            for u in range(DMA_UNROLL):
                r = g * DMA_UNROLL + u
                _tile_copy(ys_hbm, d_ref[r], ybuf, s * 2 * tm + r, sem.at[s]).start(priority=u % 2)
            return carry
        lax.fori_loop(0, 2 * tm // DMA_UNROLL, body, 0)

    @pl.when(i == 0)
    def _():
        issue(d_cur, 0)

    @pl.when(i + 1 < n)
    def _():
        issue(d_nxt, 1 - slot)

    base_tok = slot * 2 * tm
    _wait_tiles(ys_hbm, ybuf, base_tok, sem.at[slot], 2 * tm)
    wt = wt_ref[...]
    y0 = _load_token_tiles(ybuf, base_tok * ROW_TILE, tm)
    y1 = _load_token_tiles(ybuf, (base_tok + tm) * ROW_TILE, tm)
    y = DEEPNORM_ALPHA * x_ref[...] + (wt[:, 0:1] * y0 + wt[:, 1:2] * y1)
    out = _layer_norm(y, g_ref[...], b_ref[...])
    if lead_tiles is None:
        o_ref[...] = out
    else:
        @pl.when(i < lead_tiles)
        def _():
            o_ref[...] = out

        @pl.when(i >= lead_tiles)
        def _():
            o_tail_ref[...] = out


def moe_combine_deepnorm(ys8, dest, x, wts, g, b, *, tm, lead_rows=None):
    m, d = x.shape
    n_t = m // tm
    dest3 = dest.reshape(n_t, 1, 2 * tm)
    if lead_rows is None:
        lead_tiles = None
        out_shape = jax.ShapeDtypeStruct((m, d), F32)
        out_specs = pl.BlockSpec((tm, d), lambda i: (i, 0))
    else:
        assert lead_rows % tm == 0 and m - lead_rows == tm
        lead_tiles = lead_rows // tm
        out_shape = (jax.ShapeDtypeStruct((lead_rows, d), F32), jax.ShapeDtypeStruct((tm, d), F32))
        out_specs = (pl.BlockSpec((tm, d), lambda i: (jnp.minimum(i, lead_tiles - 1), 0)),
                     pl.BlockSpec((tm, d), lambda i: (0, 0)))
    return pl.pallas_call(
        functools.partial(_moe_combine_kernel, tm=tm, lead_tiles=lead_tiles),
        out_shape=out_shape,
        grid=(n_t,),
        in_specs=[pl.BlockSpec((None, None, 2 * tm), lambda i: (i, 0, 0), memory_space=pltpu.SMEM),
                  pl.BlockSpec((None, None, 2 * tm), lambda i: (jnp.minimum(i + 1, n_t - 1), 0, 0),
                               memory_space=pltpu.SMEM),
                  pl.BlockSpec(memory_space=pl.ANY),
                  pl.BlockSpec((tm, d), lambda i: (i, 0)),
                  pl.BlockSpec((tm, LANES), lambda i: (i, 0)),
                  pl.BlockSpec((1, d), lambda i: (0, 0)),
                  pl.BlockSpec((1, d), lambda i: (0, 0))],
        out_specs=out_specs,
        scratch_shapes=[pltpu.VMEM((2 * 2 * tm * ROW_TILE, LANES), U32), pltpu.SemaphoreType.DMA((2,))],
        compiler_params=_cparams(("arbitrary",)),
        name="moe_combine_deepnorm",
    )(dest3, dest3, ys8, x, wts, g.reshape(1, d), b.reshape(1, d))


def _moe_plan(eidx, rank, counts, n_tok, tm):
    n_tiles_max = n_tok * MOE_TOPK // MOE_TILE + MOE_EXPERTS
    tiles_per = (counts + MOE_TILE - 1) // MOE_TILE
    tile_end = jnp.cumsum(tiles_per)
    pad_off = (tile_end - tiles_per) * MOE_TILE
    onehot = eidx[:, :, None] == jnp.arange(MOE_EXPERTS, dtype=I32)[None, None, :]
    dest = jnp.sum(jnp.where(onehot, pad_off[None, None, :], 0), axis=-1) + rank
    tile_expert = jnp.minimum(
        jnp.sum((jnp.arange(n_tiles_max, dtype=I32)[:, None] >= tile_end[None, :]).astype(I32), axis=1),
        MOE_EXPERTS - 1)
    dest = dest.reshape(n_tok // tm, tm, MOE_TOPK).transpose(0, 2, 1).reshape(n_tok // tm, MOE_TOPK * tm)
    experts = jnp.arange(MOE_EXPERTS, dtype=I32)
    nonempty = tiles_per > 0
    slot_e = (jnp.cumsum(nonempty.astype(I32)) - 1) % 2
    later = nonempty[None, :] & (experts[None, :] > experts[:, None])
    nxt_e = jnp.min(jnp.where(later, experts[None, :], MOE_EXPERTS), axis=1)
    nxt_e = jnp.where(nxt_e == MOE_EXPERTS, -1, nxt_e)
    tiles = jnp.arange(n_tiles_max, dtype=I32)
    tile_oh = tile_expert[:, None] == experts[None, :]
    first = jnp.any((tiles[:, None] == (tile_end - tiles_per)[None, :]) & nonempty[None, :], axis=1).astype(I32)
    slot = jnp.sum(jnp.where(tile_oh, slot_e[None, :], 0), axis=1).astype(I32)
    nxt = jnp.sum(jnp.where(tile_oh, nxt_e[None, :], 0), axis=1).astype(I32)
    ffn_plan = (tile_expert, tile_end[-1:].astype(I32), first, slot, nxt)
    return ffn_plan, dest, n_tiles_max * MOE_TILE


def hier_moe_deepnorm(x, x8, w_all, b_all, w1, w3, w2, g, b, *, layer, lead_rows=None):
    n_tok = x.shape[0]
    wts, ei, cnt = moe_router(x, w_all, b_all, tm=640)
    counts = cnt[0, :MOE_EXPERTS].astype(I32)
    ffn_plan, dest, n_slots = _moe_plan(ei[:, 0:2], ei[:, 2:4], counts, n_tok, TOKEN_TILE)
    xs8 = moe_dispatch(x8, dest, n_slots, tm=TOKEN_TILE)
    ys8 = moe_ffn(xs8, ffn_plan, w1, w3, w2, layer=layer)
    return moe_combine_deepnorm(ys8, dest, x, wts, g, b, tm=TOKEN_TILE, lead_rows=lead_rows)


def kernel(x_prompt, x_sample, cache_moba_k, cache_moba_v, state_conv, state_ret, cache_mem_k, cache_mem_v,
           page_table, mem_prompt, rel_bias, ev_w_in, ev_conv_w, ev_conv_b, ev_conv_ln_g, ev_conv_ln_b,
           ev_w_out, od_w_in, od_gn_g, od_gn_b, od_w_out, mem_wq, mem_wk, mem_wv, mem_wo, ln_g, ln_b,
           moe_w_group, moe_b_group, moe_w_router, moe_b_router, moe_w1, moe_w3, moe_w2):
    bp, tp, d = x_prompt.shape
    bs, ts, _ = x_sample.shape
    n_p = bp * tp
    n_s = bs * ts
    n_all = n_p + n_s
    page = cache_moba_k.shape[1]
    past_len = page_table.shape[1] * page
    nh, hd = MOBA_HEADS, MOBA_HEAD_DIM
    assert tp % MOBA_BLOCK == 0 and past_len % MOBA_BLOCK == 0 and ts <= MOBA_BLOCK
    assert MOBA_BLOCK >= MAX_DISTANCE
    tm = 1280
    assert n_all % tm == 0

    x = jnp.concatenate([x_prompt.reshape(n_p, d), x_sample.reshape(n_s, d)], axis=0)

    tab_heads = rel_bias.T
    cache_kt = cache_moba_k.transpose(0, 2, 3, 4, 1)
    cache_vt = cache_moba_v.transpose(0, 2, 3, 4, 1)

    pos_p = jnp.arange(tp, dtype=I32)
    pos_s = past_len + jnp.arange(ts, dtype=I32)
    cos_p, sin_p = _rotary_tables(pos_p)
    cos_s, sin_s = _rotary_tables(pos_s)
    ret_c_p = _retention_consts(math.gcd(tp, RET_CHUNK))
    ret_c_s = _retention_consts(math.gcd(ts, RET_CHUNK))

    outs = {}
    for layer in range(DEPTH):
        if layer % 2 == 0:
            e = layer // 2
            proj = matmul(x, ev_w_in[e].astype(BF16), tm=tm, tn=ev_w_in.shape[2])
            k_new = proj[:, 2 * CONV_CH + MOBA_WIDTH:2 * CONV_CH + 2 * MOBA_WIDTH]
            v_new = proj[:, 2 * CONV_CH + 2 * MOBA_WIDTH:]
            q_s = proj[n_p:, 2 * CONV_CH:2 * CONV_CH + MOBA_WIDTH]
            conv_args = (ev_conv_w[e], ev_conv_b[e], ev_conv_ln_g[e], ev_conv_ln_b[e])
            c_p, cst_p = conv_module(proj, jnp.zeros((bp, CONV_WIDTH - 1, CONV_CH), F32), *conv_args,
                                     row0=0, n_batch=bp, t_len=tp, nb=1, out_dtype=BF16)
            c_s, cst_s = conv_module(proj, state_conv[e], *conv_args,
                                     row0=n_p, n_batch=bs, t_len=ts, nb=bs, out_dtype=F32)
            a_p = moba_prompt(proj, tab_heads, n_batch=bp, t_len=tp, out_dtype=BF16)

            def by_head(a):
                return a.reshape(bs, ts, nh, hd).transpose(0, 2, 1, 3)

            a_s = moba_sample(page_table, cache_kt, cache_vt, by_head(q_s), by_head(k_new[n_p:]),
                              by_head(v_new[n_p:]), tab_heads, layer_e=e, t_len=ts)
            a_s = a_s.reshape(bs, nh, ts, hd).transpose(0, 2, 1, 3).reshape(n_s, MOBA_WIDTH).astype(BF16)
            w_out = ev_w_out[e].astype(BF16)
            x = matmul_deepnorm([(c_p, c_s.astype(BF16)), (a_p, a_s)], [w_out[:CONV_CH], w_out[CONV_CH:]], x,
                                ln_g[layer, 0], ln_b[layer, 0], tm=tm)
            outs.setdefault("kp", []).append(k_new[:n_p].reshape(bp, tp, nh, hd))
            outs.setdefault("vp", []).append(v_new[:n_p].reshape(bp, tp, nh, hd))
            outs.setdefault("ks", []).append(k_new[n_p:].reshape(bs, ts, nh, hd))
            outs.setdefault("vs", []).append(v_new[n_p:].reshape(bs, ts, nh, hd))
            outs.setdefault("cp", []).append(cst_p)
            outs.setdefault("cs", []).append(cst_s)
        else:
            o = layer // 2
            proj = matmul(x, od_w_in[o].astype(BF16), tm=tm, tn=2048)
            r_p, st_p = retention(proj, cos_p, sin_p, ret_c_p, od_gn_g[o], od_gn_b[o], None,
                                  row0=0, n_batch=bp, t_len=tp, nb=1, out_dtype=BF16)
            r_s, st_s = retention(proj, cos_s, sin_s, ret_c_s, od_gn_g[o], od_gn_b[o], state_ret[o],
                                  row0=n_p, n_batch=bs, t_len=ts, nb=2, out_dtype=BF16)
            x = matmul_deepnorm([(r_p, r_s)], [od_w_out[o].astype(BF16)], x, ln_g[layer, 0], ln_b[layer, 0], tm=tm)
            outs.setdefault("sp", []).append(st_p)
            outs.setdefault("ss", []).append(st_s)

        m_tok = mem_prompt.shape[1]
        mem2 = mem_prompt.reshape(bp * m_tok, d)
        mk_p = matmul(mem2, mem_wk[layer].astype(BF16), tm=512, tn=d)
        mv_p = matmul(mem2, mem_wv[layer].astype(BF16), tm=512, tn=d)
        outs.setdefault("mk", []).append(mk_p.reshape(bp, m_tok, MEM_HEADS, MEM_HEAD_DIM))
        outs.setdefault("mv", []).append(mv_p.reshape(bp, m_tok, MEM_HEADS, MEM_HEAD_DIM))
        q = matmul(x, mem_wq[layer].astype(BF16), tm=tm, tn=d, out_dtype=BF16)
        o_p = mem_attention(q, mk_p.reshape(1, bp, m_tok, d), mv_p.reshape(1, bp, m_tok, d),
                            layer=0, row0=0, n_batch=bp, t_len=tp, nb=1, tq=512, out_dtype=BF16)
        qt_s = q[n_p:].astype(F32).reshape(bs, ts, MEM_HEADS, MEM_HEAD_DIM).transpose(0, 3, 2, 1)
        o_s = mem_attention_cache(qt_s.reshape(bs, MEM_HEAD_DIM, MEM_HEADS * ts), cache_mem_k, cache_mem_v,
                                  layer=layer, nb=2, tq=ts)
        o_s = o_s.reshape(bs, MEM_HEAD_DIM, MEM_HEADS, ts).transpose(0, 3, 2, 1).reshape(n_s, d).astype(BF16)
        x, x8 = matmul_deepnorm([(o_p, o_s)], [mem_wo[layer].astype(BF16)], x, ln_g[layer, 1], ln_b[layer, 1], tm=tm,
                                token_tiles=True)

        w_all = jnp.zeros((d, LANES), F32)
        w_all = w_all.at[:, :MOE_GROUPS].set(moe_w_group[layer])
        w_all = w_all.at[:, MOE_GROUPS:MOE_GROUPS + MOE_EXPERTS].set(moe_w_router[layer])
        b_all = jnp.zeros((1, LANES), F32)
        b_all = b_all.at[0, :MOE_GROUPS].set(moe_b_group[layer])
        b_all = b_all.at[0, MOE_GROUPS:MOE_GROUPS + MOE_EXPERTS].set(moe_b_router[layer])
        x = hier_moe_deepnorm(x, x8, w_all, b_all, moe_w1, moe_w3, moe_w2, ln_g[layer, 2], ln_b[layer, 2],
                              layer=layer, lead_rows=n_p if layer == DEPTH - 1 else None)

    y_prompt = x[0].reshape(bp, tp, d)
    y_sample = x[1].reshape(bs, ts, d)
    return (y_prompt, y_sample,
            jnp.stack(outs["kp"], axis=2), jnp.stack(outs["vp"], axis=2),
            jnp.stack(outs["ks"], axis=2), jnp.stack(outs["vs"], axis=2),
            jnp.stack(outs["cp"], axis=0), jnp.stack(outs["cs"], axis=0),
            jnp.stack(outs["sp"], axis=0), jnp.stack(outs["ss"], axis=0),
            jnp.stack(outs["mk"], axis=0), jnp.stack(outs["mv"], axis=0))
```

```python
import functools
import math

import numpy as np
import jax
import jax.numpy as jnp
from jax import lax
from jax.experimental import pallas as pl
from jax.experimental.pallas import tpu as pltpu

F32 = jnp.float32
BF16 = jnp.bfloat16
I32 = jnp.int32

D_MODEL = 1024
DEPTH = 2
CONV_CH = 512
CONV_WIDTH = 31
MOBA_HEADS = 8
MOBA_HEAD_DIM = 64
MOBA_WIDTH = 512
MOBA_BLOCK = 256
MOBA_TOPK = 3
NUM_BUCKETS = 32
MAX_DISTANCE = 128
RET_HEADS = 4
RET_DK = 256
RET_DV = 512
RET_CHUNK = 128
MEM_HEADS = 4
MEM_HEAD_DIM = 256
MOE_GROUPS = 4
MOE_EPG = 8
MOE_EXPERTS = 32
MOE_TOPK = 2
MOE_D_FF = 512
DEEPNORM_ALPHA = (2 * DEPTH) ** 0.25
LN_EPS = 1e-5

LANES = 128
VMEM_LIMIT = 56 * 1024 * 1024
MOE_TILE = 384
TOKEN_TILE = 256
DMA_UNROLL = 8
NEG_INF = float("-inf")


def _cparams(sem):
    return pltpu.CompilerParams(dimension_semantics=sem, vmem_limit_bytes=VMEM_LIMIT)


def _layer_norm(y, g, b):
    mu = jnp.mean(y, axis=-1, keepdims=True)
    yc = y - mu
    var = jnp.mean(yc * yc, axis=-1, keepdims=True)
    return yc * lax.rsqrt(var + LN_EPS) * g + b


def _silu(x):
    return x * (1.0 / (1.0 + jnp.exp(-x)))


def _bdot(a, b):
    return jnp.dot(a.astype(BF16), b.astype(BF16), preferred_element_type=F32)


def _bdot_nt(a, b):
    return lax.dot_general(a.astype(BF16), b.astype(BF16), (((1,), (1,)), ((), ())),
                           preferred_element_type=F32)


def _bdot_tn(a, b):
    return lax.dot_general(a.astype(BF16), b.astype(BF16), (((0,), (0,)), ((), ())),
                           preferred_element_type=F32)


def _split3(x):
    hi = x.astype(BF16)
    lo = (x - hi.astype(F32)).astype(BF16)
    return hi, lo


def _dot_hi(a, b, dims=(((1,), (0,)), ((), ()))):
    ah, al = _split3(a)
    bh, bl = _split3(b)
    dg = functools.partial(lax.dot_general, dimension_numbers=dims, preferred_element_type=F32)
    return dg(ah, bh) + (dg(al, bh) + dg(ah, bl))


U32 = jnp.uint32
ROW_TILE = D_MODEL // (2 * LANES)
HALF = D_MODEL // 2


def _bf16_bits(v):
    b = pltpu.bitcast(v, U32)
    return b + (jnp.uint32(0x7FFF) + ((b >> 16) & jnp.uint32(1)))


def _load_token_tiles(ref, start, n_tok):
    u = jnp.concatenate([ref[pl.ds(start + c, n_tok, stride=ROW_TILE), :] for c in range(ROW_TILE)], axis=1)
    lo = pltpu.bitcast(u << 16, F32)
    hi = pltpu.bitcast(u & jnp.uint32(0xFFFF0000), F32)
    return jnp.concatenate([lo, hi], axis=1)


def _store_token_tiles(ref, val):
    n_tok = val.shape[0]
    packed = (_bf16_bits(val[:, :HALF]) >> 16) | (_bf16_bits(val[:, HALF:]) & jnp.uint32(0xFFFF0000))
    for c in range(ROW_TILE):
        ref[pl.ds(c, n_tok, stride=ROW_TILE), :] = packed[:, c * LANES:(c + 1) * LANES]


def _tile_copy(src_hbm, src_tok, dst, dst_tok, sem):
    return pltpu.make_async_copy(src_hbm.at[pl.ds(src_tok * ROW_TILE, ROW_TILE), :],
                                 dst.at[pl.ds(dst_tok * ROW_TILE, ROW_TILE), :], sem)


def _wait_tiles(src_hbm, dst, dst_tok, sem, n_tok):
    pltpu.make_async_copy(src_hbm.at[pl.ds(0, n_tok * ROW_TILE), :],
                          dst.at[pl.ds(dst_tok * ROW_TILE, n_tok * ROW_TILE), :], sem).wait()


def _with_tail(a, t_ref, last):
    return jnp.where(last, jnp.concatenate([a[:a.shape[0] - t_ref.shape[0]], t_ref[...]], axis=0), a)


def _mm_kernel(*refs, has_tail):
    if has_tail:
        x_ref, t_ref, w_ref, o_ref = refs
        a = _with_tail(x_ref[...], t_ref, pl.program_id(0) == pl.num_programs(0) - 1)
    else:
        x_ref, w_ref, o_ref = refs
        a = x_ref[...]
    o_ref[...] = _bdot(a, w_ref[...]).astype(o_ref.dtype)


def matmul(x, w, *, tm, tn, out_dtype=F32):
    tail = None
    if isinstance(x, tuple):
        x, tail = x
    k = x.shape[1]
    m = x.shape[0] + (0 if tail is None else tail.shape[0])
    n = w.shape[1]
    assert m % tm == 0 and n % tn == 0
    in_specs = [pl.BlockSpec((tm, k), lambda i, j: (i, 0))]
    args = [x]
    if tail is not None:
        assert 0 < tail.shape[0] < tm and x.shape[0] // tm == m // tm - 1
        in_specs.append(pl.BlockSpec(tail.shape, lambda i, j: (0, 0)))
        args.append(tail)
    return pl.pallas_call(
        functools.partial(_mm_kernel, has_tail=tail is not None),
        out_shape=jax.ShapeDtypeStruct((m, n), out_dtype),
        grid=(m // tm, n // tn),
        in_specs=in_specs + [pl.BlockSpec((k, tn), lambda i, j: (0, j))],
        out_specs=pl.BlockSpec((tm, tn), lambda i, j: (i, j)),
        compiler_params=_cparams(("parallel", "parallel")),
        name="matmul",
    )(*args, w)


def _mm_dn_kernel(*refs, n_lhs, token_tiles, tail_rows, res_tail):
    lhs = refs[:2 * n_lhs:2]
    tails = refs[1:2 * n_lhs:2]
    ws = refs[2 * n_lhs:3 * n_lhs]
    refs = refs[3 * n_lhs:]
    last = pl.program_id(0) == pl.num_programs(0) - 1
    if res_tail:
        res = _with_tail(refs[0][...], refs[1], last)
        refs = refs[2:]
    else:
        res = refs[0][...]
        refs = refs[1:]
    g_ref, b_ref, o_ref = refs[:3]

    def rows(a_ref, t_ref):
        a = a_ref[...]
        if tail_rows:
            a = jnp.where(last, jnp.concatenate([a[:a.shape[0] - tail_rows], t_ref[...]], axis=0), a)
        return a

    acc = _bdot(rows(lhs[0], tails[0]), ws[0][...])
    for a, t, w in zip(lhs[1:], tails[1:], ws[1:]):
        acc = acc + _bdot(rows(a, t), w[...])
    y = DEEPNORM_ALPHA * res + acc
    out = _layer_norm(y, g_ref[...], b_ref[...])
    o_ref[...] = out
    if token_tiles:
        _store_token_tiles(refs[3], out)


def matmul_deepnorm(lhs_list, w_list, res, g, b, *, tm, token_tiles=False):
    res_tail = None
    if isinstance(res, tuple):
        res, res_tail = res
    d = res.shape[1]
    m = res.shape[0] + (0 if res_tail is None else res_tail.shape[0])
    n_lhs = len(lhs_list)
    tail_rows = lhs_list[0][1].shape[0]
    assert res_tail is None or res_tail.shape[0] == tail_rows
    for a, t in lhs_list:
        assert a.shape[0] + t.shape[0] == m and t.shape[0] == tail_rows and a.shape[1] == t.shape[1]
        assert 0 < tail_rows < tm and (m - tail_rows) // tm == m // tm - 1
    flat_lhs = [x for pair in lhs_list for x in pair]
    in_specs = ([spec for a, t in lhs_list
                 for spec in (pl.BlockSpec((tm, a.shape[1]), lambda i: (i, 0)),
                              pl.BlockSpec(t.shape, lambda i: (0, 0)))]
                + [pl.BlockSpec(w.shape, lambda i: (0, 0)) for w in w_list]
                + [pl.BlockSpec((tm, d), lambda i: (i, 0))]
                + ([] if res_tail is None else [pl.BlockSpec(res_tail.shape, lambda i: (0, 0))])
                + [pl.BlockSpec((1, d), lambda i: (0, 0)),
                   pl.BlockSpec((1, d), lambda i: (0, 0))])
    res_args = [res] if res_tail is None else [res, res_tail]
    out_shape = [jax.ShapeDtypeStruct((m, d), F32)]
    out_specs = [pl.BlockSpec((tm, d), lambda i: (i, 0))]
    if token_tiles:
        out_shape.append(jax.ShapeDtypeStruct((m * ROW_TILE, LANES), U32))
        out_specs.append(pl.BlockSpec((tm * ROW_TILE, LANES), lambda i: (i, 0)))
    res_out = pl.pallas_call(
        functools.partial(_mm_dn_kernel, n_lhs=n_lhs, token_tiles=token_tiles, tail_rows=tail_rows,
                          res_tail=res_tail is not None),
        out_shape=tuple(out_shape),
        grid=(m // tm,),
        in_specs=in_specs,
        out_specs=tuple(out_specs),
        compiler_params=_cparams(("parallel",)),
        name="matmul_deepnorm",
    )(*flat_lhs, *w_list, *res_args, g.reshape(1, d), b.reshape(1, d))
    return res_out if token_tiles else res_out[0]


CONV_PAD = 32


def _conv_kernel(p_ref, hist_ref, w_ref, cb_ref, g_ref, b_ref, c_ref, st_ref, u_scr, sh_scr, *, nb, t_len, rc):
    hw = CONV_WIDTH - 1
    w = w_ref[...]
    for bb in range(nb):
        rows = pl.ds(bb * t_len, t_len)
        a = p_ref[rows, 0:CONV_CH]
        gt = p_ref[rows, CONV_CH:2 * CONV_CH]
        u = a * (1.0 / (1.0 + jnp.exp(-gt)))
        u_scr[0:CONV_PAD, :] = jnp.concatenate(
            [jnp.zeros((CONV_PAD - hw, CONV_CH), F32), hist_ref[bb]], axis=0)
        u_scr[CONV_PAD:CONV_PAD + t_len, :] = u
        st_ref[bb] = u_scr[CONV_PAD + t_len - hw:CONV_PAD + t_len, :]

        def chunk(ci, carry):
            r0 = pl.multiple_of(ci * rc, rc)
            span = rc + CONV_PAD - 8
            for r in range(1, 8):
                sh_scr[r - 1] = u_scr[pl.ds(r0, rc + CONV_PAD), :][r:r + span, :]
            acc = jnp.broadcast_to(cb_ref[...], (rc, CONV_CH))
            for j in range(CONV_WIDTH):
                off = j + CONV_PAD - hw
                r = off % 8
                if r == 0:
                    tap = u_scr[pl.ds(r0 + off, rc), :]
                else:
                    tap = sh_scr[r - 1, off - r:off - r + rc, :]
                acc = acc + w[j:j + 1, :] * tap
            y = _silu(_layer_norm(acc, g_ref[...], b_ref[...]))
            c_ref[pl.ds(bb * t_len + r0, rc), :] = y.astype(c_ref.dtype)
            return carry

        n_chunks = t_len // rc
        if n_chunks == 1:
            chunk(0, 0)
        else:
            lax.fori_loop(0, n_chunks, chunk, 0)


def conv_module(p, hist, conv_w, conv_b, ln_g, ln_b, *, row0, n_batch, t_len, nb, out_dtype):
    rc = min(256, t_len)
    blk_rows = nb * t_len
    assert row0 % blk_rows == 0 and n_batch % nb == 0
    base = row0 // blk_rows
    hw = CONV_WIDTH - 1
    c, st = pl.pallas_call(
        functools.partial(_conv_kernel, nb=nb, t_len=t_len, rc=rc),
        out_shape=(jax.ShapeDtypeStruct((n_batch * t_len, CONV_CH), out_dtype),
                   jax.ShapeDtypeStruct((n_batch, hw, CONV_CH), F32)),
        grid=(n_batch // nb,),
        in_specs=[pl.BlockSpec((blk_rows, 2 * CONV_CH), lambda i: (base + i, 0)),
                  pl.BlockSpec((nb, hw, CONV_CH), lambda i: (i, 0, 0)),
                  pl.BlockSpec((CONV_WIDTH, CONV_CH), lambda i: (0, 0)),
                  pl.BlockSpec((1, CONV_CH), lambda i: (0, 0)),
                  pl.BlockSpec((1, CONV_CH), lambda i: (0, 0)),
                  pl.BlockSpec((1, CONV_CH), lambda i: (0, 0))],
        out_specs=(pl.BlockSpec((blk_rows, CONV_CH), lambda i: (i, 0)),
                   pl.BlockSpec((nb, hw, CONV_CH), lambda i: (i, 0, 0))),
        scratch_shapes=[pltpu.VMEM((t_len + CONV_PAD, CONV_CH), F32),
                        pltpu.VMEM((7, rc + CONV_PAD - 8, CONV_CH), F32)],
        compiler_params=_cparams(("parallel",)),
        name="conv_module",
    )(p, hist, conv_w, conv_b.reshape(1, -1), ln_g.reshape(1, -1), ln_b.reshape(1, -1))
    return c, st


def _bucket_thresholds():
    max_exact = NUM_BUCKETS // 2
    d = np.arange(0, MAX_DISTANCE + 1)
    val = (np.log(np.maximum(d, 1).astype(np.float32) / np.float32(max_exact))
           / np.float32(math.log(MAX_DISTANCE / max_exact)) * np.float32(NUM_BUCKETS - max_exact))
    inner = (d > max_exact) & (d < MAX_DISTANCE)
    assert np.all(np.abs(val[inner] - np.round(val[inner])) > 1e-3)
    bucket = np.where(d < max_exact, d, np.minimum(max_exact + val.astype(np.int32), NUM_BUCKETS - 1))
    return [int(np.argmax(bucket >= k)) for k in range(1, NUM_BUCKETS)]


def _bias_chain(dist, tab_at, thr):
    b = jnp.where(dist >= thr[0], tab_at(1), tab_at(0))
    for k in range(2, NUM_BUCKETS):
        b = jnp.where(dist >= thr[k - 1], tab_at(k), b)
    return b


def _moba_prompt_kernel(tab_ref, q_ref, k_ref, v_ref, o_ref,
                        kmean_scr, kbf_scr, vt_scr, d0_scr, d1_scr, far_scr, sel_scr, qb_scr, m_scr, l_scr, acc_scr,
                        *, n_blk, thr):
    b = pl.program_id(0)
    i = pl.program_id(1)
    blk = MOBA_BLOCK
    hd = MOBA_HEAD_DIM
    n_pair = MOBA_HEADS // 2
    scale = hd ** -0.5
    nt_dims = (((1,), (1,)), ((), ()))
    key = lax.broadcasted_iota(I32, (blk, blk), 0)
    qry = lax.broadcasted_iota(I32, (blk, blk), 1)
    causal = jnp.concatenate([key <= qry, key <= qry], axis=1)

    @pl.when((b == 0) & (i == 0))
    def _():
        for h in range(MOBA_HEADS):
            tab_at = functools.partial(lambda k, hh: tab_ref[hh, k], hh=h)
            half = slice((h % 2) * blk, (h % 2 + 1) * blk)
            d0_scr[h // 2, :, half] = _bias_chain(qry - key, tab_at, thr)
            d1_scr[h // 2, :, half] = _bias_chain(blk + qry - key, tab_at, thr)
            far_scr[h // 2, :, half] = jnp.full((1, blk), tab_ref[h, NUM_BUCKETS - 1], F32)

    @pl.when(i == 0)
    def _():
        for n in range(n_blk):
            kb = k_ref[n * blk:(n + 1) * blk, :]
            kmean_scr[n:n + 1, :] = jnp.sum(kb, axis=0, keepdims=True) * (1.0 / blk)
            kbf_scr[n * blk:(n + 1) * blk, :] = kb.astype(BF16)
            for pair in range(n_pair):
                vt_scr[n, pair * LANES:(pair + 1) * LANES, :] = (
                    v_ref[n * blk:(n + 1) * blk, pair * LANES:(pair + 1) * LANES].T.astype(BF16))

    q = q_ref[...]
    lane = lax.broadcasted_iota(I32, (blk, LANES), 1)
    rown = lax.broadcasted_iota(I32, (n_blk, 2 * blk), 0)
    rowd = lax.broadcasted_iota(I32, (LANES, blk), 0)
    r0 = pl.multiple_of(i * blk, blk)
    pairs = [(pr, slice(pr * LANES, (pr + 1) * LANES)) for pr in range(n_pair)]

    for pr, cols in pairs:
        qm = jnp.concatenate([jnp.where((lane >= sub * hd) & (lane < (sub + 1) * hd), q[:, cols], 0.0)
                              for sub in range(2)], axis=0)
        gate = _dot_hi(kmean_scr[:, cols], qm, nt_dims)
        rank = jnp.zeros((n_blk, 2 * blk), F32)
        for m in range(n_blk):
            gm = gate[m:m + 1, :]
            tie = jnp.where(rown > m, 1.0, 0.0)
            cnt = jnp.where(gm > gate, 1.0, jnp.where(gm == gate, tie, 0.0))
            rank = rank + jnp.where(m < i, cnt, 0.0)
        sel_scr[pr] = jnp.where((rown < i) & (rank < MOBA_TOPK), 1.0, 0.0)
        qb = (qm * scale).astype(BF16)
        qb_scr[pr] = qb
        s = lax.dot_general(kbf_scr[pl.ds(r0, blk), cols], qb, nt_dims,
                            preferred_element_type=F32) + d0_scr[pr]
        s = jnp.where(causal, s, NEG_INF)
        m_run = jnp.max(s, axis=0, keepdims=True)
        p = jnp.exp(s - m_run)
        m_scr[pr] = m_run
        l_scr[pr] = jnp.sum(p, axis=0, keepdims=True)
        acc_scr[pr] = jnp.dot(vt_scr[i, cols, :], p.astype(BF16), preferred_element_type=F32)

    def merge_block(n, pr, cols, s, shift):
        picked = sel_scr[pr, pl.ds(n, 1), :] > 0.5
        m_run = m_scr[pr]
        m_new = jnp.maximum(m_run, jnp.where(picked, jnp.max(s, axis=0, keepdims=True) + shift, NEG_INF))
        alpha = jnp.exp(m_run - m_new)
        p = jnp.exp(s - jnp.where(picked, m_new - shift, jnp.inf))
        m_scr[pr] = m_new
        l_scr[pr] = alpha * l_scr[pr] + jnp.sum(p, axis=0, keepdims=True)
        acc_scr[pr] = alpha * acc_scr[pr] + jnp.dot(vt_scr[n, cols, :], p.astype(BF16), preferred_element_type=F32)

    @pl.when(i > 0)
    def _():
        rr = pl.multiple_of((i - 1) * blk, blk)
        for pr, cols in pairs:
            s = lax.dot_general(kbf_scr[pl.ds(rr, blk), cols], qb_scr[pr], nt_dims,
                                preferred_element_type=F32) + d1_scr[pr]
            merge_block(i - 1, pr, cols, s, 0.0)

    def far_scores(n, pr, cols):
        rr = pl.multiple_of(n * blk, blk)
        return lax.dot_general(kbf_scr[pl.ds(rr, blk), cols], qb_scr[pr], nt_dims, preferred_element_type=F32)

    def merge_two(n0, n1, pr, cols, s0, s1, shift):
        pk0 = sel_scr[pr, pl.ds(n0, 1), :] > 0.5
        pk1 = sel_scr[pr, pl.ds(n1, 1), :] > 0.5
        m_run = m_scr[pr]
        bm0 = jnp.where(pk0, jnp.max(s0, axis=0, keepdims=True) + shift, NEG_INF)
        bm1 = jnp.where(pk1, jnp.max(s1, axis=0, keepdims=True) + shift, NEG_INF)
        m_new = jnp.maximum(m_run, jnp.maximum(bm0, bm1))
        alpha = jnp.exp(m_run - m_new)
        p0 = jnp.exp(s0 - jnp.where(pk0, m_new - shift, jnp.inf))
        p1 = jnp.exp(s1 - jnp.where(pk1, m_new - shift, jnp.inf))
        m_scr[pr] = m_new
        l_scr[pr] = (alpha * l_scr[pr] + jnp.sum(p0, axis=0, keepdims=True)) + jnp.sum(p1, axis=0, keepdims=True)
        acc_scr[pr] = (alpha * acc_scr[pr]
                       + jnp.dot(vt_scr[n0, cols, :], p0.astype(BF16), preferred_element_type=F32)
                       + jnp.dot(vt_scr[n1, cols, :], p1.astype(BF16), preferred_element_type=F32))

    n_far = jnp.maximum(i - 1, 0)

    def body(n2, carry):
        for pr, cols in pairs:
            merge_two(2 * n2, 2 * n2 + 1, pr, cols, far_scores(2 * n2, pr, cols),
                      far_scores(2 * n2 + 1, pr, cols), far_scr[pr])
        return carry

    lax.fori_loop(0, n_far // 2, body, 0)

    @pl.when(n_far % 2 == 1)
    def _():
        for pr, cols in pairs:
            merge_block(n_far - 1, pr, cols, far_scores(n_far - 1, pr, cols), far_scr[pr])

    outs = []
    for pr, _ in pairs:
        o = acc_scr[pr] / l_scr[pr]
        outs.append(jnp.where(rowd < hd, o[:, :blk], o[:, blk:]).T)
    o_ref[...] = jnp.concatenate(outs, axis=1).astype(o_ref.dtype)


def moba_prompt(p, tab, *, n_batch, t_len, out_dtype):
    n_blk = t_len // MOBA_BLOCK
    n_pair = MOBA_HEADS // 2
    wide = 2 * MOBA_BLOCK
    return pl.pallas_call(
        functools.partial(_moba_prompt_kernel, n_blk=n_blk, thr=_bucket_thresholds()),
        out_shape=jax.ShapeDtypeStruct((n_batch * t_len, MOBA_WIDTH), out_dtype),
        grid=(n_batch, n_blk),
        in_specs=[pl.BlockSpec(memory_space=pltpu.SMEM),
                  pl.BlockSpec((MOBA_BLOCK, MOBA_WIDTH), lambda b, i: (b * n_blk + i, 2)),
                  pl.BlockSpec((t_len, MOBA_WIDTH), lambda b, i: (b, 3)),
                  pl.BlockSpec((t_len, MOBA_WIDTH), lambda b, i: (b, 4))],
        out_specs=pl.BlockSpec((MOBA_BLOCK, MOBA_WIDTH), lambda b, i: (b * n_blk + i, 0)),
        scratch_shapes=[pltpu.VMEM((n_blk, MOBA_WIDTH), F32),
                        pltpu.VMEM((t_len, MOBA_WIDTH), BF16),
                        pltpu.VMEM((n_blk, MOBA_WIDTH, MOBA_BLOCK), BF16),
                        pltpu.VMEM((n_pair, MOBA_BLOCK, wide), F32),
                        pltpu.VMEM((n_pair, MOBA_BLOCK, wide), F32),
                        pltpu.VMEM((n_pair, 1, wide), F32),
                        pltpu.VMEM((n_pair, n_blk, wide), F32),
                        pltpu.VMEM((n_pair, wide, LANES), BF16),
                        pltpu.VMEM((n_pair, 1, wide), F32),
                        pltpu.VMEM((n_pair, 1, wide), F32),
                        pltpu.VMEM((n_pair, LANES, wide), F32)],
        compiler_params=_cparams(("arbitrary", "arbitrary")),
        name="moba_prompt",
    )(tab, p, p, p)


def _moba_sample_kernel(pt_ref, *refs, n_blk, nbs, page, t_len, thr):
    npg = 2 * nbs
    k_refs = refs[:npg]
    v_refs = refs[npg:2 * npg]
    q_ref, qbd_ref, kn_ref, vn_ref, tab_ref, o_ref, g_scr, m_scr, l_scr, o_scr, bl_scr = refs[2 * npg:]
    n = pl.program_id(1)
    nh, hd = MOBA_HEADS, MOBA_HEAD_DIM
    nrow = nh * t_len
    scale = hd ** -0.5
    nt_dims = (((1,), (1,)), ((), ()))
    lane = lax.broadcasted_iota(I32, (nrow, LANES), 1)
    row = lax.broadcasted_iota(I32, (nrow, LANES), 0)
    tab = tab_ref[...]

    def tab_at(k):
        return tab[:, k:k + 1]

    @pl.when(n == 0)
    def _():
        g_scr[...] = jnp.zeros(g_scr.shape, F32)
        m_scr[...] = jnp.zeros(m_scr.shape, F32)
        l_scr[...] = jnp.zeros(l_scr.shape, F32)
        for half in range(2):
            bl_scr[:, half * page:(half + 1) * page] = _bias_chain(
                MOBA_BLOCK + row % t_len - (half * page + lane), tab_at, thr)

    qbd = (qbd_ref[0] * scale).astype(BF16)
    far = tab_at(NUM_BUCKETS - 1)
    bw = 2 * page
    kt = jnp.concatenate([r[...].reshape(nh * hd, page).astype(BF16) for r in k_refs], axis=1)
    vt = jnp.concatenate([r[...].reshape(nh * hd, page).astype(BF16) for r in v_refs], axis=1)
    s_all = jnp.dot(qbd, kt, preferred_element_type=F32)
    p_rows = []
    stats = []
    for j in range(nbs):
        blk_idx = n * nbs + j
        s_raw = s_all[:, j * bw:(j + 1) * bw]
        gate = jnp.sum(s_raw, axis=1, keepdims=True) * (1.0 / (MOBA_BLOCK * scale))
        s = s_raw + jnp.where(blk_idx == n_blk - 1, bl_scr[...], far)
        m = jnp.max(s, axis=1, keepdims=True)
        p = jnp.exp(s - m)
        l = jnp.sum(p, axis=1, keepdims=True)
        stats.append((blk_idx, gate, m, l))
        zl = jnp.zeros((nrow, j * bw), F32)
        zr = jnp.zeros((nrow, (nbs - 1 - j) * bw), F32)
        p_rows.append(jnp.concatenate([x for x in (zl, p, zr) if x.shape[1]], axis=1))
    p_bd = jnp.concatenate(p_rows, axis=0).astype(BF16)
    pv = lax.dot_general(p_bd, vt, nt_dims, preferred_element_type=F32)
    for j, (blk_idx, gate, m, l) in enumerate(stats):
        at_blk = lane == blk_idx
        g_scr[...] = jnp.where(at_blk, gate, g_scr[...])
        m_scr[...] = jnp.where(at_blk, m, m_scr[...])
        l_scr[...] = jnp.where(at_blk, l, l_scr[...])
        o_scr[blk_idx] = jnp.concatenate(
            [pv[j * nrow + h * t_len:j * nrow + (h + 1) * t_len, h * hd:(h + 1) * hd] for h in range(nh)], axis=0)

    @pl.when(n == n_blk // nbs - 1)
    def _():
        gate = g_scr[...]
        rank = jnp.zeros((nrow, LANES), F32)
        for mm in range(n_blk):
            gm = gate[:, mm:mm + 1]
            tie = jnp.where(lane > mm, 1.0, 0.0)
            rank = rank + jnp.where(gm > gate, 1.0, jnp.where(gm == gate, tie, 0.0))
        sel = (rank < MOBA_TOPK) & (lane < n_blk)

        q = q_ref[0]
        kn = kn_ref[0]
        vn = vn_ref[0]
        tq = lax.broadcasted_iota(I32, (nrow, t_len), 0) % t_len
        sk = lax.broadcasted_iota(I32, (nrow, t_len), 1)
        s_own = jnp.concatenate(
            [lax.dot_general((q[h] * scale).astype(BF16), kn[h].astype(BF16), nt_dims, preferred_element_type=F32)
             for h in range(nh)], axis=0) + _bias_chain(tq - sk, tab_at, thr)
        s_own = jnp.where(sk <= tq, s_own, NEG_INF)
        m_o = jnp.max(s_own, axis=1, keepdims=True)
        p_o = jnp.exp(s_own - m_o)
        l_o = jnp.sum(p_o, axis=1, keepdims=True)
        o_o = jnp.concatenate(
            [jnp.dot(p_o[h * t_len:(h + 1) * t_len].astype(BF16), vn[h].astype(BF16), preferred_element_type=F32)
             for h in range(nh)], axis=0)

        m_sel = jnp.where(sel, m_scr[...], NEG_INF)
        m_fin = jnp.maximum(jnp.max(m_sel, axis=1, keepdims=True), m_o)
        w = jnp.where(sel, jnp.exp(m_sel - m_fin), 0.0)
        w_o = jnp.exp(m_o - m_fin)
        l_fin = jnp.sum(w * l_scr[...], axis=1, keepdims=True) + w_o * l_o
        acc = w_o * o_o
        for nn in range(n_blk):
            acc = acc + w[:, nn:nn + 1] * o_scr[nn]
        o_ref[0] = acc / l_fin


MOBA_SAMPLE_BLOCKS_PER_STEP = 8


def moba_sample(page_table, cache_kt, cache_vt, q, kn, vn, tab_heads, *, layer_e, t_len):
    n_b, n_pages = page_table.shape
    page = cache_kt.shape[-1]
    nbs = MOBA_SAMPLE_BLOCKS_PER_STEP
    ppb = MOBA_BLOCK // page
    assert ppb == 2 and page == LANES and n_pages % (ppb * nbs) == 0
    n_blk = n_pages // ppb
    assert n_blk <= LANES
    nh, hd = MOBA_HEADS, MOBA_HEAD_DIM
    nrow = nh * t_len
    tab = jnp.pad(jnp.repeat(tab_heads, t_len, axis=0), ((0, 0), (0, LANES - NUM_BUCKETS)))
    qbd = (q[:, :, :, None, :] * jnp.eye(nh, dtype=F32)[None, :, None, :, None]).reshape(n_b, nrow, nh * hd)

    def pspec(p):
        return pl.BlockSpec((None, None, nh, hd, page),
                            lambda b, n, pt: (pt[b, n * ppb * nbs + p], layer_e, 0, 0, 0))

    def bspec():
        return pl.BlockSpec((1, nh, t_len, hd), lambda b, n, pt: (b, 0, 0, 0))

    pages = [pspec(p) for p in range(ppb * nbs)]
    grid_spec = pltpu.PrefetchScalarGridSpec(
        num_scalar_prefetch=1,
        grid=(n_b, n_blk // nbs),
        in_specs=pages + pages + [
            bspec(),
            pl.BlockSpec((1, nrow, nh * hd), lambda b, n, pt: (b, 0, 0)),
            bspec(), bspec(),
            pl.BlockSpec(tab.shape, lambda b, n, pt: (0, 0))],
        out_specs=pl.BlockSpec((1, nrow, hd), lambda b, n, pt: (b, 0, 0)),
        scratch_shapes=[pltpu.VMEM((nrow, LANES), F32), pltpu.VMEM((nrow, LANES), F32),
                        pltpu.VMEM((nrow, LANES), F32), pltpu.VMEM((n_blk, nrow, hd), F32),
                        pltpu.VMEM((nrow, MOBA_BLOCK), F32)],
    )
    return pl.pallas_call(
        functools.partial(_moba_sample_kernel, n_blk=n_blk, nbs=nbs, page=page, t_len=t_len,
                          thr=_bucket_thresholds()),
        out_shape=jax.ShapeDtypeStruct((n_b, nrow, hd), F32),
        grid_spec=grid_spec,
        compiler_params=_cparams(("parallel", "arbitrary")),
        name="moba_sample",
    )(page_table, *([cache_kt] * (ppb * nbs)), *([cache_vt] * (ppb * nbs)), q, qbd, kn, vn, tab)


def _retention_kernel(q_ref, k_ref, v_ref, g_ref, cos_ref, sin_ref, dmask_ref, xi_ref, zeta_ref, gc_ref,
                      gng_ref, gnb_ref, *rest, nb, chunk, has_s0):
    if has_s0:
        s0_ref, o_ref, s_out_ref, s_scr = rest
    else:
        o_ref, s_out_ref, s_scr = rest
    c = pl.program_id(1)
    dk, dv = RET_DK, RET_DV
    half = dk // 2

    @pl.when(c == 0)
    def _():
        if has_s0:
            s_scr[...] = s0_ref[...]
        else:
            s_scr[...] = jnp.zeros(s_scr.shape, F32)

    cos = cos_ref[...]
    sin = sin_ref[...]

    def rot(x):
        x1 = x[:, :half]
        x2 = x[:, half:]
        return jnp.concatenate([x1 * cos - x2 * sin, x2 * cos + x1 * sin], axis=1)

    out_rows = []
    for bb in range(nb):
        rows = pl.ds(bb * chunk, chunk)
        out_heads = []
        for h in range(RET_HEADS):
            qh = rot(q_ref[rows, h * dk:(h + 1) * dk])
            kh = rot(k_ref[rows, h * dk:(h + 1) * dk]) * (dk ** -0.5)
            vh = v_ref[rows, h * dv:(h + 1) * dv]
            s = s_scr[bb, h]
            att = _bdot_nt(qh, kh) * dmask_ref[h]
            o = _bdot(att, vh) + _bdot(qh, s) * xi_ref[h]
            s_scr[bb, h] = s * gc_ref[h, 0:1, 0:1] + _bdot_tn(kh * zeta_ref[h], vh)
            mu = jnp.mean(o, axis=-1, keepdims=True)
            oc = o - mu
            var = jnp.mean(oc * oc, axis=-1, keepdims=True)
            on = oc * lax.rsqrt(var + LN_EPS) * gng_ref[:, h * dv:(h + 1) * dv] + gnb_ref[:, h * dv:(h + 1) * dv]
            gate = _silu(g_ref[rows, h * dv:(h + 1) * dv])
            out_heads.append(gate * on)
        out_rows.append(jnp.concatenate(out_heads, axis=1))
    o_ref[...] = jnp.concatenate(out_rows, axis=0).astype(o_ref.dtype)

    @pl.when(c == pl.num_programs(1) - 1)
    def _():
        s_out_ref[...] = s_scr[...]


def retention(p, cos, sin, consts, gn_g, gn_b, s0, *, row0, n_batch, t_len, nb, out_dtype):
    chunk = math.gcd(t_len, RET_CHUNK)
    n_chunk = t_len // chunk
    dmask, xi, zeta, gc = consts
    blk_rows = nb * chunk
    if nb > 1:
        assert n_chunk == 1
    assert row0 % blk_rows == 0
    base = row0 // blk_rows
    qk_w = RET_HEADS * RET_DK
    v_w = RET_HEADS * RET_DV
    has_s0 = s0 is not None

    def rmap(col):
        return lambda b, c: (base + b * n_chunk + c, col)

    in_specs = [pl.BlockSpec((blk_rows, qk_w), rmap(0)),
                pl.BlockSpec((blk_rows, qk_w), rmap(1)),
                pl.BlockSpec((blk_rows, v_w), rmap(1)),
                pl.BlockSpec((blk_rows, v_w), rmap(2)),
                pl.BlockSpec((chunk, RET_DK // 2), lambda b, c: (c, 0)),
                pl.BlockSpec((chunk, RET_DK // 2), lambda b, c: (c, 0)),
                pl.BlockSpec(dmask.shape, lambda b, c: (0, 0, 0)),
                pl.BlockSpec(xi.shape, lambda b, c: (0, 0, 0)),
                pl.BlockSpec(zeta.shape, lambda b, c: (0, 0, 0)),
                pl.BlockSpec(gc.shape, lambda b, c: (0, 0, 0)),
                pl.BlockSpec((1, v_w), lambda b, c: (0, 0)),
                pl.BlockSpec((1, v_w), lambda b, c: (0, 0))]
    args = [p, p, p, p, cos, sin, dmask, xi, zeta, gc, gn_g.reshape(1, -1), gn_b.reshape(1, -1)]
    s_spec = pl.BlockSpec((nb, RET_HEADS, RET_DK, RET_DV), lambda b, c: (b, 0, 0, 0))
    if has_s0:
        in_specs.append(s_spec)
        args.append(s0)
    o, s_out = pl.pallas_call(
        functools.partial(_retention_kernel, nb=nb, chunk=chunk, has_s0=has_s0),
        out_shape=(jax.ShapeDtypeStruct((n_batch * t_len, v_w), out_dtype),
                   jax.ShapeDtypeStruct((n_batch, RET_HEADS, RET_DK, RET_DV), F32)),
        grid=(n_batch // nb, n_chunk),
        in_specs=in_specs,
        out_specs=(pl.BlockSpec((blk_rows, v_w), lambda b, c: (b * n_chunk + c, 0)), s_spec),
        scratch_shapes=[pltpu.VMEM((nb, RET_HEADS, RET_DK, RET_DV), F32)],
        compiler_params=_cparams(("parallel", "arbitrary")),
        name="retention",
    )(*args)
    return o, s_out


def _retention_consts(chunk):
    h = jnp.arange(RET_HEADS, dtype=F32)
    lg = jnp.log1p(-jnp.exp2(-5.0 - h))
    idx = jnp.arange(chunk, dtype=F32)
    diff = idx[:, None] - idx[None, :]
    dmask = jnp.where(diff >= 0, jnp.exp(lg[:, None, None] * jnp.maximum(diff, 0.0)), 0.0)
    xi = jnp.exp(lg[:, None] * (idx[None, :] + 1.0))
    zeta = jnp.exp(lg[:, None] * (chunk - 1.0 - idx[None, :]))
    g_c = jnp.exp(lg * chunk)
    return (dmask,
            jnp.broadcast_to(xi[:, :, None], (RET_HEADS, chunk, RET_DV)),
            jnp.broadcast_to(zeta[:, :, None], (RET_HEADS, chunk, RET_DK)),
            jnp.broadcast_to(g_c[:, None, None], (RET_HEADS, 8, LANES)))


def _rotary_tables(pos):
    half = RET_DK // 2
    inv = 10000.0 ** (-jnp.linspace(0.0, 1.0, half, dtype=F32))
    ang = pos.astype(F32)[:, None] * inv[None, :]
    return jnp.cos(ang), jnp.sin(ang)


def _mem_attn_kernel(q_ref, mk_ref, mv_ref, o_ref, *, nb, tq):
    hd = MEM_HEAD_DIM
    scale = hd ** -0.5
    qf = q_ref[...].astype(F32)
    out_rows = []
    for bb in range(nb):
        out_heads = []
        for h in range(MEM_HEADS):
            cols = slice(h * hd, (h + 1) * hd)
            q = qf[bb * tq:(bb + 1) * tq, cols]
            s = _bdot_nt(q, mk_ref[bb, :, cols]) * scale
            m = jnp.max(s, axis=-1, keepdims=True)
            p = jnp.exp(s - m)
            p = p / jnp.sum(p, axis=-1, keepdims=True)
            out_heads.append(_bdot(p, mv_ref[bb, :, cols]))
        out_rows.append(jnp.concatenate(out_heads, axis=1))
    o_ref[...] = jnp.concatenate(out_rows, axis=0).astype(o_ref.dtype)


def mem_attention(q, mk, mv, *, layer, row0, n_batch, t_len, nb, tq, out_dtype):
    n_t = t_len // tq
    blk_rows = nb * tq
    if nb > 1:
        assert n_t == 1
    assert row0 % blk_rows == 0
    base = row0 // blk_rows
    m_tok = mk.shape[2]
    return pl.pallas_call(
        functools.partial(_mem_attn_kernel, nb=nb, tq=tq),
        out_shape=jax.ShapeDtypeStruct((n_batch * t_len, D_MODEL), out_dtype),
        grid=(n_batch // nb, n_t),
        in_specs=[pl.BlockSpec((blk_rows, D_MODEL), lambda b, t: (base + b * n_t + t, 0)),
                  pl.BlockSpec((None, nb, m_tok, D_MODEL), lambda b, t: (layer, b, 0, 0)),
                  pl.BlockSpec((None, nb, m_tok, D_MODEL), lambda b, t: (layer, b, 0, 0))],
        out_specs=pl.BlockSpec((blk_rows, D_MODEL), lambda b, t: (b * n_t + t, 0)),
        compiler_params=_cparams(("parallel", "arbitrary")),
        name="mem_attention",
    )(q, mk, mv)


def _mem_attn_cache_kernel(qt_ref, mk_ref, mv_ref, o_ref, *, nb, tq):
    nh, hd = MEM_HEADS, MEM_HEAD_DIM
    ncol = nh * tq
    m_tok = mk_ref.shape[1]
    scale = hd ** -0.5
    r = lax.broadcasted_iota(I32, (m_tok * nh, ncol), 0)
    c = lax.broadcasted_iota(I32, (m_tok * nh, ncol), 1)
    match = (r % nh) == (c // tq)
    for bb in range(nb):
        k2 = mk_ref[bb].reshape(m_tok * nh, hd)
        v2 = mv_ref[bb].reshape(m_tok * nh, hd)
        s = jnp.where(match, _bdot(k2, qt_ref[bb]) * scale, NEG_INF)
        m = jnp.max(s, axis=0, keepdims=True)
        p = jnp.exp(s - m)
        p = p / jnp.sum(p, axis=0, keepdims=True)
        o_ref[bb] = _bdot_tn(v2, p)


def mem_attention_cache(qt, mk5, mv5, *, layer, nb, tq):
    n_b = qt.shape[0]
    m_tok, nh, hd = mk5.shape[2:]
    ncol = nh * tq
    return pl.pallas_call(
        functools.partial(_mem_attn_cache_kernel, nb=nb, tq=tq),
        out_shape=jax.ShapeDtypeStruct((n_b, hd, ncol), F32),
        grid=(n_b // nb,),
        in_specs=[pl.BlockSpec((nb, hd, ncol), lambda b: (b, 0, 0)),
                  pl.BlockSpec((None, nb, m_tok, nh, hd), lambda b: (layer, b, 0, 0, 0)),
                  pl.BlockSpec((None, nb, m_tok, nh, hd), lambda b: (layer, b, 0, 0, 0))],
        out_specs=pl.BlockSpec((nb, hd, ncol), lambda b: (b, 0, 0)),
        compiler_params=_cparams(("parallel",)),
        name="mem_attention_cache",
    )(qt, mk5, mv5)


def _router_kernel(x_ref, w_ref, b_ref, tri_ref, wt_ref, ei_ref, cnt_ref, base_scr):
    @pl.when(pl.program_id(0) == 0)
    def _():
        base_scr[...] = jnp.zeros(base_scr.shape, F32)

    x = x_ref[...]
    logits = _dot_hi(x, w_ref[...]) + b_ref[...]
    tm = x.shape[0]
    lane = lax.broadcasted_iota(I32, (tm, LANES), 1)
    big = jnp.int32(LANES)

    def masked_softmax(mask):
        lm = jnp.where(mask, logits, NEG_INF)
        mx = jnp.max(lm, axis=-1, keepdims=True)
        e = jnp.exp(lm - mx)
        return e / jnp.sum(e, axis=-1, keepdims=True)

    g_prob = masked_softmax(lane < MOE_GROUPS)
    gp = jnp.max(g_prob, axis=-1, keepdims=True)
    gi = jnp.min(jnp.where(g_prob == gp, lane, big), axis=-1, keepdims=True)
    lo = MOE_GROUPS + gi * MOE_EPG
    in_group = (lane >= lo) & (lane < lo + MOE_EPG)
    e_prob = jnp.where(in_group, masked_softmax(in_group), -1.0)
    p1 = jnp.max(e_prob, axis=-1, keepdims=True)
    i1 = jnp.min(jnp.where(e_prob == p1, lane, big), axis=-1, keepdims=True)
    rest = jnp.where(lane == i1, -1.0, e_prob)
    p2 = jnp.max(rest, axis=-1, keepdims=True)
    i2 = jnp.min(jnp.where(rest == p2, lane, big), axis=-1, keepdims=True)
    tot = p1 + p2
    w1 = gp * (p1 / tot)
    w2 = gp * (p2 / tot)
    e1 = i1 - MOE_GROUPS
    e2 = i2 - MOE_GROUPS
    oh1 = jnp.where(lane == e1, 1.0, 0.0)
    oh2 = jnp.where(lane == e2, 1.0, 0.0)
    both = oh1 + oh2
    base = base_scr[0:1, :]
    before = jnp.dot(tri_ref[...], both.astype(BF16), preferred_element_type=F32) + base
    r1 = jnp.sum(oh1 * before, axis=-1, keepdims=True).astype(I32)
    r2 = jnp.sum(oh2 * before, axis=-1, keepdims=True).astype(I32)
    base_new = base + jnp.sum(both, axis=0, keepdims=True)
    base_scr[0:1, :] = base_new
    wt_ref[...] = jnp.where(lane == 0, w1, jnp.where(lane == 1, w2, 0.0))
    ei_ref[...] = jnp.where(lane == 0, e1, jnp.where(lane == 1, e2, jnp.where(lane == 2, r1,
                                                                              jnp.where(lane == 3, r2, 0))))
    cnt_ref[...] = jnp.broadcast_to(base_new, cnt_ref.shape)


def moe_router(x, w_all, b_all, *, tm):
    m, d = x.shape
    tri = jnp.asarray(np.tril(np.ones((tm, tm), np.float32), -1), BF16)
    return pl.pallas_call(
        _router_kernel,
        out_shape=(jax.ShapeDtypeStruct((m, LANES), F32), jax.ShapeDtypeStruct((m, LANES), I32),
                   jax.ShapeDtypeStruct((8, LANES), F32)),
        grid=(m // tm,),
        in_specs=[pl.BlockSpec((tm, d), lambda i: (i, 0)),
                  pl.BlockSpec((d, LANES), lambda i: (0, 0)),
                  pl.BlockSpec((1, LANES), lambda i: (0, 0)),
                  pl.BlockSpec((tm, tm), lambda i: (0, 0))],
        out_specs=(pl.BlockSpec((tm, LANES), lambda i: (i, 0)), pl.BlockSpec((tm, LANES), lambda i: (i, 0)),
                   pl.BlockSpec((8, LANES), lambda i: (0, 0))),
        scratch_shapes=[pltpu.VMEM((8, LANES), F32)],
        compiler_params=_cparams(("arbitrary",)),
        name="moe_router",
    )(x, w_all, b_all, tri)


def _moe_dispatch_kernel(dest_ref, x_ref, xs_in, xs_out, sem, *, tm):
    del xs_in

    def body(g, carry):
        for u in range(DMA_UNROLL):
            r = g * DMA_UNROLL + u
            _tile_copy(x_ref, r, xs_out, dest_ref[r], sem.at[0]).start(priority=0)
            _tile_copy(x_ref, r, xs_out, dest_ref[tm + r], sem.at[0]).start(priority=1)
        return carry

    lax.fori_loop(0, tm // DMA_UNROLL, body, 0)
    for _ in range(MOE_TOPK):
        _wait_tiles(x_ref, xs_out, 0, sem.at[0], tm)


def moe_dispatch(x8, dest, n_slots, *, tm):
    n_t = dest.shape[0]
    xs0 = jnp.zeros((n_slots * ROW_TILE, LANES), U32)
    return pl.pallas_call(
        functools.partial(_moe_dispatch_kernel, tm=tm),
        out_shape=jax.ShapeDtypeStruct(xs0.shape, U32),
        grid=(n_t,),
        in_specs=[pl.BlockSpec((None, None, 2 * tm), lambda i: (i, 0, 0), memory_space=pltpu.SMEM),
                  pl.BlockSpec((tm * ROW_TILE, LANES), lambda i: (i, 0)),
                  pl.BlockSpec(memory_space=pl.ANY)],
        out_specs=pl.BlockSpec(memory_space=pl.ANY),
        scratch_shapes=[pltpu.SemaphoreType.DMA((1,))],
        input_output_aliases={2: 0},
        compiler_params=_cparams(("arbitrary",)),
        name="moe_dispatch",
    )(dest.reshape(n_t, 1, 2 * tm), x8, xs0)


def _moe_ffn_kernel(te_ref, nt_ref, first_ref, slot_ref, nxt_ref, xs_ref, w1_hbm, w3_hbm, w2_hbm, ys_ref,
                    wb1, wb3, wb2, sem, *, layer):
    j = pl.program_id(0)

    def weight_copies(e, s):
        return (pltpu.make_async_copy(w1_hbm.at[layer, e], wb1.at[s], sem.at[s]),
                pltpu.make_async_copy(w3_hbm.at[layer, e], wb3.at[s], sem.at[s]),
                pltpu.make_async_copy(w2_hbm.at[layer, e], wb2.at[s], sem.at[s]))

    @pl.when(j == 0)
    def _():
        for c in weight_copies(te_ref[0], slot_ref[0]):
            c.start()

    @pl.when(j < nt_ref[0])
    def _():
        s = slot_ref[j]

        @pl.when(first_ref[j] == 1)
        def _():
            for c in weight_copies(te_ref[j], s):
                c.wait()

            @pl.when(nxt_ref[j] >= 0)
            def _():
                for c in weight_copies(nxt_ref[j], 1 - s):
                    c.start()

        x = _load_token_tiles(xs_ref, 0, MOE_TILE).astype(BF16)
        h = _silu(_bdot(x, wb1[s])) * _bdot(x, wb3[s])
        _store_token_tiles(ys_ref, _bdot(h, wb2[s]))

    @pl.when(j >= nt_ref[0])
    def _():
        ys_ref[...] = jnp.zeros(ys_ref.shape, U32)


def moe_ffn(xs8, plan, w1, w3, w2, *, layer):
    tile_expert, n_tiles, first, slot, nxt = plan
    n_t = tile_expert.shape[0]
    d, dff = w1.shape[2], w1.shape[3]
    blk = MOE_TILE * ROW_TILE
    grid_spec = pltpu.PrefetchScalarGridSpec(
        num_scalar_prefetch=5,
        grid=(n_t,),
        in_specs=[pl.BlockSpec((blk, LANES), lambda j, te, nt, *_: (jnp.minimum(j, nt[0] - 1), 0)),
                  pl.BlockSpec(memory_space=pl.ANY),
                  pl.BlockSpec(memory_space=pl.ANY),
                  pl.BlockSpec(memory_space=pl.ANY)],
        out_specs=pl.BlockSpec((blk, LANES), lambda j, te, nt, *_: (j, 0)),
        scratch_shapes=[pltpu.VMEM((2, d, dff), F32), pltpu.VMEM((2, d, dff), F32), pltpu.VMEM((2, dff, d), F32),
                        pltpu.SemaphoreType.DMA((2,))],
    )
    return pl.pallas_call(
        functools.partial(_moe_ffn_kernel, layer=layer),
        out_shape=jax.ShapeDtypeStruct(xs8.shape, U32),
        grid_spec=grid_spec,
        compiler_params=_cparams(("arbitrary",)),
        name="moe_ffn",
    )(tile_expert, n_tiles, first, slot, nxt, xs8, w1, w3, w2)


def _moe_combine_kernel(d_cur, d_nxt, ys_hbm, x_ref, wt_ref, g_ref, b_ref, *rest, tm, lead_tiles):
    if lead_tiles is None:
        o_ref, ybuf, sem = rest
    else:
        o_ref, o_tail_ref, ybuf, sem = rest
    i = pl.program_id(0)
    n = pl.num_programs(0)
    slot = i % 2

    def issue(d_ref, s):
        def body(g, carry):
            for u in range(DMA_UNROLL):
                r = g * DMA_UNROLL + u
                _tile_copy(ys_hbm, d_ref[r], ybuf, s * 2 * tm + r, sem.at[s]).start(priority=u % 2)
            return carry
        lax.fori_loop(0, 2 * tm // DMA_UNROLL, body, 0)

    @pl.when(i == 0)
    def _():
        issue(d_cur, 0)

    @pl.when(i + 1 < n)
    def _():
        issue(d_nxt, 1 - slot)

    base_tok = slot * 2 * tm
    _wait_tiles(ys_hbm, ybuf, base_tok, sem.at[slot], 2 * tm)
    wt = wt_ref[...]
    y0 = _load_token_tiles(ybuf, base_tok * ROW_TILE, tm)
    y1 = _load_token_tiles(ybuf, (base_tok + tm) * ROW_TILE, tm)
    y = DEEPNORM_ALPHA * x_ref[...] + (wt[:, 0:1] * y0 + wt[:, 1:2] * y1)
    out = _layer_norm(y, g_ref[...], b_ref[...])
    if lead_tiles is None:
        o_ref[...] = out
    else:
        @pl.when(i < lead_tiles)
        def _():
            o_ref[...] = out

        @pl.when(i >= lead_tiles)
        def _():
            o_tail_ref[...] = out


def moe_combine_deepnorm(ys8, dest, x, wts, g, b, *, tm, lead_rows=None):
    m, d = x.shape
    n_t = m // tm
    dest3 = dest.reshape(n_t, 1, 2 * tm)
    if lead_rows is None:
        lead_tiles = None
        out_shape = jax.ShapeDtypeStruct((m, d), F32)
        out_specs = pl.BlockSpec((tm, d), lambda i: (i, 0))
    else:
        assert lead_rows % tm == 0 and m - lead_rows == tm
        lead_tiles = lead_rows // tm
        out_shape = (jax.ShapeDtypeStruct((lead_rows, d), F32), jax.ShapeDtypeStruct((tm, d), F32))
        out_specs = (pl.BlockSpec((tm, d), lambda i: (jnp.minimum(i, lead_tiles - 1), 0)),
                     pl.BlockSpec((tm, d), lambda i: (0, 0)))
    return pl.pallas_call(
        functools.partial(_moe_combine_kernel, tm=tm, lead_tiles=lead_tiles),
        out_shape=out_shape,
        grid=(n_t,),
        in_specs=[pl.BlockSpec((None, None, 2 * tm), lambda i: (i, 0, 0), memory_space=pltpu.SMEM),
                  pl.BlockSpec((None, None, 2 * tm), lambda i: (jnp.minimum(i + 1, n_t - 1), 0, 0),
                               memory_space=pltpu.SMEM),
                  pl.BlockSpec(memory_space=pl.ANY),
                  pl.BlockSpec((tm, d), lambda i: (i, 0)),
                  pl.BlockSpec((tm, LANES), lambda i: (i, 0)),
                  pl.BlockSpec((1, d), lambda i: (0, 0)),
                  pl.BlockSpec((1, d), lambda i: (0, 0))],
        out_specs=out_specs,
        scratch_shapes=[pltpu.VMEM((2 * 2 * tm * ROW_TILE, LANES), U32), pltpu.SemaphoreType.DMA((2,))],
        compiler_params=_cparams(("arbitrary",)),
        name="moe_combine_deepnorm",
    )(dest3, dest3, ys8, x, wts, g.reshape(1, d), b.reshape(1, d))


def _moe_plan(eidx, rank, counts, n_tok, tm):
    n_tiles_max = n_tok * MOE_TOPK // MOE_TILE + MOE_EXPERTS
    tiles_per = (counts + MOE_TILE - 1) // MOE_TILE
    tile_end = jnp.cumsum(tiles_per)
    pad_off = (tile_end - tiles_per) * MOE_TILE
    onehot = eidx[:, :, None] == jnp.arange(MOE_EXPERTS, dtype=I32)[None, None, :]
    dest = jnp.sum(jnp.where(onehot, pad_off[None, None, :], 0), axis=-1) + rank
    tile_expert = jnp.minimum(
        jnp.sum((jnp.arange(n_tiles_max, dtype=I32)[:, None] >= tile_end[None, :]).astype(I32), axis=1),
        MOE_EXPERTS - 1)
    dest = dest.reshape(n_tok // tm, tm, MOE_TOPK).transpose(0, 2, 1).reshape(n_tok // tm, MOE_TOPK * tm)
    experts = jnp.arange(MOE_EXPERTS, dtype=I32)
    nonempty = tiles_per > 0
    slot_e = (jnp.cumsum(nonempty.astype(I32)) - 1) % 2
    later = nonempty[None, :] & (experts[None, :] > experts[:, None])
    nxt_e = jnp.min(jnp.where(later, experts[None, :], MOE_EXPERTS), axis=1)
    nxt_e = jnp.where(nxt_e == MOE_EXPERTS, -1, nxt_e)
    tiles = jnp.arange(n_tiles_max, dtype=I32)
    tile_oh = tile_expert[:, None] == experts[None, :]
    first = jnp.any((tiles[:, None] == (tile_end - tiles_per)[None, :]) & nonempty[None, :], axis=1).astype(I32)
    slot = jnp.sum(jnp.where(tile_oh, slot_e[None, :], 0), axis=1).astype(I32)
    nxt = jnp.sum(jnp.where(tile_oh, nxt_e[None, :], 0), axis=1).astype(I32)
    ffn_plan = (tile_expert, tile_end[-1:].astype(I32), first, slot, nxt)
    return ffn_plan, dest, n_tiles_max * MOE_TILE


def hier_moe_deepnorm(x, x8, w_all, b_all, w1, w3, w2, g, b, *, layer, lead_rows=None):
    n_tok = x.shape[0]
    wts, ei, cnt = moe_router(x, w_all, b_all, tm=640)
    counts = cnt[0, :MOE_EXPERTS].astype(I32)
    ffn_plan, dest, n_slots = _moe_plan(ei[:, 0:2], ei[:, 2:4], counts, n_tok, TOKEN_TILE)
    xs8 = moe_dispatch(x8, dest, n_slots, tm=TOKEN_TILE)
    ys8 = moe_ffn(xs8, ffn_plan, w1, w3, w2, layer=layer)
    return moe_combine_deepnorm(ys8, dest, x, wts, g, b, tm=TOKEN_TILE, lead_rows=lead_rows)


def kernel(x_prompt, x_sample, cache_moba_k, cache_moba_v, state_conv, state_ret, cache_mem_k, cache_mem_v,
           page_table, mem_prompt, rel_bias, ev_w_in, ev_conv_w, ev_conv_b, ev_conv_ln_g, ev_conv_ln_b,
           ev_w_out, od_w_in, od_gn_g, od_gn_b, od_w_out, mem_wq, mem_wk, mem_wv, mem_wo, ln_g, ln_b,
           moe_w_group, moe_b_group, moe_w_router, moe_b_router, moe_w1, moe_w3, moe_w2):
    bp, tp, d = x_prompt.shape
    bs, ts, _ = x_sample.shape
    n_p = bp * tp
    n_s = bs * ts
    n_all = n_p + n_s
    page = cache_moba_k.shape[1]
    past_len = page_table.shape[1] * page
    nh, hd = MOBA_HEADS, MOBA_HEAD_DIM
    assert tp % MOBA_BLOCK == 0 and past_len % MOBA_BLOCK == 0 and ts <= MOBA_BLOCK
    assert MOBA_BLOCK >= MAX_DISTANCE
    tm = 1280
    assert n_all % tm == 0

    x = (x_prompt.reshape(n_p, d), x_sample.reshape(n_s, d))

    tab_heads = rel_bias.T
    cache_kt = cache_moba_k.transpose(0, 2, 3, 4, 1)
    cache_vt = cache_moba_v.transpose(0, 2, 3, 4, 1)

    pos_p = jnp.arange(tp, dtype=I32)
    pos_s = past_len + jnp.arange(ts, dtype=I32)
    cos_p, sin_p = _rotary_tables(pos_p)
    cos_s, sin_s = _rotary_tables(pos_s)
    ret_c_p = _retention_consts(math.gcd(tp, RET_CHUNK))
    ret_c_s = _retention_consts(math.gcd(ts, RET_CHUNK))

    outs = {}
    for layer in range(DEPTH):
        if layer % 2 == 0:
            e = layer // 2
            proj = matmul(x, ev_w_in[e].astype(BF16), tm=tm, tn=ev_w_in.shape[2])
            k_new = proj[:, 2 * CONV_CH + MOBA_WIDTH:2 * CONV_CH + 2 * MOBA_WIDTH]
            v_new = proj[:, 2 * CONV_CH + 2 * MOBA_WIDTH:]
            q_s = proj[n_p:, 2 * CONV_CH:2 * CONV_CH + MOBA_WIDTH]
            conv_args = (ev_conv_w[e], ev_conv_b[e], ev_conv_ln_g[e], ev_conv_ln_b[e])
            c_p, cst_p = conv_module(proj, jnp.zeros((bp, CONV_WIDTH - 1, CONV_CH), F32), *conv_args,
                                     row0=0, n_batch=bp, t_len=tp, nb=1, out_dtype=BF16)
            c_s, cst_s = conv_module(proj, state_conv[e], *conv_args,
                                     row0=n_p, n_batch=bs, t_len=ts, nb=bs, out_dtype=F32)
            a_p = moba_prompt(proj, tab_heads, n_batch=bp, t_len=tp, out_dtype=BF16)

            def by_head(a):
                return a.reshape(bs, ts, nh, hd).transpose(0, 2, 1, 3)

            a_s = moba_sample(page_table, cache_kt, cache_vt, by_head(q_s), by_head(k_new[n_p:]),
                              by_head(v_new[n_p:]), tab_heads, layer_e=e, t_len=ts)
            a_s = a_s.reshape(bs, nh, ts, hd).transpose(0, 2, 1, 3).reshape(n_s, MOBA_WIDTH).astype(BF16)
            w_out = ev_w_out[e].astype(BF16)
            x = matmul_deepnorm([(c_p, c_s.astype(BF16)), (a_p, a_s)], [w_out[:CONV_CH], w_out[CONV_CH:]], x,
                                ln_g[layer, 0], ln_b[layer, 0], tm=tm)
            outs.setdefault("kp", []).append(k_new[:n_p].reshape(bp, tp, nh, hd))
            outs.setdefault("vp", []).append(v_new[:n_p].reshape(bp, tp, nh, hd))
            outs.setdefault("ks", []).append(k_new[n_p:].reshape(bs, ts, nh, hd))
            outs.setdefault("vs", []).append(v_new[n_p:].reshape(bs, ts, nh, hd))
            outs.setdefault("cp", []).append(cst_p)
            outs.setdefault("cs", []).append(cst_s)
        else:
            o = layer // 2
            proj = matmul(x, od_w_in[o].astype(BF16), tm=tm, tn=2048)
            r_p, st_p = retention(proj, cos_p, sin_p, ret_c_p, od_gn_g[o], od_gn_b[o], None,
                                  row0=0, n_batch=bp, t_len=tp, nb=1, out_dtype=BF16)
            r_s, st_s = retention(proj, cos_s, sin_s, ret_c_s, od_gn_g[o], od_gn_b[o], state_ret[o],
                                  row0=n_p, n_batch=bs, t_len=ts, nb=2, out_dtype=BF16)
            x = matmul_deepnorm([(r_p, r_s)], [od_w_out[o].astype(BF16)], x, ln_g[layer, 0], ln_b[layer, 0], tm=tm)
            outs.setdefault("sp", []).append(st_p)
            outs.setdefault("ss", []).append(st_s)

        m_tok = mem_prompt.shape[1]
        mem2 = mem_prompt.reshape(bp * m_tok, d)
        mk_p = matmul(mem2, mem_wk[layer].astype(BF16), tm=512, tn=d)
        mv_p = matmul(mem2, mem_wv[layer].astype(BF16), tm=512, tn=d)
        outs.setdefault("mk", []).append(mk_p.reshape(bp, m_tok, MEM_HEADS, MEM_HEAD_DIM))
        outs.setdefault("mv", []).append(mv_p.reshape(bp, m_tok, MEM_HEADS, MEM_HEAD_DIM))
        q = matmul(x, mem_wq[layer].astype(BF16), tm=tm, tn=d, out_dtype=BF16)
        o_p = mem_attention(q, mk_p.reshape(1, bp, m_tok, d), mv_p.reshape(1, bp, m_tok, d),
                            layer=0, row0=0, n_batch=bp, t_len=tp, nb=1, tq=512, out_dtype=BF16)
        qt_s = q[n_p:].astype(F32).reshape(bs, ts, MEM_HEADS, MEM_HEAD_DIM).transpose(0, 3, 2, 1)
        o_s = mem_attention_cache(qt_s.reshape(bs, MEM_HEAD_DIM, MEM_HEADS * ts), cache_mem_k, cache_mem_v,
                                  layer=layer, nb=2, tq=ts)
        o_s = o_s.reshape(bs, MEM_HEAD_DIM, MEM_HEADS, ts).transpose(0, 3, 2, 1).reshape(n_s, d).astype(BF16)
        x, x8 = matmul_deepnorm([(o_p, o_s)], [mem_wo[layer].astype(BF16)], x, ln_g[layer, 1], ln_b[layer, 1], tm=tm,
                                token_tiles=True)

        w_all = jnp.zeros((d, LANES), F32)
        w_all = w_all.at[:, :MOE_GROUPS].set(moe_w_group[layer])
        w_all = w_all.at[:, MOE_GROUPS:MOE_GROUPS + MOE_EXPERTS].set(moe_w_router[layer])
        b_all = jnp.zeros((1, LANES), F32)
        b_all = b_all.at[0, :MOE_GROUPS].set(moe_b_group[layer])
        b_all = b_all.at[0, MOE_GROUPS:MOE_GROUPS + MOE_EXPERTS].set(moe_b_router[layer])
        x = hier_moe_deepnorm(x, x8, w_all, b_all, moe_w1, moe_w3, moe_w2, ln_g[layer, 2], ln_b[layer, 2],
                              layer=layer, lead_rows=n_p if layer == DEPTH - 1 else None)

    y_prompt = x[0].reshape(bp, tp, d)
    y_sample = x[1].reshape(bs, ts, d)
    return (y_prompt, y_sample,
            jnp.stack(outs["kp"], axis=2), jnp.stack(outs["vp"], axis=2),
            jnp.stack(outs["ks"], axis=2), jnp.stack(outs["vs"], axis=2),
            jnp.stack(outs["cp"], axis=0), jnp.stack(outs["cs"], axis=0),
            jnp.stack(outs["sp"], axis=0), jnp.stack(outs["ss"], axis=0),
            jnp.stack(outs["mk"], axis=0), jnp.stack(outs["mv"], axis=0))
```
